```python
import jax
import jax.numpy as jnp
from jax import lax
import numpy as np

D_MODEL = 1024
BATCH = 4
SEQ = 8192
DEPTH = 1

MEM_TOKENS = 256
RWKV_HEADS = 8
RWKV_HEAD_DIM = 64
RWKV_WIDTH = RWKV_HEADS * RWKV_HEAD_DIM
DECAY_RANK = 64
AAA_RANK = 64
GATE_RANK = 128
RWKV_SPLITS = [RWKV_WIDTH, 2 * RWKV_WIDTH, 3 * RWKV_WIDTH, 3 * RWKV_WIDTH + DECAY_RANK, 3 * RWKV_WIDTH + DECAY_RANK + AAA_RANK]
RWKV_COLS = 3 * RWKV_WIDTH + DECAY_RANK + AAA_RANK + GATE_RANK
SWA_Q_HEADS = 8
SWA_KV_HEADS = 2
SWA_HEAD_DIM = 64
SWA_GROUP = SWA_Q_HEADS // SWA_KV_HEADS
SWA_WIDTH = SWA_Q_HEADS * SWA_HEAD_DIM
SWA_KV_WIDTH = SWA_KV_HEADS * SWA_HEAD_DIM
SWA_COLS = SWA_WIDTH + 2 * SWA_KV_WIDTH
IN_COLS = RWKV_COLS + SWA_COLS
MIX_WIDTH = RWKV_WIDTH + SWA_WIDTH
WINDOW = 128
ROPE_THETA = 10000.0
MEM_HEADS = 4
MEM_HEAD_DIM = D_MODEL // MEM_HEADS
N_EXPERTS = 256
TOP_K = 8
N_GROUPS = 8
TOPK_GROUPS = 4
EXPERT_DIM = D_MODEL // 4
SHARED_DIM = EXPERT_DIM
ROUTED_SCALE = 2.5
MOE_BLOCK = 128
LN_EPS = 1e-5
GN_EPS = 64e-5
NEG_INF = -1e30
DEEPNORM_ALPHA = (2 * DEPTH) ** 0.25
DEEPNORM_BETA = (8 * DEPTH) ** -0.25

kernel_name = "hybrid_rwkv7_swa_sink_memxattn_moe_deepnorm"


def layer_norm(x, g, b):
    xf = x.astype(jnp.float32)
    mu = xf.mean(-1, keepdims=True)
    var = jnp.square(xf - mu).mean(-1, keepdims=True)
    return ((xf - mu) * lax.rsqrt(var + LN_EPS) * g + b).astype(x.dtype)


def rope(x, positions):
    hd = x.shape[-1]
    inv_freq = ROPE_THETA ** (-jnp.arange(0, hd, 2, dtype=jnp.float32) / hd)
    ang = positions.astype(jnp.float32)[..., None] * inv_freq
    cos = jnp.cos(ang)[:, :, None, :]
    sin = jnp.sin(ang)[:, :, None, :]
    x1, x2 = jnp.split(x.astype(jnp.float32), 2, axis=-1)
    return jnp.concatenate([x1 * cos - x2 * sin, x2 * cos + x1 * sin], axis=-1).astype(x.dtype)


def wkv7_scan(r, w, k, v, a, b):
    B, S, H, N = r.shape

    def step(state, inp):
        r_t, w_t, k_t, v_t, a_t, b_t = inp
        sa = jnp.einsum('bhvk,bhk->bhv', state, a_t)
        state = state * w_t[:, :, None, :] + sa[..., None] * b_t[:, :, None, :] + v_t[..., None] * k_t[:, :, None, :]
        return state, jnp.einsum('bhvk,bhk->bhv', state, r_t)

    xs = tuple(jnp.moveaxis(t, 1, 0) for t in (r, w, k, v, a, b))
    _, y = lax.scan(step, jnp.zeros((B, H, N, N), jnp.float32), xs)
    return jnp.moveaxis(y, 0, 1)


def rwkv7_group(u, mu_shift, w_decay_up, w0, a_up, a0, g_up, k_k, k_a, r_k, lnx_g, lnx_b):
    B, S, _ = u.shape
    f32 = jnp.float32
    prev = jnp.pad(u, ((0, 0), (1, 0), (0, 0)))[:, :-1]
    u = u + (prev - u) * mu_shift
    r, k, v, wd, ad, gd = jnp.split(u, RWKV_SPLITS, axis=-1)
    logw = -jax.nn.softplus(-(w0 + jnp.tanh(wd) @ w_decay_up).astype(f32)) - 0.5
    decay = jnp.exp(-jnp.exp(logw))
    a = jax.nn.sigmoid((a0 + ad @ a_up).astype(f32))
    g = jax.nn.sigmoid(gd) @ g_up

    def heads(t):
        return t.astype(f32).reshape(B, S, RWKV_HEADS, RWKV_HEAD_DIM)

    kk = heads(k * k_k)
    kk = kk / jnp.maximum(jnp.sqrt(jnp.sum(kk * kk, -1, keepdims=True)), 1e-12)
    k_mod = k.astype(f32) * (1.0 + (a - 1.0) * k_a)
    rh, kh, vh, ah = heads(r), heads(k_mod), heads(v), heads(a)
    y = wkv7_scan(rh, heads(decay), kh, vh, -kk, kk * ah)
    mu = y.mean(-1, keepdims=True)
    var = jnp.square(y - mu).mean(-1, keepdims=True)
    yn = ((y - mu) * lax.rsqrt(var + GN_EPS)).reshape(B, S, RWKV_WIDTH) * lnx_g + lnx_b
    bonus = jnp.sum(rh * kh * r_k, -1, keepdims=True) * vh
    y = yn + bonus.reshape(B, S, RWKV_WIDTH)
    return (y * g).astype(u.dtype)


def swa_group(u, positions, sinks):
    B, S, _ = u.shape
    q, k, v = jnp.split(u, [SWA_WIDTH, SWA_WIDTH + SWA_KV_WIDTH], axis=-1)
    q = rope(q.reshape(B, S, SWA_Q_HEADS, SWA_HEAD_DIM), positions)
    k = rope(k.reshape(B, S, SWA_KV_HEADS, SWA_HEAD_DIM), positions)
    v = v.reshape(B, S, SWA_KV_HEADS, SWA_HEAD_DIM)
    nb = S // WINDOW
    qb = q.reshape(B, nb, WINDOW, SWA_KV_HEADS, SWA_GROUP, SWA_HEAD_DIM)
    kb = k.reshape(B, nb, WINDOW, SWA_KV_HEADS, SWA_HEAD_DIM)
    vb = v.reshape(B, nb, WINDOW, SWA_KV_HEADS, SWA_HEAD_DIM)
    pad = ((0, 0), (1, 0), (0, 0), (0, 0), (0, 0))
    kw = jnp.concatenate([jnp.pad(kb, pad)[:, :-1], kb], axis=2)
    vw = jnp.concatenate([jnp.pad(vb, pad)[:, :-1], vb], axis=2)
    s = jnp.einsum('bnqhgd,bnkhd->bnhgqk', qb, kw).astype(jnp.float32) * (SWA_HEAD_DIM ** -0.5)
    qi = jnp.arange(WINDOW)[:, None]
    kj = jnp.arange(2 * WINDOW)[None, :]
    rel = qi + WINDOW - kj
    band = (rel >= 0) & (rel < WINDOW)
    valid = (jnp.arange(nb)[:, None, None] * WINDOW + kj[None] - WINDOW) >= 0
    mask = band[None] & valid
    s = jnp.where(mask[None, :, None, None], s, NEG_INF)
    sink = jnp.broadcast_to(sinks.astype(jnp.float32).reshape(1, 1, SWA_KV_HEADS, SWA_GROUP, 1, 1),
                            s.shape[:-1] + (1,))
    p = jax.nn.softmax(jnp.concatenate([s, sink], axis=-1), axis=-1)[..., :-1]
    o = jnp.einsum('bnhgqk,bnkhd->bnqhgd', p.astype(v.dtype), vw)
    return o.reshape(B, S, SWA_WIDTH)


def hybrid_mixer(h, positions, w_in, mu_shift, w_decay_up, w0, a_up, a0, g_up, k_k, k_a, r_k,
                 lnx_g, lnx_b, sinks, w_o):
    u = h @ w_in
    y_rwkv = rwkv7_group(u[..., :RWKV_COLS], mu_shift, w_decay_up, w0, a_up, a0, g_up,
                         k_k, k_a, r_k, lnx_g, lnx_b)
    y_swa = swa_group(u[..., RWKV_COLS:], positions, sinks)
    return jnp.concatenate([y_rwkv, y_swa], axis=-1) @ w_o


def memory_cross_attention(h, mem, wm_q, wm_kv, wm_o):
    B, S, D = h.shape
    q = (h @ wm_q).reshape(B, S, MEM_HEADS, MEM_HEAD_DIM)
    k, v = jnp.split(mem @ wm_kv, 2, axis=-1)
    k = k.reshape(B, -1, MEM_HEADS, MEM_HEAD_DIM)
    v = v.reshape(B, -1, MEM_HEADS, MEM_HEAD_DIM)
    s = jnp.einsum('bshd,bmhd->bhsm', q, k).astype(jnp.float32) * (MEM_HEAD_DIM ** -0.5)
    p = jax.nn.softmax(s, axis=-1)
    o = jnp.einsum('bhsm,bmhd->bshd', p.astype(v.dtype), v)
    return o.reshape(B, S, D) @ wm_o


def moe_ffn(h, w_router, router_bias, we_gate, we_up, we_down, ws_gate, ws_up, ws_down):
    B, S, D = h.shape
    xt = h.reshape(-1, D)
    N = xt.shape[0]
    scores = jax.nn.sigmoid((xt @ w_router).astype(jnp.float32))
    biased = scores + router_bias.astype(jnp.float32)
    grp = biased.reshape(N, N_GROUPS, N_EXPERTS // N_GROUPS)
    gscore = lax.top_k(grp, 2)[0].sum(-1)
    _, gidx = lax.top_k(gscore, TOPK_GROUPS)
    gmask = jax.nn.one_hot(gidx, N_GROUPS, dtype=jnp.float32).sum(1) > 0
    emask = jnp.repeat(gmask, N_EXPERTS // N_GROUPS, axis=1)
    _, eidx = lax.top_k(jnp.where(emask, biased, NEG_INF), TOP_K)
    sel = jnp.take_along_axis(scores, eidx, axis=1)
    gates = sel / jnp.sum(sel, -1, keepdims=True) * ROUTED_SCALE
    NK = N * TOP_K
    e_flat = eidx.reshape(-1).astype(jnp.int32)
    tok_flat = jnp.repeat(jnp.arange(N, dtype=jnp.int32), TOP_K)
    g_flat = gates.reshape(-1)
    order = jnp.argsort(e_flat)
    e_sorted = e_flat[order]
    counts = jnp.bincount(e_flat, length=N_EXPERTS)
    starts = jnp.cumsum(counts) - counts
    padded = ((counts + MOE_BLOCK - 1) // MOE_BLOCK) * MOE_BLOCK
    pad_ends = jnp.cumsum(padded)
    pad_starts = pad_ends - padded
    dest = pad_starts[e_sorted] + (jnp.arange(NK, dtype=jnp.int32) - starts[e_sorted])
    n_blocks = (NK + MOE_BLOCK - 1) // MOE_BLOCK + N_EXPERTS
    P = n_blocks * MOE_BLOCK
    tok_pad = jnp.full((P,), N, jnp.int32).at[dest].set(tok_flat[order])
    gate_pad = jnp.zeros((P,), jnp.float32).at[dest].set(g_flat[order])
    block_expert = jnp.minimum(
        jnp.searchsorted(pad_ends, jnp.arange(n_blocks, dtype=jnp.int32) * MOE_BLOCK, side='right'),
        N_EXPERTS - 1).astype(jnp.int32)
    x_ext = jnp.concatenate([xt, jnp.zeros((1, D), xt.dtype)], axis=0)

    def block(acc, inp):
        idx, g, e = inp
        xb = x_ext[idx]
        hb = jax.nn.silu(xb @ we_gate[e]) * (xb @ we_up[e])
        yb = (hb @ we_down[e]) * g[:, None].astype(xb.dtype)
        return acc.at[idx].add(yb), None

    acc, _ = lax.scan(block, jnp.zeros((N + 1, D), xt.dtype),
                      (tok_pad.reshape(n_blocks, MOE_BLOCK), gate_pad.reshape(n_blocks, MOE_BLOCK), block_expert))
    shared = (jax.nn.silu(xt @ ws_gate) * (xt @ ws_up)) @ ws_down
    return (acc[:N] + shared).reshape(B, S, D)


def setup_inputs(seed: int = 0) -> dict:
    key = jax.random.key(seed)
    ks = list(jax.random.split(key, 40))
    ctr = [0]

    def nk():
        ctr[0] += 1
        return ks[ctr[0] - 1]

    def nrm(shape, scale):
        return jax.random.normal(nk(), shape, jnp.float32) * scale

    def uni(shape, lo, hi):
        return jax.random.uniform(nk(), shape, jnp.float32, lo, hi)

    L, D = DEPTH, D_MODEL
    x = nrm((BATCH, SEQ, D), 1.0)
    mem = nrm((BATCH, MEM_TOKENS, D), 1.0)
    offset = jax.random.randint(nk(), (BATCH, 1), 0, 1024, jnp.int32)
    positions = offset + jnp.arange(SEQ, dtype=jnp.int32)[None, :]
    return {
        "x": x,
        "mem": mem,
        "positions": positions,
        "w_in": nrm((L, D, IN_COLS), D ** -0.5),
        "mu_shift": uni((L, RWKV_COLS), 0.0, 1.0),
        "w_decay_up": nrm((L, DECAY_RANK, RWKV_WIDTH), 0.5 * DECAY_RANK ** -0.5),
        "w0": uni((L, RWKV_WIDTH), -6.0, -1.0),
        "a_up": nrm((L, AAA_RANK, RWKV_WIDTH), AAA_RANK ** -0.5),
        "a0": nrm((L, RWKV_WIDTH), 0.1),
        "g_up": nrm((L, GATE_RANK, RWKV_WIDTH), GATE_RANK ** -0.5),
        "k_k": 0.85 + nrm((L, RWKV_WIDTH), 0.05),
        "k_a": 1.0 + nrm((L, RWKV_WIDTH), 0.05),
        "r_k": nrm((L, RWKV_HEADS, RWKV_HEAD_DIM), 0.1),
        "lnx_g": 1.0 + nrm((L, RWKV_WIDTH), 0.05),
        "lnx_b": nrm((L, RWKV_WIDTH), 0.01),
        "sinks": nrm((L, SWA_Q_HEADS), 0.5),
        "w_o": nrm((L, MIX_WIDTH, D), DEEPNORM_BETA * MIX_WIDTH ** -0.5),
        "ln1_g": 1.0 + nrm((L, D), 0.05),
        "ln1_b": nrm((L, D), 0.01),
        "wm_q": nrm((L, D, D), D ** -0.5),
        "wm_kv": nrm((L, D, 2 * D), D ** -0.5),
        "wm_o": nrm((L, D, D), DEEPNORM_BETA * D ** -0.5),
        "ln2_g": 1.0 + nrm((L, D), 0.05),
        "ln2_b": nrm((L, D), 0.01),
        "w_router": nrm((L, D, N_EXPERTS), D ** -0.5),
        "router_bias": nrm((L, N_EXPERTS), 0.01),
        "we_gate": nrm((L, N_EXPERTS, D, EXPERT_DIM), D ** -0.5),
        "we_up": nrm((L, N_EXPERTS, D, EXPERT_DIM), D ** -0.5),
        "we_down": nrm((L, N_EXPERTS, EXPERT_DIM, D), DEEPNORM_BETA * EXPERT_DIM ** -0.5),
        "ws_gate": nrm((L, D, SHARED_DIM), D ** -0.5),
        "ws_up": nrm((L, D, SHARED_DIM), D ** -0.5),
        "ws_down": nrm((L, SHARED_DIM, D), DEEPNORM_BETA * SHARED_DIM ** -0.5),
        "ln3_g": 1.0 + nrm((L, D), 0.05),
        "ln3_b": nrm((L, D), 0.01),
    }


def reference(x, mem, positions, w_in, mu_shift, w_decay_up, w0, a_up, a0, g_up, k_k, k_a, r_k,
              lnx_g, lnx_b, sinks, w_o, ln1_g, ln1_b, wm_q, wm_kv, wm_o, ln2_g, ln2_b,
              w_router, router_bias, we_gate, we_up, we_down, ws_gate, ws_up, ws_down, ln3_g, ln3_b):
    for l in range(DEPTH):
        mix = hybrid_mixer(x, positions, w_in[l], mu_shift[l], w_decay_up[l], w0[l], a_up[l], a0[l],
                           g_up[l], k_k[l], k_a[l], r_k[l], lnx_g[l], lnx_b[l], sinks[l], w_o[l])
        x = layer_norm(DEEPNORM_ALPHA * x + mix, ln1_g[l], ln1_b[l])
        xa = memory_cross_attention(x, mem, wm_q[l], wm_kv[l], wm_o[l])
        x = layer_norm(DEEPNORM_ALPHA * x + xa, ln2_g[l], ln2_b[l])
        ff = moe_ffn(x, w_router[l], router_bias[l], we_gate[l], we_up[l], we_down[l],
                     ws_gate[l], ws_up[l], ws_down[l])
        x = layer_norm(DEEPNORM_ALPHA * x + ff, ln3_g[l], ln3_b[l])
    return x
```

```python
import functools

import jax
import jax.numpy as jnp
from jax import lax
from jax.experimental import pallas as pl
from jax.experimental.pallas import tpu as pltpu

f32 = jnp.float32
bf16 = jnp.bfloat16
i32 = jnp.int32

RWKV_HEADS = 8
HEAD_DIM = 64
RWKV_WIDTH = RWKV_HEADS * HEAD_DIM
DECAY_RANK = 64
AAA_RANK = 64
GATE_RANK = 128
RWKV_COLS = 3 * RWKV_WIDTH + DECAY_RANK + AAA_RANK + GATE_RANK
SWA_Q_HEADS = 8
SWA_KV_HEADS = 2
SWA_GROUP = SWA_Q_HEADS // SWA_KV_HEADS
SWA_WIDTH = SWA_Q_HEADS * HEAD_DIM
SWA_KV_WIDTH = SWA_KV_HEADS * HEAD_DIM
SWA_COLS = SWA_WIDTH + 2 * SWA_KV_WIDTH
WINDOW = 128
ROPE_THETA = 10000.0
MEM_HEADS = 4
N_EXPERTS = 256
TOP_K = 8
N_GROUPS = 8
GROUP_SIZE = N_EXPERTS // N_GROUPS
TOPK_GROUPS = 4
ROUTED_SCALE = 2.5
LN_EPS = 1e-5
GN_EPS = 64e-5
NEG_INF = -1e30

LANES = 128
SUBLANES = 8
WKV_CHUNK = 64
WKV_GROUP = 4
VMEM_LIMIT = 56 * 1024 * 1024


def _cparams(sem):
    return pltpu.CompilerParams(dimension_semantics=sem, vmem_limit_bytes=VMEM_LIMIT)


def _const_spec(shape):
    nd = len(shape)
    return pl.BlockSpec(shape, lambda *_: (0,) * nd)


def _dot(a, b):
    return jnp.dot(a, b, preferred_element_type=f32)


def _dot_nt(a, b):
    return lax.dot_general(a, b, (((1,), (1,)), ((), ())), preferred_element_type=f32)


def _dot_tn(a, b):
    return lax.dot_general(a, b, (((0,), (0,)), ((), ())), preferred_element_type=f32)


def _split2(x):
    hi = x.astype(bf16)
    lo = (x - hi.astype(f32)).astype(bf16)
    return hi, lo


def _split3(x):
    hi = x.astype(bf16)
    r1 = x - hi.astype(f32)
    mid = r1.astype(bf16)
    lo = (r1 - mid.astype(f32)).astype(bf16)
    return hi, mid, lo


def _dot_hp(a, b):
    ah, al = _split2(a)
    bh, bl = _split2(b)
    return _dot(ah, bh) + _dot(ah, bl) + _dot(al, bh)


def _dot_exact_lhs(m_bf16, x):
    hi, mid, lo = _split3(x)
    return _dot(m_bf16, hi) + _dot(m_bf16, mid) + _dot(m_bf16, lo)


def _dot_exact_rhs(x, m_bf16):
    hi, mid, lo = _split3(x)
    return _dot(hi, m_bf16) + _dot(mid, m_bf16) + _dot(lo, m_bf16)


def _sigmoid(x):
    return 1.0 / (1.0 + jnp.exp(-x))


def _layer_norm(h, g, b):
    mu = jnp.mean(h, axis=-1, keepdims=True)
    d = h - mu
    var = jnp.mean(d * d, axis=-1, keepdims=True)
    return d * lax.rsqrt(var + LN_EPS) * g + b


def _proj_body(*refs, n_out):
    x_ref = refs[0]
    w_refs = refs[1:1 + n_out]
    o_refs = refs[1 + n_out:]
    xb = x_ref[...].astype(bf16)
    for w_ref, o_ref in zip(w_refs, o_refs):
        o_ref[...] = _dot(xb, w_ref[...]).astype(o_ref.dtype)


def _proj(x, ws, out_dtypes, tile):
    n, k = x.shape
    tile = min(tile, n)
    outs = pl.pallas_call(
        functools.partial(_proj_body, n_out=len(ws)),
        out_shape=[jax.ShapeDtypeStruct((n, w.shape[1]), dt) for w, dt in zip(ws, out_dtypes)],
        grid=(n // tile,),
        in_specs=[pl.BlockSpec((tile, k), lambda i: (i, 0))] + [_const_spec(w.shape) for w in ws],
        out_specs=[pl.BlockSpec((tile, w.shape[1]), lambda i: (i, 0)) for w in ws],
        compiler_params=_cparams(("parallel",)),
        name="proj",
    )(x, *ws)
    return outs


def _wkv_chunk(at, bt, kt, rt, v, wlast, s_prev, masks):
    bd, strict, incl, eye = masks

    def stack(x):
        return jnp.where(bd, jnp.concatenate([x] * WKV_GROUP, axis=0), 0.0)

    a_b = stack(at).astype(bf16)
    b_b = stack(bt).astype(bf16)
    k_b = stack(kt).astype(bf16)
    r_f = stack(rt)
    v_b = stack(v).astype(bf16)
    n = a_b.shape[0]
    g = _dot_nt(jnp.concatenate([a_b, r_f.astype(bf16)], axis=0), jnp.concatenate([b_b, k_b], axis=0))
    l_ab = jnp.where(strict, g[:n, :n], 0.0)
    l_ak = jnp.where(strict, g[:n, n:], 0.0)
    m_rb = jnp.where(incl, g[n:, :n], 0.0)
    m_rk = jnp.where(incl, g[n:, n:], 0.0)
    x = l_ab
    t = eye + x
    for _ in range(5):
        xb = x.astype(bf16)
        x = _dot(xb, xb)
        t = t + _dot(t.astype(bf16), x.astype(bf16))
    lakv = _dot(l_ak.astype(bf16), v_b)
    au = _dot(t.astype(bf16), jnp.concatenate([a_b, lakv.astype(bf16)], axis=1))
    au_b = au.astype(bf16)
    ry = _dot(m_rb.astype(bf16), au_b)
    r_bar = r_f + ry[:, :n]
    y_bar = ry[:, n:] + _dot(m_rk.astype(bf16), v_b)
    pq = _dot_tn(au_b, b_b)
    p = (eye + pq[:n]) * wlast
    q = (pq[n:] + _dot_tn(v_b, k_b)) * wlast
    s_b = s_prev.astype(bf16)
    y_bd = _dot_nt(r_bar.astype(bf16), s_b) + y_bar
    s_next = _dot(s_b, p.astype(bf16)) + q
    c = WKV_CHUNK
    y = y_bd[0:c] + y_bd[c:2 * c] + y_bd[2 * c:3 * c] + y_bd[3 * c:4 * c]
    return y, s_next


def _rwkv_body(u_ref, mu_ref, wdec_ref, w0_ref, aup_ref, a0_ref, gup_ref, kk_ref, ka_ref, rk_ref,
               lng_ref, lnb_ref, seg_ref, tri_ref, y_ref, state_ref, carry_ref, *, tt):
    j = pl.program_id(1)

    @pl.when(j == 0)
    def _():
        state_ref[...] = jnp.zeros_like(state_ref)
        carry_ref[...] = jnp.zeros_like(carry_ref)

    w = RWKV_WIDTH
    u = u_ref[...]
    row = lax.broadcasted_iota(i32, u.shape, 0)
    prev = jnp.where(row == 0, carry_ref[0:1, :], pltpu.roll(u, 1, axis=0))
    carry_ref[0:1, :] = u[tt - 1:tt, :]
    us = u + (prev - u) * mu_ref[...]
    r = us[:, 0:w]
    k = us[:, w:2 * w]
    v = us[:, 2 * w:3 * w]
    wa = us[:, 3 * w:3 * w + DECAY_RANK + AAA_RANK]
    gd = us[:, 3 * w + DECAY_RANK + AAA_RANK:]
    z = w0_ref[...] + _dot_hp(jnp.tanh(wa), wdec_ref[...])
    softplus_neg_z = jnp.maximum(-z, 0.0) + jnp.log(1.0 + jnp.exp(-jnp.abs(z)))
    lw = -jnp.exp(-softplus_neg_z - 0.5)
    a = _sigmoid(a0_ref[...] + _dot_hp(wa, aup_ref[...]))
    gate = _dot(_sigmoid(gd).astype(bf16), gup_ref[...].astype(bf16))
    seg = seg_ref[...]
    kk = k * kk_ref[...]
    kk = kk / jnp.maximum(jnp.sqrt(_dot_exact_rhs(kk * kk, seg)), 1e-12)
    kmod = k * (1.0 + (a - 1.0) * ka_ref[...])
    cum = _dot_exact_lhs(tri_ref[...], lw)
    wc = jnp.exp(cum)
    iwc = jnp.exp(-cum)
    at = -kk * jnp.exp(cum - lw)
    bt = kk * a * iwc
    kt = kmod * iwc
    rt = r * wc

    n = WKV_GROUP * HEAD_DIM
    ri = lax.broadcasted_iota(i32, (n, n), 0)
    ci = lax.broadcasted_iota(i32, (n, n), 1)
    bd = (ri // WKV_CHUNK) == (ci // HEAD_DIM)
    strict = bd & ((ri % WKV_CHUNK) > (ci % WKV_CHUNK))
    incl = bd & ((ri % WKV_CHUNK) >= (ci % WKV_CHUNK))
    eye = jnp.where(ri == ci, 1.0, 0.0).astype(f32)
    masks = (bd, strict, incl, eye)

    n_groups = w // n
    states = [state_ref[gi] for gi in range(n_groups)]
    y_rows = []
    for c in range(tt // WKV_CHUNK):
        rs = slice(c * WKV_CHUNK, (c + 1) * WKV_CHUNK)
        last = (c + 1) * WKV_CHUNK - 1
        y_cols = []
        for gi in range(n_groups):
            cs = slice(gi * n, (gi + 1) * n)
            y_c, states[gi] = _wkv_chunk(at[rs, cs], bt[rs, cs], kt[rs, cs], rt[rs, cs], v[rs, cs],
                                         wc[last:last + 1, cs], states[gi], masks)
            y_cols.append(y_c)
        y_rows.append(jnp.concatenate(y_cols, axis=1))
    for gi in range(n_groups):
        state_ref[gi] = states[gi]
    y = jnp.concatenate(y_rows, axis=0)

    inv_n = 1.0 / HEAD_DIM
    mu = _dot_exact_rhs(y, seg) * inv_n
    d = y - mu
    var = _dot_exact_rhs(d * d, seg) * inv_n
    yn = d * lax.rsqrt(var + GN_EPS) * lng_ref[...] + lnb_ref[...]
    bonus = _dot_exact_rhs(r * kmod * rk_ref[...], seg) * v
    y_ref[...] = ((yn + bonus) * gate).astype(y_ref.dtype)


def _rwkv(u_r, mu_shift, w_decay_up, w0, a_up, a0, g_up, k_k, k_a, r_k, lnx_g, lnx_b, tt):
    b, s, cols = u_r.shape
    tt = min(tt, s)
    w = RWKV_WIDTH
    row = lambda p: p.reshape(1, -1).astype(f32)
    wdec = jnp.concatenate([w_decay_up, jnp.zeros((AAA_RANK, w), f32)], axis=0)
    aup = jnp.concatenate([jnp.zeros((DECAY_RANK, w), f32), a_up], axis=0)
    hid = jnp.arange(w) // HEAD_DIM
    seg = (hid[:, None] == hid[None, :]).astype(bf16)
    ti = jnp.arange(tt)
    tri = ((ti[:, None] // WKV_CHUNK == ti[None, :] // WKV_CHUNK) & (ti[:, None] >= ti[None, :])).astype(bf16)
    params = [row(mu_shift), wdec, row(w0), aup, row(a0), g_up, row(k_k), row(k_a), row(r_k), row(lnx_g),
              row(lnx_b), seg, tri]
    n = WKV_GROUP * HEAD_DIM
    return pl.pallas_call(
        functools.partial(_rwkv_body, tt=tt),
        out_shape=jax.ShapeDtypeStruct((b, s, w), bf16),
        grid=(b, s // tt),
        in_specs=[pl.BlockSpec((None, tt, cols), lambda bi, j: (bi, j, 0))] + [_const_spec(p.shape) for p in params],
        out_specs=pl.BlockSpec((None, tt, w), lambda bi, j: (bi, j, 0)),
        scratch_shapes=[pltpu.VMEM((w // n, n, n), f32), pltpu.VMEM((SUBLANES, cols), f32)],
        compiler_params=_cparams(("parallel", "arbitrary")),
        name="rwkv7",
    )(u_r, *params)


def _swa_body(sink_ref, u_ref, pos_ref, invf_ref, o_ref, kprev_ref, vprev_ref):
    j = pl.program_id(1)

    @pl.when(j == 0)
    def _():
        kprev_ref[...] = jnp.zeros_like(kprev_ref)
        vprev_ref[...] = jnp.zeros_like(vprev_ref)

    wq = SWA_WIDTH
    kvw = SWA_KV_WIDTH
    u = u_ref[...]
    ang = pos_ref[...].astype(f32) * invf_ref[...]
    cos = jnp.cos(ang)
    sin = jnp.sin(ang)

    def rope(x, c, s):
        n = x.shape[1]
        lane = lax.broadcasted_iota(i32, x.shape, 1)
        half = HEAD_DIM // 2
        rot = jnp.where((lane % HEAD_DIM) < half, -pltpu.roll(x, n - half, axis=1), pltpu.roll(x, half, axis=1))
        return x * c + rot * s

    q = rope(u[:, :wq] * (HEAD_DIM ** -0.5), jnp.concatenate([cos] * (wq // LANES), axis=1),
             jnp.concatenate([sin] * (wq // LANES), axis=1))
    k_cur = rope(u[:, wq:wq + kvw], cos, sin)
    v_cur = u[:, wq + kvw:]
    k_prev = kprev_ref[...]
    v_prev = vprev_ref[...]
    kprev_ref[...] = k_cur
    vprev_ref[...] = v_cur

    gw = SWA_GROUP * HEAD_DIM
    rows = SWA_GROUP * WINDOW
    lane_kv = lax.broadcasted_iota(i32, (WINDOW, kvw), 1)

    def rep(x, gi):
        sw = pltpu.roll(x, HEAD_DIM, axis=1)
        one = jnp.where((lane_kv // HEAD_DIM) == gi, x, sw)
        return jnp.concatenate([one] * (gw // kvw), axis=1).astype(bf16)

    ri = lax.broadcasted_iota(i32, (rows, WINDOW), 0)
    ci = lax.broadcasted_iota(i32, (rows, WINDOW), 1)
    t_idx = ri % WINDOW
    mask_prev = ci > t_idx + jnp.where(j > 0, 0, WINDOW)
    mask_cur = ci <= t_idx
    rb = lax.broadcasted_iota(i32, (rows, 1), 0) // WINDOW
    rbo = lax.broadcasted_iota(i32, (rows, gw), 0) // WINDOW
    cbo = lax.broadcasted_iota(i32, (rows, gw), 1) // HEAD_DIM
    bd = rbo == cbo
    outs = []
    for gi in range(SWA_KV_HEADS):
        qg = q[:, gi * gw:(gi + 1) * gw]
        q_bd = jnp.where(bd, jnp.concatenate([qg] * SWA_GROUP, axis=0), 0.0).astype(bf16)
        s_prev = jnp.where(mask_prev, _dot_nt(q_bd, rep(k_prev, gi)), NEG_INF)
        s_cur = jnp.where(mask_cur, _dot_nt(q_bd, rep(k_cur, gi)), NEG_INF)
        sink = jnp.zeros((rows, 1), f32)
        for h in range(SWA_GROUP):
            sink = jnp.where(rb == h, sink_ref[gi * SWA_GROUP + h], sink)
        m = jnp.maximum(jnp.maximum(jnp.max(s_prev, axis=-1, keepdims=True),
                                    jnp.max(s_cur, axis=-1, keepdims=True)), sink)
        p_prev = jnp.exp(s_prev - m)
        p_cur = jnp.exp(s_cur - m)
        denom = jnp.sum(p_prev, axis=-1, keepdims=True) + jnp.sum(p_cur, axis=-1, keepdims=True) + jnp.exp(sink - m)
        o_bd = _dot(p_prev.astype(bf16), rep(v_prev, gi)) + _dot(p_cur.astype(bf16), rep(v_cur, gi))
        o_bd = jnp.where(bd, o_bd / denom, 0.0)
        og = o_bd[0:WINDOW]
        for h in range(1, SWA_GROUP):
            og = og + o_bd[h * WINDOW:(h + 1) * WINDOW]
        outs.append(og)
    o_ref[...] = jnp.concatenate(outs, axis=1).astype(o_ref.dtype)


def _swa(u_s, positions, sinks):
    b, s, cols = u_s.shape
    half = HEAD_DIM // 2
    inv_freq = ROPE_THETA ** (-jnp.arange(0, HEAD_DIM, 2, dtype=f32) / HEAD_DIM)
    invf = jnp.tile(inv_freq, LANES // half).reshape(1, LANES)
    pos = positions.reshape(b, s, 1).astype(i32)
    return pl.pallas_call(
        _swa_body,
        out_shape=jax.ShapeDtypeStruct((b, s, SWA_WIDTH), bf16),
        grid=(b, s // WINDOW),
        in_specs=[pl.BlockSpec(memory_space=pltpu.SMEM),
                  pl.BlockSpec((None, WINDOW, cols), lambda bi, j: (bi, j, 0)),
                  pl.BlockSpec((None, WINDOW, 1), lambda bi, j: (bi, j, 0)),
                  _const_spec((1, LANES))],
        out_specs=pl.BlockSpec((None, WINDOW, SWA_WIDTH), lambda bi, j: (bi, j, 0)),
        scratch_shapes=[pltpu.VMEM((WINDOW, SWA_KV_WIDTH), f32), pltpu.VMEM((WINDOW, SWA_KV_WIDTH), f32)],
        compiler_params=_cparams(("parallel", "arbitrary")),
        name="swa",
    )(sinks.astype(f32), u_s, pos, invf)


def _mix_out_body(ya_ref, yb_ref, wa_ref, wb_ref, x_ref, g_ref, b_ref, o_ref, *, alpha):
    mix = _dot(ya_ref[...], wa_ref[...]) + _dot(yb_ref[...], wb_ref[...])
    o_ref[...] = _layer_norm(alpha * x_ref[...] + mix, g_ref[...], b_ref[...])


def _mix_out(ya, yb, w_o, x, g, b, alpha, tile):
    n, d = x.shape
    tile = min(tile, n)
    wa = w_o[:ya.shape[1]].astype(bf16)
    wb = w_o[ya.shape[1]:].astype(bf16)
    rows = lambda width: pl.BlockSpec((tile, width), lambda i: (i, 0))
    return pl.pallas_call(
        functools.partial(_mix_out_body, alpha=alpha),
        out_shape=jax.ShapeDtypeStruct((n, d), f32),
        grid=(n // tile,),
        in_specs=[rows(ya.shape[1]), rows(yb.shape[1]), _const_spec(wa.shape), _const_spec(wb.shape), rows(d),
                  _const_spec((1, d)), _const_spec((1, d))],
        out_specs=rows(d),
        compiler_params=_cparams(("parallel",)),
        name="mix_out_ln1",
    )(ya, yb, wa, wb, x, g.reshape(1, d), b.reshape(1, d))


def _xattn_body(x_ref, kv_ref, wq_ref, wo_ref, g_ref, b_ref, o_ref, o3_ref, *, alpha):
    x = x_ref[...]
    d = x.shape[1]
    hd = d // MEM_HEADS
    q = _dot(x.astype(bf16), wq_ref[...]) * (hd ** -0.5)
    kv = kv_ref[...]
    outs = []
    for h in range(MEM_HEADS):
        qh = q[:, h * hd:(h + 1) * hd].astype(bf16)
        kh = kv[:, h * hd:(h + 1) * hd]
        vh = kv[:, d + h * hd:d + (h + 1) * hd]
        s = _dot_nt(qh, kh)
        p = jnp.exp(s - jnp.max(s, axis=-1, keepdims=True))
        l = jnp.sum(p, axis=-1, keepdims=True)
        outs.append(_dot(p.astype(bf16), vh) / l)
    o = jnp.concatenate(outs, axis=1)
    xa = _dot(o.astype(bf16), wo_ref[...])
    y = _layer_norm(alpha * x + xa, g_ref[...], b_ref[...])
    o_ref[...] = y
    for s8 in range(d // LANES):
        o3_ref[:, s8, :] = y[:, s8 * LANES:(s8 + 1) * LANES]


def _xattn(x1, kv, wm_q, wm_o, g, b, alpha, tile):
    bsz, s, d = x1.shape
    m = kv.shape[1]
    tile = min(tile, s)
    nj = s // tile
    wq = wm_q.astype(bf16)
    wo = wm_o.astype(bf16)
    return pl.pallas_call(
        functools.partial(_xattn_body, alpha=alpha),
        out_shape=[jax.ShapeDtypeStruct((bsz, s, d), f32), jax.ShapeDtypeStruct((bsz * s, d // LANES, LANES), f32)],
        grid=(bsz, nj),
        in_specs=[pl.BlockSpec((None, tile, d), lambda bi, j: (bi, j, 0)),
                  pl.BlockSpec((None, m, 2 * d), lambda bi, j: (bi, 0, 0)),
                  _const_spec(wq.shape), _const_spec(wo.shape), _const_spec((1, d)), _const_spec((1, d))],
        out_specs=[pl.BlockSpec((None, tile, d), lambda bi, j: (bi, j, 0)),
                   pl.BlockSpec((tile, d // LANES, LANES), lambda bi, j: (bi * nj + j, 0, 0))],
        compiler_params=_cparams(("parallel", "parallel")),
        name="mem_xattn_ln2",
    )(x1, kv, wq, wo, g.reshape(1, d), b.reshape(1, d))


def _router_body(x_ref, wt_ref, bias_ref, upper_ref, e_ref, g_ref, r_ref, cnt_out_ref, cnt_ref, *, t):
    @pl.when(pl.program_id(0) == 0)
    def _():
        cnt_ref[...] = jnp.zeros_like(cnt_ref)

    xh, xl = _split2(x_ref[...])
    wh, wl = _split2(wt_ref[...])
    logits = _dot_nt(wh, xh) + _dot_nt(wh, xl) + _dot_nt(wl, xh)
    scores = _sigmoid(logits)
    biased = scores + bias_ref[...][:, 0:1]
    ne = N_EXPERTS
    neg = -jnp.inf

    def top1(vals):
        rows = lax.broadcasted_iota(i32, vals.shape, 0).astype(f32)
        m = jnp.max(vals, axis=0, keepdims=True)
        idx = jnp.min(jnp.where(vals == m, rows, float(vals.shape[0])), axis=0, keepdims=True)
        return m, idx, rows == idx

    gscores = []
    for gi in range(N_GROUPS):
        blk = biased[gi * GROUP_SIZE:(gi + 1) * GROUP_SIZE, :]
        m1, _, hit = top1(blk)
        m2 = jnp.max(jnp.where(hit, neg, blk), axis=0, keepdims=True)
        gscores.append(m1 + m2)
    gs = jnp.concatenate(gscores, axis=0)
    gsel = jnp.zeros(gs.shape, f32)
    for _ in range(TOPK_GROUPS):
        _, _, hit = top1(gs)
        gsel = jnp.where(hit, 1.0, gsel)
        gs = jnp.where(hit, neg, gs)
    emask = jnp.concatenate(
        [jnp.broadcast_to(gsel[gi:gi + 1, :], (GROUP_SIZE, t)) for gi in range(N_GROUPS)], axis=0) > 0.5
    cand = jnp.where(emask, biased, NEG_INF)
    idxs, sels = [], []
    chosen = jnp.zeros((ne, t), f32)
    for _ in range(TOP_K):
        _, idx, hit = top1(cand)
        idxs.append(idx)
        sels.append(jnp.sum(jnp.where(hit, scores, 0.0), axis=0, keepdims=True))
        chosen = chosen + jnp.where(hit, 1.0, 0.0)
        cand = jnp.where(hit, neg, cand)
    sel = jnp.concatenate(sels, axis=0)
    g_ref[...] = sel / jnp.sum(sel, axis=0, keepdims=True) * ROUTED_SCALE
    e_ref[...] = jnp.concatenate(idxs, axis=0).astype(i32)
    before = _dot(chosen.astype(bf16), upper_ref[...]) + cnt_ref[...][:, 0:1]
    rows = lax.broadcasted_iota(i32, (ne, t), 0).astype(f32)
    ranks = [jnp.sum(jnp.where(rows == idx, before, 0.0), axis=0, keepdims=True) for idx in idxs]
    r_ref[...] = jnp.concatenate(ranks, axis=0).astype(i32)
    cnt_ref[...] = cnt_ref[...] + jnp.sum(chosen, axis=1, keepdims=True)
    cnt_out_ref[...] = cnt_ref[...].astype(i32)


def _router(x2, w_router, router_bias, tile):
    n, d = x2.shape
    ne = N_EXPERTS
    t = min(tile, n)
    wt = w_router.T
    bias = jnp.broadcast_to(router_bias.reshape(ne, 1).astype(f32), (ne, LANES))
    ti = jnp.arange(t)
    upper = (ti[:, None] < ti[None, :]).astype(bf16)
    cols = pl.BlockSpec((TOP_K, t), lambda i: (0, i))
    e_t, g_t, r_t, cnt = pl.pallas_call(
        functools.partial(_router_body, t=t),
        out_shape=[jax.ShapeDtypeStruct((TOP_K, n), i32), jax.ShapeDtypeStruct((TOP_K, n), f32),
                   jax.ShapeDtypeStruct((TOP_K, n), i32), jax.ShapeDtypeStruct((ne, LANES), i32)],
        grid=(n // t,),
        in_specs=[pl.BlockSpec((t, d), lambda i: (i, 0)), _const_spec((ne, d)), _const_spec((ne, LANES)),
                  _const_spec((t, t))],
        out_specs=[cols, cols, cols, _const_spec((ne, LANES))],
        scratch_shapes=[pltpu.VMEM((ne, LANES), f32)],
        compiler_params=_cparams(("arbitrary",)),
        name="router",
    )(x2, wt, bias, upper)
    return e_t, g_t, r_t, cnt[:, 0]


def _dispatch_body(e_ref, r_ref, st_ref, x_ref, o_ref, sem, *, t):
    def issue(ti, c):
        for kk in range(TOP_K):
            slot = st_ref[e_ref[ti * TOP_K + kk]] + r_ref[ti * TOP_K + kk]
            pltpu.make_async_copy(x_ref.at[ti], o_ref.at[slot], sem).start()
        return c

    lax.fori_loop(0, t, issue, 0)
    for _ in range(TOP_K):
        pltpu.make_async_copy(x_ref, o_ref.at[pl.ds(0, t)], sem).wait()


def _dispatch(x3d, e_flat, r_flat, starts, tile):
    n = x3d.shape[0]
    t = min(tile, n)
    smem_rows = pl.BlockSpec((t * TOP_K,), lambda i: (i,), memory_space=pltpu.SMEM)
    return pl.pallas_call(
        functools.partial(_dispatch_body, t=t),
        out_shape=jax.ShapeDtypeStruct((n * TOP_K,) + x3d.shape[1:], x3d.dtype),
        grid=(n // t,),
        in_specs=[smem_rows, smem_rows, pl.BlockSpec(memory_space=pltpu.SMEM),
                  pl.BlockSpec((t,) + x3d.shape[1:], lambda i: (i, 0, 0))],
        out_specs=pl.BlockSpec(memory_space=pl.ANY),
        scratch_shapes=[pltpu.SemaphoreType.DMA],
        compiler_params=_cparams(("arbitrary",)),
        name="moe_dispatch",
    )(e_flat, r_flat, starts, x3d)


def _gmm_body(gid_ref, tid_ref, lo_ref, hi_ref, first_ref, x_ref, wg_ref, wu_ref, wd_ref, o_ref, *, tm):
    v = pl.program_id(0)
    lo = lo_ref[v]
    hi = hi_ref[v]

    @pl.when(first_ref[v] == 1)
    def _():
        o_ref[...] = jnp.zeros_like(o_ref)

    @pl.when(hi > lo)
    def _():
        rows = tid_ref[v] * tm + lax.broadcasted_iota(i32, (tm, 1), 0)
        mask = (rows >= lo) & (rows < hi)
        ns = x_ref.shape[1]
        kc = 2 * LANES
        hg = jnp.zeros((tm, wg_ref.shape[1]), f32)
        hu = jnp.zeros((tm, wu_ref.shape[1]), f32)
        for jj in range(ns // 2):
            xj = jnp.concatenate([x_ref[:, 2 * jj, :], x_ref[:, 2 * jj + 1, :]], axis=1).astype(bf16)
            hg = hg + _dot(xj, wg_ref[jj * kc:(jj + 1) * kc, :].astype(bf16))
            hu = hu + _dot(xj, wu_ref[jj * kc:(jj + 1) * kc, :].astype(bf16))
        h = hg * _sigmoid(hg) * hu
        y = _dot(h.astype(bf16), wd_ref[...].astype(bf16))
        for s8 in range(ns):
            o_ref[:, s8, :] = jnp.where(mask, y[:, s8 * LANES:(s8 + 1) * LANES], o_ref[:, s8, :])


def _gmm(xs, we_gate, we_up, we_down, layer, counts, tm):
    nk = xs.shape[0]
    _, ne, d, de = we_gate.shape
    tm = min(tm, nk)
    n_tiles = nk // tm
    n_visits = n_tiles + ne - 1
    ends = jnp.cumsum(counts)
    starts = ends - counts
    tile_lo = starts // tm
    n_touch = jnp.where(counts > 0, (ends - 1) // tm - tile_lo + 1, 0)
    vis_end = jnp.cumsum(n_touch)
    vis_start = vis_end - n_touch
    vi = jnp.arange(n_visits, dtype=i32)
    valid = vi < vis_end[-1]
    gid = jnp.minimum(jnp.searchsorted(vis_end, vi, side="right"), ne - 1).astype(i32)
    tid = jnp.where(valid, tile_lo[gid] + vi - vis_start[gid], n_tiles - 1).astype(i32)
    lo = jnp.where(valid, starts[gid], 0).astype(i32)
    hi = jnp.where(valid, ends[gid], 0).astype(i32)
    first = jnp.concatenate([jnp.ones((1,), i32), (tid[1:] != tid[:-1]).astype(i32)])
    wspec = lambda shape: pl.BlockSpec((None, None) + shape, lambda v, g, *_: (layer, g[v], 0, 0))
    rows = pl.BlockSpec((tm,) + xs.shape[1:], lambda v, g, t, *_: (t[v], 0, 0))
    return pl.pallas_call(
        functools.partial(_gmm_body, tm=tm),
        out_shape=jax.ShapeDtypeStruct(xs.shape, f32),
        grid_spec=pltpu.PrefetchScalarGridSpec(
            num_scalar_prefetch=5, grid=(n_visits,),
            in_specs=[rows, wspec((d, de)), wspec((d, de)), wspec((de, d))],
            out_specs=rows),
        compiler_params=_cparams(("arbitrary",)),
        name="moe_experts",
    )(gid, tid, lo, hi, first, xs, we_gate, we_up, we_down)


def _combine_body(e_ref, r_ref, g_ref, st_ref, y_ref, o_ref, buf, sem, *, t):
    def issue(ti, c):
        for kk in range(TOP_K):
            slot = st_ref[e_ref[ti * TOP_K + kk]] + r_ref[ti * TOP_K + kk]
            pltpu.make_async_copy(y_ref.at[slot], buf.at[kk, ti], sem).start()
        return c

    lax.fori_loop(0, t, issue, 0)
    for kk in range(TOP_K):
        pltpu.make_async_copy(y_ref.at[pl.ds(0, t)], buf.at[kk], sem).wait()

    def reduce(ti, c):
        acc = g_ref[ti * TOP_K] * buf[0, ti]
        for kk in range(1, TOP_K):
            acc = acc + g_ref[ti * TOP_K + kk] * buf[kk, ti]
        o_ref[ti] = acc
        return c

    lax.fori_loop(0, t, reduce, 0)


def _combine(ys, e_flat, r_flat, g_flat, starts, n, tile):
    t = min(tile, n)
    tail = ys.shape[1:]
    smem_rows = pl.BlockSpec((t * TOP_K,), lambda i: (i,), memory_space=pltpu.SMEM)
    return pl.pallas_call(
        functools.partial(_combine_body, t=t),
        out_shape=jax.ShapeDtypeStruct((n,) + tail, f32),
        grid=(n // t,),
        in_specs=[smem_rows, smem_rows, smem_rows, pl.BlockSpec(memory_space=pltpu.SMEM),
                  pl.BlockSpec(memory_space=pl.ANY)],
        out_specs=pl.BlockSpec((t,) + tail, lambda i: (i, 0, 0)),
        scratch_shapes=[pltpu.VMEM((TOP_K, t) + tail, f32), pltpu.SemaphoreType.DMA],
        compiler_params=_cparams(("arbitrary",)),
        name="moe_combine",
    )(e_flat, r_flat, g_flat, starts, ys)


def _ffn_out_body(x_ref, r_ref, wg_ref, wu_ref, wd_ref, g_ref, b_ref, o_ref, *, alpha):
    x = x_ref[...]
    xb = x.astype(bf16)
    hg = _dot(xb, wg_ref[...])
    h = hg * _sigmoid(hg) * _dot(xb, wu_ref[...])
    shared = _dot(h.astype(bf16), wd_ref[...])
    routed = jnp.concatenate([r_ref[:, s8, :] for s8 in range(r_ref.shape[1])], axis=1)
    o_ref[...] = _layer_norm(alpha * x + routed + shared, g_ref[...], b_ref[...])


def _ffn_out(x2, routed3d, ws_gate, ws_up, ws_down, g, b, alpha, tile):
    n, d = x2.shape
    tile = min(tile, n)
    wg, wu, wd = ws_gate.astype(bf16), ws_up.astype(bf16), ws_down.astype(bf16)
    rows = pl.BlockSpec((tile, d), lambda i: (i, 0))
    return pl.pallas_call(
        functools.partial(_ffn_out_body, alpha=alpha),
        out_shape=jax.ShapeDtypeStruct((n, d), f32),
        grid=(n // tile,),
        in_specs=[rows, pl.BlockSpec((tile,) + routed3d.shape[1:], lambda i: (i, 0, 0)), _const_spec(wg.shape),
                  _const_spec(wu.shape), _const_spec(wd.shape), _const_spec((1, d)), _const_spec((1, d))],
        out_specs=rows,
        compiler_params=_cparams(("parallel",)),
        name="ffn_out_ln3",
    )(x2, routed3d, wg, wu, wd, g.reshape(1, d), b.reshape(1, d))


def _layer(x, mem, positions, w_in, mu_shift, w_decay_up, w0, a_up, a0, g_up, k_k, k_a, r_k, lnx_g, lnx_b, sinks,
           w_o, ln1_g, ln1_b, wm_q, wm_kv, wm_o, ln2_g, ln2_b, w_router, router_bias, we_gate, we_up, we_down,
           ws_gate, ws_up, ws_down, ln3_g, ln3_b, *, layer, alpha):
    b, s, d = x.shape
    n = b * s
    xf = x.reshape(n, d)
    w_in_b = w_in.astype(bf16)
    u_r, u_s = _proj(xf, [w_in_b[:, :RWKV_COLS], w_in_b[:, RWKV_COLS:]], [f32, f32], tile=512)
    y_r = _rwkv(u_r.reshape(b, s, RWKV_COLS), mu_shift, w_decay_up, w0, a_up, a0, g_up, k_k, k_a, r_k, lnx_g, lnx_b,
                tt=256)
    y_s = _swa(u_s.reshape(b, s, SWA_COLS), positions, sinks)
    x1 = _mix_out(y_r.reshape(n, RWKV_WIDTH), y_s.reshape(n, SWA_WIDTH), w_o, xf, ln1_g, ln1_b, alpha, tile=512)
    m = mem.shape[1]
    (kv,) = _proj(mem.reshape(b * m, d), [wm_kv.astype(bf16)], [bf16], tile=512)
    x2, x2_rows = _xattn(x1.reshape(b, s, d), kv.reshape(b, m, 2 * d), wm_q, wm_o, ln2_g, ln2_b, alpha, tile=512)
    x2 = x2.reshape(n, d)
    e_t, g_t, r_t, counts = _router(x2, w_router, router_bias, tile=512)
    e_flat = e_t.T.reshape(-1)
    r_flat = r_t.T.reshape(-1)
    g_flat = g_t.T.reshape(-1)
    starts = (jnp.cumsum(counts) - counts).astype(i32)
    xs = _dispatch(x2_rows, e_flat, r_flat, starts, tile=512)
    ys = _gmm(xs, we_gate, we_up, we_down, layer, counts, tm=256)
    routed = _combine(ys, e_flat, r_flat, g_flat, starts, n, tile=256)
    x3 = _ffn_out(x2, routed, ws_gate, ws_up, ws_down, ln3_g, ln3_b, alpha, tile=512)
    return x3.reshape(b, s, d)


def kernel(x, mem, positions, w_in, mu_shift, w_decay_up, w0, a_up, a0, g_up, k_k, k_a, r_k, lnx_g, lnx_b, sinks, w_o, ln1_g, ln1_b, wm_q, wm_kv, wm_o, ln2_g, ln2_b, w_router, router_bias, we_gate, we_up, we_down, ws_gate, ws_up, ws_down, ln3_g, ln3_b):
    depth = w_in.shape[0]
    alpha = (2 * depth) ** 0.25
    for l in range(depth):
        x = _layer(x, mem, positions, w_in[l], mu_shift[l], w_decay_up[l], w0[l], a_up[l], a0[l], g_up[l], k_k[l],
                   k_a[l], r_k[l], lnx_g[l], lnx_b[l], sinks[l], w_o[l], ln1_g[l], ln1_b[l], wm_q[l], wm_kv[l],
                   wm_o[l], ln2_g[l], ln2_b[l], w_router[l], router_bias[l], we_gate, we_up, we_down,
                   ws_gate[l], ws_up[l], ws_down[l], ln3_g[l], ln3_b[l], layer=l, alpha=alpha)
    return x
```

```python
import functools

import jax
import jax.numpy as jnp
from jax import lax
from jax.experimental import pallas as pl
from jax.experimental.pallas import tpu as pltpu

f32 = jnp.float32
bf16 = jnp.bfloat16
i32 = jnp.int32

RWKV_HEADS = 8
HEAD_DIM = 64
RWKV_WIDTH = RWKV_HEADS * HEAD_DIM
DECAY_RANK = 64
AAA_RANK = 64
GATE_RANK = 128
RWKV_COLS = 3 * RWKV_WIDTH + DECAY_RANK + AAA_RANK + GATE_RANK
SWA_Q_HEADS = 8
SWA_KV_HEADS = 2
SWA_GROUP = SWA_Q_HEADS // SWA_KV_HEADS
SWA_WIDTH = SWA_Q_HEADS * HEAD_DIM
SWA_KV_WIDTH = SWA_KV_HEADS * HEAD_DIM
SWA_COLS = SWA_WIDTH + 2 * SWA_KV_WIDTH
WINDOW = 128
ROPE_THETA = 10000.0
MEM_HEADS = 4
N_EXPERTS = 256
TOP_K = 8
N_GROUPS = 8
GROUP_SIZE = N_EXPERTS // N_GROUPS
TOPK_GROUPS = 4
ROUTED_SCALE = 2.5
LN_EPS = 1e-5
GN_EPS = 64e-5
NEG_INF = -1e30

LANES = 128
SUBLANES = 8
WKV_CHUNK = 64
WKV_GROUP = 4
VMEM_LIMIT = 56 * 1024 * 1024


def _cparams(sem):
    return pltpu.CompilerParams(dimension_semantics=sem, vmem_limit_bytes=VMEM_LIMIT)


def _const_spec(shape):
    nd = len(shape)
    return pl.BlockSpec(shape, lambda *_: (0,) * nd)


def _dot(a, b):
    return jnp.dot(a, b, preferred_element_type=f32)


def _dot_nt(a, b):
    return lax.dot_general(a, b, (((1,), (1,)), ((), ())), preferred_element_type=f32)


def _dot_tn(a, b):
    return lax.dot_general(a, b, (((0,), (0,)), ((), ())), preferred_element_type=f32)


def _split2(x):
    hi = x.astype(bf16)
    lo = (x - hi.astype(f32)).astype(bf16)
    return hi, lo


def _seg_sums(xs, seg_b):
    parts = []
    for x in xs:
        parts.extend(_split2(x))
    out = _dot(jnp.concatenate(parts, axis=0), seg_b)
    t = xs[0].shape[0]
    return [out[2 * i * t:(2 * i + 1) * t] + out[(2 * i + 1) * t:(2 * i + 2) * t] for i in range(len(xs))]


def _dot_hp(a, b):
    ah, al = _split2(a)
    bh, bl = _split2(b)
    return _dot(ah, bh) + _dot(ah, bl) + _dot(al, bh)


def _dot_exact_lhs(m_bf16, x):
    hi, lo = _split2(x)
    return _dot(m_bf16, hi) + _dot(m_bf16, lo)


def _sigmoid(x):
    return 1.0 / (1.0 + jnp.exp(-x))


def _lane_chunk(n_rows, n_chunks, c):
    return (pl.ds(c, n_rows, stride=n_chunks), slice(None))


def _layer_norm(h, g, b):
    mu = jnp.mean(h, axis=-1, keepdims=True)
    d = h - mu
    var = jnp.mean(d * d, axis=-1, keepdims=True)
    return d * lax.rsqrt(var + LN_EPS) * g + b


def _proj_body(*refs, n_out):
    x_ref = refs[0]
    w_refs = refs[1:1 + n_out]
    o_refs = refs[1 + n_out:]
    xb = x_ref[...].astype(bf16)
    for w_ref, o_ref in zip(w_refs, o_refs):
        o_ref[...] = _dot(xb, w_ref[...]).astype(o_ref.dtype)


def _proj(x, ws, out_dtypes, tile):
    n, k = x.shape
    tile = min(tile, n)
    outs = pl.pallas_call(
        functools.partial(_proj_body, n_out=len(ws)),
        out_shape=[jax.ShapeDtypeStruct((n, w.shape[1]), dt) for w, dt in zip(ws, out_dtypes)],
        grid=(n // tile,),
        in_specs=[pl.BlockSpec((tile, k), lambda i: (i, 0))] + [_const_spec(w.shape) for w in ws],
        out_specs=[pl.BlockSpec((tile, w.shape[1]), lambda i: (i, 0)) for w in ws],
        compiler_params=_cparams(("parallel",)),
        name="proj",
    )(x, *ws)
    return outs


def _wkv_chunks(chains, states, masks):
    bd_b, bd, strict, incl, eye, eye_full = masks
    c, n = chains[0][1].shape
    nch = len(chains)

    def stack(x_b):
        return jnp.where(bd_b, jnp.concatenate([x_b] * WKV_GROUP, axis=0), jnp.zeros((), bf16))

    cast = [tuple(x.astype(bf16) for x in ch[1:6]) for ch in chains]
    v_s = [stack(cb[4]) for cb in cast]
    g = [_dot_nt(jnp.concatenate([cb[0], cb[3]], axis=0), jnp.concatenate([stack(cb[1]), stack(cb[2])], axis=0))
         for cb in cast]
    l_ak = [jnp.where(strict, gi[:c, n:], 0.0).astype(bf16) for gi in g]
    m_rb = [jnp.where(incl, gi[c:, :n], 0.0).astype(bf16) for gi in g]
    m_rk = [jnp.where(incl, gi[c:, n:], 0.0).astype(bf16) for gi in g]
    x = [jnp.where(strict, gi[:c, :n], 0.0) for gi in g]
    t = [eye + xi for xi in x]
    for _ in range(5):
        xb = [xi.astype(bf16) for xi in x]
        x = [_dot(xi, stack(xi)) for xi in xb]
        t = [ti + _dot(ti.astype(bf16), stack(xi.astype(bf16))) for ti, xi in zip(t, x)]
    lakv = [_dot(l_ak[i], v_s[i]).astype(bf16) for i in range(nch)]
    au = [_dot(t[i].astype(bf16), jnp.concatenate([stack(cast[i][0]), stack(lakv[i])], axis=1))
          for i in range(nch)]
    abar = [a[:, :n].astype(bf16) for a in au]
    ubar = [a[:, n:].astype(bf16) for a in au]
    ry = [_dot(m_rb[i], jnp.concatenate([stack(abar[i]), stack(ubar[i])], axis=1)) for i in range(nch)]
    r_bar = [(chains[i][4] + ry[i][:, :n]).astype(bf16) for i in range(nch)]
    y_bar = [ry[i][:, n:] + _dot(m_rk[i], v_s[i]) for i in range(nch)]
    p = [((eye_full + jnp.where(bd, _dot_tn(abar[i], cast[i][1]), 0.0)) * chains[i][6]).astype(bf16)
         for i in range(nch)]
    q = []
    for i in range(nch):
        q_bd = jnp.where(bd, _dot_tn(jnp.concatenate([ubar[i], cast[i][4]], axis=0),
                                     jnp.concatenate([cast[i][1], cast[i][2]], axis=0)), 0.0)
        qi = q_bd[0:c]
        for h in range(1, WKV_GROUP):
            qi = qi + q_bd[h * c:(h + 1) * c]
        q.append(qi * chains[i][6])
    states = list(states)
    ys = []
    for i in range(nch):
        gi = chains[i][0]
        s_b = states[gi].astype(bf16)
        ys.append(_dot_nt(r_bar[i], stack(s_b)) + y_bar[i])
        states[gi] = _dot(s_b, p[i]) + q[i]
    return ys, states


def _rwkv_body(u_ref, mu_ref, wdec_ref, w0_ref, aup_ref, a0_ref, gup_ref, kk_ref, ka_ref, rk_ref,
               lng_ref, lnb_ref, seg_ref, tri_ref, y_ref, state_ref, carry_ref, *, tt):
    j = pl.program_id(1)

    @pl.when(j == 0)
    def _():
        state_ref[...] = jnp.zeros_like(state_ref)
        carry_ref[...] = jnp.zeros_like(carry_ref)

    w = RWKV_WIDTH
    u = u_ref[...]
    row = lax.broadcasted_iota(i32, u.shape, 0)
    prev = jnp.where(row == 0, carry_ref[0:1, :], pltpu.roll(u, 1, axis=0))
    carry_ref[0:1, :] = u[tt - 1:tt, :]
    us = u + (prev - u) * mu_ref[...]
    r = us[:, 0:w]
    k = us[:, w:2 * w]
    v = us[:, 2 * w:3 * w]
    wa = us[:, 3 * w:3 * w + DECAY_RANK + AAA_RANK]
    gd = us[:, 3 * w + DECAY_RANK + AAA_RANK:]
    z = w0_ref[...] + _dot_hp(jnp.tanh(wa), wdec_ref[...])
    softplus_neg_z = jnp.maximum(-z, 0.0) + jnp.log(1.0 + jnp.exp(-jnp.abs(z)))
    lw = -jnp.exp(-softplus_neg_z - 0.5)
    a = _sigmoid(a0_ref[...] + _dot_hp(wa, aup_ref[...]))
    gate = _dot(_sigmoid(gd).astype(bf16), gup_ref[...].astype(bf16))
    seg = seg_ref[...]
    kk = k * kk_ref[...]
    kmod = k * (1.0 + (a - 1.0) * ka_ref[...])
    kk_sq, bonus_dot = _seg_sums([kk * kk, r * kmod * rk_ref[...]], seg)
    kk = kk / jnp.maximum(jnp.sqrt(kk_sq), 1e-12)
    cum = _dot_exact_lhs(tri_ref[...], lw)
    wc = jnp.exp(cum)
    iwc = jnp.exp(-cum)
    at = -kk * jnp.exp(cum - lw)
    bt = kk * a * iwc
    kt = kmod * iwc
    rt = r * wc

    n = WKV_GROUP * HEAD_DIM
    ri = lax.broadcasted_iota(i32, (n, n), 0)
    ci = lax.broadcasted_iota(i32, (n, n), 1)
    bd = (ri // WKV_CHUNK) == (ci // HEAD_DIM)
    bd_b = jnp.where(bd, 1.0, 0.0).astype(bf16) > 0
    eye_full = jnp.where(ri == ci, 1.0, 0.0).astype(f32)
    ti = lax.broadcasted_iota(i32, (WKV_CHUNK, n), 0)
    si = lax.broadcasted_iota(i32, (WKV_CHUNK, n), 1) % WKV_CHUNK
    masks = (bd_b, bd, ti > si, ti >= si, jnp.where(ti == si, 1.0, 0.0).astype(f32), eye_full)

    n_groups = w // n
    n_chunks = tt // WKV_CHUNK
    chains = []
    for c in range(n_chunks):
        rs = slice(c * WKV_CHUNK, (c + 1) * WKV_CHUNK)
        last = (c + 1) * WKV_CHUNK - 1
        for gi in range(n_groups):
            cs = slice(gi * n, (gi + 1) * n)
            chains.append((gi, at[rs, cs], bt[rs, cs], kt[rs, cs], rt[rs, cs], v[rs, cs], wc[last:last + 1, cs]))
    ys, states = _wkv_chunks(chains, [state_ref[gi] for gi in range(n_groups)], masks)
    for gi in range(n_groups):
        state_ref[gi] = states[gi]
    y = jnp.concatenate([jnp.concatenate(ys[c * n_groups:(c + 1) * n_groups], axis=1) for c in range(n_chunks)],
                        axis=0)

    inv_n = 1.0 / HEAD_DIM
    d = y - _seg_sums([y], seg)[0] * inv_n
    var = _seg_sums([d * d], seg)[0] * inv_n
    yn = d * lax.rsqrt(var + GN_EPS) * lng_ref[...] + lnb_ref[...]
    y_ref[...] = ((yn + bonus_dot * v) * gate).astype(y_ref.dtype)


def _rwkv(u_r, mu_shift, w_decay_up, w0, a_up, a0, g_up, k_k, k_a, r_k, lnx_g, lnx_b, tt):
    b, s, cols = u_r.shape
    tt = min(tt, s)
    w = RWKV_WIDTH
    row = lambda p: p.reshape(1, -1).astype(f32)
    wdec = jnp.concatenate([w_decay_up, jnp.zeros((AAA_RANK, w), f32)], axis=0)
    aup = jnp.concatenate([jnp.zeros((DECAY_RANK, w), f32), a_up], axis=0)
    hid = jnp.arange(w) // HEAD_DIM
    seg = (hid[:, None] == hid[None, :]).astype(bf16)
    ti = jnp.arange(tt)
    tri = ((ti[:, None] // WKV_CHUNK == ti[None, :] // WKV_CHUNK) & (ti[:, None] >= ti[None, :])).astype(bf16)
    params = [row(mu_shift), wdec, row(w0), aup, row(a0), g_up, row(k_k), row(k_a), row(r_k), row(lnx_g),
              row(lnx_b), seg, tri]
    n = WKV_GROUP * HEAD_DIM
    return pl.pallas_call(
        functools.partial(_rwkv_body, tt=tt),
        out_shape=jax.ShapeDtypeStruct((b, s, w), bf16),
        grid=(b, s // tt),
        in_specs=[pl.BlockSpec((None, tt, cols), lambda bi, j: (bi, j, 0))] + [_const_spec(p.shape) for p in params],
        out_specs=pl.BlockSpec((None, tt, w), lambda bi, j: (bi, j, 0)),
        scratch_shapes=[pltpu.VMEM((w // n, HEAD_DIM, n), f32), pltpu.VMEM((SUBLANES, cols), f32)],
        compiler_params=_cparams(("parallel", "arbitrary")),
        name="rwkv7",
    )(u_r, *params)


def _swa_body(sink_ref, u_ref, pos_ref, invf_ref, o_ref, kprev_ref, vprev_ref):
    j = pl.program_id(1)

    @pl.when(j == 0)
    def _():
        kprev_ref[...] = jnp.zeros_like(kprev_ref)
        vprev_ref[...] = jnp.zeros_like(vprev_ref)

    wq = SWA_WIDTH
    kvw = SWA_KV_WIDTH
    u = u_ref[...]
    ang = pos_ref[...].astype(f32) * invf_ref[...]
    cos = jnp.cos(ang)
    sin = jnp.sin(ang)

    def rope(x, c, s):
        n = x.shape[1]
        lane = lax.broadcasted_iota(i32, x.shape, 1)
        half = HEAD_DIM // 2
        rot = jnp.where((lane % HEAD_DIM) < half, -pltpu.roll(x, n - half, axis=1), pltpu.roll(x, half, axis=1))
        return x * c + rot * s

    q = rope(u[:, :wq] * (HEAD_DIM ** -0.5), jnp.concatenate([cos] * (wq // LANES), axis=1),
             jnp.concatenate([sin] * (wq // LANES), axis=1))
    k_cur = rope(u[:, wq:wq + kvw], cos, sin)
    v_cur = u[:, wq + kvw:]
    k_prev = kprev_ref[...]
    v_prev = vprev_ref[...]
    kprev_ref[...] = k_cur
    vprev_ref[...] = v_cur

    gw = SWA_GROUP * HEAD_DIM
    rows = SWA_GROUP * WINDOW
    lane_kv = lax.broadcasted_iota(i32, (WINDOW, kvw), 1)

    def rep(x, gi):
        sw = pltpu.roll(x, HEAD_DIM, axis=1)
        one = jnp.where((lane_kv // HEAD_DIM) == gi, x, sw)
        return jnp.concatenate([one] * (gw // kvw), axis=1).astype(bf16)

    ri = lax.broadcasted_iota(i32, (rows, WINDOW), 0)
    ci = lax.broadcasted_iota(i32, (rows, WINDOW), 1)
    t_idx = ri % WINDOW
    mask_prev = ci > t_idx + jnp.where(j > 0, 0, WINDOW)
    mask_cur = ci <= t_idx
    rb = lax.broadcasted_iota(i32, (rows, 1), 0) // WINDOW
    rbo = lax.broadcasted_iota(i32, (rows, gw), 0) // WINDOW
    cbo = lax.broadcasted_iota(i32, (rows, gw), 1) // HEAD_DIM
    bd = rbo == cbo
    outs = []
    for gi in range(SWA_KV_HEADS):
        qg = q[:, gi * gw:(gi + 1) * gw]
        q_bd = jnp.where(bd, jnp.concatenate([qg] * SWA_GROUP, axis=0), 0.0).astype(bf16)
        s_prev = jnp.where(mask_prev, _dot_nt(q_bd, rep(k_prev, gi)), NEG_INF)
        s_cur = jnp.where(mask_cur, _dot_nt(q_bd, rep(k_cur, gi)), NEG_INF)
        sink = jnp.zeros((rows, 1), f32)
        for h in range(SWA_GROUP):
            sink = jnp.where(rb == h, sink_ref[gi * SWA_GROUP + h], sink)
        m = jnp.maximum(jnp.maximum(jnp.max(s_prev, axis=-1, keepdims=True),
                                    jnp.max(s_cur, axis=-1, keepdims=True)), sink)
        p_prev = jnp.exp(s_prev - m)
        p_cur = jnp.exp(s_cur - m)
        denom = jnp.sum(p_prev, axis=-1, keepdims=True) + jnp.sum(p_cur, axis=-1, keepdims=True) + jnp.exp(sink - m)
        o_bd = _dot(p_prev.astype(bf16), rep(v_prev, gi)) + _dot(p_cur.astype(bf16), rep(v_cur, gi))
        o_bd = jnp.where(bd, o_bd / denom, 0.0)
        og = o_bd[0:WINDOW]
        for h in range(1, SWA_GROUP):
            og = og + o_bd[h * WINDOW:(h + 1) * WINDOW]
        outs.append(og)
    o_ref[...] = jnp.concatenate(outs, axis=1).astype(o_ref.dtype)


def _swa(u_s, positions, sinks):
    b, s, cols = u_s.shape
    half = HEAD_DIM // 2
    inv_freq = ROPE_THETA ** (-jnp.arange(0, HEAD_DIM, 2, dtype=f32) / HEAD_DIM)
    invf = jnp.tile(inv_freq, LANES // half).reshape(1, LANES)
    pos = positions.reshape(b, s, 1).astype(i32)
    return pl.pallas_call(
        _swa_body,
        out_shape=jax.ShapeDtypeStruct((b, s, SWA_WIDTH), bf16),
        grid=(b, s // WINDOW),
        in_specs=[pl.BlockSpec(memory_space=pltpu.SMEM),
                  pl.BlockSpec((None, WINDOW, cols), lambda bi, j: (bi, j, 0)),
                  pl.BlockSpec((None, WINDOW, 1), lambda bi, j: (bi, j, 0)),
                  _const_spec((1, LANES))],
        out_specs=pl.BlockSpec((None, WINDOW, SWA_WIDTH), lambda bi, j: (bi, j, 0)),
        scratch_shapes=[pltpu.VMEM((WINDOW, SWA_KV_WIDTH), f32), pltpu.VMEM((WINDOW, SWA_KV_WIDTH), f32)],
        compiler_params=_cparams(("parallel", "arbitrary")),
        name="swa",
    )(sinks.astype(f32), u_s, pos, invf)


def _mix_out_body(ya_ref, yb_ref, wa_ref, wb_ref, x_ref, g_ref, b_ref, o_ref, *, alpha):
    mix = _dot(ya_ref[...], wa_ref[...]) + _dot(yb_ref[...], wb_ref[...])
    o_ref[...] = _layer_norm(alpha * x_ref[...] + mix, g_ref[...], b_ref[...])


def _mix_out(ya, yb, w_o, x, g, b, alpha, tile):
    n, d = x.shape
    tile = min(tile, n)
    wa = w_o[:ya.shape[1]].astype(bf16)
    wb = w_o[ya.shape[1]:].astype(bf16)
    rows = lambda width: pl.BlockSpec((tile, width), lambda i: (i, 0))
    return pl.pallas_call(
        functools.partial(_mix_out_body, alpha=alpha),
        out_shape=jax.ShapeDtypeStruct((n, d), f32),
        grid=(n // tile,),
        in_specs=[rows(ya.shape[1]), rows(yb.shape[1]), _const_spec(wa.shape), _const_spec(wb.shape), rows(d),
                  _const_spec((1, d)), _const_spec((1, d))],
        out_specs=rows(d),
        compiler_params=_cparams(("parallel",)),
        name="mix_out_ln1",
    )(ya, yb, wa, wb, x, g.reshape(1, d), b.reshape(1, d))


def _xattn_body(x_ref, kv_ref, wq_ref, wo_ref, g_ref, b_ref, o_ref, o3_ref, *, alpha):
    x = x_ref[...]
    d = x.shape[1]
    hd = d // MEM_HEADS
    q = _dot(x.astype(bf16), wq_ref[...]) * (hd ** -0.5)
    kv = kv_ref[...]
    outs = []
    for h in range(MEM_HEADS):
        qh = q[:, h * hd:(h + 1) * hd].astype(bf16)
        kh = kv[:, h * hd:(h + 1) * hd]
        vh = kv[:, d + h * hd:d + (h + 1) * hd]
        s = _dot_nt(qh, kh)
        p = jnp.exp(s - jnp.max(s, axis=-1, keepdims=True))
        l = jnp.sum(p, axis=-1, keepdims=True)
        outs.append(_dot(p.astype(bf16), vh) / l)
    o = jnp.concatenate(outs, axis=1)
    xa = _dot(o.astype(bf16), wo_ref[...])
    y = _layer_norm(alpha * x + xa, g_ref[...], b_ref[...])
    o_ref[...] = y
    nc = d // LANES
    for c in range(nc):
        o3_ref[_lane_chunk(x.shape[0], nc, c)] = y[:, c * LANES:(c + 1) * LANES]


def _xattn(x1, kv, wm_q, wm_o, g, b, alpha, tile):
    bsz, s, d = x1.shape
    m = kv.shape[1]
    tile = min(tile, s)
    nj = s // tile
    wq = wm_q.astype(bf16)
    wo = wm_o.astype(bf16)
    return pl.pallas_call(
        functools.partial(_xattn_body, alpha=alpha),
        out_shape=[jax.ShapeDtypeStruct((bsz, s, d), f32), jax.ShapeDtypeStruct((bsz * s * (d // LANES), LANES), f32)],
        grid=(bsz, nj),
        in_specs=[pl.BlockSpec((None, tile, d), lambda bi, j: (bi, j, 0)),
                  pl.BlockSpec((None, m, 2 * d), lambda bi, j: (bi, 0, 0)),
                  _const_spec(wq.shape), _const_spec(wo.shape), _const_spec((1, d)), _const_spec((1, d))],
        out_specs=[pl.BlockSpec((None, tile, d), lambda bi, j: (bi, j, 0)),
                   pl.BlockSpec((tile * (d // LANES), LANES), lambda bi, j: (bi * nj + j, 0))],
        compiler_params=_cparams(("parallel", "parallel")),
        name="mem_xattn_ln2",
    )(x1, kv, wq, wo, g.reshape(1, d), b.reshape(1, d))


def _router_body(x_ref, wt_ref, bias_ref, upper_ref, e_ref, g_ref, r_ref, cnt_out_ref, cnt_ref, *, t):
    @pl.when(pl.program_id(0) == 0)
    def _():
        cnt_ref[...] = jnp.zeros_like(cnt_ref)

    xh, xl = _split2(x_ref[...])
    wh, wl = _split2(wt_ref[...])
    logits = _dot_nt(wh, xh) + _dot_nt(wh, xl) + _dot_nt(wl, xh)
    scores = _sigmoid(logits)
    biased = scores + bias_ref[...][:, 0:1]
    ne = N_EXPERTS
    neg = -jnp.inf

    def top1(vals):
        rows = lax.broadcasted_iota(i32, vals.shape, 0).astype(f32)
        m = jnp.max(vals, axis=0, keepdims=True)
        idx = jnp.min(jnp.where(vals == m, rows, float(vals.shape[0])), axis=0, keepdims=True)
        return m, idx, rows == idx

    gscores = []
    for gi in range(N_GROUPS):
        blk = biased[gi * GROUP_SIZE:(gi + 1) * GROUP_SIZE, :]
        m1, _, hit = top1(blk)
        m2 = jnp.max(jnp.where(hit, neg, blk), axis=0, keepdims=True)
        gscores.append(m1 + m2)
    gs = jnp.concatenate(gscores, axis=0)
    gsel = jnp.zeros(gs.shape, f32)
    for _ in range(TOPK_GROUPS):
        _, _, hit = top1(gs)
        gsel = jnp.where(hit, 1.0, gsel)
        gs = jnp.where(hit, neg, gs)
    emask = jnp.concatenate(
        [jnp.broadcast_to(gsel[gi:gi + 1, :], (GROUP_SIZE, t)) for gi in range(N_GROUPS)], axis=0) > 0.5
    cand = jnp.where(emask, biased, NEG_INF)
    idxs, sels = [], []
    chosen = jnp.zeros((ne, t), f32)
    for _ in range(TOP_K):
        _, idx, hit = top1(cand)
        idxs.append(idx)
        sels.append(jnp.sum(jnp.where(hit, scores, 0.0), axis=0, keepdims=True))
        chosen = chosen + jnp.where(hit, 1.0, 0.0)
        cand = jnp.where(hit, neg, cand)
    sel = jnp.concatenate(sels, axis=0)
    g_ref[...] = sel / jnp.sum(sel, axis=0, keepdims=True) * ROUTED_SCALE
    e_ref[...] = jnp.concatenate(idxs, axis=0).astype(i32)
    before = _dot(chosen.astype(bf16), upper_ref[...]) + cnt_ref[...][:, 0:1]
    rows = lax.broadcasted_iota(i32, (ne, t), 0).astype(f32)
    ranks = [jnp.sum(jnp.where(rows == idx, before, 0.0), axis=0, keepdims=True) for idx in idxs]
    r_ref[...] = jnp.concatenate(ranks, axis=0).astype(i32)
    cnt_ref[...] = cnt_ref[...] + jnp.sum(chosen, axis=1, keepdims=True)
    cnt_out_ref[...] = cnt_ref[...].astype(i32)


def _router(x2, w_router, router_bias, tile):
    n, d = x2.shape
    ne = N_EXPERTS
    t = min(tile, n)
    wt = w_router.T
    bias = jnp.broadcast_to(router_bias.reshape(ne, 1).astype(f32), (ne, LANES))
    ti = jnp.arange(t)
    upper = (ti[:, None] < ti[None, :]).astype(bf16)
    cols = pl.BlockSpec((TOP_K, t), lambda i: (0, i))
    e_t, g_t, r_t, cnt = pl.pallas_call(
        functools.partial(_router_body, t=t),
        out_shape=[jax.ShapeDtypeStruct((TOP_K, n), i32), jax.ShapeDtypeStruct((TOP_K, n), f32),
                   jax.ShapeDtypeStruct((TOP_K, n), i32), jax.ShapeDtypeStruct((ne, LANES), i32)],
        grid=(n // t,),
        in_specs=[pl.BlockSpec((t, d), lambda i: (i, 0)), _const_spec((ne, d)), _const_spec((ne, LANES)),
                  _const_spec((t, t))],
        out_specs=[cols, cols, cols, _const_spec((ne, LANES))],
        scratch_shapes=[pltpu.VMEM((ne, LANES), f32)],
        compiler_params=_cparams(("arbitrary",)),
        name="router",
    )(x2, wt, bias, upper)
    return e_t, g_t, r_t, cnt[:, 0]


def _dispatch_body(e_ref, r_ref, st_ref, x_ref, o_ref, sem, *, t):
    def issue(ti, c):
        for kk in range(TOP_K):
            slot = st_ref[e_ref[kk, ti]] + r_ref[kk, ti]
            pltpu.make_async_copy(x_ref.at[ti], o_ref.at[slot], sem).start()
        return c

    lax.fori_loop(0, t, issue, 0)
    for _ in range(TOP_K):
        pltpu.make_async_copy(x_ref, o_ref.at[pl.ds(0, t)], sem).wait()


def _dispatch(x3d, e_t, r_t, starts, tile):
    n = x3d.shape[0]
    t = min(tile, n)
    smem_rows = pl.BlockSpec((TOP_K, t), lambda i: (0, i), memory_space=pltpu.SMEM)
    return pl.pallas_call(
        functools.partial(_dispatch_body, t=t),
        out_shape=jax.ShapeDtypeStruct((n * TOP_K,) + x3d.shape[1:], x3d.dtype),
        grid=(n // t,),
        in_specs=[smem_rows, smem_rows, pl.BlockSpec(memory_space=pltpu.SMEM),
                  pl.BlockSpec((t,) + x3d.shape[1:], lambda i: (i, 0, 0))],
        out_specs=pl.BlockSpec(memory_space=pl.ANY),
        scratch_shapes=[pltpu.SemaphoreType.DMA],
        compiler_params=_cparams(("arbitrary",)),
        name="moe_dispatch",
    )(e_t, r_t, starts, x3d)


def _gmm_body(gid_ref, tid_ref, lo_ref, hi_ref, first_ref, newg_ref, x_ref, wg_ref, wu_ref, wd_ref, o_ref,
              wg_b, wu_b, wd_b, *, tm):
    v = pl.program_id(0)
    lo = lo_ref[v]
    hi = hi_ref[v]
    row0 = tid_ref[v] * tm
    nc = wg_ref.shape[0] // LANES
    full = (lo <= row0) & (hi >= row0 + tm)

    @pl.when(newg_ref[v] == 1)
    def _():
        wg_b[...] = wg_ref[...].astype(bf16)
        wu_b[...] = wu_ref[...].astype(bf16)
        wd_b[...] = wd_ref[...].astype(bf16)

    def ffn():
        x = jnp.concatenate([x_ref[_lane_chunk(tm, nc, c)] for c in range(nc)], axis=1).astype(bf16)
        hg = _dot(x, wg_b[...])
        h = hg * _sigmoid(hg) * _dot(x, wu_b[...])
        return _dot(h.astype(bf16), wd_b[...])

    @pl.when(full)
    def _():
        y = ffn()
        for c in range(nc):
            o_ref[_lane_chunk(tm, nc, c)] = y[:, c * LANES:(c + 1) * LANES]

    @pl.when(jnp.logical_not(full) & (hi > lo))
    def _():
        @pl.when(first_ref[v] == 1)
        def _():
            o_ref[...] = jnp.zeros_like(o_ref)

        rows = row0 + lax.broadcasted_iota(i32, (tm, 1), 0)
        mask = (rows >= lo) & (rows < hi)
        y = ffn()
        for c in range(nc):
            idx = _lane_chunk(tm, nc, c)
            o_ref[idx] = jnp.where(mask, y[:, c * LANES:(c + 1) * LANES], o_ref[idx])


def _gmm(xs, we_gate, we_up, we_down, layer, counts, tm):
    _, ne, d, de = we_gate.shape
    nc = d // LANES
    nk = xs.shape[0] // nc
    tm = min(tm, nk)
    n_tiles = nk // tm
    n_visits = n_tiles + ne - 1
    ends = jnp.cumsum(counts)
    starts = ends - counts
    tile_lo = starts // tm
    n_touch = jnp.where(counts > 0, (ends - 1) // tm - tile_lo + 1, 0)
    vis_end = jnp.cumsum(n_touch)
    vis_start = vis_end - n_touch
    vi = jnp.arange(n_visits, dtype=i32)
    valid = vi < vis_end[-1]
    gid = jnp.minimum(jnp.sum((vis_end[None, :] <= vi[:, None]).astype(i32), axis=1), ne - 1)
    onehot = gid[:, None] == jnp.arange(ne, dtype=i32)[None, :]
    pick = lambda table: jnp.sum(jnp.where(onehot, table[None, :], 0), axis=1)
    tid = jnp.where(valid, pick(tile_lo) + vi - pick(vis_start), n_tiles - 1).astype(i32)
    lo = jnp.where(valid, pick(starts), 0).astype(i32)
    hi = jnp.where(valid, pick(ends), 0).astype(i32)
    one = jnp.ones((1,), i32)
    first = jnp.concatenate([one, (tid[1:] != tid[:-1]).astype(i32)])
    newg = jnp.concatenate([one, (gid[1:] != gid[:-1]).astype(i32)])
    wspec = lambda shape: pl.BlockSpec((None, None) + shape, lambda v, g, *_: (layer, g[v], 0, 0))
    rows = pl.BlockSpec((tm * nc, LANES), lambda v, g, t, *_: (t[v], 0))
    return pl.pallas_call(
        functools.partial(_gmm_body, tm=tm),
        out_shape=jax.ShapeDtypeStruct(xs.shape, f32),
        grid_spec=pltpu.PrefetchScalarGridSpec(
            num_scalar_prefetch=6, grid=(n_visits,),
            in_specs=[rows, wspec((d, de)), wspec((d, de)), wspec((de, d))],
            out_specs=rows,
            scratch_shapes=[pltpu.VMEM((d, de), bf16), pltpu.VMEM((d, de), bf16), pltpu.VMEM((de, d), bf16)]),
        compiler_params=_cparams(("arbitrary",)),
        name="moe_experts",
    )(gid, tid, lo, hi, first, newg, xs, we_gate, we_up, we_down)


def _combine_body(e_ref, r_ref, g_ref, st_ref, y_ref, o_ref, buf, sem, *, t):
    def issue(ti, c):
        for kk in range(TOP_K):
            slot = st_ref[e_ref[kk, ti]] + r_ref[kk, ti]
            pltpu.make_async_copy(y_ref.at[slot], buf.at[kk, ti], sem).start()
        return c

    lax.fori_loop(0, t, issue, 0)
    for kk in range(TOP_K):
        pltpu.make_async_copy(y_ref.at[pl.ds(0, t)], buf.at[kk], sem).wait()

    def reduce(ti, c):
        acc = g_ref[0, ti] * buf[0, ti]
        for kk in range(1, TOP_K):
            acc = acc + g_ref[kk, ti] * buf[kk, ti]
        o_ref[ti] = acc
        return c

    lax.fori_loop(0, t, reduce, 0)


def _combine(ys, e_t, r_t, g_t, starts, n, tile):
    t = min(tile, n)
    tail = ys.shape[1:]
    smem_rows = pl.BlockSpec((TOP_K, t), lambda i: (0, i), memory_space=pltpu.SMEM)
    return pl.pallas_call(
        functools.partial(_combine_body, t=t),
        out_shape=jax.ShapeDtypeStruct((n,) + tail, f32),
        grid=(n // t,),
        in_specs=[smem_rows, smem_rows, smem_rows, pl.BlockSpec(memory_space=pltpu.SMEM),
                  pl.BlockSpec(memory_space=pl.ANY)],
        out_specs=pl.BlockSpec((t,) + tail, lambda i: (i, 0, 0)),
        scratch_shapes=[pltpu.VMEM((TOP_K, t) + tail, f32), pltpu.SemaphoreType.DMA],
        compiler_params=_cparams(("arbitrary",)),
        name="moe_combine",
    )(e_t, r_t, g_t, starts, ys)


def _ffn_out_body(x_ref, r_ref, wg_ref, wu_ref, wd_ref, g_ref, b_ref, o_ref, *, alpha):
    x = x_ref[...]
    xb = x.astype(bf16)
    hg = _dot(xb, wg_ref[...])
    h = hg * _sigmoid(hg) * _dot(xb, wu_ref[...])
    shared = _dot(h.astype(bf16), wd_ref[...])
    nc = x.shape[1] // LANES
    routed = jnp.concatenate([r_ref[_lane_chunk(x.shape[0], nc, c)] for c in range(nc)], axis=1)
    o_ref[...] = _layer_norm(alpha * x + routed + shared, g_ref[...], b_ref[...])


def _ffn_out(x2, routed_rows, ws_gate, ws_up, ws_down, g, b, alpha, tile):
    n, d = x2.shape
    tile = min(tile, n)
    wg, wu, wd = ws_gate.astype(bf16), ws_up.astype(bf16), ws_down.astype(bf16)
    rows = pl.BlockSpec((tile, d), lambda i: (i, 0))
    return pl.pallas_call(
        functools.partial(_ffn_out_body, alpha=alpha),
        out_shape=jax.ShapeDtypeStruct((n, d), f32),
        grid=(n // tile,),
        in_specs=[rows, pl.BlockSpec((tile * (d // LANES), LANES), lambda i: (i, 0)), _const_spec(wg.shape),
                  _const_spec(wu.shape), _const_spec(wd.shape), _const_spec((1, d)), _const_spec((1, d))],
        out_specs=rows,
        compiler_params=_cparams(("parallel",)),
        name="ffn_out_ln3",
    )(x2, routed_rows, wg, wu, wd, g.reshape(1, d), b.reshape(1, d))


def _layer(x, mem, positions, w_in, mu_shift, w_decay_up, w0, a_up, a0, g_up, k_k, k_a, r_k, lnx_g, lnx_b, sinks,
           w_o, ln1_g, ln1_b, wm_q, wm_kv, wm_o, ln2_g, ln2_b, w_router, router_bias, we_gate, we_up, we_down,
           ws_gate, ws_up, ws_down, ln3_g, ln3_b, *, layer, alpha):
    b, s, d = x.shape
    n = b * s
    xf = x.reshape(n, d)
    w_in_b = w_in.astype(bf16)
    u_r, u_s = _proj(xf, [w_in_b[:, :RWKV_COLS], w_in_b[:, RWKV_COLS:]], [f32, f32], tile=512)
    y_r = _rwkv(u_r.reshape(b, s, RWKV_COLS), mu_shift, w_decay_up, w0, a_up, a0, g_up, k_k, k_a, r_k, lnx_g, lnx_b,
                tt=256)
    y_s = _swa(u_s.reshape(b, s, SWA_COLS), positions, sinks)
    x1 = _mix_out(y_r.reshape(n, RWKV_WIDTH), y_s.reshape(n, SWA_WIDTH), w_o, xf, ln1_g, ln1_b, alpha, tile=512)
    m = mem.shape[1]
    (kv,) = _proj(mem.reshape(b * m, d), [wm_kv.astype(bf16)], [bf16], tile=512)
    x2, x2_rows = _xattn(x1.reshape(b, s, d), kv.reshape(b, m, 2 * d), wm_q, wm_o, ln2_g, ln2_b, alpha, tile=512)
    x2 = x2.reshape(n, d)
    e_t, g_t, r_t, counts = _router(x2, w_router, router_bias, tile=512)
    starts = (jnp.cumsum(counts) - counts).astype(i32)
    nc = d // LANES
    xs = _dispatch(x2_rows.reshape(n, nc, LANES), e_t, r_t, starts, tile=512)
    ys = _gmm(xs.reshape(n * TOP_K * nc, LANES), we_gate, we_up, we_down, layer, counts, tm=256)
    routed = _combine(ys.reshape(n * TOP_K, nc, LANES), e_t, r_t, g_t, starts, n, tile=256)
    x3 = _ffn_out(x2, routed.reshape(n * nc, LANES), ws_gate, ws_up, ws_down, ln3_g, ln3_b, alpha, tile=512)
    return x3.reshape(b, s, d)


def kernel(x, mem, positions, w_in, mu_shift, w_decay_up, w0, a_up, a0, g_up, k_k, k_a, r_k, lnx_g, lnx_b, sinks, w_o, ln1_g, ln1_b, wm_q, wm_kv, wm_o, ln2_g, ln2_b, w_router, router_bias, we_gate, we_up, we_down, ws_gate, ws_up, ws_down, ln3_g, ln3_b):
    depth = w_in.shape[0]
    alpha = (2 * depth) ** 0.25
    for l in range(depth):
        x = _layer(x, mem, positions, w_in[l], mu_shift[l], w_decay_up[l], w0[l], a_up[l], a0[l], g_up[l], k_k[l],
                   k_a[l], r_k[l], lnx_g[l], lnx_b[l], sinks[l], w_o[l], ln1_g[l], ln1_b[l], wm_q[l], wm_kv[l],
                   wm_o[l], ln2_g[l], ln2_b[l], w_router[l], router_bias[l], we_gate, we_up, we_down,
                   ws_gate[l], ws_up[l], ws_down[l], ln3_g[l], ln3_b[l], layer=l, alpha=alpha)
    return x
```

```python
import functools

import jax
import jax.numpy as jnp
from jax import lax
from jax.experimental import pallas as pl
from jax.experimental.pallas import tpu as pltpu
from jax.experimental.pallas import tpu_sc as plsc

f32 = jnp.float32
bf16 = jnp.bfloat16
i32 = jnp.int32

RWKV_HEADS = 8
HEAD_DIM = 64
RWKV_WIDTH = RWKV_HEADS * HEAD_DIM
DECAY_RANK = 64
AAA_RANK = 64
GATE_RANK = 128
RWKV_COLS = 3 * RWKV_WIDTH + DECAY_RANK + AAA_RANK + GATE_RANK
SWA_Q_HEADS = 8
SWA_KV_HEADS = 2
SWA_GROUP = SWA_Q_HEADS // SWA_KV_HEADS
SWA_WIDTH = SWA_Q_HEADS * HEAD_DIM
SWA_KV_WIDTH = SWA_KV_HEADS * HEAD_DIM
SWA_COLS = SWA_WIDTH + 2 * SWA_KV_WIDTH
WINDOW = 128
ROPE_THETA = 10000.0
MEM_HEADS = 4
N_EXPERTS = 256
TOP_K = 8
N_GROUPS = 8
GROUP_SIZE = N_EXPERTS // N_GROUPS
TOPK_GROUPS = 4
ROUTED_SCALE = 2.5
LN_EPS = 1e-5
GN_EPS = 64e-5
NEG_INF = -1e30

LANES = 128
SUBLANES = 8
WKV_CHUNK = 64
WKV_GROUP = 4
VMEM_LIMIT = 56 * 1024 * 1024

SC_CORES = 2
SC_SUBCORES = 16
SC_LANES = 16
SC_INDEX_GROUP = 128
SC_ROW_CHUNK = 64


def _cparams(sem):
    return pltpu.CompilerParams(dimension_semantics=sem, vmem_limit_bytes=VMEM_LIMIT)


def _const_spec(shape):
    nd = len(shape)
    return pl.BlockSpec(shape, lambda *_: (0,) * nd)


def _dot(a, b):
    return jnp.dot(a, b, preferred_element_type=f32)


def _dot_nt(a, b):
    return lax.dot_general(a, b, (((1,), (1,)), ((), ())), preferred_element_type=f32)


def _dot_tn(a, b):
    return lax.dot_general(a, b, (((0,), (0,)), ((), ())), preferred_element_type=f32)


def _split2(x):
    hi = x.astype(bf16)
    lo = (x - hi.astype(f32)).astype(bf16)
    return hi, lo


def _seg_sums(xs, seg_b):
    parts = []
    for x in xs:
        parts.extend(_split2(x))
    out = _dot(jnp.concatenate(parts, axis=0), seg_b)
    t = xs[0].shape[0]
    return [out[2 * i * t:(2 * i + 1) * t] + out[(2 * i + 1) * t:(2 * i + 2) * t] for i in range(len(xs))]


def _dot_hp(a, b):
    ah, al = _split2(a)
    bh, bl = _split2(b)
    return _dot(ah, bh) + _dot(ah, bl) + _dot(al, bh)


def _dot_exact_lhs(m_bf16, x):
    hi, lo = _split2(x)
    return _dot(m_bf16, hi) + _dot(m_bf16, lo)


def _sigmoid(x):
    return 1.0 / (1.0 + jnp.exp(-x))


def _lane_chunk(n_rows, n_chunks, c):
    return (pl.ds(c, n_rows, stride=n_chunks), slice(None))


def _layer_norm(h, g, b):
    mu = jnp.mean(h, axis=-1, keepdims=True)
    d = h - mu
    var = jnp.mean(d * d, axis=-1, keepdims=True)
    return d * lax.rsqrt(var + LN_EPS) * g + b


def _proj_body(*refs, n_out):
    x_ref = refs[0]
    w_refs = refs[1:1 + n_out]
    o_refs = refs[1 + n_out:]
    xb = x_ref[...].astype(bf16)
    for w_ref, o_ref in zip(w_refs, o_refs):
        o_ref[...] = _dot(xb, w_ref[...]).astype(o_ref.dtype)


def _proj(x, ws, out_dtypes, tile):
    n, k = x.shape
    tile = min(tile, n)
    outs = pl.pallas_call(
        functools.partial(_proj_body, n_out=len(ws)),
        out_shape=[jax.ShapeDtypeStruct((n, w.shape[1]), dt) for w, dt in zip(ws, out_dtypes)],
        grid=(n // tile,),
        in_specs=[pl.BlockSpec((tile, k), lambda i: (i, 0))] + [_const_spec(w.shape) for w in ws],
        out_specs=[pl.BlockSpec((tile, w.shape[1]), lambda i: (i, 0)) for w in ws],
        compiler_params=_cparams(("parallel",)),
        name="proj",
    )(x, *ws)
    return outs


def _wkv_chunks(chains, states, masks):
    bd_b, bd, strict, incl, eye, eye_full = masks
    c, n = chains[0][1].shape
    nch = len(chains)

    def stack(x_b):
        return jnp.where(bd_b, jnp.concatenate([x_b] * WKV_GROUP, axis=0), jnp.zeros((), bf16))

    cast = [tuple(x.astype(bf16) for x in ch[1:6]) for ch in chains]
    v_s = [stack(cb[4]) for cb in cast]
    g = [_dot_nt(jnp.concatenate([cb[0], cb[3]], axis=0), jnp.concatenate([stack(cb[1]), stack(cb[2])], axis=0))
         for cb in cast]
    l_ak = [jnp.where(strict, gi[:c, n:], 0.0).astype(bf16) for gi in g]
    m_rb = [jnp.where(incl, gi[c:, :n], 0.0).astype(bf16) for gi in g]
    m_rk = [jnp.where(incl, gi[c:, n:], 0.0).astype(bf16) for gi in g]
    x = [jnp.where(strict, gi[:c, :n], 0.0) for gi in g]
    t = [eye + xi for xi in x]
    for _ in range(5):
        xb = [xi.astype(bf16) for xi in x]
        x = [_dot(xi, stack(xi)) for xi in xb]
        t = [ti + _dot(ti.astype(bf16), stack(xi.astype(bf16))) for ti, xi in zip(t, x)]
    lakv = [_dot(l_ak[i], v_s[i]).astype(bf16) for i in range(nch)]
    au = [_dot(t[i].astype(bf16), jnp.concatenate([stack(cast[i][0]), stack(lakv[i])], axis=1))
          for i in range(nch)]
    abar = [a[:, :n].astype(bf16) for a in au]
    ubar = [a[:, n:].astype(bf16) for a in au]
    ry = [_dot(m_rb[i], jnp.concatenate([stack(abar[i]), stack(ubar[i])], axis=1)) for i in range(nch)]
    r_bar = [(chains[i][4] + ry[i][:, :n]).astype(bf16) for i in range(nch)]
    y_bar = [ry[i][:, n:] + _dot(m_rk[i], v_s[i]) for i in range(nch)]
    p = [((eye_full + jnp.where(bd, _dot_tn(abar[i], cast[i][1]), 0.0)) * chains[i][6]).astype(bf16)
         for i in range(nch)]
    q = []
    for i in range(nch):
        q_bd = jnp.where(bd, _dot_tn(jnp.concatenate([ubar[i], cast[i][4]], axis=0),
                                     jnp.concatenate([cast[i][1], cast[i][2]], axis=0)), 0.0)
        qi = q_bd[0:c]
        for h in range(1, WKV_GROUP):
            qi = qi + q_bd[h * c:(h + 1) * c]
        q.append(qi * chains[i][6])
    states = list(states)
    ys = []
    for i in range(nch):
        gi = chains[i][0]
        s_b = states[gi].astype(bf16)
        ys.append(_dot_nt(r_bar[i], stack(s_b)) + y_bar[i])
        states[gi] = _dot(s_b, p[i]) + q[i]
    return ys, states


def _rwkv_body(u_ref, mu_ref, wdec_ref, w0_ref, aup_ref, a0_ref, gup_ref, kk_ref, ka_ref, rk_ref,
               lng_ref, lnb_ref, seg_ref, tri_ref, y_ref, state_ref, carry_ref, *, tt):
    j = pl.program_id(1)

    @pl.when(j == 0)
    def _():
        state_ref[...] = jnp.zeros_like(state_ref)
        carry_ref[...] = jnp.zeros_like(carry_ref)

    w = RWKV_WIDTH
    u = u_ref[...]
    row = lax.broadcasted_iota(i32, u.shape, 0)
    prev = jnp.where(row == 0, carry_ref[0:1, :], pltpu.roll(u, 1, axis=0))
    carry_ref[0:1, :] = u[tt - 1:tt, :]
    us = u + (prev - u) * mu_ref[...]
    r = us[:, 0:w]
    k = us[:, w:2 * w]
    v = us[:, 2 * w:3 * w]
    wa = us[:, 3 * w:3 * w + DECAY_RANK + AAA_RANK]
    gd = us[:, 3 * w + DECAY_RANK + AAA_RANK:]
    z = w0_ref[...] + _dot_hp(jnp.tanh(wa), wdec_ref[...])
    softplus_neg_z = jnp.maximum(-z, 0.0) + jnp.log(1.0 + jnp.exp(-jnp.abs(z)))
    lw = -jnp.exp(-softplus_neg_z - 0.5)
    a = _sigmoid(a0_ref[...] + _dot_hp(wa, aup_ref[...]))
    gate = _dot(_sigmoid(gd).astype(bf16), gup_ref[...].astype(bf16))
    seg = seg_ref[...]
    kk = k * kk_ref[...]
    kmod = k * (1.0 + (a - 1.0) * ka_ref[...])
    kk_sq, bonus_dot = _seg_sums([kk * kk, r * kmod * rk_ref[...]], seg)
    kk = kk / jnp.maximum(jnp.sqrt(kk_sq), 1e-12)
    cum = _dot_exact_lhs(tri_ref[...], lw)
    wc = jnp.exp(cum)
    iwc = jnp.exp(-cum)
    at = -kk * jnp.exp(cum - lw)
    bt = kk * a * iwc
    kt = kmod * iwc
    rt = r * wc

    n = WKV_GROUP * HEAD_DIM
    ri = lax.broadcasted_iota(i32, (n, n), 0)
    ci = lax.broadcasted_iota(i32, (n, n), 1)
    bd = (ri // WKV_CHUNK) == (ci // HEAD_DIM)
    bd_b = jnp.where(bd, 1.0, 0.0).astype(bf16) > 0
    eye_full = jnp.where(ri == ci, 1.0, 0.0).astype(f32)
    ti = lax.broadcasted_iota(i32, (WKV_CHUNK, n), 0)
    si = lax.broadcasted_iota(i32, (WKV_CHUNK, n), 1) % WKV_CHUNK
    masks = (bd_b, bd, ti > si, ti >= si, jnp.where(ti == si, 1.0, 0.0).astype(f32), eye_full)

    n_groups = w // n
    n_chunks = tt // WKV_CHUNK
    chains = []
    for c in range(n_chunks):
        rs = slice(c * WKV_CHUNK, (c + 1) * WKV_CHUNK)
        last = (c + 1) * WKV_CHUNK - 1
        for gi in range(n_groups):
            cs = slice(gi * n, (gi + 1) * n)
            chains.append((gi, at[rs, cs], bt[rs, cs], kt[rs, cs], rt[rs, cs], v[rs, cs], wc[last:last + 1, cs]))
    ys, states = _wkv_chunks(chains, [state_ref[gi] for gi in range(n_groups)], masks)
    for gi in range(n_groups):
        state_ref[gi] = states[gi]
    y = jnp.concatenate([jnp.concatenate(ys[c * n_groups:(c + 1) * n_groups], axis=1) for c in range(n_chunks)],
                        axis=0)

    inv_n = 1.0 / HEAD_DIM
    d = y - _seg_sums([y], seg)[0] * inv_n
    var = _seg_sums([d * d], seg)[0] * inv_n
    yn = d * lax.rsqrt(var + GN_EPS) * lng_ref[...] + lnb_ref[...]
    y_ref[...] = ((yn + bonus_dot * v) * gate).astype(y_ref.dtype)


def _rwkv(u_r, mu_shift, w_decay_up, w0, a_up, a0, g_up, k_k, k_a, r_k, lnx_g, lnx_b, tt):
    b, s, cols = u_r.shape
    tt = min(tt, s)
    w = RWKV_WIDTH
    row = lambda p: p.reshape(1, -1).astype(f32)
    wdec = jnp.concatenate([w_decay_up, jnp.zeros((AAA_RANK, w), f32)], axis=0)
    aup = jnp.concatenate([jnp.zeros((DECAY_RANK, w), f32), a_up], axis=0)
    hid = jnp.arange(w) // HEAD_DIM
    seg = (hid[:, None] == hid[None, :]).astype(bf16)
    ti = jnp.arange(tt)
    tri = ((ti[:, None] // WKV_CHUNK == ti[None, :] // WKV_CHUNK) & (ti[:, None] >= ti[None, :])).astype(bf16)
    params = [row(mu_shift), wdec, row(w0), aup, row(a0), g_up, row(k_k), row(k_a), row(r_k), row(lnx_g),
              row(lnx_b), seg, tri]
    n = WKV_GROUP * HEAD_DIM
    return pl.pallas_call(
        functools.partial(_rwkv_body, tt=tt),
        out_shape=jax.ShapeDtypeStruct((b, s, w), bf16),
        grid=(b, s // tt),
        in_specs=[pl.BlockSpec((None, tt, cols), lambda bi, j: (bi, j, 0))] + [_const_spec(p.shape) for p in params],
        out_specs=pl.BlockSpec((None, tt, w), lambda bi, j: (bi, j, 0)),
        scratch_shapes=[pltpu.VMEM((w // n, HEAD_DIM, n), f32), pltpu.VMEM((SUBLANES, cols), f32)],
        compiler_params=_cparams(("parallel", "arbitrary")),
        name="rwkv7",
    )(u_r, *params)


def _swa_body(sink_ref, u_ref, pos_ref, invf_ref, o_ref, kprev_ref, vprev_ref):
    j = pl.program_id(1)

    @pl.when(j == 0)
    def _():
        kprev_ref[...] = jnp.zeros_like(kprev_ref)
        vprev_ref[...] = jnp.zeros_like(vprev_ref)

    wq = SWA_WIDTH
    kvw = SWA_KV_WIDTH
    u = u_ref[...]
    ang = pos_ref[...].astype(f32) * invf_ref[...]
    cos = jnp.cos(ang)
    sin = jnp.sin(ang)

    def rope(x, c, s):
        n = x.shape[1]
        lane = lax.broadcasted_iota(i32, x.shape, 1)
        half = HEAD_DIM // 2
        rot = jnp.where((lane % HEAD_DIM) < half, -pltpu.roll(x, n - half, axis=1), pltpu.roll(x, half, axis=1))
        return x * c + rot * s

    q = rope(u[:, :wq] * (HEAD_DIM ** -0.5), jnp.concatenate([cos] * (wq // LANES), axis=1),
             jnp.concatenate([sin] * (wq // LANES), axis=1))
    k_cur = rope(u[:, wq:wq + kvw], cos, sin)
    v_cur = u[:, wq + kvw:]
    k_prev = kprev_ref[...]
    v_prev = vprev_ref[...]
    kprev_ref[...] = k_cur
    vprev_ref[...] = v_cur

    gw = SWA_GROUP * HEAD_DIM
    rows = SWA_GROUP * WINDOW
    lane_kv = lax.broadcasted_iota(i32, (WINDOW, kvw), 1)

    def rep(x, gi):
        sw = pltpu.roll(x, HEAD_DIM, axis=1)
        one = jnp.where((lane_kv // HEAD_DIM) == gi, x, sw)
        return jnp.concatenate([one] * (gw // kvw), axis=1).astype(bf16)

    ri = lax.broadcasted_iota(i32, (rows, WINDOW), 0)
    ci = lax.broadcasted_iota(i32, (rows, WINDOW), 1)
    t_idx = ri % WINDOW
    mask_prev = ci > t_idx + jnp.where(j > 0, 0, WINDOW)
    mask_cur = ci <= t_idx
    rb = lax.broadcasted_iota(i32, (rows, 1), 0) // WINDOW
    rbo = lax.broadcasted_iota(i32, (rows, gw), 0) // WINDOW
    cbo = lax.broadcasted_iota(i32, (rows, gw), 1) // HEAD_DIM
    bd = rbo == cbo
    outs = []
    for gi in range(SWA_KV_HEADS):
        qg = q[:, gi * gw:(gi + 1) * gw]
        q_bd = jnp.where(bd, jnp.concatenate([qg] * SWA_GROUP, axis=0), 0.0).astype(bf16)
        s_prev = jnp.where(mask_prev, _dot_nt(q_bd, rep(k_prev, gi)), NEG_INF)
        s_cur = jnp.where(mask_cur, _dot_nt(q_bd, rep(k_cur, gi)), NEG_INF)
        sink = jnp.zeros((rows, 1), f32)
        for h in range(SWA_GROUP):
            sink = jnp.where(rb == h, sink_ref[gi * SWA_GROUP + h], sink)
        m = jnp.maximum(jnp.maximum(jnp.max(s_prev, axis=-1, keepdims=True),
                                    jnp.max(s_cur, axis=-1, keepdims=True)), sink)
        p_prev = jnp.exp(s_prev - m)
        p_cur = jnp.exp(s_cur - m)
        denom = jnp.sum(p_prev, axis=-1, keepdims=True) + jnp.sum(p_cur, axis=-1, keepdims=True) + jnp.exp(sink - m)
        o_bd = _dot(p_prev.astype(bf16), rep(v_prev, gi)) + _dot(p_cur.astype(bf16), rep(v_cur, gi))
        o_bd = jnp.where(bd, o_bd / denom, 0.0)
        og = o_bd[0:WINDOW]
        for h in range(1, SWA_GROUP):
            og = og + o_bd[h * WINDOW:(h + 1) * WINDOW]
        outs.append(og)
    o_ref[...] = jnp.concatenate(outs, axis=1).astype(o_ref.dtype)


def _swa(u_s, positions, sinks):
    b, s, cols = u_s.shape
    half = HEAD_DIM // 2
    inv_freq = ROPE_THETA ** (-jnp.arange(0, HEAD_DIM, 2, dtype=f32) / HEAD_DIM)
    invf = jnp.tile(inv_freq, LANES // half).reshape(1, LANES)
    pos = positions.reshape(b, s, 1).astype(i32)
    return pl.pallas_call(
        _swa_body,
        out_shape=jax.ShapeDtypeStruct((b, s, SWA_WIDTH), bf16),
        grid=(b, s // WINDOW),
        in_specs=[pl.BlockSpec(memory_space=pltpu.SMEM),
                  pl.BlockSpec((None, WINDOW, cols), lambda bi, j: (bi, j, 0)),
                  pl.BlockSpec((None, WINDOW, 1), lambda bi, j: (bi, j, 0)),
                  _const_spec((1, LANES))],
        out_specs=pl.BlockSpec((None, WINDOW, SWA_WIDTH), lambda bi, j: (bi, j, 0)),
        scratch_shapes=[pltpu.VMEM((WINDOW, SWA_KV_WIDTH), f32), pltpu.VMEM((WINDOW, SWA_KV_WIDTH), f32)],
        compiler_params=_cparams(("parallel", "arbitrary")),
        name="swa",
    )(sinks.astype(f32), u_s, pos, invf)


def _mix_out_body(ya_ref, yb_ref, wa_ref, wb_ref, x_ref, g_ref, b_ref, o_ref, *, alpha):
    mix = _dot(ya_ref[...], wa_ref[...]) + _dot(yb_ref[...], wb_ref[...])
    o_ref[...] = _layer_norm(alpha * x_ref[...] + mix, g_ref[...], b_ref[...])


def _mix_out(ya, yb, w_o, x, g, b, alpha, tile):
    n, d = x.shape
    tile = min(tile, n)
    wa = w_o[:ya.shape[1]].astype(bf16)
    wb = w_o[ya.shape[1]:].astype(bf16)
    rows = lambda width: pl.BlockSpec((tile, width), lambda i: (i, 0))
    return pl.pallas_call(
        functools.partial(_mix_out_body, alpha=alpha),
        out_shape=jax.ShapeDtypeStruct((n, d), f32),
        grid=(n // tile,),
        in_specs=[rows(ya.shape[1]), rows(yb.shape[1]), _const_spec(wa.shape), _const_spec(wb.shape), rows(d),
                  _const_spec((1, d)), _const_spec((1, d))],
        out_specs=rows(d),
        compiler_params=_cparams(("parallel",)),
        name="mix_out_ln1",
    )(ya, yb, wa, wb, x, g.reshape(1, d), b.reshape(1, d))


def _xattn_body(x_ref, kv_ref, wq_ref, wo_ref, g_ref, b_ref, o_ref, o3_ref, *, alpha):
    x = x_ref[...]
    d = x.shape[1]
    hd = d // MEM_HEADS
    q = _dot(x.astype(bf16), wq_ref[...]) * (hd ** -0.5)
    kv = kv_ref[...]
    outs = []
    for h in range(MEM_HEADS):
        qh = q[:, h * hd:(h + 1) * hd].astype(bf16)
        kh = kv[:, h * hd:(h + 1) * hd]
        vh = kv[:, d + h * hd:d + (h + 1) * hd]
        s = _dot_nt(qh, kh)
        p = jnp.exp(s - jnp.max(s, axis=-1, keepdims=True))
        l = jnp.sum(p, axis=-1, keepdims=True)
        outs.append(_dot(p.astype(bf16), vh) / l)
    o = jnp.concatenate(outs, axis=1)
    xa = _dot(o.astype(bf16), wo_ref[...])
    y = _layer_norm(alpha * x + xa, g_ref[...], b_ref[...])
    o_ref[...] = y
    nc = d // LANES
    for c in range(nc):
        o3_ref[_lane_chunk(x.shape[0], nc, c)] = y[:, c * LANES:(c + 1) * LANES]


def _xattn(x1, kv, wm_q, wm_o, g, b, alpha, tile):
    bsz, s, d = x1.shape
    m = kv.shape[1]
    tile = min(tile, s)
    nj = s // tile
    wq = wm_q.astype(bf16)
    wo = wm_o.astype(bf16)
    return pl.pallas_call(
        functools.partial(_xattn_body, alpha=alpha),
        out_shape=[jax.ShapeDtypeStruct((bsz, s, d), f32), jax.ShapeDtypeStruct((bsz * s * (d // LANES), LANES), f32)],
        grid=(bsz, nj),
        in_specs=[pl.BlockSpec((None, tile, d), lambda bi, j: (bi, j, 0)),
                  pl.BlockSpec((None, m, 2 * d), lambda bi, j: (bi, 0, 0)),
                  _const_spec(wq.shape), _const_spec(wo.shape), _const_spec((1, d)), _const_spec((1, d))],
        out_specs=[pl.BlockSpec((None, tile, d), lambda bi, j: (bi, j, 0)),
                   pl.BlockSpec((tile * (d // LANES), LANES), lambda bi, j: (bi * nj + j, 0))],
        compiler_params=_cparams(("parallel", "parallel")),
        name="mem_xattn_ln2",
    )(x1, kv, wq, wo, g.reshape(1, d), b.reshape(1, d))


def _router_body(x_ref, wt_ref, bias_ref, upper_ref, e_ref, g_ref, r_ref, cnt_out_ref, cnt_ref, *, t):
    @pl.when(pl.program_id(0) == 0)
    def _():
        cnt_ref[...] = jnp.zeros_like(cnt_ref)

    xh, xl = _split2(x_ref[...])
    wh, wl = _split2(wt_ref[...])
    logits = _dot_nt(wh, xh) + _dot_nt(wh, xl) + _dot_nt(wl, xh)
    scores = _sigmoid(logits)
    biased = scores + bias_ref[...][:, 0:1]
    ne = N_EXPERTS
    neg = -jnp.inf

    def top1(vals):
        rows = lax.broadcasted_iota(i32, vals.shape, 0).astype(f32)
        m = jnp.max(vals, axis=0, keepdims=True)
        idx = jnp.min(jnp.where(vals == m, rows, float(vals.shape[0])), axis=0, keepdims=True)
        return m, idx, rows == idx

    gscores = []
    for gi in range(N_GROUPS):
        blk = biased[gi * GROUP_SIZE:(gi + 1) * GROUP_SIZE, :]
        m1, _, hit = top1(blk)
        m2 = jnp.max(jnp.where(hit, neg, blk), axis=0, keepdims=True)
        gscores.append(m1 + m2)
    gs = jnp.concatenate(gscores, axis=0)
    gsel = jnp.zeros(gs.shape, f32)
    for _ in range(TOPK_GROUPS):
        _, _, hit = top1(gs)
        gsel = jnp.where(hit, 1.0, gsel)
        gs = jnp.where(hit, neg, gs)
    emask = jnp.concatenate(
        [jnp.broadcast_to(gsel[gi:gi + 1, :], (GROUP_SIZE, t)) for gi in range(N_GROUPS)], axis=0) > 0.5
    cand = jnp.where(emask, biased, NEG_INF)
    idxs, sels = [], []
    chosen = jnp.zeros((ne, t), f32)
    for _ in range(TOP_K):
        _, idx, hit = top1(cand)
        idxs.append(idx)
        sels.append(jnp.sum(jnp.where(hit, scores, 0.0), axis=0, keepdims=True))
        chosen = chosen + jnp.where(hit, 1.0, 0.0)
        cand = jnp.where(hit, neg, cand)
    sel = jnp.concatenate(sels, axis=0)
    g_ref[...] = sel / jnp.sum(sel, axis=0, keepdims=True) * ROUTED_SCALE
    e_ref[...] = jnp.concatenate(idxs, axis=0).astype(i32)
    before = _dot(chosen.astype(bf16), upper_ref[...]) + cnt_ref[...][:, 0:1]
    rows = lax.broadcasted_iota(i32, (ne, t), 0).astype(f32)
    ranks = [jnp.sum(jnp.where(rows == idx, before, 0.0), axis=0, keepdims=True) for idx in idxs]
    r_ref[...] = jnp.concatenate(ranks, axis=0).astype(i32)
    cnt_ref[...] = cnt_ref[...] + jnp.sum(chosen, axis=1, keepdims=True)
    cnt_out_ref[...] = cnt_ref[...].astype(i32)


def _router(x2, w_router, router_bias, tile):
    n, d = x2.shape
    ne = N_EXPERTS
    t = min(tile, n)
    wt = w_router.T
    bias = jnp.broadcast_to(router_bias.reshape(ne, 1).astype(f32), (ne, LANES))
    ti = jnp.arange(t)
    upper = (ti[:, None] < ti[None, :]).astype(bf16)
    cols = pl.BlockSpec((TOP_K, t), lambda i: (0, i))
    e_t, g_t, r_t, cnt = pl.pallas_call(
        functools.partial(_router_body, t=t),
        out_shape=[jax.ShapeDtypeStruct((TOP_K, n), i32), jax.ShapeDtypeStruct((TOP_K, n), f32),
                   jax.ShapeDtypeStruct((TOP_K, n), i32), jax.ShapeDtypeStruct((ne, LANES), i32)],
        grid=(n // t,),
        in_specs=[pl.BlockSpec((t, d), lambda i: (i, 0)), _const_spec((ne, d)), _const_spec((ne, LANES)),
                  _const_spec((t, t))],
        out_specs=[cols, cols, cols, _const_spec((ne, LANES))],
        scratch_shapes=[pltpu.VMEM((ne, LANES), f32)],
        compiler_params=_cparams(("arbitrary",)),
        name="router",
    )(x2, wt, bias, upper)
    return e_t, g_t, r_t, cnt[:, 0]


def _dispatch_body(e_ref, r_ref, st_ref, x_ref, o_ref, sem, *, t):
    def issue(ti, c):
        for kk in range(TOP_K):
            slot = st_ref[e_ref[kk, ti]] + r_ref[kk, ti]
            pltpu.make_async_copy(x_ref.at[ti], o_ref.at[slot], sem).start()
        return c

    lax.fori_loop(0, t, issue, 0)
    for _ in range(TOP_K):
        pltpu.make_async_copy(x_ref, o_ref.at[pl.ds(0, t)], sem).wait()


def _dispatch(x3d, e_t, r_t, starts, tile):
    n = x3d.shape[0]
    t = min(tile, n)
    smem_rows = pl.BlockSpec((TOP_K, t), lambda i: (0, i), memory_space=pltpu.SMEM)
    return pl.pallas_call(
        functools.partial(_dispatch_body, t=t),
        out_shape=jax.ShapeDtypeStruct((n * TOP_K,) + x3d.shape[1:], x3d.dtype),
        grid=(n // t,),
        in_specs=[smem_rows, smem_rows, pl.BlockSpec(memory_space=pltpu.SMEM),
                  pl.BlockSpec((t,) + x3d.shape[1:], lambda i: (i, 0, 0))],
        out_specs=pl.BlockSpec(memory_space=pl.ANY),
        scratch_shapes=[pltpu.SemaphoreType.DMA],
        compiler_params=_cparams(("arbitrary",)),
        name="moe_dispatch",
    )(e_t, r_t, starts, x3d)


def _dispatch_sc(x3d, e_t, r_t, starts):
    n, nc, lanes = x3d.shape
    workers = SC_CORES * SC_SUBCORES
    per_w = n // workers
    assert n % (workers * SC_INDEX_GROUP) == 0 and SC_INDEX_GROUP % SC_ROW_CHUNK == 0
    mesh = plsc.VectorSubcoreMesh(core_axis_name="c", subcore_axis_name="s")

    @functools.partial(
        pl.kernel, mesh=mesh,
        out_type=jax.ShapeDtypeStruct((n * TOP_K, nc, lanes), x3d.dtype),
        scratch_types=[
            pltpu.VMEM((SC_ROW_CHUNK, nc, lanes), x3d.dtype),
            pltpu.VMEM((TOP_K, SC_INDEX_GROUP), i32),
            pltpu.VMEM((TOP_K, SC_INDEX_GROUP), i32),
            pltpu.VMEM((TOP_K, SC_ROW_CHUNK), i32),
            pltpu.VMEM((N_EXPERTS,), i32),
            pltpu.SemaphoreType.DMA,
        ],
        compiler_params=pltpu.CompilerParams(use_tc_tiling_on_sc=True, needs_layout_passes=False),
    )
    def dispatch(x_hbm, e_hbm, r_hbm, st_hbm, o_hbm, rows_v, e_v, r_v, slot_v, st_v, sem):
        wid = lax.axis_index("s") * SC_CORES + lax.axis_index("c")
        pltpu.sync_copy(st_hbm, st_v)

        @pl.loop(0, per_w // SC_INDEX_GROUP)
        def _(gi):
            base = wid * per_w + gi * SC_INDEX_GROUP
            pltpu.sync_copy(e_hbm.at[:, pl.ds(base, SC_INDEX_GROUP)], e_v)
            pltpu.sync_copy(r_hbm.at[:, pl.ds(base, SC_INDEX_GROUP)], r_v)
            for h in range(SC_INDEX_GROUP // SC_ROW_CHUNK):
                off = h * SC_ROW_CHUNK
                pltpu.sync_copy(x_hbm.at[pl.ds(base + off, SC_ROW_CHUNK)], rows_v)
                for kk in range(TOP_K):
                    for j in range(SC_ROW_CHUNK // SC_LANES):
                        src = pl.ds(off + j * SC_LANES, SC_LANES)
                        start = plsc.load_gather(st_v, [e_v[kk, src]])
                        slot_v[kk, pl.ds(j * SC_LANES, SC_LANES)] = r_v[kk, src] + start
                copies = [pltpu.async_copy(rows_v, o_hbm.at[slot_v.at[kk]], sem) for kk in range(TOP_K)]
                for cp in copies:
                    cp.wait()

    return dispatch(x3d, e_t, r_t, starts)


def _gmm_body(gid_ref, tid_ref, lo_ref, hi_ref, first_ref, newg_ref, x_ref, wg_ref, wu_ref, wd_ref, o_ref,
              wg_b, wu_b, wd_b, *, tm):
    v = pl.program_id(0)
    lo = lo_ref[v]
    hi = hi_ref[v]
    row0 = tid_ref[v] * tm
    nc = wg_ref.shape[0] // LANES
    full = (lo <= row0) & (hi >= row0 + tm)

    @pl.when(newg_ref[v] == 1)
    def _():
        wg_b[...] = wg_ref[...].astype(bf16)
        wu_b[...] = wu_ref[...].astype(bf16)
        wd_b[...] = wd_ref[...].astype(bf16)

    def ffn():
        x = jnp.concatenate([x_ref[_lane_chunk(tm, nc, c)] for c in range(nc)], axis=1).astype(bf16)
        hg = _dot(x, wg_b[...])
        h = hg * _sigmoid(hg) * _dot(x, wu_b[...])
        return _dot(h.astype(bf16), wd_b[...])

    @pl.when(full)
    def _():
        y = ffn()
        for c in range(nc):
            o_ref[_lane_chunk(tm, nc, c)] = y[:, c * LANES:(c + 1) * LANES]

    @pl.when(jnp.logical_not(full) & (hi > lo))
    def _():
        @pl.when(first_ref[v] == 1)
        def _():
            o_ref[...] = jnp.zeros_like(o_ref)

        rows = row0 + lax.broadcasted_iota(i32, (tm, 1), 0)
        mask = (rows >= lo) & (rows < hi)
        y = ffn()
        for c in range(nc):
            idx = _lane_chunk(tm, nc, c)
            o_ref[idx] = jnp.where(mask, y[:, c * LANES:(c + 1) * LANES], o_ref[idx])


def _gmm(xs, we_gate, we_up, we_down, layer, counts, tm):
    _, ne, d, de = we_gate.shape
    nc = d // LANES
    nk = xs.shape[0] // nc
    tm = min(tm, nk)
    n_tiles = nk // tm
    n_visits = n_tiles + ne - 1
    ends = jnp.cumsum(counts)
    starts = ends - counts
    tile_lo = starts // tm
    n_touch = jnp.where(counts > 0, (ends - 1) // tm - tile_lo + 1, 0)
    vis_end = jnp.cumsum(n_touch)
    vis_start = vis_end - n_touch
    vi = jnp.arange(n_visits, dtype=i32)
    valid = vi < vis_end[-1]
    gid = jnp.minimum(jnp.sum((vis_end[None, :] <= vi[:, None]).astype(i32), axis=1), ne - 1)
    onehot = gid[:, None] == jnp.arange(ne, dtype=i32)[None, :]
    pick = lambda table: jnp.sum(jnp.where(onehot, table[None, :], 0), axis=1)
    tid = jnp.where(valid, pick(tile_lo) + vi - pick(vis_start), n_tiles - 1).astype(i32)
    lo = jnp.where(valid, pick(starts), 0).astype(i32)
    hi = jnp.where(valid, pick(ends), 0).astype(i32)
    one = jnp.ones((1,), i32)
    first = jnp.concatenate([one, (tid[1:] != tid[:-1]).astype(i32)])
    newg = jnp.concatenate([one, (gid[1:] != gid[:-1]).astype(i32)])
    wspec = lambda shape: pl.BlockSpec((None, None) + shape, lambda v, g, *_: (layer, g[v], 0, 0))
    rows = pl.BlockSpec((tm * nc, LANES), lambda v, g, t, *_: (t[v], 0))
    return pl.pallas_call(
        functools.partial(_gmm_body, tm=tm),
        out_shape=jax.ShapeDtypeStruct(xs.shape, f32),
        grid_spec=pltpu.PrefetchScalarGridSpec(
            num_scalar_prefetch=6, grid=(n_visits,),
            in_specs=[rows, wspec((d, de)), wspec((d, de)), wspec((de, d))],
            out_specs=rows,
            scratch_shapes=[pltpu.VMEM((d, de), bf16), pltpu.VMEM((d, de), bf16), pltpu.VMEM((de, d), bf16)]),
        compiler_params=_cparams(("arbitrary",)),
        name="moe_experts",
    )(gid, tid, lo, hi, first, newg, xs, we_gate, we_up, we_down)


def _combine_body(e_ref, r_ref, g_ref, st_ref, y_ref, o_ref, buf, sem, *, t):
    def issue(ti, c):
        for kk in range(TOP_K):
            slot = st_ref[e_ref[kk, ti]] + r_ref[kk, ti]
            pltpu.make_async_copy(y_ref.at[slot], buf.at[kk, ti], sem).start()
        return c

    lax.fori_loop(0, t, issue, 0)
    for kk in range(TOP_K):
        pltpu.make_async_copy(y_ref.at[pl.ds(0, t)], buf.at[kk], sem).wait()

    def reduce(ti, c):
        acc = g_ref[0, ti] * buf[0, ti]
        for kk in range(1, TOP_K):
            acc = acc + g_ref[kk, ti] * buf[kk, ti]
        o_ref[ti] = acc
        return c

    lax.fori_loop(0, t, reduce, 0)


def _combine(ys, e_t, r_t, g_t, starts, n, tile):
    t = min(tile, n)
    tail = ys.shape[1:]
    smem_rows = pl.BlockSpec((TOP_K, t), lambda i: (0, i), memory_space=pltpu.SMEM)
    return pl.pallas_call(
        functools.partial(_combine_body, t=t),
        out_shape=jax.ShapeDtypeStruct((n,) + tail, f32),
        grid=(n // t,),
        in_specs=[smem_rows, smem_rows, smem_rows, pl.BlockSpec(memory_space=pltpu.SMEM),
                  pl.BlockSpec(memory_space=pl.ANY)],
        out_specs=pl.BlockSpec((t,) + tail, lambda i: (i, 0, 0)),
        scratch_shapes=[pltpu.VMEM((TOP_K, t) + tail, f32), pltpu.SemaphoreType.DMA],
        compiler_params=_cparams(("arbitrary",)),
        name="moe_combine",
    )(e_t, r_t, g_t, starts, ys)


def _ffn_out_body(x_ref, r_ref, wg_ref, wu_ref, wd_ref, g_ref, b_ref, o_ref, *, alpha):
    x = x_ref[...]
    xb = x.astype(bf16)
    hg = _dot(xb, wg_ref[...])
    h = hg * _sigmoid(hg) * _dot(xb, wu_ref[...])
    shared = _dot(h.astype(bf16), wd_ref[...])
    nc = x.shape[1] // LANES
    routed = jnp.concatenate([r_ref[_lane_chunk(x.shape[0], nc, c)] for c in range(nc)], axis=1)
    o_ref[...] = _layer_norm(alpha * x + routed + shared, g_ref[...], b_ref[...])


def _ffn_out(x2, routed_rows, ws_gate, ws_up, ws_down, g, b, alpha, tile):
    n, d = x2.shape
    tile = min(tile, n)
    wg, wu, wd = ws_gate.astype(bf16), ws_up.astype(bf16), ws_down.astype(bf16)
    rows = pl.BlockSpec((tile, d), lambda i: (i, 0))
    return pl.pallas_call(
        functools.partial(_ffn_out_body, alpha=alpha),
        out_shape=jax.ShapeDtypeStruct((n, d), f32),
        grid=(n // tile,),
        in_specs=[rows, pl.BlockSpec((tile * (d // LANES), LANES), lambda i: (i, 0)), _const_spec(wg.shape),
                  _const_spec(wu.shape), _const_spec(wd.shape), _const_spec((1, d)), _const_spec((1, d))],
        out_specs=rows,
        compiler_params=_cparams(("parallel",)),
        name="ffn_out_ln3",
    )(x2, routed_rows, wg, wu, wd, g.reshape(1, d), b.reshape(1, d))


def _layer(x, mem, positions, w_in, mu_shift, w_decay_up, w0, a_up, a0, g_up, k_k, k_a, r_k, lnx_g, lnx_b, sinks,
           w_o, ln1_g, ln1_b, wm_q, wm_kv, wm_o, ln2_g, ln2_b, w_router, router_bias, we_gate, we_up, we_down,
           ws_gate, ws_up, ws_down, ln3_g, ln3_b, *, layer, alpha):
    b, s, d = x.shape
    n = b * s
    xf = x.reshape(n, d)
    w_in_b = w_in.astype(bf16)
    u_r, u_s = _proj(xf, [w_in_b[:, :RWKV_COLS], w_in_b[:, RWKV_COLS:]], [f32, f32], tile=512)
    y_r = _rwkv(u_r.reshape(b, s, RWKV_COLS), mu_shift, w_decay_up, w0, a_up, a0, g_up, k_k, k_a, r_k, lnx_g, lnx_b,
                tt=256)
    y_s = _swa(u_s.reshape(b, s, SWA_COLS), positions, sinks)
    x1 = _mix_out(y_r.reshape(n, RWKV_WIDTH), y_s.reshape(n, SWA_WIDTH), w_o, xf, ln1_g, ln1_b, alpha, tile=512)
    m = mem.shape[1]
    (kv,) = _proj(mem.reshape(b * m, d), [wm_kv.astype(bf16)], [bf16], tile=512)
    x2, x2_rows = _xattn(x1.reshape(b, s, d), kv.reshape(b, m, 2 * d), wm_q, wm_o, ln2_g, ln2_b, alpha, tile=512)
    x2 = x2.reshape(n, d)
    e_t, g_t, r_t, counts = _router(x2, w_router, router_bias, tile=512)
    starts = (jnp.cumsum(counts) - counts).astype(i32)
    nc = d // LANES
    xs = _dispatch_sc(x2_rows.reshape(n, nc, LANES), e_t, r_t, starts)
    ys = _gmm(xs.reshape(n * TOP_K * nc, LANES), we_gate, we_up, we_down, layer, counts, tm=256)
    routed = _combine(ys.reshape(n * TOP_K, nc, LANES), e_t, r_t, g_t, starts, n, tile=256)
    x3 = _ffn_out(x2, routed.reshape(n * nc, LANES), ws_gate, ws_up, ws_down, ln3_g, ln3_b, alpha, tile=512)
    return x3.reshape(b, s, d)


def kernel(x, mem, positions, w_in, mu_shift, w_decay_up, w0, a_up, a0, g_up, k_k, k_a, r_k, lnx_g, lnx_b, sinks, w_o, ln1_g, ln1_b, wm_q, wm_kv, wm_o, ln2_g, ln2_b, w_router, router_bias, we_gate, we_up, we_down, ws_gate, ws_up, ws_down, ln3_g, ln3_b):
    depth = w_in.shape[0]
    alpha = (2 * depth) ** 0.25
    for l in range(depth):
        x = _layer(x, mem, positions, w_in[l], mu_shift[l], w_decay_up[l], w0[l], a_up[l], a0[l], g_up[l], k_k[l],
                   k_a[l], r_k[l], lnx_g[l], lnx_b[l], sinks[l], w_o[l], ln1_g[l], ln1_b[l], wm_q[l], wm_kv[l],
                   wm_o[l], ln2_g[l], ln2_b[l], w_router[l], router_bias[l], we_gate, we_up, we_down,
                   ws_gate[l], ws_up[l], ws_down[l], ln3_g[l], ln3_b[l], layer=l, alpha=alpha)
    return x
```

```python
import functools

import jax
import jax.numpy as jnp
from jax import lax
from jax.experimental import pallas as pl
from jax.experimental.pallas import tpu as pltpu
from jax.experimental.pallas import tpu_sc as plsc

f32 = jnp.float32
bf16 = jnp.bfloat16
i32 = jnp.int32

RWKV_HEADS = 8
HEAD_DIM = 64
RWKV_WIDTH = RWKV_HEADS * HEAD_DIM
DECAY_RANK = 64
AAA_RANK = 64
GATE_RANK = 128
RWKV_COLS = 3 * RWKV_WIDTH + DECAY_RANK + AAA_RANK + GATE_RANK
SWA_Q_HEADS = 8
SWA_KV_HEADS = 2
SWA_GROUP = SWA_Q_HEADS // SWA_KV_HEADS
SWA_WIDTH = SWA_Q_HEADS * HEAD_DIM
SWA_KV_WIDTH = SWA_KV_HEADS * HEAD_DIM
SWA_COLS = SWA_WIDTH + 2 * SWA_KV_WIDTH
WINDOW = 128
ROPE_THETA = 10000.0
MEM_HEADS = 4
N_EXPERTS = 256
TOP_K = 8
N_GROUPS = 8
GROUP_SIZE = N_EXPERTS // N_GROUPS
TOPK_GROUPS = 4
ROUTED_SCALE = 2.5
LN_EPS = 1e-5
GN_EPS = 64e-5
NEG_INF = -1e30

LANES = 128
SUBLANES = 8
WKV_CHUNK = 64
WKV_GROUP = 4
VMEM_LIMIT = 56 * 1024 * 1024

SC_CORES = 2
SC_SUBCORES = 16
SC_LANES = 16
SC_INDEX_GROUP = 128
SC_ROW_CHUNK = 64
SC_COMBINE_ROWS = 4


def _cparams(sem):
    return pltpu.CompilerParams(dimension_semantics=sem, vmem_limit_bytes=VMEM_LIMIT)


def _const_spec(shape):
    nd = len(shape)
    return pl.BlockSpec(shape, lambda *_: (0,) * nd)


def _dot(a, b):
    return jnp.dot(a, b, preferred_element_type=f32)


def _dot_nt(a, b):
    return lax.dot_general(a, b, (((1,), (1,)), ((), ())), preferred_element_type=f32)


def _dot_tn(a, b):
    return lax.dot_general(a, b, (((0,), (0,)), ((), ())), preferred_element_type=f32)


def _split2(x):
    hi = x.astype(bf16)
    lo = (x - hi.astype(f32)).astype(bf16)
    return hi, lo


def _seg_sums(xs, seg_b):
    parts = []
    for x in xs:
        parts.extend(_split2(x))
    out = _dot(jnp.concatenate(parts, axis=0), seg_b)
    t = xs[0].shape[0]
    return [out[2 * i * t:(2 * i + 1) * t] + out[(2 * i + 1) * t:(2 * i + 2) * t] for i in range(len(xs))]


def _dot_hp(a, b):
    ah, al = _split2(a)
    bh, bl = _split2(b)
    return _dot(ah, bh) + _dot(ah, bl) + _dot(al, bh)


def _dot_exact_lhs(m_bf16, x):
    hi, lo = _split2(x)
    return _dot(m_bf16, hi) + _dot(m_bf16, lo)


def _sigmoid(x):
    return 1.0 / (1.0 + jnp.exp(-x))


def _lane_chunk(n_rows, n_chunks, c):
    return (pl.ds(c, n_rows, stride=n_chunks), slice(None))


def _layer_norm(h, g, b):
    mu = jnp.mean(h, axis=-1, keepdims=True)
    d = h - mu
    var = jnp.mean(d * d, axis=-1, keepdims=True)
    return d * lax.rsqrt(var + LN_EPS) * g + b


def _proj_body(*refs, n_out):
    x_ref = refs[0]
    w_refs = refs[1:1 + n_out]
    o_refs = refs[1 + n_out:]
    xb = x_ref[...].astype(bf16)
    for w_ref, o_ref in zip(w_refs, o_refs):
        o_ref[...] = _dot(xb, w_ref[...]).astype(o_ref.dtype)


def _proj(x, ws, out_dtypes, tile):
    n, k = x.shape
    tile = min(tile, n)
    outs = pl.pallas_call(
        functools.partial(_proj_body, n_out=len(ws)),
        out_shape=[jax.ShapeDtypeStruct((n, w.shape[1]), dt) for w, dt in zip(ws, out_dtypes)],
        grid=(n // tile,),
        in_specs=[pl.BlockSpec((tile, k), lambda i: (i, 0))] + [_const_spec(w.shape) for w in ws],
        out_specs=[pl.BlockSpec((tile, w.shape[1]), lambda i: (i, 0)) for w in ws],
        compiler_params=_cparams(("parallel",)),
        name="proj",
    )(x, *ws)
    return outs


def _wkv_chunks(chains, states, masks):
    bd_b, bd, strict, incl, eye, eye_full = masks
    c, n = chains[0][1].shape
    nch = len(chains)

    def stack(x_b):
        return jnp.where(bd_b, jnp.concatenate([x_b] * WKV_GROUP, axis=0), jnp.zeros((), bf16))

    cast = [tuple(x.astype(bf16) for x in ch[1:6]) for ch in chains]
    v_s = [stack(cb[4]) for cb in cast]
    g = [_dot_nt(jnp.concatenate([cb[0], cb[3]], axis=0), jnp.concatenate([stack(cb[1]), stack(cb[2])], axis=0))
         for cb in cast]
    l_ak = [jnp.where(strict, gi[:c, n:], 0.0).astype(bf16) for gi in g]
    m_rb = [jnp.where(incl, gi[c:, :n], 0.0).astype(bf16) for gi in g]
    m_rk = [jnp.where(incl, gi[c:, n:], 0.0).astype(bf16) for gi in g]
    x = [jnp.where(strict, gi[:c, :n], 0.0) for gi in g]
    t = [eye + xi for xi in x]
    for _ in range(5):
        xb = [xi.astype(bf16) for xi in x]
        x = [_dot(xi, stack(xi)) for xi in xb]
        t = [ti + _dot(ti.astype(bf16), stack(xi.astype(bf16))) for ti, xi in zip(t, x)]
    lakv = [_dot(l_ak[i], v_s[i]).astype(bf16) for i in range(nch)]
    au = [_dot(t[i].astype(bf16), jnp.concatenate([stack(cast[i][0]), stack(lakv[i])], axis=1))
          for i in range(nch)]
    abar = [a[:, :n].astype(bf16) for a in au]
    ubar = [a[:, n:].astype(bf16) for a in au]
    ry = [_dot(m_rb[i], jnp.concatenate([stack(abar[i]), stack(ubar[i])], axis=1)) for i in range(nch)]
    r_bar = [(chains[i][4] + ry[i][:, :n]).astype(bf16) for i in range(nch)]
    y_bar = [ry[i][:, n:] + _dot(m_rk[i], v_s[i]) for i in range(nch)]
    p = [((eye_full + jnp.where(bd, _dot_tn(abar[i], cast[i][1]), 0.0)) * chains[i][6]).astype(bf16)
         for i in range(nch)]
    q = []
    for i in range(nch):
        q_bd = jnp.where(bd, _dot_tn(jnp.concatenate([ubar[i], cast[i][4]], axis=0),
                                     jnp.concatenate([cast[i][1], cast[i][2]], axis=0)), 0.0)
        qi = q_bd[0:c]
        for h in range(1, WKV_GROUP):
            qi = qi + q_bd[h * c:(h + 1) * c]
        q.append(qi * chains[i][6])
    states = list(states)
    ys = []
    for i in range(nch):
        gi = chains[i][0]
        s_b = states[gi].astype(bf16)
        ys.append(_dot_nt(r_bar[i], stack(s_b)) + y_bar[i])
        states[gi] = _dot(s_b, p[i]) + q[i]
    return ys, states


def _rwkv_body(u_ref, mu_ref, wdec_ref, w0_ref, aup_ref, a0_ref, gup_ref, kk_ref, ka_ref, rk_ref,
               lng_ref, lnb_ref, seg_ref, tri_ref, y_ref, state_ref, carry_ref, *, tt):
    j = pl.program_id(1)

    @pl.when(j == 0)
    def _():
        state_ref[...] = jnp.zeros_like(state_ref)
        carry_ref[...] = jnp.zeros_like(carry_ref)

    w = RWKV_WIDTH
    u = u_ref[...]
    row = lax.broadcasted_iota(i32, u.shape, 0)
    prev = jnp.where(row == 0, carry_ref[0:1, :], pltpu.roll(u, 1, axis=0))
    carry_ref[0:1, :] = u[tt - 1:tt, :]
    us = u + (prev - u) * mu_ref[...]
    r = us[:, 0:w]
    k = us[:, w:2 * w]
    v = us[:, 2 * w:3 * w]
    wa = us[:, 3 * w:3 * w + DECAY_RANK + AAA_RANK]
    gd = us[:, 3 * w + DECAY_RANK + AAA_RANK:]
    z = w0_ref[...] + _dot_hp(jnp.tanh(wa), wdec_ref[...])
    softplus_neg_z = jnp.maximum(-z, 0.0) + jnp.log(1.0 + jnp.exp(-jnp.abs(z)))
    lw = -jnp.exp(-softplus_neg_z - 0.5)
    a = _sigmoid(a0_ref[...] + _dot_hp(wa, aup_ref[...]))
    gate = _dot(_sigmoid(gd).astype(bf16), gup_ref[...].astype(bf16))
    seg = seg_ref[...]
    kk = k * kk_ref[...]
    kmod = k * (1.0 + (a - 1.0) * ka_ref[...])
    kk_sq, bonus_dot = _seg_sums([kk * kk, r * kmod * rk_ref[...]], seg)
    kk = kk / jnp.maximum(jnp.sqrt(kk_sq), 1e-12)
    cum = _dot_exact_lhs(tri_ref[...], lw)
    wc = jnp.exp(cum)
    iwc = jnp.exp(-cum)
    at = -kk * jnp.exp(cum - lw)
    bt = kk * a * iwc
    kt = kmod * iwc
    rt = r * wc

    n = WKV_GROUP * HEAD_DIM
    ri = lax.broadcasted_iota(i32, (n, n), 0)
    ci = lax.broadcasted_iota(i32, (n, n), 1)
    bd = (ri // WKV_CHUNK) == (ci // HEAD_DIM)
    bd_b = jnp.where(bd, 1.0, 0.0).astype(bf16) > 0
    eye_full = jnp.where(ri == ci, 1.0, 0.0).astype(f32)
    ti = lax.broadcasted_iota(i32, (WKV_CHUNK, n), 0)
    si = lax.broadcasted_iota(i32, (WKV_CHUNK, n), 1) % WKV_CHUNK
    masks = (bd_b, bd, ti > si, ti >= si, jnp.where(ti == si, 1.0, 0.0).astype(f32), eye_full)

    n_groups = w // n
    n_chunks = tt // WKV_CHUNK
    chains = []
    for c in range(n_chunks):
        rs = slice(c * WKV_CHUNK, (c + 1) * WKV_CHUNK)
        last = (c + 1) * WKV_CHUNK - 1
        for gi in range(n_groups):
            cs = slice(gi * n, (gi + 1) * n)
            chains.append((gi, at[rs, cs], bt[rs, cs], kt[rs, cs], rt[rs, cs], v[rs, cs], wc[last:last + 1, cs]))
    ys, states = _wkv_chunks(chains, [state_ref[gi] for gi in range(n_groups)], masks)
    for gi in range(n_groups):
        state_ref[gi] = states[gi]
    y = jnp.concatenate([jnp.concatenate(ys[c * n_groups:(c + 1) * n_groups], axis=1) for c in range(n_chunks)],
                        axis=0)

    inv_n = 1.0 / HEAD_DIM
    d = y - _seg_sums([y], seg)[0] * inv_n
    var = _seg_sums([d * d], seg)[0] * inv_n
    yn = d * lax.rsqrt(var + GN_EPS) * lng_ref[...] + lnb_ref[...]
    y_ref[...] = ((yn + bonus_dot * v) * gate).astype(y_ref.dtype)


def _rwkv(u_r, mu_shift, w_decay_up, w0, a_up, a0, g_up, k_k, k_a, r_k, lnx_g, lnx_b, tt):
    b, s, cols = u_r.shape
    tt = min(tt, s)
    w = RWKV_WIDTH
    row = lambda p: p.reshape(1, -1).astype(f32)
    wdec = jnp.concatenate([w_decay_up, jnp.zeros((AAA_RANK, w), f32)], axis=0)
    aup = jnp.concatenate([jnp.zeros((DECAY_RANK, w), f32), a_up], axis=0)
    hid = jnp.arange(w) // HEAD_DIM
    seg = (hid[:, None] == hid[None, :]).astype(bf16)
    ti = jnp.arange(tt)
    tri = ((ti[:, None] // WKV_CHUNK == ti[None, :] // WKV_CHUNK) & (ti[:, None] >= ti[None, :])).astype(bf16)
    params = [row(mu_shift), wdec, row(w0), aup, row(a0), g_up, row(k_k), row(k_a), row(r_k), row(lnx_g),
              row(lnx_b), seg, tri]
    n = WKV_GROUP * HEAD_DIM
    return pl.pallas_call(
        functools.partial(_rwkv_body, tt=tt),
        out_shape=jax.ShapeDtypeStruct((b, s, w), bf16),
        grid=(b, s // tt),
        in_specs=[pl.BlockSpec((None, tt, cols), lambda bi, j: (bi, j, 0))] + [_const_spec(p.shape) for p in params],
        out_specs=pl.BlockSpec((None, tt, w), lambda bi, j: (bi, j, 0)),
        scratch_shapes=[pltpu.VMEM((w // n, HEAD_DIM, n), f32), pltpu.VMEM((SUBLANES, cols), f32)],
        compiler_params=_cparams(("parallel", "arbitrary")),
        name="rwkv7",
    )(u_r, *params)


def _swa_body(sink_ref, u_ref, pos_ref, invf_ref, o_ref, kprev_ref, vprev_ref):
    j = pl.program_id(1)

    @pl.when(j == 0)
    def _():
        kprev_ref[...] = jnp.zeros_like(kprev_ref)
        vprev_ref[...] = jnp.zeros_like(vprev_ref)

    wq = SWA_WIDTH
    kvw = SWA_KV_WIDTH
    u = u_ref[...]
    ang = pos_ref[...].astype(f32) * invf_ref[...]
    cos = jnp.cos(ang)
    sin = jnp.sin(ang)

    def rope(x, c, s):
        n = x.shape[1]
        lane = lax.broadcasted_iota(i32, x.shape, 1)
        half = HEAD_DIM // 2
        rot = jnp.where((lane % HEAD_DIM) < half, -pltpu.roll(x, n - half, axis=1), pltpu.roll(x, half, axis=1))
        return x * c + rot * s

    q = rope(u[:, :wq] * (HEAD_DIM ** -0.5), jnp.concatenate([cos] * (wq // LANES), axis=1),
             jnp.concatenate([sin] * (wq // LANES), axis=1))
    k_cur = rope(u[:, wq:wq + kvw], cos, sin)
    v_cur = u[:, wq + kvw:]
    k_prev = kprev_ref[...]
    v_prev = vprev_ref[...]
    kprev_ref[...] = k_cur
    vprev_ref[...] = v_cur

    gw = SWA_GROUP * HEAD_DIM
    rows = SWA_GROUP * WINDOW
    lane_kv = lax.broadcasted_iota(i32, (WINDOW, kvw), 1)

    def rep(x, gi):
        sw = pltpu.roll(x, HEAD_DIM, axis=1)
        one = jnp.where((lane_kv // HEAD_DIM) == gi, x, sw)
        return jnp.concatenate([one] * (gw // kvw), axis=1).astype(bf16)

    ri = lax.broadcasted_iota(i32, (rows, WINDOW), 0)
    ci = lax.broadcasted_iota(i32, (rows, WINDOW), 1)
    t_idx = ri % WINDOW
    mask_prev = ci > t_idx + jnp.where(j > 0, 0, WINDOW)
    mask_cur = ci <= t_idx
    rb = lax.broadcasted_iota(i32, (rows, 1), 0) // WINDOW
    rbo = lax.broadcasted_iota(i32, (rows, gw), 0) // WINDOW
    cbo = lax.broadcasted_iota(i32, (rows, gw), 1) // HEAD_DIM
    bd = rbo == cbo
    outs = []
    for gi in range(SWA_KV_HEADS):
        qg = q[:, gi * gw:(gi + 1) * gw]
        q_bd = jnp.where(bd, jnp.concatenate([qg] * SWA_GROUP, axis=0), 0.0).astype(bf16)
        s_prev = jnp.where(mask_prev, _dot_nt(q_bd, rep(k_prev, gi)), NEG_INF)
        s_cur = jnp.where(mask_cur, _dot_nt(q_bd, rep(k_cur, gi)), NEG_INF)
        sink = jnp.zeros((rows, 1), f32)
        for h in range(SWA_GROUP):
            sink = jnp.where(rb == h, sink_ref[gi * SWA_GROUP + h], sink)
        m = jnp.maximum(jnp.maximum(jnp.max(s_prev, axis=-1, keepdims=True),
                                    jnp.max(s_cur, axis=-1, keepdims=True)), sink)
        p_prev = jnp.exp(s_prev - m)
        p_cur = jnp.exp(s_cur - m)
        denom = jnp.sum(p_prev, axis=-1, keepdims=True) + jnp.sum(p_cur, axis=-1, keepdims=True) + jnp.exp(sink - m)
        o_bd = _dot(p_prev.astype(bf16), rep(v_prev, gi)) + _dot(p_cur.astype(bf16), rep(v_cur, gi))
        o_bd = jnp.where(bd, o_bd / denom, 0.0)
        og = o_bd[0:WINDOW]
        for h in range(1, SWA_GROUP):
            og = og + o_bd[h * WINDOW:(h + 1) * WINDOW]
        outs.append(og)
    o_ref[...] = jnp.concatenate(outs, axis=1).astype(o_ref.dtype)


def _swa(u_s, positions, sinks):
    b, s, cols = u_s.shape
    half = HEAD_DIM // 2
    inv_freq = ROPE_THETA ** (-jnp.arange(0, HEAD_DIM, 2, dtype=f32) / HEAD_DIM)
    invf = jnp.tile(inv_freq, LANES // half).reshape(1, LANES)
    pos = positions.reshape(b, s, 1).astype(i32)
    return pl.pallas_call(
        _swa_body,
        out_shape=jax.ShapeDtypeStruct((b, s, SWA_WIDTH), bf16),
        grid=(b, s // WINDOW),
        in_specs=[pl.BlockSpec(memory_space=pltpu.SMEM),
                  pl.BlockSpec((None, WINDOW, cols), lambda bi, j: (bi, j, 0)),
                  pl.BlockSpec((None, WINDOW, 1), lambda bi, j: (bi, j, 0)),
                  _const_spec((1, LANES))],
        out_specs=pl.BlockSpec((None, WINDOW, SWA_WIDTH), lambda bi, j: (bi, j, 0)),
        scratch_shapes=[pltpu.VMEM((WINDOW, SWA_KV_WIDTH), f32), pltpu.VMEM((WINDOW, SWA_KV_WIDTH), f32)],
        compiler_params=_cparams(("parallel", "arbitrary")),
        name="swa",
    )(sinks.astype(f32), u_s, pos, invf)


def _mix_out_body(ya_ref, yb_ref, wa_ref, wb_ref, x_ref, g_ref, b_ref, o_ref, *, alpha):
    mix = _dot(ya_ref[...], wa_ref[...]) + _dot(yb_ref[...], wb_ref[...])
    o_ref[...] = _layer_norm(alpha * x_ref[...] + mix, g_ref[...], b_ref[...])


def _mix_out(ya, yb, w_o, x, g, b, alpha, tile):
    n, d = x.shape
    tile = min(tile, n)
    wa = w_o[:ya.shape[1]].astype(bf16)
    wb = w_o[ya.shape[1]:].astype(bf16)
    rows = lambda width: pl.BlockSpec((tile, width), lambda i: (i, 0))
    return pl.pallas_call(
        functools.partial(_mix_out_body, alpha=alpha),
        out_shape=jax.ShapeDtypeStruct((n, d), f32),
        grid=(n // tile,),
        in_specs=[rows(ya.shape[1]), rows(yb.shape[1]), _const_spec(wa.shape), _const_spec(wb.shape), rows(d),
                  _const_spec((1, d)), _const_spec((1, d))],
        out_specs=rows(d),
        compiler_params=_cparams(("parallel",)),
        name="mix_out_ln1",
    )(ya, yb, wa, wb, x, g.reshape(1, d), b.reshape(1, d))


def _xattn_body(x_ref, kv_ref, wq_ref, wo_ref, g_ref, b_ref, o_ref, o3_ref, *, alpha):
    x = x_ref[...]
    d = x.shape[1]
    hd = d // MEM_HEADS
    q = _dot(x.astype(bf16), wq_ref[...]) * (hd ** -0.5)
    kv = kv_ref[...]
    outs = []
    for h in range(MEM_HEADS):
        qh = q[:, h * hd:(h + 1) * hd].astype(bf16)
        kh = kv[:, h * hd:(h + 1) * hd]
        vh = kv[:, d + h * hd:d + (h + 1) * hd]
        s = _dot_nt(qh, kh)
        p = jnp.exp(s - jnp.max(s, axis=-1, keepdims=True))
        l = jnp.sum(p, axis=-1, keepdims=True)
        outs.append(_dot(p.astype(bf16), vh) / l)
    o = jnp.concatenate(outs, axis=1)
    xa = _dot(o.astype(bf16), wo_ref[...])
    y = _layer_norm(alpha * x + xa, g_ref[...], b_ref[...])
    o_ref[...] = y
    nc = d // LANES
    for c in range(nc):
        o3_ref[_lane_chunk(x.shape[0], nc, c)] = y[:, c * LANES:(c + 1) * LANES]


def _xattn(x1, kv, wm_q, wm_o, g, b, alpha, tile):
    bsz, s, d = x1.shape
    m = kv.shape[1]
    tile = min(tile, s)
    nj = s // tile
    wq = wm_q.astype(bf16)
    wo = wm_o.astype(bf16)
    return pl.pallas_call(
        functools.partial(_xattn_body, alpha=alpha),
        out_shape=[jax.ShapeDtypeStruct((bsz, s, d), f32), jax.ShapeDtypeStruct((bsz * s * (d // LANES), LANES), f32)],
        grid=(bsz, nj),
        in_specs=[pl.BlockSpec((None, tile, d), lambda bi, j: (bi, j, 0)),
                  pl.BlockSpec((None, m, 2 * d), lambda bi, j: (bi, 0, 0)),
                  _const_spec(wq.shape), _const_spec(wo.shape), _const_spec((1, d)), _const_spec((1, d))],
        out_specs=[pl.BlockSpec((None, tile, d), lambda bi, j: (bi, j, 0)),
                   pl.BlockSpec((tile * (d // LANES), LANES), lambda bi, j: (bi * nj + j, 0))],
        compiler_params=_cparams(("parallel", "parallel")),
        name="mem_xattn_ln2",
    )(x1, kv, wq, wo, g.reshape(1, d), b.reshape(1, d))


def _router_body(x_ref, wt_ref, bias_ref, upper_ref, e_ref, g_ref, r_ref, cnt_out_ref, cnt_ref, *, t):
    @pl.when(pl.program_id(0) == 0)
    def _():
        cnt_ref[...] = jnp.zeros_like(cnt_ref)

    xh, xl = _split2(x_ref[...])
    wh, wl = _split2(wt_ref[...])
    logits = _dot_nt(wh, xh) + _dot_nt(wh, xl) + _dot_nt(wl, xh)
    scores = _sigmoid(logits)
    biased = scores + bias_ref[...][:, 0:1]
    ne = N_EXPERTS
    neg = -jnp.inf

    def top1(vals):
        rows = lax.broadcasted_iota(i32, vals.shape, 0).astype(f32)
        m = jnp.max(vals, axis=0, keepdims=True)
        idx = jnp.min(jnp.where(vals == m, rows, float(vals.shape[0])), axis=0, keepdims=True)
        return m, idx, rows == idx

    gscores = []
    for gi in range(N_GROUPS):
        blk = biased[gi * GROUP_SIZE:(gi + 1) * GROUP_SIZE, :]
        m1, _, hit = top1(blk)
        m2 = jnp.max(jnp.where(hit, neg, blk), axis=0, keepdims=True)
        gscores.append(m1 + m2)
    gs = jnp.concatenate(gscores, axis=0)
    gsel = jnp.zeros(gs.shape, f32)
    for _ in range(TOPK_GROUPS):
        _, _, hit = top1(gs)
        gsel = jnp.where(hit, 1.0, gsel)
        gs = jnp.where(hit, neg, gs)
    emask = jnp.concatenate(
        [jnp.broadcast_to(gsel[gi:gi + 1, :], (GROUP_SIZE, t)) for gi in range(N_GROUPS)], axis=0) > 0.5
    cand = jnp.where(emask, biased, NEG_INF)
    idxs, sels = [], []
    chosen = jnp.zeros((ne, t), f32)
    for _ in range(TOP_K):
        _, idx, hit = top1(cand)
        idxs.append(idx)
        sels.append(jnp.sum(jnp.where(hit, scores, 0.0), axis=0, keepdims=True))
        chosen = chosen + jnp.where(hit, 1.0, 0.0)
        cand = jnp.where(hit, neg, cand)
    sel = jnp.concatenate(sels, axis=0)
    g_ref[...] = sel / jnp.sum(sel, axis=0, keepdims=True) * ROUTED_SCALE
    e_ref[...] = jnp.concatenate(idxs, axis=0).astype(i32)
    before = _dot(chosen.astype(bf16), upper_ref[...]) + cnt_ref[...][:, 0:1]
    rows = lax.broadcasted_iota(i32, (ne, t), 0).astype(f32)
    ranks = [jnp.sum(jnp.where(rows == idx, before, 0.0), axis=0, keepdims=True) for idx in idxs]
    r_ref[...] = jnp.concatenate(ranks, axis=0).astype(i32)
    cnt_ref[...] = cnt_ref[...] + jnp.sum(chosen, axis=1, keepdims=True)
    cnt_out_ref[...] = cnt_ref[...].astype(i32)


def _router(x2, w_router, router_bias, tile):
    n, d = x2.shape
    ne = N_EXPERTS
    t = min(tile, n)
    wt = w_router.T
    bias = jnp.broadcast_to(router_bias.reshape(ne, 1).astype(f32), (ne, LANES))
    ti = jnp.arange(t)
    upper = (ti[:, None] < ti[None, :]).astype(bf16)
    cols = pl.BlockSpec((TOP_K, t), lambda i: (0, i))
    e_t, g_t, r_t, cnt = pl.pallas_call(
        functools.partial(_router_body, t=t),
        out_shape=[jax.ShapeDtypeStruct((TOP_K, n), i32), jax.ShapeDtypeStruct((TOP_K, n), f32),
                   jax.ShapeDtypeStruct((TOP_K, n), i32), jax.ShapeDtypeStruct((ne, LANES), i32)],
        grid=(n // t,),
        in_specs=[pl.BlockSpec((t, d), lambda i: (i, 0)), _const_spec((ne, d)), _const_spec((ne, LANES)),
                  _const_spec((t, t))],
        out_specs=[cols, cols, cols, _const_spec((ne, LANES))],
        scratch_shapes=[pltpu.VMEM((ne, LANES), f32)],
        compiler_params=_cparams(("arbitrary",)),
        name="router",
    )(x2, wt, bias, upper)
    return e_t, g_t, r_t, cnt[:, 0]


def _dispatch_body(e_ref, r_ref, st_ref, x_ref, o_ref, sem, *, t):
    def issue(ti, c):
        for kk in range(TOP_K):
            slot = st_ref[e_ref[kk, ti]] + r_ref[kk, ti]
            pltpu.make_async_copy(x_ref.at[ti], o_ref.at[slot], sem).start()
        return c

    lax.fori_loop(0, t, issue, 0)
    for _ in range(TOP_K):
        pltpu.make_async_copy(x_ref, o_ref.at[pl.ds(0, t)], sem).wait()


def _dispatch(x3d, e_t, r_t, starts, tile):
    n = x3d.shape[0]
    t = min(tile, n)
    smem_rows = pl.BlockSpec((TOP_K, t), lambda i: (0, i), memory_space=pltpu.SMEM)
    return pl.pallas_call(
        functools.partial(_dispatch_body, t=t),
        out_shape=jax.ShapeDtypeStruct((n * TOP_K,) + x3d.shape[1:], x3d.dtype),
        grid=(n // t,),
        in_specs=[smem_rows, smem_rows, pl.BlockSpec(memory_space=pltpu.SMEM),
                  pl.BlockSpec((t,) + x3d.shape[1:], lambda i: (i, 0, 0))],
        out_specs=pl.BlockSpec(memory_space=pl.ANY),
        scratch_shapes=[pltpu.SemaphoreType.DMA],
        compiler_params=_cparams(("arbitrary",)),
        name="moe_dispatch",
    )(e_t, r_t, starts, x3d)


def _dispatch_sc(x3d, e_t, r_t, starts):
    n, nc, lanes = x3d.shape
    workers = SC_CORES * SC_SUBCORES
    per_w = n // workers
    assert n % (workers * SC_INDEX_GROUP) == 0 and SC_INDEX_GROUP % SC_ROW_CHUNK == 0
    mesh = plsc.VectorSubcoreMesh(core_axis_name="c", subcore_axis_name="s")

    @functools.partial(
        pl.kernel, mesh=mesh,
        out_type=jax.ShapeDtypeStruct((n * TOP_K, nc, lanes), x3d.dtype),
        scratch_types=[
            pltpu.VMEM((SC_ROW_CHUNK, nc, lanes), x3d.dtype),
            pltpu.VMEM((TOP_K, SC_INDEX_GROUP), i32),
            pltpu.VMEM((TOP_K, SC_INDEX_GROUP), i32),
            pltpu.VMEM((TOP_K, SC_ROW_CHUNK), i32),
            pltpu.VMEM((N_EXPERTS,), i32),
            pltpu.SemaphoreType.DMA,
        ],
        compiler_params=pltpu.CompilerParams(use_tc_tiling_on_sc=True, needs_layout_passes=False),
    )
    def dispatch(x_hbm, e_hbm, r_hbm, st_hbm, o_hbm, rows_v, e_v, r_v, slot_v, st_v, sem):
        wid = lax.axis_index("s") * SC_CORES + lax.axis_index("c")
        pltpu.sync_copy(st_hbm, st_v)

        @pl.loop(0, per_w // SC_INDEX_GROUP)
        def _(gi):
            base = wid * per_w + gi * SC_INDEX_GROUP
            pltpu.sync_copy(e_hbm.at[:, pl.ds(base, SC_INDEX_GROUP)], e_v)
            pltpu.sync_copy(r_hbm.at[:, pl.ds(base, SC_INDEX_GROUP)], r_v)
            for h in range(SC_INDEX_GROUP // SC_ROW_CHUNK):
                off = h * SC_ROW_CHUNK
                pltpu.sync_copy(x_hbm.at[pl.ds(base + off, SC_ROW_CHUNK)], rows_v)
                for kk in range(TOP_K):
                    for j in range(SC_ROW_CHUNK // SC_LANES):
                        src = pl.ds(off + j * SC_LANES, SC_LANES)
                        start = plsc.load_gather(st_v, [e_v[kk, src]])
                        slot_v[kk, pl.ds(j * SC_LANES, SC_LANES)] = r_v[kk, src] + start
                copies = [pltpu.async_copy(rows_v, o_hbm.at[slot_v.at[kk]], sem) for kk in range(TOP_K)]
                for cp in copies:
                    cp.wait()

    return dispatch(x3d, e_t, r_t, starts)


def _gmm_body(gid_ref, tid_ref, lo_ref, hi_ref, first_ref, newg_ref, x_ref, wg_ref, wu_ref, wd_ref, o_ref,
              wg_b, wu_b, wd_b, *, tm):
    v = pl.program_id(0)
    lo = lo_ref[v]
    hi = hi_ref[v]
    row0 = tid_ref[v] * tm
    nc = wg_ref.shape[0] // LANES
    full = (lo <= row0) & (hi >= row0 + tm)

    @pl.when(newg_ref[v] == 1)
    def _():
        wg_b[...] = wg_ref[...].astype(bf16)
        wu_b[...] = wu_ref[...].astype(bf16)
        wd_b[...] = wd_ref[...].astype(bf16)

    def ffn():
        x = jnp.concatenate([x_ref[_lane_chunk(tm, nc, c)] for c in range(nc)], axis=1).astype(bf16)
        hg = _dot(x, wg_b[...])
        h = hg * _sigmoid(hg) * _dot(x, wu_b[...])
        return _dot(h.astype(bf16), wd_b[...])

    @pl.when(full)
    def _():
        y = ffn()
        for c in range(nc):
            o_ref[_lane_chunk(tm, nc, c)] = y[:, c * LANES:(c + 1) * LANES]

    @pl.when(jnp.logical_not(full) & (hi > lo))
    def _():
        @pl.when(first_ref[v] == 1)
        def _():
            o_ref[...] = jnp.zeros_like(o_ref)

        rows = row0 + lax.broadcasted_iota(i32, (tm, 1), 0)
        mask = (rows >= lo) & (rows < hi)
        y = ffn()
        for c in range(nc):
            idx = _lane_chunk(tm, nc, c)
            o_ref[idx] = jnp.where(mask, y[:, c * LANES:(c + 1) * LANES], o_ref[idx])


def _gmm(xs, we_gate, we_up, we_down, layer, counts, tm):
    _, ne, d, de = we_gate.shape
    nc = d // LANES
    nk = xs.shape[0] // nc
    tm = min(tm, nk)
    n_tiles = nk // tm
    n_visits = n_tiles + ne - 1
    ends = jnp.cumsum(counts)
    starts = ends - counts
    tile_lo = starts // tm
    n_touch = jnp.where(counts > 0, (ends - 1) // tm - tile_lo + 1, 0)
    vis_end = jnp.cumsum(n_touch)
    vis_start = vis_end - n_touch
    vi = jnp.arange(n_visits, dtype=i32)
    valid = vi < vis_end[-1]
    gid = jnp.minimum(jnp.sum((vis_end[None, :] <= vi[:, None]).astype(i32), axis=1), ne - 1)
    onehot = gid[:, None] == jnp.arange(ne, dtype=i32)[None, :]
    pick = lambda table: jnp.sum(jnp.where(onehot, table[None, :], 0), axis=1)
    tid = jnp.where(valid, pick(tile_lo) + vi - pick(vis_start), n_tiles - 1).astype(i32)
    lo = jnp.where(valid, pick(starts), 0).astype(i32)
    hi = jnp.where(valid, pick(ends), 0).astype(i32)
    one = jnp.ones((1,), i32)
    first = jnp.concatenate([one, (tid[1:] != tid[:-1]).astype(i32)])
    newg = jnp.concatenate([one, (gid[1:] != gid[:-1]).astype(i32)])
    wspec = lambda shape: pl.BlockSpec((None, None) + shape, lambda v, g, *_: (layer, g[v], 0, 0))
    rows = pl.BlockSpec((tm * nc, LANES), lambda v, g, t, *_: (t[v], 0))
    return pl.pallas_call(
        functools.partial(_gmm_body, tm=tm),
        out_shape=jax.ShapeDtypeStruct(xs.shape, f32),
        grid_spec=pltpu.PrefetchScalarGridSpec(
            num_scalar_prefetch=6, grid=(n_visits,),
            in_specs=[rows, wspec((d, de)), wspec((d, de)), wspec((de, d))],
            out_specs=rows,
            scratch_shapes=[pltpu.VMEM((d, de), bf16), pltpu.VMEM((d, de), bf16), pltpu.VMEM((de, d), bf16)]),
        compiler_params=_cparams(("arbitrary",)),
        name="moe_experts",
    )(gid, tid, lo, hi, first, newg, xs, we_gate, we_up, we_down)


def _combine_body(e_ref, r_ref, g_ref, st_ref, y_ref, o_ref, buf, sem, *, t):
    def issue(ti, c):
        for kk in range(TOP_K):
            slot = st_ref[e_ref[kk, ti]] + r_ref[kk, ti]
            pltpu.make_async_copy(y_ref.at[slot], buf.at[kk, ti], sem).start()
        return c

    lax.fori_loop(0, t, issue, 0)
    for kk in range(TOP_K):
        pltpu.make_async_copy(y_ref.at[pl.ds(0, t)], buf.at[kk], sem).wait()

    def reduce(ti, c):
        acc = g_ref[0, ti] * buf[0, ti]
        for kk in range(1, TOP_K):
            acc = acc + g_ref[kk, ti] * buf[kk, ti]
        o_ref[ti] = acc
        return c

    lax.fori_loop(0, t, reduce, 0)


def _combine(ys, e_t, r_t, g_t, starts, n, tile):
    t = min(tile, n)
    tail = ys.shape[1:]
    smem_rows = pl.BlockSpec((TOP_K, t), lambda i: (0, i), memory_space=pltpu.SMEM)
    return pl.pallas_call(
        functools.partial(_combine_body, t=t),
        out_shape=jax.ShapeDtypeStruct((n,) + tail, f32),
        grid=(n // t,),
        in_specs=[smem_rows, smem_rows, smem_rows, pl.BlockSpec(memory_space=pltpu.SMEM),
                  pl.BlockSpec(memory_space=pl.ANY)],
        out_specs=pl.BlockSpec((t,) + tail, lambda i: (i, 0, 0)),
        scratch_shapes=[pltpu.VMEM((TOP_K, t) + tail, f32), pltpu.SemaphoreType.DMA],
        compiler_params=_cparams(("arbitrary",)),
        name="moe_combine",
    )(e_t, r_t, g_t, starts, ys)


def _combine_sc(ys3d, e_t, r_t, g_t, starts, n):
    _, nc, lanes = ys3d.shape
    workers = SC_CORES * SC_SUBCORES
    per_w = n // workers
    tc = SC_LANES
    kh = SC_COMBINE_ROWS
    assert n % (workers * SC_INDEX_GROUP) == 0 and TOP_K % kh == 0
    mesh = plsc.VectorSubcoreMesh(core_axis_name="c", subcore_axis_name="s")

    @functools.partial(
        pl.kernel, mesh=mesh,
        out_type=jax.ShapeDtypeStruct((n, nc, lanes), f32),
        scratch_types=[
            pltpu.VMEM((kh, tc, nc, lanes), f32),
            pltpu.VMEM((tc, nc, lanes), f32),
            pltpu.VMEM((TOP_K, SC_INDEX_GROUP), i32),
            pltpu.VMEM((TOP_K, SC_INDEX_GROUP), i32),
            pltpu.VMEM((TOP_K, SC_INDEX_GROUP), f32),
            pltpu.VMEM((TOP_K, tc), i32),
            pltpu.VMEM((N_EXPERTS,), i32),
            pltpu.SemaphoreType.DMA,
        ],
        compiler_params=pltpu.CompilerParams(use_tc_tiling_on_sc=True, needs_layout_passes=False),
    )
    def combine(y_hbm, e_hbm, r_hbm, g_hbm, st_hbm, o_hbm, bufs, acc_v, e_v, r_v, g_v, slot_v, st_v, sem):
        wid = lax.axis_index("s") * SC_CORES + lax.axis_index("c")
        pltpu.sync_copy(st_hbm, st_v)

        @pl.loop(0, per_w // SC_INDEX_GROUP)
        def _(gi):
            base = wid * per_w + gi * SC_INDEX_GROUP
            pltpu.sync_copy(e_hbm.at[:, pl.ds(base, SC_INDEX_GROUP)], e_v)
            pltpu.sync_copy(r_hbm.at[:, pl.ds(base, SC_INDEX_GROUP)], r_v)
            pltpu.sync_copy(g_hbm.at[:, pl.ds(base, SC_INDEX_GROUP)], g_v)

            @pl.loop(0, SC_INDEX_GROUP // tc)
            def _(h):
                off = pl.multiple_of(h * tc, tc)
                src = pl.ds(off, tc)
                for kk in range(TOP_K):
                    slot_v[kk, :] = r_v[kk, src] + plsc.load_gather(st_v, [e_v[kk, src]])
                for part in range(TOP_K // kh):
                    copies = [pltpu.async_copy(y_hbm.at[slot_v.at[part * kh + q]], bufs.at[q], sem)
                              for q in range(kh)]
                    for cp in copies:
                        cp.wait()

                    @pl.loop(0, tc)
                    def _(t):
                        tok = jnp.full((SC_LANES,), off + t, i32)
                        gate = [plsc.load_gather(g_v, [jnp.full((SC_LANES,), part * kh + q, i32), tok])
                                for q in range(kh)]

                        @pl.loop(0, nc)
                        def _(ci):
                            for l in range(lanes // SC_LANES):
                                sl = pl.ds(l * SC_LANES, SC_LANES)
                                v = gate[0] * bufs[0, t, ci, sl]
                                for q in range(1, kh):
                                    v = v + gate[q] * bufs[q, t, ci, sl]
                                if part == 0:
                                    acc_v[t, ci, sl] = v
                                else:
                                    acc_v[t, ci, sl] = acc_v[t, ci, sl] + v

                pltpu.sync_copy(acc_v, o_hbm.at[pl.ds(base + off, tc)])

    return combine(ys3d, e_t, r_t, g_t, starts)


def _ffn_out_body(x_ref, r_ref, wg_ref, wu_ref, wd_ref, g_ref, b_ref, o_ref, *, alpha):
    x = x_ref[...]
    xb = x.astype(bf16)
    hg = _dot(xb, wg_ref[...])
    h = hg * _sigmoid(hg) * _dot(xb, wu_ref[...])
    shared = _dot(h.astype(bf16), wd_ref[...])
    nc = x.shape[1] // LANES
    routed = jnp.concatenate([r_ref[_lane_chunk(x.shape[0], nc, c)] for c in range(nc)], axis=1)
    o_ref[...] = _layer_norm(alpha * x + routed + shared, g_ref[...], b_ref[...])


def _ffn_out(x2, routed_rows, ws_gate, ws_up, ws_down, g, b, alpha, tile):
    n, d = x2.shape
    tile = min(tile, n)
    wg, wu, wd = ws_gate.astype(bf16), ws_up.astype(bf16), ws_down.astype(bf16)
    rows = pl.BlockSpec((tile, d), lambda i: (i, 0))
    return pl.pallas_call(
        functools.partial(_ffn_out_body, alpha=alpha),
        out_shape=jax.ShapeDtypeStruct((n, d), f32),
        grid=(n // tile,),
        in_specs=[rows, pl.BlockSpec((tile * (d // LANES), LANES), lambda i: (i, 0)), _const_spec(wg.shape),
                  _const_spec(wu.shape), _const_spec(wd.shape), _const_spec((1, d)), _const_spec((1, d))],
        out_specs=rows,
        compiler_params=_cparams(("parallel",)),
        name="ffn_out_ln3",
    )(x2, routed_rows, wg, wu, wd, g.reshape(1, d), b.reshape(1, d))


def _layer(x, mem, positions, w_in, mu_shift, w_decay_up, w0, a_up, a0, g_up, k_k, k_a, r_k, lnx_g, lnx_b, sinks,
           w_o, ln1_g, ln1_b, wm_q, wm_kv, wm_o, ln2_g, ln2_b, w_router, router_bias, we_gate, we_up, we_down,
           ws_gate, ws_up, ws_down, ln3_g, ln3_b, *, layer, alpha):
    b, s, d = x.shape
    n = b * s
    xf = x.reshape(n, d)
    w_in_b = w_in.astype(bf16)
    u_r, u_s = _proj(xf, [w_in_b[:, :RWKV_COLS], w_in_b[:, RWKV_COLS:]], [f32, f32], tile=512)
    y_r = _rwkv(u_r.reshape(b, s, RWKV_COLS), mu_shift, w_decay_up, w0, a_up, a0, g_up, k_k, k_a, r_k, lnx_g, lnx_b,
                tt=256)
    y_s = _swa(u_s.reshape(b, s, SWA_COLS), positions, sinks)
    x1 = _mix_out(y_r.reshape(n, RWKV_WIDTH), y_s.reshape(n, SWA_WIDTH), w_o, xf, ln1_g, ln1_b, alpha, tile=512)
    m = mem.shape[1]
    (kv,) = _proj(mem.reshape(b * m, d), [wm_kv.astype(bf16)], [bf16], tile=512)
    x2, x2_rows = _xattn(x1.reshape(b, s, d), kv.reshape(b, m, 2 * d), wm_q, wm_o, ln2_g, ln2_b, alpha, tile=512)
    x2 = x2.reshape(n, d)
    e_t, g_t, r_t, counts = _router(x2, w_router, router_bias, tile=512)
    starts = (jnp.cumsum(counts) - counts).astype(i32)
    nc = d // LANES
    xs = _dispatch_sc(x2_rows.reshape(n, nc, LANES), e_t, r_t, starts)
    ys = _gmm(xs.reshape(n * TOP_K * nc, LANES), we_gate, we_up, we_down, layer, counts, tm=512)
    routed = _combine_sc(ys.reshape(n * TOP_K, nc, LANES), e_t, r_t, g_t, starts, n)
    x3 = _ffn_out(x2, routed.reshape(n * nc, LANES), ws_gate, ws_up, ws_down, ln3_g, ln3_b, alpha, tile=512)
    return x3.reshape(b, s, d)


def kernel(x, mem, positions, w_in, mu_shift, w_decay_up, w0, a_up, a0, g_up, k_k, k_a, r_k, lnx_g, lnx_b, sinks, w_o, ln1_g, ln1_b, wm_q, wm_kv, wm_o, ln2_g, ln2_b, w_router, router_bias, we_gate, we_up, we_down, ws_gate, ws_up, ws_down, ln3_g, ln3_b):
    depth = w_in.shape[0]
    alpha = (2 * depth) ** 0.25
    for l in range(depth):
        x = _layer(x, mem, positions, w_in[l], mu_shift[l], w_decay_up[l], w0[l], a_up[l], a0[l], g_up[l], k_k[l],
                   k_a[l], r_k[l], lnx_g[l], lnx_b[l], sinks[l], w_o[l], ln1_g[l], ln1_b[l], wm_q[l], wm_kv[l],
                   wm_o[l], ln2_g[l], ln2_b[l], w_router[l], router_bias[l], we_gate, we_up, we_down,
                   ws_gate[l], ws_up[l], ws_down[l], ln3_g[l], ln3_b[l], layer=l, alpha=alpha)
    return x
```

```python
import functools

import jax
import jax.numpy as jnp
from jax import lax
from jax.experimental import pallas as pl
from jax.experimental.pallas import tpu as pltpu
from jax.experimental.pallas import tpu_sc as plsc

f32 = jnp.float32
bf16 = jnp.bfloat16
i32 = jnp.int32

RWKV_HEADS = 8
HEAD_DIM = 64
RWKV_WIDTH = RWKV_HEADS * HEAD_DIM
DECAY_RANK = 64
AAA_RANK = 64
GATE_RANK = 128
RWKV_COLS = 3 * RWKV_WIDTH + DECAY_RANK + AAA_RANK + GATE_RANK
SWA_Q_HEADS = 8
SWA_KV_HEADS = 2
SWA_GROUP = SWA_Q_HEADS // SWA_KV_HEADS
SWA_WIDTH = SWA_Q_HEADS * HEAD_DIM
SWA_KV_WIDTH = SWA_KV_HEADS * HEAD_DIM
SWA_COLS = SWA_WIDTH + 2 * SWA_KV_WIDTH
WINDOW = 128
ROPE_THETA = 10000.0
MEM_HEADS = 4
N_EXPERTS = 256
TOP_K = 8
N_GROUPS = 8
GROUP_SIZE = N_EXPERTS // N_GROUPS
TOPK_GROUPS = 4
ROUTED_SCALE = 2.5
LN_EPS = 1e-5
GN_EPS = 64e-5
NEG_INF = -1e30

LANES = 128
SUBLANES = 8
WKV_CHUNK = 64
WKV_GROUP = 4
VMEM_LIMIT = 56 * 1024 * 1024

SC_CORES = 2
SC_SUBCORES = 16
SC_LANES = 16
SC_INDEX_GROUP = 128
SC_ROW_CHUNK = 64


def _cparams(sem):
    return pltpu.CompilerParams(dimension_semantics=sem, vmem_limit_bytes=VMEM_LIMIT)


def _const_spec(shape):
    nd = len(shape)
    return pl.BlockSpec(shape, lambda *_: (0,) * nd)


def _dot(a, b):
    return jnp.dot(a, b, preferred_element_type=f32)


def _dot_nt(a, b):
    return lax.dot_general(a, b, (((1,), (1,)), ((), ())), preferred_element_type=f32)


def _dot_tn(a, b):
    return lax.dot_general(a, b, (((0,), (0,)), ((), ())), preferred_element_type=f32)


def _split2(x):
    hi = x.astype(bf16)
    lo = (x - hi.astype(f32)).astype(bf16)
    return hi, lo


def _seg_sums(xs, seg_b):
    parts = []
    for x in xs:
        parts.extend(_split2(x))
    out = _dot(jnp.concatenate(parts, axis=0), seg_b)
    t = xs[0].shape[0]
    return [out[2 * i * t:(2 * i + 1) * t] + out[(2 * i + 1) * t:(2 * i + 2) * t] for i in range(len(xs))]


def _dot_hp(a, b):
    ah, al = _split2(a)
    bh, bl = _split2(b)
    return _dot(ah, bh) + _dot(ah, bl) + _dot(al, bh)


def _dot_exact_lhs(m_bf16, x):
    hi, lo = _split2(x)
    return _dot(m_bf16, hi) + _dot(m_bf16, lo)


def _sigmoid(x):
    return 1.0 / (1.0 + jnp.exp(-x))


def _lane_chunk(n_rows, n_chunks, c):
    return (pl.ds(c, n_rows, stride=n_chunks), slice(None))


def _layer_norm(h, g, b):
    mu = jnp.mean(h, axis=-1, keepdims=True)
    d = h - mu
    var = jnp.mean(d * d, axis=-1, keepdims=True)
    return d * lax.rsqrt(var + LN_EPS) * g + b


def _proj_body(*refs, n_out):
    x_ref = refs[0]
    w_refs = refs[1:1 + n_out]
    o_refs = refs[1 + n_out:]
    xb = x_ref[...].astype(bf16)
    for w_ref, o_ref in zip(w_refs, o_refs):
        o_ref[...] = _dot(xb, w_ref[...]).astype(o_ref.dtype)


def _proj(x, ws, out_dtypes, tile):
    n, k = x.shape
    tile = min(tile, n)
    outs = pl.pallas_call(
        functools.partial(_proj_body, n_out=len(ws)),
        out_shape=[jax.ShapeDtypeStruct((n, w.shape[1]), dt) for w, dt in zip(ws, out_dtypes)],
        grid=(n // tile,),
        in_specs=[pl.BlockSpec((tile, k), lambda i: (i, 0))] + [_const_spec(w.shape) for w in ws],
        out_specs=[pl.BlockSpec((tile, w.shape[1]), lambda i: (i, 0)) for w in ws],
        compiler_params=_cparams(("parallel",)),
        name="proj",
    )(x, *ws)
    return outs


def _wkv_chunks(chains, states, masks):
    bd_b, bd, strict, incl, eye, eye_full = masks
    c, n = chains[0][1].shape
    nch = len(chains)

    def stack(x_b):
        return jnp.where(bd_b, jnp.concatenate([x_b] * WKV_GROUP, axis=0), jnp.zeros((), bf16))

    cast = [tuple(x.astype(bf16) for x in ch[1:6]) for ch in chains]
    v_s = [stack(cb[4]) for cb in cast]
    g = [_dot_nt(jnp.concatenate([cb[0], cb[3]], axis=0), jnp.concatenate([stack(cb[1]), stack(cb[2])], axis=0))
         for cb in cast]
    l_ak = [jnp.where(strict, gi[:c, n:], 0.0).astype(bf16) for gi in g]
    m_rb = [jnp.where(incl, gi[c:, :n], 0.0).astype(bf16) for gi in g]
    m_rk = [jnp.where(incl, gi[c:, n:], 0.0).astype(bf16) for gi in g]
    x = [jnp.where(strict, gi[:c, :n], 0.0) for gi in g]
    t = [eye + xi for xi in x]
    for _ in range(5):
        xb = [xi.astype(bf16) for xi in x]
        x = [_dot(xi, stack(xi)) for xi in xb]
        t = [ti + _dot(ti.astype(bf16), stack(xi.astype(bf16))) for ti, xi in zip(t, x)]
    lakv = [_dot(l_ak[i], v_s[i]).astype(bf16) for i in range(nch)]
    au = [_dot(t[i].astype(bf16), jnp.concatenate([stack(cast[i][0]), stack(lakv[i])], axis=1))
          for i in range(nch)]
    abar = [a[:, :n].astype(bf16) for a in au]
    ubar = [a[:, n:].astype(bf16) for a in au]
    ry = [_dot(m_rb[i], jnp.concatenate([stack(abar[i]), stack(ubar[i])], axis=1)) for i in range(nch)]
    r_bar = [(chains[i][4] + ry[i][:, :n]).astype(bf16) for i in range(nch)]
    y_bar = [ry[i][:, n:] + _dot(m_rk[i], v_s[i]) for i in range(nch)]
    p = [((eye_full + jnp.where(bd, _dot_tn(abar[i], cast[i][1]), 0.0)) * chains[i][6]).astype(bf16)
         for i in range(nch)]
    q = []
    for i in range(nch):
        q_bd = jnp.where(bd, _dot_tn(jnp.concatenate([ubar[i], cast[i][4]], axis=0),
                                     jnp.concatenate([cast[i][1], cast[i][2]], axis=0)), 0.0)
        qi = q_bd[0:c]
        for h in range(1, WKV_GROUP):
            qi = qi + q_bd[h * c:(h + 1) * c]
        q.append(qi * chains[i][6])
    states = list(states)
    ys = []
    for i in range(nch):
        gi = chains[i][0]
        s_b = states[gi].astype(bf16)
        ys.append(_dot_nt(r_bar[i], stack(s_b)) + y_bar[i])
        states[gi] = _dot(s_b, p[i]) + q[i]
    return ys, states


def _rwkv_body(u_ref, mu_ref, wdec_ref, w0_ref, aup_ref, a0_ref, gup_ref, kk_ref, ka_ref, rk_ref,
               lng_ref, lnb_ref, seg_ref, tri_ref, y_ref, state_ref, carry_ref, *, tt):
    j = pl.program_id(1)

    @pl.when(j == 0)
    def _():
        state_ref[...] = jnp.zeros_like(state_ref)
        carry_ref[...] = jnp.zeros_like(carry_ref)

    w = RWKV_WIDTH
    u = u_ref[...]
    row = lax.broadcasted_iota(i32, u.shape, 0)
    prev = jnp.where(row == 0, carry_ref[0:1, :], pltpu.roll(u, 1, axis=0))
    carry_ref[0:1, :] = u[tt - 1:tt, :]
    us = u + (prev - u) * mu_ref[...]
    r = us[:, 0:w]
    k = us[:, w:2 * w]
    v = us[:, 2 * w:3 * w]
    wa = us[:, 3 * w:3 * w + DECAY_RANK + AAA_RANK]
    gd = us[:, 3 * w + DECAY_RANK + AAA_RANK:]
    z = w0_ref[...] + _dot_hp(jnp.tanh(wa), wdec_ref[...])
    softplus_neg_z = jnp.maximum(-z, 0.0) + jnp.log(1.0 + jnp.exp(-jnp.abs(z)))
    lw = -jnp.exp(-softplus_neg_z - 0.5)
    a = _sigmoid(a0_ref[...] + _dot_hp(wa, aup_ref[...]))
    gate = _dot(_sigmoid(gd).astype(bf16), gup_ref[...].astype(bf16))
    seg = seg_ref[...]
    kk = k * kk_ref[...]
    kmod = k * (1.0 + (a - 1.0) * ka_ref[...])
    kk_sq, bonus_dot = _seg_sums([kk * kk, r * kmod * rk_ref[...]], seg)
    kk = kk / jnp.maximum(jnp.sqrt(kk_sq), 1e-12)
    cum = _dot_exact_lhs(tri_ref[...], lw)
    wc = jnp.exp(cum)
    iwc = jnp.exp(-cum)
    at = -kk * jnp.exp(cum - lw)
    bt = kk * a * iwc
    kt = kmod * iwc
    rt = r * wc

    n = WKV_GROUP * HEAD_DIM
    ri = lax.broadcasted_iota(i32, (n, n), 0)
    ci = lax.broadcasted_iota(i32, (n, n), 1)
    bd = (ri // WKV_CHUNK) == (ci // HEAD_DIM)
    bd_b = jnp.where(bd, 1.0, 0.0).astype(bf16) > 0
    eye_full = jnp.where(ri == ci, 1.0, 0.0).astype(f32)
    ti = lax.broadcasted_iota(i32, (WKV_CHUNK, n), 0)
    si = lax.broadcasted_iota(i32, (WKV_CHUNK, n), 1) % WKV_CHUNK
    masks = (bd_b, bd, ti > si, ti >= si, jnp.where(ti == si, 1.0, 0.0).astype(f32), eye_full)

    n_groups = w // n
    n_chunks = tt // WKV_CHUNK
    chains = []
    for c in range(n_chunks):
        rs = slice(c * WKV_CHUNK, (c + 1) * WKV_CHUNK)
        last = (c + 1) * WKV_CHUNK - 1
        for gi in range(n_groups):
            cs = slice(gi * n, (gi + 1) * n)
            chains.append((gi, at[rs, cs], bt[rs, cs], kt[rs, cs], rt[rs, cs], v[rs, cs], wc[last:last + 1, cs]))
    ys, states = _wkv_chunks(chains, [state_ref[gi] for gi in range(n_groups)], masks)
    for gi in range(n_groups):
        state_ref[gi] = states[gi]
    y = jnp.concatenate([jnp.concatenate(ys[c * n_groups:(c + 1) * n_groups], axis=1) for c in range(n_chunks)],
                        axis=0)

    inv_n = 1.0 / HEAD_DIM
    d = y - _seg_sums([y], seg)[0] * inv_n
    var = _seg_sums([d * d], seg)[0] * inv_n
    yn = d * lax.rsqrt(var + GN_EPS) * lng_ref[...] + lnb_ref[...]
    y_ref[...] = ((yn + bonus_dot * v) * gate).astype(y_ref.dtype)


def _rwkv(u_r, mu_shift, w_decay_up, w0, a_up, a0, g_up, k_k, k_a, r_k, lnx_g, lnx_b, tt):
    b, s, cols = u_r.shape
    tt = min(tt, s)
    w = RWKV_WIDTH
    row = lambda p: p.reshape(1, -1).astype(f32)
    wdec = jnp.concatenate([w_decay_up, jnp.zeros((AAA_RANK, w), f32)], axis=0)
    aup = jnp.concatenate([jnp.zeros((DECAY_RANK, w), f32), a_up], axis=0)
    hid = jnp.arange(w) // HEAD_DIM
    seg = (hid[:, None] == hid[None, :]).astype(bf16)
    ti = jnp.arange(tt)
    tri = ((ti[:, None] // WKV_CHUNK == ti[None, :] // WKV_CHUNK) & (ti[:, None] >= ti[None, :])).astype(bf16)
    params = [row(mu_shift), wdec, row(w0), aup, row(a0), g_up, row(k_k), row(k_a), row(r_k), row(lnx_g),
              row(lnx_b), seg, tri]
    n = WKV_GROUP * HEAD_DIM
    return pl.pallas_call(
        functools.partial(_rwkv_body, tt=tt),
        out_shape=jax.ShapeDtypeStruct((b, s, w), bf16),
        grid=(b, s // tt),
        in_specs=[pl.BlockSpec((None, tt, cols), lambda bi, j: (bi, j, 0))] + [_const_spec(p.shape) for p in params],
        out_specs=pl.BlockSpec((None, tt, w), lambda bi, j: (bi, j, 0)),
        scratch_shapes=[pltpu.VMEM((w // n, HEAD_DIM, n), f32), pltpu.VMEM((SUBLANES, cols), f32)],
        compiler_params=_cparams(("parallel", "arbitrary")),
        name="rwkv7",
    )(u_r, *params)


def _swa_body(sink_ref, u_ref, pos_ref, invf_ref, o_ref, kprev_ref, vprev_ref):
    j = pl.program_id(1)

    @pl.when(j == 0)
    def _():
        kprev_ref[...] = jnp.zeros_like(kprev_ref)
        vprev_ref[...] = jnp.zeros_like(vprev_ref)

    wq = SWA_WIDTH
    kvw = SWA_KV_WIDTH
    u = u_ref[...]
    ang = pos_ref[...].astype(f32) * invf_ref[...]
    cos = jnp.cos(ang)
    sin = jnp.sin(ang)

    def rope(x, c, s):
        n = x.shape[1]
        lane = lax.broadcasted_iota(i32, x.shape, 1)
        half = HEAD_DIM // 2
        rot = jnp.where((lane % HEAD_DIM) < half, -pltpu.roll(x, n - half, axis=1), pltpu.roll(x, half, axis=1))
        return x * c + rot * s

    q = rope(u[:, :wq] * (HEAD_DIM ** -0.5), jnp.concatenate([cos] * (wq // LANES), axis=1),
             jnp.concatenate([sin] * (wq // LANES), axis=1))
    k_cur = rope(u[:, wq:wq + kvw], cos, sin)
    v_cur = u[:, wq + kvw:]
    k_prev = kprev_ref[...]
    v_prev = vprev_ref[...]
    kprev_ref[...] = k_cur
    vprev_ref[...] = v_cur

    gw = SWA_GROUP * HEAD_DIM
    rows = SWA_GROUP * WINDOW
    lane_kv = lax.broadcasted_iota(i32, (WINDOW, kvw), 1)

    def rep(x, gi):
        sw = pltpu.roll(x, HEAD_DIM, axis=1)
        one = jnp.where((lane_kv // HEAD_DIM) == gi, x, sw)
        return jnp.concatenate([one] * (gw // kvw), axis=1).astype(bf16)

    ri = lax.broadcasted_iota(i32, (rows, WINDOW), 0)
    ci = lax.broadcasted_iota(i32, (rows, WINDOW), 1)
    t_idx = ri % WINDOW
    mask_prev = ci > t_idx + jnp.where(j > 0, 0, WINDOW)
    mask_cur = ci <= t_idx
    rb = lax.broadcasted_iota(i32, (rows, 1), 0) // WINDOW
    rbo = lax.broadcasted_iota(i32, (rows, gw), 0) // WINDOW
    cbo = lax.broadcasted_iota(i32, (rows, gw), 1) // HEAD_DIM
    bd = rbo == cbo
    outs = []
    for gi in range(SWA_KV_HEADS):
        qg = q[:, gi * gw:(gi + 1) * gw]
        q_bd = jnp.where(bd, jnp.concatenate([qg] * SWA_GROUP, axis=0), 0.0).astype(bf16)
        s_prev = jnp.where(mask_prev, _dot_nt(q_bd, rep(k_prev, gi)), NEG_INF)
        s_cur = jnp.where(mask_cur, _dot_nt(q_bd, rep(k_cur, gi)), NEG_INF)
        sink = jnp.zeros((rows, 1), f32)
        for h in range(SWA_GROUP):
            sink = jnp.where(rb == h, sink_ref[gi * SWA_GROUP + h], sink)
        m = jnp.maximum(jnp.maximum(jnp.max(s_prev, axis=-1, keepdims=True),
                                    jnp.max(s_cur, axis=-1, keepdims=True)), sink)
        p_prev = jnp.exp(s_prev - m)
        p_cur = jnp.exp(s_cur - m)
        denom = jnp.sum(p_prev, axis=-1, keepdims=True) + jnp.sum(p_cur, axis=-1, keepdims=True) + jnp.exp(sink - m)
        o_bd = _dot(p_prev.astype(bf16), rep(v_prev, gi)) + _dot(p_cur.astype(bf16), rep(v_cur, gi))
        o_bd = jnp.where(bd, o_bd / denom, 0.0)
        og = o_bd[0:WINDOW]
        for h in range(1, SWA_GROUP):
            og = og + o_bd[h * WINDOW:(h + 1) * WINDOW]
        outs.append(og)
    o_ref[...] = jnp.concatenate(outs, axis=1).astype(o_ref.dtype)


def _swa(u_s, positions, sinks):
    b, s, cols = u_s.shape
    half = HEAD_DIM // 2
    inv_freq = ROPE_THETA ** (-jnp.arange(0, HEAD_DIM, 2, dtype=f32) / HEAD_DIM)
    invf = jnp.tile(inv_freq, LANES // half).reshape(1, LANES)
    pos = positions.reshape(b, s, 1).astype(i32)
    return pl.pallas_call(
        _swa_body,
        out_shape=jax.ShapeDtypeStruct((b, s, SWA_WIDTH), bf16),
        grid=(b, s // WINDOW),
        in_specs=[pl.BlockSpec(memory_space=pltpu.SMEM),
                  pl.BlockSpec((None, WINDOW, cols), lambda bi, j: (bi, j, 0)),
                  pl.BlockSpec((None, WINDOW, 1), lambda bi, j: (bi, j, 0)),
                  _const_spec((1, LANES))],
        out_specs=pl.BlockSpec((None, WINDOW, SWA_WIDTH), lambda bi, j: (bi, j, 0)),
        scratch_shapes=[pltpu.VMEM((WINDOW, SWA_KV_WIDTH), f32), pltpu.VMEM((WINDOW, SWA_KV_WIDTH), f32)],
        compiler_params=_cparams(("parallel", "arbitrary")),
        name="swa",
    )(sinks.astype(f32), u_s, pos, invf)


def _mix_out_body(ya_ref, yb_ref, wa_ref, wb_ref, x_ref, g_ref, b_ref, o_ref, *, alpha):
    mix = _dot(ya_ref[...], wa_ref[...]) + _dot(yb_ref[...], wb_ref[...])
    o_ref[...] = _layer_norm(alpha * x_ref[...] + mix, g_ref[...], b_ref[...])


def _mix_out(ya, yb, w_o, x, g, b, alpha, tile):
    n, d = x.shape
    tile = min(tile, n)
    wa = w_o[:ya.shape[1]].astype(bf16)
    wb = w_o[ya.shape[1]:].astype(bf16)
    rows = lambda width: pl.BlockSpec((tile, width), lambda i: (i, 0))
    return pl.pallas_call(
        functools.partial(_mix_out_body, alpha=alpha),
        out_shape=jax.ShapeDtypeStruct((n, d), f32),
        grid=(n // tile,),
        in_specs=[rows(ya.shape[1]), rows(yb.shape[1]), _const_spec(wa.shape), _const_spec(wb.shape), rows(d),
                  _const_spec((1, d)), _const_spec((1, d))],
        out_specs=rows(d),
        compiler_params=_cparams(("parallel",)),
        name="mix_out_ln1",
    )(ya, yb, wa, wb, x, g.reshape(1, d), b.reshape(1, d))


def _xattn_body(x_ref, kv_ref, wq_ref, wo_ref, g_ref, b_ref, o_ref, o3_ref, *, alpha):
    x = x_ref[...]
    d = x.shape[1]
    hd = d // MEM_HEADS
    q = _dot(x.astype(bf16), wq_ref[...]) * (hd ** -0.5)
    kv = kv_ref[...]
    outs = []
    for h in range(MEM_HEADS):
        qh = q[:, h * hd:(h + 1) * hd].astype(bf16)
        kh = kv[:, h * hd:(h + 1) * hd]
        vh = kv[:, d + h * hd:d + (h + 1) * hd]
        s = _dot_nt(qh, kh)
        p = jnp.exp(s - jnp.max(s, axis=-1, keepdims=True))
        l = jnp.sum(p, axis=-1, keepdims=True)
        outs.append(_dot(p.astype(bf16), vh) / l)
    o = jnp.concatenate(outs, axis=1)
    xa = _dot(o.astype(bf16), wo_ref[...])
    y = _layer_norm(alpha * x + xa, g_ref[...], b_ref[...])
    o_ref[...] = y
    nc = d // LANES
    for c in range(nc):
        o3_ref[_lane_chunk(x.shape[0], nc, c)] = y[:, c * LANES:(c + 1) * LANES]


def _xattn(x1, kv, wm_q, wm_o, g, b, alpha, tile):
    bsz, s, d = x1.shape
    m = kv.shape[1]
    tile = min(tile, s)
    nj = s // tile
    wq = wm_q.astype(bf16)
    wo = wm_o.astype(bf16)
    return pl.pallas_call(
        functools.partial(_xattn_body, alpha=alpha),
        out_shape=[jax.ShapeDtypeStruct((bsz, s, d), f32), jax.ShapeDtypeStruct((bsz * s * (d // LANES), LANES), f32)],
        grid=(bsz, nj),
        in_specs=[pl.BlockSpec((None, tile, d), lambda bi, j: (bi, j, 0)),
                  pl.BlockSpec((None, m, 2 * d), lambda bi, j: (bi, 0, 0)),
                  _const_spec(wq.shape), _const_spec(wo.shape), _const_spec((1, d)), _const_spec((1, d))],
        out_specs=[pl.BlockSpec((None, tile, d), lambda bi, j: (bi, j, 0)),
                   pl.BlockSpec((tile * (d // LANES), LANES), lambda bi, j: (bi * nj + j, 0))],
        compiler_params=_cparams(("parallel", "parallel")),
        name="mem_xattn_ln2",
    )(x1, kv, wq, wo, g.reshape(1, d), b.reshape(1, d))


def _router_body(x_ref, wt_ref, bias_ref, upper_ref, e_ref, g_ref, r_ref, cnt_out_ref, cnt_ref, *, t):
    @pl.when(pl.program_id(0) == 0)
    def _():
        cnt_ref[...] = jnp.zeros_like(cnt_ref)

    xh, xl = _split2(x_ref[...])
    wh, wl = _split2(wt_ref[...])
    logits = _dot_nt(wh, xh) + _dot_nt(wh, xl) + _dot_nt(wl, xh)
    scores = _sigmoid(logits)
    biased = scores + bias_ref[...][:, 0:1]
    ne = N_EXPERTS
    neg = -jnp.inf

    def top1(vals):
        rows = lax.broadcasted_iota(i32, vals.shape, 0).astype(f32)
        m = jnp.max(vals, axis=0, keepdims=True)
        idx = jnp.min(jnp.where(vals == m, rows, float(vals.shape[0])), axis=0, keepdims=True)
        return m, idx, rows == idx

    gscores = []
    for gi in range(N_GROUPS):
        blk = biased[gi * GROUP_SIZE:(gi + 1) * GROUP_SIZE, :]
        m1, _, hit = top1(blk)
        m2 = jnp.max(jnp.where(hit, neg, blk), axis=0, keepdims=True)
        gscores.append(m1 + m2)
    gs = jnp.concatenate(gscores, axis=0)
    gsel = jnp.zeros(gs.shape, f32)
    for _ in range(TOPK_GROUPS):
        _, _, hit = top1(gs)
        gsel = jnp.where(hit, 1.0, gsel)
        gs = jnp.where(hit, neg, gs)
    emask = jnp.concatenate(
        [jnp.broadcast_to(gsel[gi:gi + 1, :], (GROUP_SIZE, t)) for gi in range(N_GROUPS)], axis=0) > 0.5
    cand = jnp.where(emask, biased, NEG_INF)
    idxs, sels = [], []
    chosen = jnp.zeros((ne, t), f32)
    for _ in range(TOP_K):
        _, idx, hit = top1(cand)
        idxs.append(idx)
        sels.append(jnp.sum(jnp.where(hit, scores, 0.0), axis=0, keepdims=True))
        chosen = chosen + jnp.where(hit, 1.0, 0.0)
        cand = jnp.where(hit, neg, cand)
    sel = jnp.concatenate(sels, axis=0)
    g_ref[...] = sel / jnp.sum(sel, axis=0, keepdims=True) * ROUTED_SCALE
    e_ref[...] = jnp.concatenate(idxs, axis=0).astype(i32)
    before = _dot(chosen.astype(bf16), upper_ref[...]) + cnt_ref[...][:, 0:1]
    rows = lax.broadcasted_iota(i32, (ne, t), 0).astype(f32)
    ranks = [jnp.sum(jnp.where(rows == idx, before, 0.0), axis=0, keepdims=True) for idx in idxs]
    r_ref[...] = jnp.concatenate(ranks, axis=0).astype(i32)
    cnt_ref[...] = cnt_ref[...] + jnp.sum(chosen, axis=1, keepdims=True)
    cnt_out_ref[...] = cnt_ref[...].astype(i32)


def _router(x2, w_router, router_bias, tile):
    n, d = x2.shape
    ne = N_EXPERTS
    t = min(tile, n)
    wt = w_router.T
    bias = jnp.broadcast_to(router_bias.reshape(ne, 1).astype(f32), (ne, LANES))
    ti = jnp.arange(t)
    upper = (ti[:, None] < ti[None, :]).astype(bf16)
    cols = pl.BlockSpec((TOP_K, t), lambda i: (0, i))
    e_t, g_t, r_t, cnt = pl.pallas_call(
        functools.partial(_router_body, t=t),
        out_shape=[jax.ShapeDtypeStruct((TOP_K, n), i32), jax.ShapeDtypeStruct((TOP_K, n), f32),
                   jax.ShapeDtypeStruct((TOP_K, n), i32), jax.ShapeDtypeStruct((ne, LANES), i32)],
        grid=(n // t,),
        in_specs=[pl.BlockSpec((t, d), lambda i: (i, 0)), _const_spec((ne, d)), _const_spec((ne, LANES)),
                  _const_spec((t, t))],
        out_specs=[cols, cols, cols, _const_spec((ne, LANES))],
        scratch_shapes=[pltpu.VMEM((ne, LANES), f32)],
        compiler_params=_cparams(("arbitrary",)),
        name="router",
    )(x2, wt, bias, upper)
    return e_t, g_t, r_t, cnt[:, 0]


def _dispatch_sc(x3d, e_t, r_t, g_t, starts):
    n, nc, lanes = x3d.shape
    workers = SC_CORES * SC_SUBCORES
    per_w = n // workers
    assert n % (workers * SC_INDEX_GROUP) == 0 and SC_INDEX_GROUP % SC_ROW_CHUNK == 0
    mesh = plsc.VectorSubcoreMesh(core_axis_name="c", subcore_axis_name="s")

    @functools.partial(
        pl.kernel, mesh=mesh,
        out_type=[jax.ShapeDtypeStruct((n * TOP_K, nc, lanes), x3d.dtype),
                  jax.ShapeDtypeStruct((n * TOP_K,), f32)],
        scratch_types=[
            pltpu.VMEM((SC_ROW_CHUNK, nc, lanes), x3d.dtype),
            pltpu.VMEM((TOP_K, SC_INDEX_GROUP), i32),
            pltpu.VMEM((TOP_K, SC_INDEX_GROUP), i32),
            pltpu.VMEM((TOP_K, SC_INDEX_GROUP), f32),
            pltpu.VMEM((TOP_K, SC_ROW_CHUNK), i32),
            pltpu.VMEM((TOP_K, SC_ROW_CHUNK), f32),
            pltpu.VMEM((N_EXPERTS,), i32),
            pltpu.SemaphoreType.DMA,
        ],
        compiler_params=pltpu.CompilerParams(use_tc_tiling_on_sc=True, needs_layout_passes=False),
    )
    def dispatch(x_hbm, e_hbm, r_hbm, g_hbm, st_hbm, o_hbm, gs_hbm, rows_v, e_v, r_v, g_v, slot_v, gate_v, st_v, sem):
        wid = lax.axis_index("s") * SC_CORES + lax.axis_index("c")
        pltpu.sync_copy(st_hbm, st_v)

        @pl.loop(0, per_w // SC_INDEX_GROUP)
        def _(gi):
            base = wid * per_w + gi * SC_INDEX_GROUP
            pltpu.sync_copy(e_hbm.at[:, pl.ds(base, SC_INDEX_GROUP)], e_v)
            pltpu.sync_copy(r_hbm.at[:, pl.ds(base, SC_INDEX_GROUP)], r_v)
            pltpu.sync_copy(g_hbm.at[:, pl.ds(base, SC_INDEX_GROUP)], g_v)
            for h in range(SC_INDEX_GROUP // SC_ROW_CHUNK):
                off = h * SC_ROW_CHUNK
                pltpu.sync_copy(x_hbm.at[pl.ds(base + off, SC_ROW_CHUNK)], rows_v)
                for kk in range(TOP_K):
                    for j in range(SC_ROW_CHUNK // SC_LANES):
                        src = pl.ds(off + j * SC_LANES, SC_LANES)
                        dst = pl.ds(j * SC_LANES, SC_LANES)
                        slot_v[kk, dst] = r_v[kk, src] + plsc.load_gather(st_v, [e_v[kk, src]])
                        gate_v[kk, dst] = g_v[kk, src]
                copies = [pltpu.async_copy(rows_v, o_hbm.at[slot_v.at[kk]], sem) for kk in range(TOP_K)]
                copies += [pltpu.async_copy(gate_v.at[kk], gs_hbm.at[slot_v.at[kk]], sem) for kk in range(TOP_K)]
                for cp in copies:
                    cp.wait()

    return dispatch(x3d, e_t, r_t, g_t, starts)


def _gmm_body(gid_ref, tid_ref, lo_ref, hi_ref, first_ref, newg_ref, x_ref, g_ref, wg_ref, wu_ref, wd_ref, o_ref,
              wg_b, wu_b, wd_b, *, tm):
    v = pl.program_id(0)
    lo = lo_ref[v]
    hi = hi_ref[v]
    row0 = tid_ref[v] * tm
    nc = wg_ref.shape[0] // LANES
    full = (lo <= row0) & (hi >= row0 + tm)

    @pl.when(newg_ref[v] == 1)
    def _():
        wg_b[...] = wg_ref[...].astype(bf16)
        wu_b[...] = wu_ref[...].astype(bf16)
        wd_b[...] = wd_ref[...].astype(bf16)

    def ffn():
        x = jnp.concatenate([x_ref[_lane_chunk(tm, nc, c)] for c in range(nc)], axis=1).astype(bf16)
        hg = _dot(x, wg_b[...])
        h = hg * _sigmoid(hg) * _dot(x, wu_b[...])
        g = g_ref[...]
        g_cols = jnp.concatenate([g, jnp.zeros((SUBLANES - g.shape[0], LANES), f32)], axis=0).T
        h = jnp.concatenate([h[r * LANES:(r + 1) * LANES, :] * g_cols[:, r:r + 1] for r in range(tm // LANES)],
                            axis=0)
        return _dot(h.astype(bf16), wd_b[...])

    @pl.when(full)
    def _():
        y = ffn()
        for c in range(nc):
            o_ref[_lane_chunk(tm, nc, c)] = y[:, c * LANES:(c + 1) * LANES]

    @pl.when(jnp.logical_not(full) & (hi > lo))
    def _():
        @pl.when(first_ref[v] == 1)
        def _():
            o_ref[...] = jnp.zeros_like(o_ref)

        rows = row0 + lax.broadcasted_iota(i32, (tm, 1), 0)
        mask = (rows >= lo) & (rows < hi)
        y = ffn()
        for c in range(nc):
            idx = _lane_chunk(tm, nc, c)
            o_ref[idx] = jnp.where(mask, y[:, c * LANES:(c + 1) * LANES], o_ref[idx])


def _gmm(xs, gates, we_gate, we_up, we_down, layer, counts, tm):
    _, ne, d, de = we_gate.shape
    nc = d // LANES
    nk = xs.shape[0] // nc
    tm = min(tm, nk)
    assert tm % LANES == 0 and tm // LANES <= SUBLANES
    n_tiles = nk // tm
    gates3 = gates.reshape(n_tiles, tm // LANES, LANES)
    n_visits = n_tiles + ne - 1
    ends = jnp.cumsum(counts)
    starts = ends - counts
    tile_lo = starts // tm
    n_touch = jnp.where(counts > 0, (ends - 1) // tm - tile_lo + 1, 0)
    vis_end = jnp.cumsum(n_touch)
    vis_start = vis_end - n_touch
    vi = jnp.arange(n_visits, dtype=i32)
    valid = vi < vis_end[-1]
    gid = jnp.minimum(jnp.sum((vis_end[None, :] <= vi[:, None]).astype(i32), axis=1), ne - 1)
    onehot = gid[:, None] == jnp.arange(ne, dtype=i32)[None, :]
    pick = lambda table: jnp.sum(jnp.where(onehot, table[None, :], 0), axis=1)
    tid = jnp.where(valid, pick(tile_lo) + vi - pick(vis_start), n_tiles - 1).astype(i32)
    lo = jnp.where(valid, pick(starts), 0).astype(i32)
    hi = jnp.where(valid, pick(ends), 0).astype(i32)
    one = jnp.ones((1,), i32)
    first = jnp.concatenate([one, (tid[1:] != tid[:-1]).astype(i32)])
    newg = jnp.concatenate([one, (gid[1:] != gid[:-1]).astype(i32)])
    wspec = lambda shape: pl.BlockSpec((None, None) + shape, lambda v, g, *_: (layer, g[v], 0, 0))
    rows = pl.BlockSpec((tm * nc, LANES), lambda v, g, t, *_: (t[v], 0))
    return pl.pallas_call(
        functools.partial(_gmm_body, tm=tm),
        out_shape=jax.ShapeDtypeStruct(xs.shape, f32),
        grid_spec=pltpu.PrefetchScalarGridSpec(
            num_scalar_prefetch=6, grid=(n_visits,),
            in_specs=[rows, pl.BlockSpec((None, tm // LANES, LANES), lambda v, g, t, *_: (t[v], 0, 0)),
                      wspec((d, de)), wspec((d, de)), wspec((de, d))],
            out_specs=rows,
            scratch_shapes=[pltpu.VMEM((d, de), bf16), pltpu.VMEM((d, de), bf16), pltpu.VMEM((de, d), bf16)]),
        compiler_params=_cparams(("arbitrary",)),
        name="moe_experts",
    )(gid, tid, lo, hi, first, newg, xs, gates3, we_gate, we_up, we_down)


def _combine_sc(ys3d, e_t, r_t, starts, n):
    _, nc, lanes = ys3d.shape
    workers = SC_CORES * SC_SUBCORES
    per_w = n // workers
    tc = SC_ROW_CHUNK
    assert n % (workers * SC_INDEX_GROUP) == 0 and SC_INDEX_GROUP % tc == 0
    mesh = plsc.VectorSubcoreMesh(core_axis_name="c", subcore_axis_name="s")

    @functools.partial(
        pl.kernel, mesh=mesh,
        out_type=jax.ShapeDtypeStruct((n, nc, lanes), f32),
        scratch_types=[
            pltpu.VMEM((tc, nc, lanes), f32),
            pltpu.VMEM((TOP_K, SC_INDEX_GROUP), i32),
            pltpu.VMEM((TOP_K, SC_INDEX_GROUP), i32),
            pltpu.VMEM((TOP_K, tc), i32),
            pltpu.VMEM((N_EXPERTS,), i32),
            pltpu.SemaphoreType.DMA,
        ],
        compiler_params=pltpu.CompilerParams(use_tc_tiling_on_sc=True, needs_layout_passes=False),
    )
    def combine(y_hbm, e_hbm, r_hbm, st_hbm, o_hbm, acc_v, e_v, r_v, slot_v, st_v, sem):
        wid = lax.axis_index("s") * SC_CORES + lax.axis_index("c")
        pltpu.sync_copy(st_hbm, st_v)

        @pl.loop(0, per_w // SC_INDEX_GROUP)
        def _(gi):
            base = wid * per_w + gi * SC_INDEX_GROUP
            pltpu.sync_copy(e_hbm.at[:, pl.ds(base, SC_INDEX_GROUP)], e_v)
            pltpu.sync_copy(r_hbm.at[:, pl.ds(base, SC_INDEX_GROUP)], r_v)
            for h in range(SC_INDEX_GROUP // tc):
                off = h * tc
                for kk in range(TOP_K):
                    for j in range(tc // SC_LANES):
                        src = pl.ds(off + j * SC_LANES, SC_LANES)
                        slot_v[kk, pl.ds(j * SC_LANES, SC_LANES)] = (
                            r_v[kk, src] + plsc.load_gather(st_v, [e_v[kk, src]]))
                pltpu.async_copy(y_hbm.at[slot_v.at[0]], acc_v, sem).wait()
                copies = [pltpu.async_copy(y_hbm.at[slot_v.at[kk]], acc_v, sem, add=True) for kk in range(1, TOP_K)]
                for cp in copies:
                    cp.wait()
                pltpu.sync_copy(acc_v, o_hbm.at[pl.ds(base + off, tc)])

    return combine(ys3d, e_t, r_t, starts)


def _ffn_out_body(x_ref, r_ref, wg_ref, wu_ref, wd_ref, g_ref, b_ref, o_ref, *, alpha):
    x = x_ref[...]
    xb = x.astype(bf16)
    hg = _dot(xb, wg_ref[...])
    h = hg * _sigmoid(hg) * _dot(xb, wu_ref[...])
    shared = _dot(h.astype(bf16), wd_ref[...])
    nc = x.shape[1] // LANES
    routed = jnp.concatenate([r_ref[_lane_chunk(x.shape[0], nc, c)] for c in range(nc)], axis=1)
    o_ref[...] = _layer_norm(alpha * x + routed + shared, g_ref[...], b_ref[...])


def _ffn_out(x2, routed_rows, ws_gate, ws_up, ws_down, g, b, alpha, tile):
    n, d = x2.shape
    tile = min(tile, n)
    wg, wu, wd = ws_gate.astype(bf16), ws_up.astype(bf16), ws_down.astype(bf16)
    rows = pl.BlockSpec((tile, d), lambda i: (i, 0))
    return pl.pallas_call(
        functools.partial(_ffn_out_body, alpha=alpha),
        out_shape=jax.ShapeDtypeStruct((n, d), f32),
        grid=(n // tile,),
        in_specs=[rows, pl.BlockSpec((tile * (d // LANES), LANES), lambda i: (i, 0)), _const_spec(wg.shape),
                  _const_spec(wu.shape), _const_spec(wd.shape), _const_spec((1, d)), _const_spec((1, d))],
        out_specs=rows,
        compiler_params=_cparams(("parallel",)),
        name="ffn_out_ln3",
    )(x2, routed_rows, wg, wu, wd, g.reshape(1, d), b.reshape(1, d))


def _layer(x, mem, positions, w_in, mu_shift, w_decay_up, w0, a_up, a0, g_up, k_k, k_a, r_k, lnx_g, lnx_b, sinks,
           w_o, ln1_g, ln1_b, wm_q, wm_kv, wm_o, ln2_g, ln2_b, w_router, router_bias, we_gate, we_up, we_down,
           ws_gate, ws_up, ws_down, ln3_g, ln3_b, *, layer, alpha):
    b, s, d = x.shape
    n = b * s
    xf = x.reshape(n, d)
    w_in_b = w_in.astype(bf16)
    u_r, u_s = _proj(xf, [w_in_b[:, :RWKV_COLS], w_in_b[:, RWKV_COLS:]], [f32, f32], tile=512)
    y_r = _rwkv(u_r.reshape(b, s, RWKV_COLS), mu_shift, w_decay_up, w0, a_up, a0, g_up, k_k, k_a, r_k, lnx_g, lnx_b,
                tt=256)
    y_s = _swa(u_s.reshape(b, s, SWA_COLS), positions, sinks)
    x1 = _mix_out(y_r.reshape(n, RWKV_WIDTH), y_s.reshape(n, SWA_WIDTH), w_o, xf, ln1_g, ln1_b, alpha, tile=512)
    m = mem.shape[1]
    (kv,) = _proj(mem.reshape(b * m, d), [wm_kv.astype(bf16)], [bf16], tile=512)
    x2, x2_rows = _xattn(x1.reshape(b, s, d), kv.reshape(b, m, 2 * d), wm_q, wm_o, ln2_g, ln2_b, alpha, tile=512)
    x2 = x2.reshape(n, d)
    e_t, g_t, r_t, counts = _router(x2, w_router, router_bias, tile=512)
    starts = (jnp.cumsum(counts) - counts).astype(i32)
    nc = d // LANES
    xs, gs = _dispatch_sc(x2_rows.reshape(n, nc, LANES), e_t, r_t, g_t, starts)
    ys = _gmm(xs.reshape(n * TOP_K * nc, LANES), gs, we_gate, we_up, we_down, layer, counts, tm=512)
    routed = _combine_sc(ys.reshape(n * TOP_K, nc, LANES), e_t, r_t, starts, n)
    x3 = _ffn_out(x2, routed.reshape(n * nc, LANES), ws_gate, ws_up, ws_down, ln3_g, ln3_b, alpha, tile=512)
    return x3.reshape(b, s, d)


def kernel(x, mem, positions, w_in, mu_shift, w_decay_up, w0, a_up, a0, g_up, k_k, k_a, r_k, lnx_g, lnx_b, sinks, w_o, ln1_g, ln1_b, wm_q, wm_kv, wm_o, ln2_g, ln2_b, w_router, router_bias, we_gate, we_up, we_down, ws_gate, ws_up, ws_down, ln3_g, ln3_b):
    depth = w_in.shape[0]
    alpha = (2 * depth) ** 0.25
    for l in range(depth):
        x = _layer(x, mem, positions, w_in[l], mu_shift[l], w_decay_up[l], w0[l], a_up[l], a0[l], g_up[l], k_k[l],
                   k_a[l], r_k[l], lnx_g[l], lnx_b[l], sinks[l], w_o[l], ln1_g[l], ln1_b[l], wm_q[l], wm_kv[l],
                   wm_o[l], ln2_g[l], ln2_b[l], w_router[l], router_bias[l], we_gate, we_up, we_down,
                   ws_gate[l], ws_up[l], ws_down[l], ln3_g[l], ln3_b[l], layer=l, alpha=alpha)
    return x
```

```python
import functools

import jax
import jax.numpy as jnp
from jax import lax
from jax.experimental import pallas as pl
from jax.experimental.pallas import tpu as pltpu
from jax.experimental.pallas import tpu_sc as plsc

f32 = jnp.float32
bf16 = jnp.bfloat16
i32 = jnp.int32

RWKV_HEADS = 8
HEAD_DIM = 64
RWKV_WIDTH = RWKV_HEADS * HEAD_DIM
DECAY_RANK = 64
AAA_RANK = 64
GATE_RANK = 128
RWKV_COLS = 3 * RWKV_WIDTH + DECAY_RANK + AAA_RANK + GATE_RANK
SWA_Q_HEADS = 8
SWA_KV_HEADS = 2
SWA_GROUP = SWA_Q_HEADS // SWA_KV_HEADS
SWA_WIDTH = SWA_Q_HEADS * HEAD_DIM
SWA_KV_WIDTH = SWA_KV_HEADS * HEAD_DIM
SWA_COLS = SWA_WIDTH + 2 * SWA_KV_WIDTH
WINDOW = 128
ROPE_THETA = 10000.0
MEM_HEADS = 4
N_EXPERTS = 256
TOP_K = 8
N_GROUPS = 8
GROUP_SIZE = N_EXPERTS // N_GROUPS
TOPK_GROUPS = 4
ROUTED_SCALE = 2.5
LN_EPS = 1e-5
GN_EPS = 64e-5
NEG_INF = -1e30

LANES = 128
SUBLANES = 8
WKV_CHUNK = 64
WKV_GROUP = 4
VMEM_LIMIT = 56 * 1024 * 1024

SC_CORES = 2
SC_SUBCORES = 16
SC_LANES = 16
SC_INDEX_GROUP = 128
SC_CHUNK_BYTES = 256 * 1024


def _sc_chunk_rows(nc, lanes, dtype):
    return min(SC_INDEX_GROUP, SC_CHUNK_BYTES // (nc * lanes * jnp.dtype(dtype).itemsize))


def _cparams(sem):
    return pltpu.CompilerParams(dimension_semantics=sem, vmem_limit_bytes=VMEM_LIMIT)


def _const_spec(shape):
    nd = len(shape)
    return pl.BlockSpec(shape, lambda *_: (0,) * nd)


def _dot(a, b):
    return jnp.dot(a, b, preferred_element_type=f32)


def _dot_nt(a, b):
    return lax.dot_general(a, b, (((1,), (1,)), ((), ())), preferred_element_type=f32)


def _dot_tn(a, b):
    return lax.dot_general(a, b, (((0,), (0,)), ((), ())), preferred_element_type=f32)


def _split2(x):
    hi = x.astype(bf16)
    lo = (x - hi.astype(f32)).astype(bf16)
    return hi, lo


def _seg_sums(xs, seg_b):
    parts = []
    for x in xs:
        parts.extend(_split2(x))
    out = _dot(jnp.concatenate(parts, axis=0), seg_b)
    t = xs[0].shape[0]
    return [out[2 * i * t:(2 * i + 1) * t] + out[(2 * i + 1) * t:(2 * i + 2) * t] for i in range(len(xs))]


def _dot_hp(a, b):
    ah, al = _split2(a)
    bh, bl = _split2(b)
    return _dot(ah, bh) + _dot(ah, bl) + _dot(al, bh)


def _dot_exact_lhs(m_bf16, x):
    hi, lo = _split2(x)
    return _dot(m_bf16, hi) + _dot(m_bf16, lo)


def _sigmoid(x):
    return 1.0 / (1.0 + jnp.exp(-x))


def _lane_chunk(n_rows, n_chunks, c):
    return (pl.ds(c, n_rows, stride=n_chunks), slice(None))


def _pack_bf16_pairs(x):
    chunks = []
    for j in range(x.shape[1] // (2 * LANES)):
        lo = x[:, 2 * j * LANES:(2 * j + 1) * LANES].astype(bf16).astype(f32)
        hi = x[:, (2 * j + 1) * LANES:(2 * j + 2) * LANES].astype(bf16).astype(f32)
        chunks.append(lax.bitcast_convert_type(hi, i32) | lax.shift_right_logical(lax.bitcast_convert_type(lo, i32), 16))
    return chunks


def _unpack_bf16_pairs(chunks):
    cols = []
    for w in chunks:
        cols.append(lax.bitcast_convert_type(lax.shift_left(w, 16), f32))
        cols.append(lax.bitcast_convert_type(w & jnp.int32(-65536), f32))
    return jnp.concatenate(cols, axis=1)


def _layer_norm(h, g, b):
    mu = jnp.mean(h, axis=-1, keepdims=True)
    d = h - mu
    var = jnp.mean(d * d, axis=-1, keepdims=True)
    return d * lax.rsqrt(var + LN_EPS) * g + b


def _proj_body(*refs, n_out):
    x_ref = refs[0]
    w_refs = refs[1:1 + n_out]
    o_refs = refs[1 + n_out:]
    xb = x_ref[...].astype(bf16)
    for w_ref, o_ref in zip(w_refs, o_refs):
        o_ref[...] = _dot(xb, w_ref[...]).astype(o_ref.dtype)


def _proj(x, ws, out_dtypes, tile):
    n, k = x.shape
    tile = min(tile, n)
    outs = pl.pallas_call(
        functools.partial(_proj_body, n_out=len(ws)),
        out_shape=[jax.ShapeDtypeStruct((n, w.shape[1]), dt) for w, dt in zip(ws, out_dtypes)],
        grid=(n // tile,),
        in_specs=[pl.BlockSpec((tile, k), lambda i: (i, 0))] + [_const_spec(w.shape) for w in ws],
        out_specs=[pl.BlockSpec((tile, w.shape[1]), lambda i: (i, 0)) for w in ws],
        compiler_params=_cparams(("parallel",)),
        name="proj",
    )(x, *ws)
    return outs


def _wkv_chunks(chains, states, masks):
    bd_b, bd, strict, incl, eye, eye_full = masks
    c, n = chains[0][1].shape
    nch = len(chains)

    def stack(x_b):
        return jnp.where(bd_b, jnp.concatenate([x_b] * WKV_GROUP, axis=0), jnp.zeros((), bf16))

    cast = [tuple(x.astype(bf16) for x in ch[1:6]) for ch in chains]
    v_s = [stack(cb[4]) for cb in cast]
    g = [_dot_nt(jnp.concatenate([cb[0], cb[3]], axis=0), jnp.concatenate([stack(cb[1]), stack(cb[2])], axis=0))
         for cb in cast]
    l_ak = [jnp.where(strict, gi[:c, n:], 0.0).astype(bf16) for gi in g]
    m_rb = [jnp.where(incl, gi[c:, :n], 0.0).astype(bf16) for gi in g]
    m_rk = [jnp.where(incl, gi[c:, n:], 0.0).astype(bf16) for gi in g]
    x = [jnp.where(strict, gi[:c, :n], 0.0) for gi in g]
    t = [eye + xi for xi in x]
    for _ in range(5):
        xb = [xi.astype(bf16) for xi in x]
        x = [_dot(xi, stack(xi)) for xi in xb]
        t = [ti + _dot(ti.astype(bf16), stack(xi.astype(bf16))) for ti, xi in zip(t, x)]
    lakv = [_dot(l_ak[i], v_s[i]).astype(bf16) for i in range(nch)]
    au = [_dot(t[i].astype(bf16), jnp.concatenate([stack(cast[i][0]), stack(lakv[i])], axis=1))
          for i in range(nch)]
    abar = [a[:, :n].astype(bf16) for a in au]
    ubar = [a[:, n:].astype(bf16) for a in au]
    ry = [_dot(m_rb[i], jnp.concatenate([stack(abar[i]), stack(ubar[i])], axis=1)) for i in range(nch)]
    r_bar = [(chains[i][4] + ry[i][:, :n]).astype(bf16) for i in range(nch)]
    y_bar = [ry[i][:, n:] + _dot(m_rk[i], v_s[i]) for i in range(nch)]
    p = [((eye_full + jnp.where(bd, _dot_tn(abar[i], cast[i][1]), 0.0)) * chains[i][6]).astype(bf16)
         for i in range(nch)]
    q = []
    for i in range(nch):
        q_bd = jnp.where(bd, _dot_tn(jnp.concatenate([ubar[i], cast[i][4]], axis=0),
                                     jnp.concatenate([cast[i][1], cast[i][2]], axis=0)), 0.0)
        qi = q_bd[0:c]
        for h in range(1, WKV_GROUP):
            qi = qi + q_bd[h * c:(h + 1) * c]
        q.append(qi * chains[i][6])
    states = list(states)
    ys = []
    for i in range(nch):
        gi = chains[i][0]
        s_b = states[gi].astype(bf16)
        ys.append(_dot_nt(r_bar[i], stack(s_b)) + y_bar[i])
        states[gi] = _dot(s_b, p[i]) + q[i]
    return ys, states


def _rwkv_body(u_ref, mu_ref, wdec_ref, w0_ref, aup_ref, a0_ref, gup_ref, kk_ref, ka_ref, rk_ref,
               lng_ref, lnb_ref, seg_ref, tri_ref, y_ref, state_ref, carry_ref, *, tt):
    j = pl.program_id(1)

    @pl.when(j == 0)
    def _():
        state_ref[...] = jnp.zeros_like(state_ref)
        carry_ref[...] = jnp.zeros_like(carry_ref)

    w = RWKV_WIDTH
    u = u_ref[...]
    row = lax.broadcasted_iota(i32, u.shape, 0)
    prev = jnp.where(row == 0, carry_ref[0:1, :], pltpu.roll(u, 1, axis=0))
    carry_ref[0:1, :] = u[tt - 1:tt, :]
    us = u + (prev - u) * mu_ref[...]
    r = us[:, 0:w]
    k = us[:, w:2 * w]
    v = us[:, 2 * w:3 * w]
    wa = us[:, 3 * w:3 * w + DECAY_RANK + AAA_RANK]
    gd = us[:, 3 * w + DECAY_RANK + AAA_RANK:]
    z = w0_ref[...] + _dot_hp(jnp.tanh(wa), wdec_ref[...])
    softplus_neg_z = jnp.maximum(-z, 0.0) + jnp.log(1.0 + jnp.exp(-jnp.abs(z)))
    lw = -jnp.exp(-softplus_neg_z - 0.5)
    a = _sigmoid(a0_ref[...] + _dot_hp(wa, aup_ref[...]))
    gate = _dot(_sigmoid(gd).astype(bf16), gup_ref[...].astype(bf16))
    seg = seg_ref[...]
    kk = k * kk_ref[...]
    kmod = k * (1.0 + (a - 1.0) * ka_ref[...])
    kk_sq, bonus_dot = _seg_sums([kk * kk, r * kmod * rk_ref[...]], seg)
    kk = kk / jnp.maximum(jnp.sqrt(kk_sq), 1e-12)
    cum = _dot_exact_lhs(tri_ref[...], lw)
    wc = jnp.exp(cum)
    iwc = jnp.exp(-cum)
    at = -kk * jnp.exp(cum - lw)
    bt = kk * a * iwc
    kt = kmod * iwc
    rt = r * wc

    n = WKV_GROUP * HEAD_DIM
    ri = lax.broadcasted_iota(i32, (n, n), 0)
    ci = lax.broadcasted_iota(i32, (n, n), 1)
    bd = (ri // WKV_CHUNK) == (ci // HEAD_DIM)
    bd_b = jnp.where(bd, 1.0, 0.0).astype(bf16) > 0
    eye_full = jnp.where(ri == ci, 1.0, 0.0).astype(f32)
    ti = lax.broadcasted_iota(i32, (WKV_CHUNK, n), 0)
    si = lax.broadcasted_iota(i32, (WKV_CHUNK, n), 1) % WKV_CHUNK
    masks = (bd_b, bd, ti > si, ti >= si, jnp.where(ti == si, 1.0, 0.0).astype(f32), eye_full)

    n_groups = w // n
    n_chunks = tt // WKV_CHUNK
    chains = []
    for c in range(n_chunks):
        rs = slice(c * WKV_CHUNK, (c + 1) * WKV_CHUNK)
        last = (c + 1) * WKV_CHUNK - 1
        for gi in range(n_groups):
            cs = slice(gi * n, (gi + 1) * n)
            chains.append((gi, at[rs, cs], bt[rs, cs], kt[rs, cs], rt[rs, cs], v[rs, cs], wc[last:last + 1, cs]))
    ys, states = _wkv_chunks(chains, [state_ref[gi] for gi in range(n_groups)], masks)
    for gi in range(n_groups):
        state_ref[gi] = states[gi]
    y = jnp.concatenate([jnp.concatenate(ys[c * n_groups:(c + 1) * n_groups], axis=1) for c in range(n_chunks)],
                        axis=0)

    inv_n = 1.0 / HEAD_DIM
    d = y - _seg_sums([y], seg)[0] * inv_n
    var = _seg_sums([d * d], seg)[0] * inv_n
    yn = d * lax.rsqrt(var + GN_EPS) * lng_ref[...] + lnb_ref[...]
    y_ref[...] = ((yn + bonus_dot * v) * gate).astype(y_ref.dtype)


def _rwkv(u_r, mu_shift, w_decay_up, w0, a_up, a0, g_up, k_k, k_a, r_k, lnx_g, lnx_b, tt):
    b, s, cols = u_r.shape
    tt = min(tt, s)
    w = RWKV_WIDTH
    row = lambda p: p.reshape(1, -1).astype(f32)
    wdec = jnp.concatenate([w_decay_up, jnp.zeros((AAA_RANK, w), f32)], axis=0)
    aup = jnp.concatenate([jnp.zeros((DECAY_RANK, w), f32), a_up], axis=0)
    hid = jnp.arange(w) // HEAD_DIM
    seg = (hid[:, None] == hid[None, :]).astype(bf16)
    ti = jnp.arange(tt)
    tri = ((ti[:, None] // WKV_CHUNK == ti[None, :] // WKV_CHUNK) & (ti[:, None] >= ti[None, :])).astype(bf16)
    params = [row(mu_shift), wdec, row(w0), aup, row(a0), g_up, row(k_k), row(k_a), row(r_k), row(lnx_g),
              row(lnx_b), seg, tri]
    n = WKV_GROUP * HEAD_DIM
    return pl.pallas_call(
        functools.partial(_rwkv_body, tt=tt),
        out_shape=jax.ShapeDtypeStruct((b, s, w), bf16),
        grid=(b, s // tt),
        in_specs=[pl.BlockSpec((None, tt, cols), lambda bi, j: (bi, j, 0))] + [_const_spec(p.shape) for p in params],
        out_specs=pl.BlockSpec((None, tt, w), lambda bi, j: (bi, j, 0)),
        scratch_shapes=[pltpu.VMEM((w // n, HEAD_DIM, n), f32), pltpu.VMEM((SUBLANES, cols), f32)],
        compiler_params=_cparams(("parallel", "arbitrary")),
        name="rwkv7",
    )(u_r, *params)


def _swa_body(sink_ref, u_ref, pos_ref, invf_ref, o_ref, kprev_ref, vprev_ref):
    j = pl.program_id(1)

    @pl.when(j == 0)
    def _():
        kprev_ref[...] = jnp.zeros_like(kprev_ref)
        vprev_ref[...] = jnp.zeros_like(vprev_ref)

    wq = SWA_WIDTH
    kvw = SWA_KV_WIDTH
    u = u_ref[...]
    ang = pos_ref[...].astype(f32) * invf_ref[...]
    cos = jnp.cos(ang)
    sin = jnp.sin(ang)

    def rope(x, c, s):
        n = x.shape[1]
        lane = lax.broadcasted_iota(i32, x.shape, 1)
        half = HEAD_DIM // 2
        rot = jnp.where((lane % HEAD_DIM) < half, -pltpu.roll(x, n - half, axis=1), pltpu.roll(x, half, axis=1))
        return x * c + rot * s

    q = rope(u[:, :wq] * (HEAD_DIM ** -0.5), jnp.concatenate([cos] * (wq // LANES), axis=1),
             jnp.concatenate([sin] * (wq // LANES), axis=1))
    k_cur = rope(u[:, wq:wq + kvw], cos, sin)
    v_cur = u[:, wq + kvw:]
    k_prev = kprev_ref[...]
    v_prev = vprev_ref[...]
    kprev_ref[...] = k_cur
    vprev_ref[...] = v_cur

    gw = SWA_GROUP * HEAD_DIM
    rows = SWA_GROUP * WINDOW
    lane_kv = lax.broadcasted_iota(i32, (WINDOW, kvw), 1)

    def rep(x, gi):
        sw = pltpu.roll(x, HEAD_DIM, axis=1)
        one = jnp.where((lane_kv // HEAD_DIM) == gi, x, sw)
        return jnp.concatenate([one] * (gw // kvw), axis=1).astype(bf16)

    ri = lax.broadcasted_iota(i32, (rows, WINDOW), 0)
    ci = lax.broadcasted_iota(i32, (rows, WINDOW), 1)
    t_idx = ri % WINDOW
    mask_prev = ci > t_idx + jnp.where(j > 0, 0, WINDOW)
    mask_cur = ci <= t_idx
    rb = lax.broadcasted_iota(i32, (rows, 1), 0) // WINDOW
    rbo = lax.broadcasted_iota(i32, (rows, gw), 0) // WINDOW
    cbo = lax.broadcasted_iota(i32, (rows, gw), 1) // HEAD_DIM
    bd = rbo == cbo
    outs = []
    for gi in range(SWA_KV_HEADS):
        qg = q[:, gi * gw:(gi + 1) * gw]
        q_bd = jnp.where(bd, jnp.concatenate([qg] * SWA_GROUP, axis=0), 0.0).astype(bf16)
        s_prev = jnp.where(mask_prev, _dot_nt(q_bd, rep(k_prev, gi)), NEG_INF)
        s_cur = jnp.where(mask_cur, _dot_nt(q_bd, rep(k_cur, gi)), NEG_INF)
        sink = jnp.zeros((rows, 1), f32)
        for h in range(SWA_GROUP):
            sink = jnp.where(rb == h, sink_ref[gi * SWA_GROUP + h], sink)
        m = jnp.maximum(jnp.maximum(jnp.max(s_prev, axis=-1, keepdims=True),
                                    jnp.max(s_cur, axis=-1, keepdims=True)), sink)
        p_prev = jnp.exp(s_prev - m)
        p_cur = jnp.exp(s_cur - m)
        denom = jnp.sum(p_prev, axis=-1, keepdims=True) + jnp.sum(p_cur, axis=-1, keepdims=True) + jnp.exp(sink - m)
        o_bd = _dot(p_prev.astype(bf16), rep(v_prev, gi)) + _dot(p_cur.astype(bf16), rep(v_cur, gi))
        o_bd = jnp.where(bd, o_bd / denom, 0.0)
        og = o_bd[0:WINDOW]
        for h in range(1, SWA_GROUP):
            og = og + o_bd[h * WINDOW:(h + 1) * WINDOW]
        outs.append(og)
    o_ref[...] = jnp.concatenate(outs, axis=1).astype(o_ref.dtype)


def _swa(u_s, positions, sinks):
    b, s, cols = u_s.shape
    half = HEAD_DIM // 2
    inv_freq = ROPE_THETA ** (-jnp.arange(0, HEAD_DIM, 2, dtype=f32) / HEAD_DIM)
    invf = jnp.tile(inv_freq, LANES // half).reshape(1, LANES)
    pos = positions.reshape(b, s, 1).astype(i32)
    return pl.pallas_call(
        _swa_body,
        out_shape=jax.ShapeDtypeStruct((b, s, SWA_WIDTH), bf16),
        grid=(b, s // WINDOW),
        in_specs=[pl.BlockSpec(memory_space=pltpu.SMEM),
                  pl.BlockSpec((None, WINDOW, cols), lambda bi, j: (bi, j, 0)),
                  pl.BlockSpec((None, WINDOW, 1), lambda bi, j: (bi, j, 0)),
                  _const_spec((1, LANES))],
        out_specs=pl.BlockSpec((None, WINDOW, SWA_WIDTH), lambda bi, j: (bi, j, 0)),
        scratch_shapes=[pltpu.VMEM((WINDOW, SWA_KV_WIDTH), f32), pltpu.VMEM((WINDOW, SWA_KV_WIDTH), f32)],
        compiler_params=_cparams(("parallel", "arbitrary")),
        name="swa",
    )(sinks.astype(f32), u_s, pos, invf)


def _mix_out_body(ya_ref, yb_ref, wa_ref, wb_ref, x_ref, g_ref, b_ref, o_ref, *, alpha):
    mix = _dot(ya_ref[...], wa_ref[...]) + _dot(yb_ref[...], wb_ref[...])
    o_ref[...] = _layer_norm(alpha * x_ref[...] + mix, g_ref[...], b_ref[...])


def _mix_out(ya, yb, w_o, x, g, b, alpha, tile):
    n, d = x.shape
    tile = min(tile, n)
    wa = w_o[:ya.shape[1]].astype(bf16)
    wb = w_o[ya.shape[1]:].astype(bf16)
    rows = lambda width: pl.BlockSpec((tile, width), lambda i: (i, 0))
    return pl.pallas_call(
        functools.partial(_mix_out_body, alpha=alpha),
        out_shape=jax.ShapeDtypeStruct((n, d), f32),
        grid=(n // tile,),
        in_specs=[rows(ya.shape[1]), rows(yb.shape[1]), _const_spec(wa.shape), _const_spec(wb.shape), rows(d),
                  _const_spec((1, d)), _const_spec((1, d))],
        out_specs=rows(d),
        compiler_params=_cparams(("parallel",)),
        name="mix_out_ln1",
    )(ya, yb, wa, wb, x, g.reshape(1, d), b.reshape(1, d))


def _xattn_body(x_ref, kv_ref, wq_ref, wo_ref, g_ref, b_ref, o_ref, o3_ref, *, alpha):
    x = x_ref[...]
    d = x.shape[1]
    hd = d // MEM_HEADS
    q = _dot(x.astype(bf16), wq_ref[...]) * (hd ** -0.5)
    kv = kv_ref[...]
    outs = []
    for h in range(MEM_HEADS):
        qh = q[:, h * hd:(h + 1) * hd].astype(bf16)
        kh = kv[:, h * hd:(h + 1) * hd]
        vh = kv[:, d + h * hd:d + (h + 1) * hd]
        s = _dot_nt(qh, kh)
        p = jnp.exp(s - jnp.max(s, axis=-1, keepdims=True))
        l = jnp.sum(p, axis=-1, keepdims=True)
        outs.append(_dot(p.astype(bf16), vh) / l)
    o = jnp.concatenate(outs, axis=1)
    xa = _dot(o.astype(bf16), wo_ref[...])
    y = _layer_norm(alpha * x + xa, g_ref[...], b_ref[...])
    o_ref[...] = y
    words = _pack_bf16_pairs(y)
    for j, w in enumerate(words):
        o3_ref[_lane_chunk(x.shape[0], len(words), j)] = w


def _xattn(x1, kv, wm_q, wm_o, g, b, alpha, tile):
    bsz, s, d = x1.shape
    m = kv.shape[1]
    tile = min(tile, s)
    nj = s // tile
    wq = wm_q.astype(bf16)
    wo = wm_o.astype(bf16)
    return pl.pallas_call(
        functools.partial(_xattn_body, alpha=alpha),
        out_shape=[jax.ShapeDtypeStruct((bsz, s, d), f32),
                   jax.ShapeDtypeStruct((bsz * s * (d // (2 * LANES)), LANES), i32)],
        grid=(bsz, nj),
        in_specs=[pl.BlockSpec((None, tile, d), lambda bi, j: (bi, j, 0)),
                  pl.BlockSpec((None, m, 2 * d), lambda bi, j: (bi, 0, 0)),
                  _const_spec(wq.shape), _const_spec(wo.shape), _const_spec((1, d)), _const_spec((1, d))],
        out_specs=[pl.BlockSpec((None, tile, d), lambda bi, j: (bi, j, 0)),
                   pl.BlockSpec((tile * (d // (2 * LANES)), LANES), lambda bi, j: (bi * nj + j, 0))],
        compiler_params=_cparams(("parallel", "parallel")),
        name="mem_xattn_ln2",
    )(x1, kv, wq, wo, g.reshape(1, d), b.reshape(1, d))


def _router_body(x_ref, wt_ref, bias_ref, upper_ref, e_ref, g_ref, r_ref, cnt_out_ref, cnt_ref, *, t):
    @pl.when(pl.program_id(0) == 0)
    def _():
        cnt_ref[...] = jnp.zeros_like(cnt_ref)

    xh, xl = _split2(x_ref[...])
    wh, wl = _split2(wt_ref[...])
    logits = _dot_nt(wh, xh) + _dot_nt(wh, xl) + _dot_nt(wl, xh)
    scores = _sigmoid(logits)
    biased = scores + bias_ref[...][:, 0:1]
    ne = N_EXPERTS
    neg = -jnp.inf

    def top1(vals):
        rows = lax.broadcasted_iota(i32, vals.shape, 0).astype(f32)
        m = jnp.max(vals, axis=0, keepdims=True)
        idx = jnp.min(jnp.where(vals == m, rows, float(vals.shape[0])), axis=0, keepdims=True)
        return m, idx, rows == idx

    gscores = []
    for gi in range(N_GROUPS):
        blk = biased[gi * GROUP_SIZE:(gi + 1) * GROUP_SIZE, :]
        m1, _, hit = top1(blk)
        m2 = jnp.max(jnp.where(hit, neg, blk), axis=0, keepdims=True)
        gscores.append(m1 + m2)
    gs = jnp.concatenate(gscores, axis=0)
    gsel = jnp.zeros(gs.shape, f32)
    for _ in range(TOPK_GROUPS):
        _, _, hit = top1(gs)
        gsel = jnp.where(hit, 1.0, gsel)
        gs = jnp.where(hit, neg, gs)
    emask = jnp.concatenate(
        [jnp.broadcast_to(gsel[gi:gi + 1, :], (GROUP_SIZE, t)) for gi in range(N_GROUPS)], axis=0) > 0.5
    cand = jnp.where(emask, biased, NEG_INF)
    idxs, sels = [], []
    chosen = jnp.zeros((ne, t), f32)
    for _ in range(TOP_K):
        _, idx, hit = top1(cand)
        idxs.append(idx)
        sels.append(jnp.sum(jnp.where(hit, scores, 0.0), axis=0, keepdims=True))
        chosen = chosen + jnp.where(hit, 1.0, 0.0)
        cand = jnp.where(hit, neg, cand)
    sel = jnp.concatenate(sels, axis=0)
    g_ref[...] = sel / jnp.sum(sel, axis=0, keepdims=True) * ROUTED_SCALE
    e_ref[...] = jnp.concatenate(idxs, axis=0).astype(i32)
    before = _dot(chosen.astype(bf16), upper_ref[...]) + cnt_ref[...][:, 0:1]
    rows = lax.broadcasted_iota(i32, (ne, t), 0).astype(f32)
    ranks = [jnp.sum(jnp.where(rows == idx, before, 0.0), axis=0, keepdims=True) for idx in idxs]
    r_ref[...] = jnp.concatenate(ranks, axis=0).astype(i32)
    cnt_ref[...] = cnt_ref[...] + jnp.sum(chosen, axis=1, keepdims=True)
    cnt_out_ref[...] = cnt_ref[...].astype(i32)


def _router(x2, w_router, router_bias, tile):
    n, d = x2.shape
    ne = N_EXPERTS
    t = min(tile, n)
    wt = w_router.T
    bias = jnp.broadcast_to(router_bias.reshape(ne, 1).astype(f32), (ne, LANES))
    ti = jnp.arange(t)
    upper = (ti[:, None] < ti[None, :]).astype(bf16)
    cols = pl.BlockSpec((TOP_K, t), lambda i: (0, i))
    e_t, g_t, r_t, cnt = pl.pallas_call(
        functools.partial(_router_body, t=t),
        out_shape=[jax.ShapeDtypeStruct((TOP_K, n), i32), jax.ShapeDtypeStruct((TOP_K, n), f32),
                   jax.ShapeDtypeStruct((TOP_K, n), i32), jax.ShapeDtypeStruct((ne, LANES), i32)],
        grid=(n // t,),
        in_specs=[pl.BlockSpec((t, d), lambda i: (i, 0)), _const_spec((ne, d)), _const_spec((ne, LANES)),
                  _const_spec((t, t))],
        out_specs=[cols, cols, cols, _const_spec((ne, LANES))],
        scratch_shapes=[pltpu.VMEM((ne, LANES), f32)],
        compiler_params=_cparams(("arbitrary",)),
        name="router",
    )(x2, wt, bias, upper)
    return e_t, g_t, r_t, cnt[:, 0]


def _dispatch_sc(x3d, e_t, r_t, g_t, starts):
    n, nc, lanes = x3d.shape
    workers = SC_CORES * SC_SUBCORES
    per_w = n // workers
    tc = _sc_chunk_rows(nc, lanes, x3d.dtype)
    assert n % (workers * SC_INDEX_GROUP) == 0 and SC_INDEX_GROUP % tc == 0
    mesh = plsc.VectorSubcoreMesh(core_axis_name="c", subcore_axis_name="s")

    @functools.partial(
        pl.kernel, mesh=mesh,
        out_type=[jax.ShapeDtypeStruct((n * TOP_K, nc, lanes), x3d.dtype),
                  jax.ShapeDtypeStruct((n * TOP_K,), f32)],
        scratch_types=[
            pltpu.VMEM((tc, nc, lanes), x3d.dtype),
            pltpu.VMEM((TOP_K, SC_INDEX_GROUP), i32),
            pltpu.VMEM((TOP_K, SC_INDEX_GROUP), i32),
            pltpu.VMEM((TOP_K, SC_INDEX_GROUP), f32),
            pltpu.VMEM((TOP_K, tc), i32),
            pltpu.VMEM((TOP_K, tc), f32),
            pltpu.VMEM((N_EXPERTS,), i32),
            pltpu.SemaphoreType.DMA,
        ],
        compiler_params=pltpu.CompilerParams(use_tc_tiling_on_sc=True, needs_layout_passes=False),
    )
    def dispatch(x_hbm, e_hbm, r_hbm, g_hbm, st_hbm, o_hbm, gs_hbm, rows_v, e_v, r_v, g_v, slot_v, gate_v, st_v, sem):
        wid = lax.axis_index("s") * SC_CORES + lax.axis_index("c")
        pltpu.sync_copy(st_hbm, st_v)

        @pl.loop(0, per_w // SC_INDEX_GROUP)
        def _(gi):
            base = wid * per_w + gi * SC_INDEX_GROUP
            pltpu.sync_copy(e_hbm.at[:, pl.ds(base, SC_INDEX_GROUP)], e_v)
            pltpu.sync_copy(r_hbm.at[:, pl.ds(base, SC_INDEX_GROUP)], r_v)
            pltpu.sync_copy(g_hbm.at[:, pl.ds(base, SC_INDEX_GROUP)], g_v)
            for h in range(SC_INDEX_GROUP // tc):
                off = h * tc
                pltpu.sync_copy(x_hbm.at[pl.ds(base + off, tc)], rows_v)
                for kk in range(TOP_K):
                    for j in range(tc // SC_LANES):
                        src = pl.ds(off + j * SC_LANES, SC_LANES)
                        dst = pl.ds(j * SC_LANES, SC_LANES)
                        slot_v[kk, dst] = r_v[kk, src] + plsc.load_gather(st_v, [e_v[kk, src]])
                        gate_v[kk, dst] = g_v[kk, src]
                copies = [pltpu.async_copy(rows_v, o_hbm.at[slot_v.at[kk]], sem) for kk in range(TOP_K)]
                copies += [pltpu.async_copy(gate_v.at[kk], gs_hbm.at[slot_v.at[kk]], sem) for kk in range(TOP_K)]
                for cp in copies:
                    cp.wait()

    return dispatch(x3d, e_t, r_t, g_t, starts)


def _gmm_body(gid_ref, tid_ref, lo_ref, hi_ref, first_ref, newg_ref, x_ref, g_ref, wg_ref, wu_ref, wd_ref, o_ref,
              wg_b, wu_b, wd_b, *, tm):
    v = pl.program_id(0)
    lo = lo_ref[v]
    hi = hi_ref[v]
    row0 = tid_ref[v] * tm
    nc = wg_ref.shape[0] // LANES
    full = (lo <= row0) & (hi >= row0 + tm)

    @pl.when(newg_ref[v] == 1)
    def _():
        wg_b[...] = wg_ref[...].astype(bf16)
        wu_b[...] = wu_ref[...].astype(bf16)
        wd_b[...] = wd_ref[...].astype(bf16)

    def ffn():
        nw = nc // 2
        x = _unpack_bf16_pairs([x_ref[_lane_chunk(tm, nw, j)] for j in range(nw)]).astype(bf16)
        hg = _dot(x, wg_b[...])
        h = hg * _sigmoid(hg) * _dot(x, wu_b[...])
        g = g_ref[...]
        g_cols = jnp.concatenate([g, jnp.zeros((SUBLANES - g.shape[0], LANES), f32)], axis=0).T
        h = jnp.concatenate([h[r * LANES:(r + 1) * LANES, :] * g_cols[:, r:r + 1] for r in range(tm // LANES)],
                            axis=0)
        return _dot(h.astype(bf16), wd_b[...])

    @pl.when(full)
    def _():
        y = ffn()
        for c in range(nc):
            o_ref[_lane_chunk(tm, nc, c)] = y[:, c * LANES:(c + 1) * LANES]

    @pl.when(jnp.logical_not(full) & (hi > lo))
    def _():
        @pl.when(first_ref[v] == 1)
        def _():
            o_ref[...] = jnp.zeros_like(o_ref)

        rows = row0 + lax.broadcasted_iota(i32, (tm, 1), 0)
        mask = (rows >= lo) & (rows < hi)
        y = ffn()
        for c in range(nc):
            idx = _lane_chunk(tm, nc, c)
            o_ref[idx] = jnp.where(mask, y[:, c * LANES:(c + 1) * LANES], o_ref[idx])


def _gmm(xs, gates, we_gate, we_up, we_down, layer, counts, tm):
    _, ne, d, de = we_gate.shape
    nc = d // LANES
    nw = nc // 2
    nk = xs.shape[0] // nw
    tm = min(tm, nk)
    assert tm % LANES == 0 and tm // LANES <= SUBLANES
    n_tiles = nk // tm
    gates3 = gates.reshape(n_tiles, tm // LANES, LANES)
    n_visits = n_tiles + ne - 1
    ends = jnp.cumsum(counts)
    starts = ends - counts
    tile_lo = starts // tm
    n_touch = jnp.where(counts > 0, (ends - 1) // tm - tile_lo + 1, 0)
    vis_end = jnp.cumsum(n_touch)
    vis_start = vis_end - n_touch
    vi = jnp.arange(n_visits, dtype=i32)
    valid = vi < vis_end[-1]
    gid = jnp.minimum(jnp.sum((vis_end[None, :] <= vi[:, None]).astype(i32), axis=1), ne - 1)
    onehot = gid[:, None] == jnp.arange(ne, dtype=i32)[None, :]
    pick = lambda table: jnp.sum(jnp.where(onehot, table[None, :], 0), axis=1)
    tid = jnp.where(valid, pick(tile_lo) + vi - pick(vis_start), n_tiles - 1).astype(i32)
    lo = jnp.where(valid, pick(starts), 0).astype(i32)
    hi = jnp.where(valid, pick(ends), 0).astype(i32)
    one = jnp.ones((1,), i32)
    first = jnp.concatenate([one, (tid[1:] != tid[:-1]).astype(i32)])
    newg = jnp.concatenate([one, (gid[1:] != gid[:-1]).astype(i32)])
    wspec = lambda shape: pl.BlockSpec((None, None) + shape, lambda v, g, *_: (layer, g[v], 0, 0))
    rows = lambda chunks: pl.BlockSpec((tm * chunks, LANES), lambda v, g, t, *_: (t[v], 0))
    return pl.pallas_call(
        functools.partial(_gmm_body, tm=tm),
        out_shape=jax.ShapeDtypeStruct((nk * nc, LANES), f32),
        grid_spec=pltpu.PrefetchScalarGridSpec(
            num_scalar_prefetch=6, grid=(n_visits,),
            in_specs=[rows(nw), pl.BlockSpec((None, tm // LANES, LANES), lambda v, g, t, *_: (t[v], 0, 0)),
                      wspec((d, de)), wspec((d, de)), wspec((de, d))],
            out_specs=rows(nc),
            scratch_shapes=[pltpu.VMEM((d, de), bf16), pltpu.VMEM((d, de), bf16), pltpu.VMEM((de, d), bf16)]),
        compiler_params=_cparams(("arbitrary",)),
        name="moe_experts",
    )(gid, tid, lo, hi, first, newg, xs, gates3, we_gate, we_up, we_down)


def _combine_sc(ys3d, e_t, r_t, starts, n):
    _, nc, lanes = ys3d.shape
    workers = SC_CORES * SC_SUBCORES
    per_w = n // workers
    tc = _sc_chunk_rows(nc, lanes, f32)
    assert n % (workers * SC_INDEX_GROUP) == 0 and SC_INDEX_GROUP % tc == 0
    mesh = plsc.VectorSubcoreMesh(core_axis_name="c", subcore_axis_name="s")

    @functools.partial(
        pl.kernel, mesh=mesh,
        out_type=jax.ShapeDtypeStruct((n, nc, lanes), f32),
        scratch_types=[
            pltpu.VMEM((tc, nc, lanes), f32),
            pltpu.VMEM((TOP_K, SC_INDEX_GROUP), i32),
            pltpu.VMEM((TOP_K, SC_INDEX_GROUP), i32),
            pltpu.VMEM((TOP_K, tc), i32),
            pltpu.VMEM((N_EXPERTS,), i32),
            pltpu.SemaphoreType.DMA,
        ],
        compiler_params=pltpu.CompilerParams(use_tc_tiling_on_sc=True, needs_layout_passes=False),
    )
    def combine(y_hbm, e_hbm, r_hbm, st_hbm, o_hbm, acc_v, e_v, r_v, slot_v, st_v, sem):
        wid = lax.axis_index("s") * SC_CORES + lax.axis_index("c")
        pltpu.sync_copy(st_hbm, st_v)

        @pl.loop(0, per_w // SC_INDEX_GROUP)
        def _(gi):
            base = wid * per_w + gi * SC_INDEX_GROUP
            pltpu.sync_copy(e_hbm.at[:, pl.ds(base, SC_INDEX_GROUP)], e_v)
            pltpu.sync_copy(r_hbm.at[:, pl.ds(base, SC_INDEX_GROUP)], r_v)
            for h in range(SC_INDEX_GROUP // tc):
                off = h * tc
                for kk in range(TOP_K):
                    for j in range(tc // SC_LANES):
                        src = pl.ds(off + j * SC_LANES, SC_LANES)
                        slot_v[kk, pl.ds(j * SC_LANES, SC_LANES)] = (
                            r_v[kk, src] + plsc.load_gather(st_v, [e_v[kk, src]]))
                pltpu.async_copy(y_hbm.at[slot_v.at[0]], acc_v, sem).wait()
                copies = [pltpu.async_copy(y_hbm.at[slot_v.at[kk]], acc_v, sem, add=True) for kk in range(1, TOP_K)]
                for cp in copies:
                    cp.wait()
                pltpu.sync_copy(acc_v, o_hbm.at[pl.ds(base + off, tc)])

    return combine(ys3d, e_t, r_t, starts)


def _ffn_out_body(x_ref, r_ref, wg_ref, wu_ref, wd_ref, g_ref, b_ref, o_ref, *, alpha):
    x = x_ref[...]
    xb = x.astype(bf16)
    hg = _dot(xb, wg_ref[...])
    h = hg * _sigmoid(hg) * _dot(xb, wu_ref[...])
    shared = _dot(h.astype(bf16), wd_ref[...])
    nc = x.shape[1] // LANES
    routed = jnp.concatenate([r_ref[_lane_chunk(x.shape[0], nc, c)] for c in range(nc)], axis=1)
    o_ref[...] = _layer_norm(alpha * x + routed + shared, g_ref[...], b_ref[...])


def _ffn_out(x2, routed_rows, ws_gate, ws_up, ws_down, g, b, alpha, tile):
    n, d = x2.shape
    tile = min(tile, n)
    wg, wu, wd = ws_gate.astype(bf16), ws_up.astype(bf16), ws_down.astype(bf16)
    rows = pl.BlockSpec((tile, d), lambda i: (i, 0))
    return pl.pallas_call(
        functools.partial(_ffn_out_body, alpha=alpha),
        out_shape=jax.ShapeDtypeStruct((n, d), f32),
        grid=(n // tile,),
        in_specs=[rows, pl.BlockSpec((tile * (d // LANES), LANES), lambda i: (i, 0)), _const_spec(wg.shape),
                  _const_spec(wu.shape), _const_spec(wd.shape), _const_spec((1, d)), _const_spec((1, d))],
        out_specs=rows,
        compiler_params=_cparams(("parallel",)),
        name="ffn_out_ln3",
    )(x2, routed_rows, wg, wu, wd, g.reshape(1, d), b.reshape(1, d))


def _layer(x, mem, positions, w_in, mu_shift, w_decay_up, w0, a_up, a0, g_up, k_k, k_a, r_k, lnx_g, lnx_b, sinks,
           w_o, ln1_g, ln1_b, wm_q, wm_kv, wm_o, ln2_g, ln2_b, w_router, router_bias, we_gate, we_up, we_down,
           ws_gate, ws_up, ws_down, ln3_g, ln3_b, *, layer, alpha):
    b, s, d = x.shape
    n = b * s
    xf = x.reshape(n, d)
    w_in_b = w_in.astype(bf16)
    u_r, u_s = _proj(xf, [w_in_b[:, :RWKV_COLS], w_in_b[:, RWKV_COLS:]], [f32, f32], tile=512)
    y_r = _rwkv(u_r.reshape(b, s, RWKV_COLS), mu_shift, w_decay_up, w0, a_up, a0, g_up, k_k, k_a, r_k, lnx_g, lnx_b,
                tt=256)
    y_s = _swa(u_s.reshape(b, s, SWA_COLS), positions, sinks)
    x1 = _mix_out(y_r.reshape(n, RWKV_WIDTH), y_s.reshape(n, SWA_WIDTH), w_o, xf, ln1_g, ln1_b, alpha, tile=512)
    m = mem.shape[1]
    (kv,) = _proj(mem.reshape(b * m, d), [wm_kv.astype(bf16)], [bf16], tile=512)
    x2, x2_rows = _xattn(x1.reshape(b, s, d), kv.reshape(b, m, 2 * d), wm_q, wm_o, ln2_g, ln2_b, alpha, tile=512)
    x2 = x2.reshape(n, d)
    e_t, g_t, r_t, counts = _router(x2, w_router, router_bias, tile=512)
    starts = (jnp.cumsum(counts) - counts).astype(i32)
    nc = d // LANES
    nw = nc // 2
    xs, gs = _dispatch_sc(x2_rows.reshape(n, nw, LANES), e_t, r_t, g_t, starts)
    ys = _gmm(xs.reshape(n * TOP_K * nw, LANES), gs, we_gate, we_up, we_down, layer, counts, tm=512)
    routed = _combine_sc(ys.reshape(n * TOP_K, nc, LANES), e_t, r_t, starts, n)
    x3 = _ffn_out(x2, routed.reshape(n * nc, LANES), ws_gate, ws_up, ws_down, ln3_g, ln3_b, alpha, tile=512)
    return x3.reshape(b, s, d)


def kernel(x, mem, positions, w_in, mu_shift, w_decay_up, w0, a_up, a0, g_up, k_k, k_a, r_k, lnx_g, lnx_b, sinks, w_o, ln1_g, ln1_b, wm_q, wm_kv, wm_o, ln2_g, ln2_b, w_router, router_bias, we_gate, we_up, we_down, ws_gate, ws_up, ws_down, ln3_g, ln3_b):
    depth = w_in.shape[0]
    alpha = (2 * depth) ** 0.25
    for l in range(depth):
        x = _layer(x, mem, positions, w_in[l], mu_shift[l], w_decay_up[l], w0[l], a_up[l], a0[l], g_up[l], k_k[l],
                   k_a[l], r_k[l], lnx_g[l], lnx_b[l], sinks[l], w_o[l], ln1_g[l], ln1_b[l], wm_q[l], wm_kv[l],
                   wm_o[l], ln2_g[l], ln2_b[l], w_router[l], router_bias[l], we_gate, we_up, we_down,
                   ws_gate[l], ws_up[l], ws_down[l], ln3_g[l], ln3_b[l], layer=l, alpha=alpha)
    return x
```

```python
import functools

import jax
import jax.numpy as jnp
from jax import lax
from jax.experimental import pallas as pl
from jax.experimental.pallas import tpu as pltpu
from jax.experimental.pallas import tpu_sc as plsc

f32 = jnp.float32
bf16 = jnp.bfloat16
i32 = jnp.int32

RWKV_HEADS = 8
HEAD_DIM = 64
RWKV_WIDTH = RWKV_HEADS * HEAD_DIM
DECAY_RANK = 64
AAA_RANK = 64
GATE_RANK = 128
RWKV_COLS = 3 * RWKV_WIDTH + DECAY_RANK + AAA_RANK + GATE_RANK
SWA_Q_HEADS = 8
SWA_KV_HEADS = 2
SWA_GROUP = SWA_Q_HEADS // SWA_KV_HEADS
SWA_WIDTH = SWA_Q_HEADS * HEAD_DIM
SWA_KV_WIDTH = SWA_KV_HEADS * HEAD_DIM
SWA_COLS = SWA_WIDTH + 2 * SWA_KV_WIDTH
WINDOW = 128
ROPE_THETA = 10000.0
MEM_HEADS = 4
N_EXPERTS = 256
TOP_K = 8
N_GROUPS = 8
GROUP_SIZE = N_EXPERTS // N_GROUPS
TOPK_GROUPS = 4
ROUTED_SCALE = 2.5
LN_EPS = 1e-5
GN_EPS = 64e-5
NEG_INF = -1e30

LANES = 128
SUBLANES = 8
WKV_CHUNK = 64
WKV_GROUP = 2
VMEM_LIMIT = 56 * 1024 * 1024

SC_CORES = 2
SC_SUBCORES = 16
SC_LANES = 16
SC_INDEX_GROUP = 128
SC_CHUNK_BYTES = 256 * 1024


def _sc_chunk_rows(nc, lanes, dtype):
    return min(SC_INDEX_GROUP, SC_CHUNK_BYTES // (nc * lanes * jnp.dtype(dtype).itemsize))


def _cparams(sem):
    return pltpu.CompilerParams(dimension_semantics=sem, vmem_limit_bytes=VMEM_LIMIT)


def _const_spec(shape):
    nd = len(shape)
    return pl.BlockSpec(shape, lambda *_: (0,) * nd)


def _dot(a, b):
    return jnp.dot(a, b, preferred_element_type=f32)


def _dot_nt(a, b):
    return lax.dot_general(a, b, (((1,), (1,)), ((), ())), preferred_element_type=f32)


def _dot_tn(a, b):
    return lax.dot_general(a, b, (((0,), (0,)), ((), ())), preferred_element_type=f32)


def _split2(x):
    hi = x.astype(bf16)
    lo = (x - hi.astype(f32)).astype(bf16)
    return hi, lo


def _seg_sums(xs, seg_b):
    parts = []
    for x in xs:
        parts.extend(_split2(x))
    out = _dot(jnp.concatenate(parts, axis=0), seg_b)
    t = xs[0].shape[0]
    return [out[2 * i * t:(2 * i + 1) * t] + out[(2 * i + 1) * t:(2 * i + 2) * t] for i in range(len(xs))]


def _dot_hp(a, b):
    ah, al = _split2(a)
    bh, bl = _split2(b)
    return _dot(ah, bh) + _dot(ah, bl) + _dot(al, bh)


def _dot_exact_lhs(m_bf16, x):
    hi, lo = _split2(x)
    return _dot(m_bf16, hi) + _dot(m_bf16, lo)


def _sigmoid(x):
    return 1.0 / (1.0 + jnp.exp(-x))


def _lane_chunk(n_rows, n_chunks, c):
    return (pl.ds(c, n_rows, stride=n_chunks), slice(None))


def _pack_bf16_pairs(x):
    chunks = []
    for j in range(x.shape[1] // (2 * LANES)):
        lo = x[:, 2 * j * LANES:(2 * j + 1) * LANES].astype(bf16).astype(f32)
        hi = x[:, (2 * j + 1) * LANES:(2 * j + 2) * LANES].astype(bf16).astype(f32)
        chunks.append(lax.bitcast_convert_type(hi, i32) | lax.shift_right_logical(lax.bitcast_convert_type(lo, i32), 16))
    return chunks


def _unpack_bf16_pairs(chunks):
    cols = []
    for w in chunks:
        cols.append(lax.bitcast_convert_type(lax.shift_left(w, 16), f32))
        cols.append(lax.bitcast_convert_type(w & jnp.int32(-65536), f32))
    return jnp.concatenate(cols, axis=1)


def _layer_norm(h, g, b):
    mu = jnp.mean(h, axis=-1, keepdims=True)
    d = h - mu
    var = jnp.mean(d * d, axis=-1, keepdims=True)
    return d * lax.rsqrt(var + LN_EPS) * g + b


def _proj_body(*refs, n_out):
    x_ref = refs[0]
    w_refs = refs[1:1 + n_out]
    o_refs = refs[1 + n_out:]
    xb = x_ref[...].astype(bf16)
    for w_ref, o_ref in zip(w_refs, o_refs):
        o_ref[...] = _dot(xb, w_ref[...]).astype(o_ref.dtype)


def _proj(x, ws, out_dtypes, tile):
    n, k = x.shape
    tile = min(tile, n)
    outs = pl.pallas_call(
        functools.partial(_proj_body, n_out=len(ws)),
        out_shape=[jax.ShapeDtypeStruct((n, w.shape[1]), dt) for w, dt in zip(ws, out_dtypes)],
        grid=(n // tile,),
        in_specs=[pl.BlockSpec((tile, k), lambda i: (i, 0))] + [_const_spec(w.shape) for w in ws],
        out_specs=[pl.BlockSpec((tile, w.shape[1]), lambda i: (i, 0)) for w in ws],
        compiler_params=_cparams(("parallel",)),
        name="proj",
    )(x, *ws)
    return outs


def _wkv_chunks(chains, states, masks):
    bd_b, bd, strict, incl, eye, eye_full = masks
    c, n = chains[0][1].shape
    nch = len(chains)

    def stack(x_b):
        return jnp.where(bd_b, jnp.concatenate([x_b] * WKV_GROUP, axis=0), jnp.zeros((), bf16))

    cast = [tuple(x.astype(bf16) for x in ch[1:6]) for ch in chains]
    v_s = [stack(cb[4]) for cb in cast]
    g = [_dot_nt(jnp.concatenate([cb[0], cb[3]], axis=0), jnp.concatenate([stack(cb[1]), stack(cb[2])], axis=0))
         for cb in cast]
    l_ak = [jnp.where(strict, gi[:c, n:], 0.0).astype(bf16) for gi in g]
    m_rb = [jnp.where(incl, gi[c:, :n], 0.0).astype(bf16) for gi in g]
    m_rk = [jnp.where(incl, gi[c:, n:], 0.0).astype(bf16) for gi in g]
    x = [jnp.where(strict, gi[:c, :n], 0.0) for gi in g]
    t = [eye + xi for xi in x]
    for _ in range(5):
        xb = [xi.astype(bf16) for xi in x]
        x = [_dot(xi, stack(xi)) for xi in xb]
        t = [ti + _dot(ti.astype(bf16), stack(xi.astype(bf16))) for ti, xi in zip(t, x)]
    lakv = [_dot(l_ak[i], v_s[i]).astype(bf16) for i in range(nch)]
    au = [_dot(t[i].astype(bf16), jnp.concatenate([stack(cast[i][0]), stack(lakv[i])], axis=1))
          for i in range(nch)]
    abar = [a[:, :n].astype(bf16) for a in au]
    ubar = [a[:, n:].astype(bf16) for a in au]
    ry = [_dot(m_rb[i], jnp.concatenate([stack(abar[i]), stack(ubar[i])], axis=1)) for i in range(nch)]
    r_bar = [(chains[i][4] + ry[i][:, :n]).astype(bf16) for i in range(nch)]
    y_bar = [ry[i][:, n:] + _dot(m_rk[i], v_s[i]) for i in range(nch)]
    p = [((eye_full + jnp.where(bd, _dot_tn(abar[i], cast[i][1]), 0.0)) * chains[i][6]).astype(bf16)
         for i in range(nch)]
    q = []
    for i in range(nch):
        q_bd = jnp.where(bd, _dot_tn(jnp.concatenate([ubar[i], cast[i][4]], axis=0),
                                     jnp.concatenate([cast[i][1], cast[i][2]], axis=0)), 0.0)
        qi = q_bd[0:c]
        for h in range(1, WKV_GROUP):
            qi = qi + q_bd[h * c:(h + 1) * c]
        q.append(qi * chains[i][6])
    states = list(states)
    ys = []
    for i in range(nch):
        gi = chains[i][0]
        s_b = states[gi].astype(bf16)
        ys.append(_dot_nt(r_bar[i], stack(s_b)) + y_bar[i])
        states[gi] = _dot(s_b, p[i]) + q[i]
    return ys, states


def _rwkv_body(u_ref, mu_ref, wdec_ref, w0_ref, aup_ref, a0_ref, gup_ref, kk_ref, ka_ref, rk_ref,
               lng_ref, lnb_ref, seg_ref, tri_ref, y_ref, state_ref, carry_ref, *, tt):
    j = pl.program_id(1)

    @pl.when(j == 0)
    def _():
        state_ref[...] = jnp.zeros_like(state_ref)
        carry_ref[...] = jnp.zeros_like(carry_ref)

    w = RWKV_WIDTH
    u = u_ref[...]
    row = lax.broadcasted_iota(i32, u.shape, 0)
    prev = jnp.where(row == 0, carry_ref[0:1, :], pltpu.roll(u, 1, axis=0))
    carry_ref[0:1, :] = u[tt - 1:tt, :]
    us = u + (prev - u) * mu_ref[...]
    r = us[:, 0:w]
    k = us[:, w:2 * w]
    v = us[:, 2 * w:3 * w]
    wa = us[:, 3 * w:3 * w + DECAY_RANK + AAA_RANK]
    gd = us[:, 3 * w + DECAY_RANK + AAA_RANK:]
    z = w0_ref[...] + _dot_hp(jnp.tanh(wa), wdec_ref[...])
    softplus_neg_z = jnp.maximum(-z, 0.0) + jnp.log(1.0 + jnp.exp(-jnp.abs(z)))
    lw = -jnp.exp(-softplus_neg_z - 0.5)
    a = _sigmoid(a0_ref[...] + _dot_hp(wa, aup_ref[...]))
    gate = _dot(_sigmoid(gd).astype(bf16), gup_ref[...].astype(bf16))
    seg = seg_ref[...]
    kk = k * kk_ref[...]
    kmod = k * (1.0 + (a - 1.0) * ka_ref[...])
    kk_sq, bonus_dot = _seg_sums([kk * kk, r * kmod * rk_ref[...]], seg)
    kk = kk / jnp.maximum(jnp.sqrt(kk_sq), 1e-12)
    cum = _dot_exact_lhs(tri_ref[...], lw)
    wc = jnp.exp(cum)
    iwc = jnp.exp(-cum)
    at = -kk * jnp.exp(cum - lw)
    bt = kk * a * iwc
    kt = kmod * iwc
    rt = r * wc

    n = WKV_GROUP * HEAD_DIM
    ri = lax.broadcasted_iota(i32, (n, n), 0)
    ci = lax.broadcasted_iota(i32, (n, n), 1)
    bd = (ri // WKV_CHUNK) == (ci // HEAD_DIM)
    bd_b = jnp.where(bd, 1.0, 0.0).astype(bf16) > 0
    eye_full = jnp.where(ri == ci, 1.0, 0.0).astype(f32)
    ti = lax.broadcasted_iota(i32, (WKV_CHUNK, n), 0)
    si = lax.broadcasted_iota(i32, (WKV_CHUNK, n), 1) % WKV_CHUNK
    masks = (bd_b, bd, ti > si, ti >= si, jnp.where(ti == si, 1.0, 0.0).astype(f32), eye_full)

    n_groups = w // n
    n_chunks = tt // WKV_CHUNK
    chains = []
    for c in range(n_chunks):
        rs = slice(c * WKV_CHUNK, (c + 1) * WKV_CHUNK)
        last = (c + 1) * WKV_CHUNK - 1
        for gi in range(n_groups):
            cs = slice(gi * n, (gi + 1) * n)
            chains.append((gi, at[rs, cs], bt[rs, cs], kt[rs, cs], rt[rs, cs], v[rs, cs], wc[last:last + 1, cs]))
    ys, states = _wkv_chunks(chains, [state_ref[gi] for gi in range(n_groups)], masks)
    for gi in range(n_groups):
        state_ref[gi] = states[gi]
    y = jnp.concatenate([jnp.concatenate(ys[c * n_groups:(c + 1) * n_groups], axis=1) for c in range(n_chunks)],
                        axis=0)

    inv_n = 1.0 / HEAD_DIM
    d = y - _seg_sums([y], seg)[0] * inv_n
    var = _seg_sums([d * d], seg)[0] * inv_n
    yn = d * lax.rsqrt(var + GN_EPS) * lng_ref[...] + lnb_ref[...]
    y_ref[...] = ((yn + bonus_dot * v) * gate).astype(y_ref.dtype)


def _rwkv(u_r, mu_shift, w_decay_up, w0, a_up, a0, g_up, k_k, k_a, r_k, lnx_g, lnx_b, tt):
    b, s, cols = u_r.shape
    tt = min(tt, s)
    w = RWKV_WIDTH
    row = lambda p: p.reshape(1, -1).astype(f32)
    wdec = jnp.concatenate([w_decay_up, jnp.zeros((AAA_RANK, w), f32)], axis=0)
    aup = jnp.concatenate([jnp.zeros((DECAY_RANK, w), f32), a_up], axis=0)
    hid = jnp.arange(w) // HEAD_DIM
    seg = (hid[:, None] == hid[None, :]).astype(bf16)
    ti = jnp.arange(tt)
    tri = ((ti[:, None] // WKV_CHUNK == ti[None, :] // WKV_CHUNK) & (ti[:, None] >= ti[None, :])).astype(bf16)
    params = [row(mu_shift), wdec, row(w0), aup, row(a0), g_up, row(k_k), row(k_a), row(r_k), row(lnx_g),
              row(lnx_b), seg, tri]
    n = WKV_GROUP * HEAD_DIM
    return pl.pallas_call(
        functools.partial(_rwkv_body, tt=tt),
        out_shape=jax.ShapeDtypeStruct((b, s, w), bf16),
        grid=(b, s // tt),
        in_specs=[pl.BlockSpec((None, tt, cols), lambda bi, j: (bi, j, 0))] + [_const_spec(p.shape) for p in params],
        out_specs=pl.BlockSpec((None, tt, w), lambda bi, j: (bi, j, 0)),
        scratch_shapes=[pltpu.VMEM((w // n, HEAD_DIM, n), f32), pltpu.VMEM((SUBLANES, cols), f32)],
        compiler_params=_cparams(("parallel", "arbitrary")),
        name="rwkv7",
    )(u_r, *params)


def _swa_body(sink_ref, u_ref, pos_ref, invf_ref, o_ref, kprev_ref, vprev_ref):
    j = pl.program_id(1)

    @pl.when(j == 0)
    def _():
        kprev_ref[...] = jnp.zeros_like(kprev_ref)
        vprev_ref[...] = jnp.zeros_like(vprev_ref)

    wq = SWA_WIDTH
    kvw = SWA_KV_WIDTH
    u = u_ref[...]
    ang = pos_ref[...].astype(f32) * invf_ref[...]
    cos = jnp.cos(ang)
    sin = jnp.sin(ang)

    def rope(x, c, s):
        n = x.shape[1]
        lane = lax.broadcasted_iota(i32, x.shape, 1)
        half = HEAD_DIM // 2
        rot = jnp.where((lane % HEAD_DIM) < half, -pltpu.roll(x, n - half, axis=1), pltpu.roll(x, half, axis=1))
        return x * c + rot * s

    q = rope(u[:, :wq] * (HEAD_DIM ** -0.5), jnp.concatenate([cos] * (wq // LANES), axis=1),
             jnp.concatenate([sin] * (wq // LANES), axis=1))
    k_cur = rope(u[:, wq:wq + kvw], cos, sin)
    v_cur = u[:, wq + kvw:]
    k_prev = kprev_ref[...]
    v_prev = vprev_ref[...]
    kprev_ref[...] = k_cur
    vprev_ref[...] = v_cur

    gw = SWA_GROUP * HEAD_DIM
    rows = SWA_GROUP * WINDOW
    lane_kv = lax.broadcasted_iota(i32, (WINDOW, kvw), 1)

    def rep(x, gi):
        sw = pltpu.roll(x, HEAD_DIM, axis=1)
        one = jnp.where((lane_kv // HEAD_DIM) == gi, x, sw)
        return jnp.concatenate([one] * (gw // kvw), axis=1).astype(bf16)

    ri = lax.broadcasted_iota(i32, (rows, WINDOW), 0)
    ci = lax.broadcasted_iota(i32, (rows, WINDOW), 1)
    t_idx = ri % WINDOW
    mask_prev = ci > t_idx + jnp.where(j > 0, 0, WINDOW)
    mask_cur = ci <= t_idx
    rb = lax.broadcasted_iota(i32, (rows, 1), 0) // WINDOW
    rbo = lax.broadcasted_iota(i32, (rows, gw), 0) // WINDOW
    cbo = lax.broadcasted_iota(i32, (rows, gw), 1) // HEAD_DIM
    bd = rbo == cbo
    outs = []
    for gi in range(SWA_KV_HEADS):
        qg = q[:, gi * gw:(gi + 1) * gw]
        q_bd = jnp.where(bd, jnp.concatenate([qg] * SWA_GROUP, axis=0), 0.0).astype(bf16)
        s_prev = jnp.where(mask_prev, _dot_nt(q_bd, rep(k_prev, gi)), NEG_INF)
        s_cur = jnp.where(mask_cur, _dot_nt(q_bd, rep(k_cur, gi)), NEG_INF)
        sink = jnp.zeros((rows, 1), f32)
        for h in range(SWA_GROUP):
            sink = jnp.where(rb == h, sink_ref[gi * SWA_GROUP + h], sink)
        m = jnp.maximum(jnp.maximum(jnp.max(s_prev, axis=-1, keepdims=True),
                                    jnp.max(s_cur, axis=-1, keepdims=True)), sink)
        p_prev = jnp.exp(s_prev - m)
        p_cur = jnp.exp(s_cur - m)
        denom = jnp.sum(p_prev, axis=-1, keepdims=True) + jnp.sum(p_cur, axis=-1, keepdims=True) + jnp.exp(sink - m)
        o_bd = _dot(p_prev.astype(bf16), rep(v_prev, gi)) + _dot(p_cur.astype(bf16), rep(v_cur, gi))
        o_bd = jnp.where(bd, o_bd / denom, 0.0)
        og = o_bd[0:WINDOW]
        for h in range(1, SWA_GROUP):
            og = og + o_bd[h * WINDOW:(h + 1) * WINDOW]
        outs.append(og)
    o_ref[...] = jnp.concatenate(outs, axis=1).astype(o_ref.dtype)


def _swa(u_s, positions, sinks):
    b, s, cols = u_s.shape
    half = HEAD_DIM // 2
    inv_freq = ROPE_THETA ** (-jnp.arange(0, HEAD_DIM, 2, dtype=f32) / HEAD_DIM)
    invf = jnp.tile(inv_freq, LANES // half).reshape(1, LANES)
    pos = positions.reshape(b, s, 1).astype(i32)
    return pl.pallas_call(
        _swa_body,
        out_shape=jax.ShapeDtypeStruct((b, s, SWA_WIDTH), bf16),
        grid=(b, s // WINDOW),
        in_specs=[pl.BlockSpec(memory_space=pltpu.SMEM),
                  pl.BlockSpec((None, WINDOW, cols), lambda bi, j: (bi, j, 0)),
                  pl.BlockSpec((None, WINDOW, 1), lambda bi, j: (bi, j, 0)),
                  _const_spec((1, LANES))],
        out_specs=pl.BlockSpec((None, WINDOW, SWA_WIDTH), lambda bi, j: (bi, j, 0)),
        scratch_shapes=[pltpu.VMEM((WINDOW, SWA_KV_WIDTH), f32), pltpu.VMEM((WINDOW, SWA_KV_WIDTH), f32)],
        compiler_params=_cparams(("parallel", "arbitrary")),
        name="swa",
    )(sinks.astype(f32), u_s, pos, invf)


def _mix_out_body(ya_ref, yb_ref, wa_ref, wb_ref, x_ref, g_ref, b_ref, o_ref, *, alpha):
    mix = _dot(ya_ref[...], wa_ref[...]) + _dot(yb_ref[...], wb_ref[...])
    o_ref[...] = _layer_norm(alpha * x_ref[...] + mix, g_ref[...], b_ref[...])


def _mix_out(ya, yb, w_o, x, g, b, alpha, tile):
    n, d = x.shape
    tile = min(tile, n)
    wa = w_o[:ya.shape[1]].astype(bf16)
    wb = w_o[ya.shape[1]:].astype(bf16)
    rows = lambda width: pl.BlockSpec((tile, width), lambda i: (i, 0))
    return pl.pallas_call(
        functools.partial(_mix_out_body, alpha=alpha),
        out_shape=jax.ShapeDtypeStruct((n, d), f32),
        grid=(n // tile,),
        in_specs=[rows(ya.shape[1]), rows(yb.shape[1]), _const_spec(wa.shape), _const_spec(wb.shape), rows(d),
                  _const_spec((1, d)), _const_spec((1, d))],
        out_specs=rows(d),
        compiler_params=_cparams(("parallel",)),
        name="mix_out_ln1",
    )(ya, yb, wa, wb, x, g.reshape(1, d), b.reshape(1, d))


def _xattn_body(x_ref, kv_ref, wq_ref, wo_ref, g_ref, b_ref, o_ref, o3_ref, *, alpha):
    x = x_ref[...]
    d = x.shape[1]
    hd = d // MEM_HEADS
    q = _dot(x.astype(bf16), wq_ref[...]) * (hd ** -0.5)
    kv = kv_ref[...]
    outs = []
    for h in range(MEM_HEADS):
        qh = q[:, h * hd:(h + 1) * hd].astype(bf16)
        kh = kv[:, h * hd:(h + 1) * hd]
        vh = kv[:, d + h * hd:d + (h + 1) * hd]
        s = _dot_nt(qh, kh)
        p = jnp.exp(s - jnp.max(s, axis=-1, keepdims=True))
        l = jnp.sum(p, axis=-1, keepdims=True)
        outs.append(_dot(p.astype(bf16), vh) / l)
    o = jnp.concatenate(outs, axis=1)
    xa = _dot(o.astype(bf16), wo_ref[...])
    y = _layer_norm(alpha * x + xa, g_ref[...], b_ref[...])
    o_ref[...] = y
    words = _pack_bf16_pairs(y)
    for j, w in enumerate(words):
        o3_ref[_lane_chunk(x.shape[0], len(words), j)] = w


def _xattn(x1, kv, wm_q, wm_o, g, b, alpha, tile):
    bsz, s, d = x1.shape
    m = kv.shape[1]
    tile = min(tile, s)
    nj = s // tile
    wq = wm_q.astype(bf16)
    wo = wm_o.astype(bf16)
    return pl.pallas_call(
        functools.partial(_xattn_body, alpha=alpha),
        out_shape=[jax.ShapeDtypeStruct((bsz, s, d), f32),
                   jax.ShapeDtypeStruct((bsz * s * (d // (2 * LANES)), LANES), i32)],
        grid=(bsz, nj),
        in_specs=[pl.BlockSpec((None, tile, d), lambda bi, j: (bi, j, 0)),
                  pl.BlockSpec((None, m, 2 * d), lambda bi, j: (bi, 0, 0)),
                  _const_spec(wq.shape), _const_spec(wo.shape), _const_spec((1, d)), _const_spec((1, d))],
        out_specs=[pl.BlockSpec((None, tile, d), lambda bi, j: (bi, j, 0)),
                   pl.BlockSpec((tile * (d // (2 * LANES)), LANES), lambda bi, j: (bi * nj + j, 0))],
        compiler_params=_cparams(("parallel", "parallel")),
        name="mem_xattn_ln2",
    )(x1, kv, wq, wo, g.reshape(1, d), b.reshape(1, d))


def _router_body(x_ref, wt_ref, bias_ref, upper_ref, e_ref, g_ref, r_ref, cnt_out_ref, cnt_ref, *, t):
    @pl.when(pl.program_id(0) == 0)
    def _():
        cnt_ref[...] = jnp.zeros_like(cnt_ref)

    xh, xl = _split2(x_ref[...])
    wh, wl = _split2(wt_ref[...])
    logits = _dot_nt(wh, xh) + _dot_nt(wh, xl) + _dot_nt(wl, xh)
    scores = _sigmoid(logits)
    biased = scores + bias_ref[...][:, 0:1]
    ne = N_EXPERTS
    neg = -jnp.inf

    def top1(vals):
        rows = lax.broadcasted_iota(i32, vals.shape, 0).astype(f32)
        m = jnp.max(vals, axis=0, keepdims=True)
        idx = jnp.min(jnp.where(vals == m, rows, float(vals.shape[0])), axis=0, keepdims=True)
        return m, idx, rows == idx

    gscores = []
    for gi in range(N_GROUPS):
        blk = biased[gi * GROUP_SIZE:(gi + 1) * GROUP_SIZE, :]
        m1, _, hit = top1(blk)
        m2 = jnp.max(jnp.where(hit, neg, blk), axis=0, keepdims=True)
        gscores.append(m1 + m2)
    gs = jnp.concatenate(gscores, axis=0)
    gsel = jnp.zeros(gs.shape, f32)
    for _ in range(TOPK_GROUPS):
        _, _, hit = top1(gs)
        gsel = jnp.where(hit, 1.0, gsel)
        gs = jnp.where(hit, neg, gs)
    emask = jnp.concatenate(
        [jnp.broadcast_to(gsel[gi:gi + 1, :], (GROUP_SIZE, t)) for gi in range(N_GROUPS)], axis=0) > 0.5
    cand = jnp.where(emask, biased, NEG_INF)
    idxs, sels = [], []
    chosen = jnp.zeros((ne, t), f32)
    for _ in range(TOP_K):
        _, idx, hit = top1(cand)
        idxs.append(idx)
        sels.append(jnp.sum(jnp.where(hit, scores, 0.0), axis=0, keepdims=True))
        chosen = chosen + jnp.where(hit, 1.0, 0.0)
        cand = jnp.where(hit, neg, cand)
    sel = jnp.concatenate(sels, axis=0)
    g_ref[...] = sel / jnp.sum(sel, axis=0, keepdims=True) * ROUTED_SCALE
    e_ref[...] = jnp.concatenate(idxs, axis=0).astype(i32)
    before = _dot(chosen.astype(bf16), upper_ref[...]) + cnt_ref[...][:, 0:1]
    rows = lax.broadcasted_iota(i32, (ne, t), 0).astype(f32)
    ranks = [jnp.sum(jnp.where(rows == idx, before, 0.0), axis=0, keepdims=True) for idx in idxs]
    r_ref[...] = jnp.concatenate(ranks, axis=0).astype(i32)
    cnt_ref[...] = cnt_ref[...] + jnp.sum(chosen, axis=1, keepdims=True)
    cnt_out_ref[...] = cnt_ref[...].astype(i32)


def _router(x2, w_router, router_bias, tile):
    n, d = x2.shape
    ne = N_EXPERTS
    t = min(tile, n)
    wt = w_router.T
    bias = jnp.broadcast_to(router_bias.reshape(ne, 1).astype(f32), (ne, LANES))
    ti = jnp.arange(t)
    upper = (ti[:, None] < ti[None, :]).astype(bf16)
    cols = pl.BlockSpec((TOP_K, t), lambda i: (0, i))
    e_t, g_t, r_t, cnt = pl.pallas_call(
        functools.partial(_router_body, t=t),
        out_shape=[jax.ShapeDtypeStruct((TOP_K, n), i32), jax.ShapeDtypeStruct((TOP_K, n), f32),
                   jax.ShapeDtypeStruct((TOP_K, n), i32), jax.ShapeDtypeStruct((ne, LANES), i32)],
        grid=(n // t,),
        in_specs=[pl.BlockSpec((t, d), lambda i: (i, 0)), _const_spec((ne, d)), _const_spec((ne, LANES)),
                  _const_spec((t, t))],
        out_specs=[cols, cols, cols, _const_spec((ne, LANES))],
        scratch_shapes=[pltpu.VMEM((ne, LANES), f32)],
        compiler_params=_cparams(("arbitrary",)),
        name="router",
    )(x2, wt, bias, upper)
    return e_t, g_t, r_t, cnt[:, 0]


def _dispatch_sc(x3d, e_t, r_t, g_t, starts):
    n, nc, lanes = x3d.shape
    workers = SC_CORES * SC_SUBCORES
    per_w = n // workers
    tc = _sc_chunk_rows(nc, lanes, x3d.dtype)
    assert n % (workers * SC_INDEX_GROUP) == 0 and SC_INDEX_GROUP % tc == 0
    mesh = plsc.VectorSubcoreMesh(core_axis_name="c", subcore_axis_name="s")

    @functools.partial(
        pl.kernel, mesh=mesh,
        out_type=[jax.ShapeDtypeStruct((n * TOP_K, nc, lanes), x3d.dtype),
                  jax.ShapeDtypeStruct((n * TOP_K,), f32)],
        scratch_types=[
            pltpu.VMEM((tc, nc, lanes), x3d.dtype),
            pltpu.VMEM((TOP_K, SC_INDEX_GROUP), i32),
            pltpu.VMEM((TOP_K, SC_INDEX_GROUP), i32),
            pltpu.VMEM((TOP_K, SC_INDEX_GROUP), f32),
            pltpu.VMEM((TOP_K, tc), i32),
            pltpu.VMEM((TOP_K, tc), f32),
            pltpu.VMEM((N_EXPERTS,), i32),
            pltpu.SemaphoreType.DMA,
        ],
        compiler_params=pltpu.CompilerParams(use_tc_tiling_on_sc=True, needs_layout_passes=False),
    )
    def dispatch(x_hbm, e_hbm, r_hbm, g_hbm, st_hbm, o_hbm, gs_hbm, rows_v, e_v, r_v, g_v, slot_v, gate_v, st_v, sem):
        wid = lax.axis_index("s") * SC_CORES + lax.axis_index("c")
        pltpu.sync_copy(st_hbm, st_v)

        @pl.loop(0, per_w // SC_INDEX_GROUP)
        def _(gi):
            base = wid * per_w + gi * SC_INDEX_GROUP
            pltpu.sync_copy(e_hbm.at[:, pl.ds(base, SC_INDEX_GROUP)], e_v)
            pltpu.sync_copy(r_hbm.at[:, pl.ds(base, SC_INDEX_GROUP)], r_v)
            pltpu.sync_copy(g_hbm.at[:, pl.ds(base, SC_INDEX_GROUP)], g_v)
            for h in range(SC_INDEX_GROUP // tc):
                off = h * tc
                pltpu.sync_copy(x_hbm.at[pl.ds(base + off, tc)], rows_v)
                for kk in range(TOP_K):
                    for j in range(tc // SC_LANES):
                        src = pl.ds(off + j * SC_LANES, SC_LANES)
                        dst = pl.ds(j * SC_LANES, SC_LANES)
                        slot_v[kk, dst] = r_v[kk, src] + plsc.load_gather(st_v, [e_v[kk, src]])
                        gate_v[kk, dst] = g_v[kk, src]
                copies = [pltpu.async_copy(rows_v, o_hbm.at[slot_v.at[kk]], sem) for kk in range(TOP_K)]
                copies += [pltpu.async_copy(gate_v.at[kk], gs_hbm.at[slot_v.at[kk]], sem) for kk in range(TOP_K)]
                for cp in copies:
                    cp.wait()

    return dispatch(x3d, e_t, r_t, g_t, starts)


def _gmm_body(gid_ref, tid_ref, lo_ref, hi_ref, first_ref, newg_ref, nextg_ref, ord_ref, x_ref, g_ref,
              wg_hbm, wu_hbm, wd_hbm, o_ref, wg_b, wu_b, wd_b, wg_s, wu_s, wd_s, sems, *, tm, layer):
    v = pl.program_id(0)
    lo = lo_ref[v]
    hi = hi_ref[v]
    row0 = tid_ref[v] * tm
    nc = wg_b.shape[0] // LANES
    full = (lo <= row0) & (hi >= row0 + tm)

    def weight_copies(expert, slot):
        return [pltpu.make_async_copy(src.at[layer, expert], dst.at[slot], sems.at[slot])
                for src, dst in ((wg_hbm, wg_s), (wu_hbm, wu_s), (wd_hbm, wd_s))]

    @pl.when(newg_ref[v] == 1)
    def _():
        slot = ord_ref[v] % 2

        @pl.when(v == 0)
        def _():
            for cp in weight_copies(gid_ref[v], slot):
                cp.start()

        for cp in weight_copies(gid_ref[v], slot):
            cp.wait()
        wg_b[...] = wg_s[slot].astype(bf16)
        wu_b[...] = wu_s[slot].astype(bf16)
        wd_b[...] = wd_s[slot].astype(bf16)

        @pl.when(nextg_ref[v] >= 0)
        def _():
            for cp in weight_copies(nextg_ref[v], 1 - slot):
                cp.start()

    def ffn():
        nw = nc // 2
        x = _unpack_bf16_pairs([x_ref[_lane_chunk(tm, nw, j)] for j in range(nw)]).astype(bf16)
        hg = _dot(x, wg_b[...])
        h = hg * _sigmoid(hg) * _dot(x, wu_b[...])
        g = g_ref[...]
        g_cols = jnp.concatenate([g, jnp.zeros((SUBLANES - g.shape[0], LANES), f32)], axis=0).T
        h = jnp.concatenate([h[r * LANES:(r + 1) * LANES, :] * g_cols[:, r:r + 1] for r in range(tm // LANES)],
                            axis=0)
        return _dot(h.astype(bf16), wd_b[...])

    @pl.when(full)
    def _():
        y = ffn()
        for c in range(nc):
            o_ref[_lane_chunk(tm, nc, c)] = y[:, c * LANES:(c + 1) * LANES]

    @pl.when(jnp.logical_not(full) & (hi > lo))
    def _():
        @pl.when(first_ref[v] == 1)
        def _():
            o_ref[...] = jnp.zeros_like(o_ref)

        rows = row0 + lax.broadcasted_iota(i32, (tm, 1), 0)
        mask = (rows >= lo) & (rows < hi)
        y = ffn()
        for c in range(nc):
            idx = _lane_chunk(tm, nc, c)
            o_ref[idx] = jnp.where(mask, y[:, c * LANES:(c + 1) * LANES], o_ref[idx])


def _gmm(xs, gates, we_gate, we_up, we_down, layer, counts, tm):
    _, ne, d, de = we_gate.shape
    nc = d // LANES
    nw = nc // 2
    nk = xs.shape[0] // nw
    tm = min(tm, nk)
    assert tm % LANES == 0 and tm // LANES <= SUBLANES
    n_tiles = nk // tm
    gates3 = gates.reshape(n_tiles, tm // LANES, LANES)
    n_visits = n_tiles + ne - 1
    ends = jnp.cumsum(counts)
    starts = ends - counts
    tile_lo = starts // tm
    n_touch = jnp.where(counts > 0, (ends - 1) // tm - tile_lo + 1, 0)
    vis_end = jnp.cumsum(n_touch)
    vis_start = vis_end - n_touch
    vi = jnp.arange(n_visits, dtype=i32)
    valid = vi < vis_end[-1]
    gid = jnp.minimum(jnp.sum((vis_end[None, :] <= vi[:, None]).astype(i32), axis=1), ne - 1)
    gid = jnp.where(valid, gid, jnp.max(jnp.where(valid, gid, 0)))
    onehot = gid[:, None] == jnp.arange(ne, dtype=i32)[None, :]
    pick = lambda table: jnp.sum(jnp.where(onehot, table[None, :], 0), axis=1)
    tid = jnp.where(valid, pick(tile_lo) + vi - pick(vis_start), n_tiles - 1).astype(i32)
    lo = jnp.where(valid, pick(starts), 0).astype(i32)
    hi = jnp.where(valid, pick(ends), 0).astype(i32)
    one = jnp.ones((1,), i32)
    first = jnp.concatenate([one, (tid[1:] != tid[:-1]).astype(i32)])
    newg = jnp.concatenate([one, (gid[1:] != gid[:-1]).astype(i32)])
    later = gid[None, :] > gid[:, None]
    nextg = jnp.min(jnp.where(later, gid[None, :], ne), axis=1)
    nextg = jnp.where(nextg < ne, nextg, -1).astype(i32)
    order = (jnp.cumsum(newg) - 1).astype(i32)
    rows = lambda chunks: pl.BlockSpec((tm * chunks, LANES), lambda v, g, t, *_: (t[v], 0))
    hbm = pl.BlockSpec(memory_space=pl.ANY)
    return pl.pallas_call(
        functools.partial(_gmm_body, tm=tm, layer=layer),
        out_shape=jax.ShapeDtypeStruct((nk * nc, LANES), f32),
        grid_spec=pltpu.PrefetchScalarGridSpec(
            num_scalar_prefetch=8, grid=(n_visits,),
            in_specs=[rows(nw), pl.BlockSpec((None, tm // LANES, LANES), lambda v, g, t, *_: (t[v], 0, 0)),
                      hbm, hbm, hbm],
            out_specs=rows(nc),
            scratch_shapes=[pltpu.VMEM((d, de), bf16), pltpu.VMEM((d, de), bf16), pltpu.VMEM((de, d), bf16),
                            pltpu.VMEM((2, d, de), f32), pltpu.VMEM((2, d, de), f32), pltpu.VMEM((2, de, d), f32),
                            pltpu.SemaphoreType.DMA((2,))]),
        compiler_params=_cparams(("arbitrary",)),
        name="moe_experts",
    )(gid, tid, lo, hi, first, newg, nextg, order, xs, gates3, we_gate, we_up, we_down)


def _combine_sc(ys3d, e_t, r_t, starts, n):
    _, nc, lanes = ys3d.shape
    workers = SC_CORES * SC_SUBCORES
    per_w = n // workers
    tc = _sc_chunk_rows(nc, lanes, f32)
    assert n % (workers * SC_INDEX_GROUP) == 0 and SC_INDEX_GROUP % tc == 0
    mesh = plsc.VectorSubcoreMesh(core_axis_name="c", subcore_axis_name="s")

    @functools.partial(
        pl.kernel, mesh=mesh,
        out_type=jax.ShapeDtypeStruct((n, nc, lanes), f32),
        scratch_types=[
            pltpu.VMEM((tc, nc, lanes), f32),
            pltpu.VMEM((TOP_K, SC_INDEX_GROUP), i32),
            pltpu.VMEM((TOP_K, SC_INDEX_GROUP), i32),
            pltpu.VMEM((TOP_K, tc), i32),
            pltpu.VMEM((N_EXPERTS,), i32),
            pltpu.SemaphoreType.DMA,
        ],
        compiler_params=pltpu.CompilerParams(use_tc_tiling_on_sc=True, needs_layout_passes=False),
    )
    def combine(y_hbm, e_hbm, r_hbm, st_hbm, o_hbm, acc_v, e_v, r_v, slot_v, st_v, sem):
        wid = lax.axis_index("s") * SC_CORES + lax.axis_index("c")
        pltpu.sync_copy(st_hbm, st_v)

        @pl.loop(0, per_w // SC_INDEX_GROUP)
        def _(gi):
            base = wid * per_w + gi * SC_INDEX_GROUP
            pltpu.sync_copy(e_hbm.at[:, pl.ds(base, SC_INDEX_GROUP)], e_v)
            pltpu.sync_copy(r_hbm.at[:, pl.ds(base, SC_INDEX_GROUP)], r_v)
            for h in range(SC_INDEX_GROUP // tc):
                off = h * tc
                for kk in range(TOP_K):
                    for j in range(tc // SC_LANES):
                        src = pl.ds(off + j * SC_LANES, SC_LANES)
                        slot_v[kk, pl.ds(j * SC_LANES, SC_LANES)] = (
                            r_v[kk, src] + plsc.load_gather(st_v, [e_v[kk, src]]))
                pltpu.async_copy(y_hbm.at[slot_v.at[0]], acc_v, sem).wait()
                copies = [pltpu.async_copy(y_hbm.at[slot_v.at[kk]], acc_v, sem, add=True) for kk in range(1, TOP_K)]
                for cp in copies:
                    cp.wait()
                pltpu.sync_copy(acc_v, o_hbm.at[pl.ds(base + off, tc)])

    return combine(ys3d, e_t, r_t, starts)


def _ffn_out_body(x_ref, r_ref, wg_ref, wu_ref, wd_ref, g_ref, b_ref, o_ref, *, alpha):
    x = x_ref[...]
    xb = x.astype(bf16)
    hg = _dot(xb, wg_ref[...])
    h = hg * _sigmoid(hg) * _dot(xb, wu_ref[...])
    shared = _dot(h.astype(bf16), wd_ref[...])
    nc = x.shape[1] // LANES
    routed = jnp.concatenate([r_ref[_lane_chunk(x.shape[0], nc, c)] for c in range(nc)], axis=1)
    o_ref[...] = _layer_norm(alpha * x + routed + shared, g_ref[...], b_ref[...])


def _ffn_out(x2, routed_rows, ws_gate, ws_up, ws_down, g, b, alpha, tile):
    n, d = x2.shape
    tile = min(tile, n)
    wg, wu, wd = ws_gate.astype(bf16), ws_up.astype(bf16), ws_down.astype(bf16)
    rows = pl.BlockSpec((tile, d), lambda i: (i, 0))
    return pl.pallas_call(
        functools.partial(_ffn_out_body, alpha=alpha),
        out_shape=jax.ShapeDtypeStruct((n, d), f32),
        grid=(n // tile,),
        in_specs=[rows, pl.BlockSpec((tile * (d // LANES), LANES), lambda i: (i, 0)), _const_spec(wg.shape),
                  _const_spec(wu.shape), _const_spec(wd.shape), _const_spec((1, d)), _const_spec((1, d))],
        out_specs=rows,
        compiler_params=_cparams(("parallel",)),
        name="ffn_out_ln3",
    )(x2, routed_rows, wg, wu, wd, g.reshape(1, d), b.reshape(1, d))


def _layer(x, mem, positions, w_in, mu_shift, w_decay_up, w0, a_up, a0, g_up, k_k, k_a, r_k, lnx_g, lnx_b, sinks,
           w_o, ln1_g, ln1_b, wm_q, wm_kv, wm_o, ln2_g, ln2_b, w_router, router_bias, we_gate, we_up, we_down,
           ws_gate, ws_up, ws_down, ln3_g, ln3_b, *, layer, alpha):
    b, s, d = x.shape
    n = b * s
    xf = x.reshape(n, d)
    w_in_b = w_in.astype(bf16)
    u_r, u_s = _proj(xf, [w_in_b[:, :RWKV_COLS], w_in_b[:, RWKV_COLS:]], [f32, f32], tile=512)
    y_r = _rwkv(u_r.reshape(b, s, RWKV_COLS), mu_shift, w_decay_up, w0, a_up, a0, g_up, k_k, k_a, r_k, lnx_g, lnx_b,
                tt=256)
    y_s = _swa(u_s.reshape(b, s, SWA_COLS), positions, sinks)
    x1 = _mix_out(y_r.reshape(n, RWKV_WIDTH), y_s.reshape(n, SWA_WIDTH), w_o, xf, ln1_g, ln1_b, alpha, tile=512)
    m = mem.shape[1]
    (kv,) = _proj(mem.reshape(b * m, d), [wm_kv.astype(bf16)], [bf16], tile=512)
    x2, x2_rows = _xattn(x1.reshape(b, s, d), kv.reshape(b, m, 2 * d), wm_q, wm_o, ln2_g, ln2_b, alpha, tile=512)
    x2 = x2.reshape(n, d)
    e_t, g_t, r_t, counts = _router(x2, w_router, router_bias, tile=512)
    starts = (jnp.cumsum(counts) - counts).astype(i32)
    nc = d // LANES
    nw = nc // 2
    xs, gs = _dispatch_sc(x2_rows.reshape(n, nw, LANES), e_t, r_t, g_t, starts)
    ys = _gmm(xs.reshape(n * TOP_K * nw, LANES), gs, we_gate, we_up, we_down, layer, counts, tm=512)
    routed = _combine_sc(ys.reshape(n * TOP_K, nc, LANES), e_t, r_t, starts, n)
    x3 = _ffn_out(x2, routed.reshape(n * nc, LANES), ws_gate, ws_up, ws_down, ln3_g, ln3_b, alpha, tile=512)
    return x3.reshape(b, s, d)


def kernel(x, mem, positions, w_in, mu_shift, w_decay_up, w0, a_up, a0, g_up, k_k, k_a, r_k, lnx_g, lnx_b, sinks, w_o, ln1_g, ln1_b, wm_q, wm_kv, wm_o, ln2_g, ln2_b, w_router, router_bias, we_gate, we_up, we_down, ws_gate, ws_up, ws_down, ln3_g, ln3_b):
    depth = w_in.shape[0]
    alpha = (2 * depth) ** 0.25
    for l in range(depth):
        x = _layer(x, mem, positions, w_in[l], mu_shift[l], w_decay_up[l], w0[l], a_up[l], a0[l], g_up[l], k_k[l],
                   k_a[l], r_k[l], lnx_g[l], lnx_b[l], sinks[l], w_o[l], ln1_g[l], ln1_b[l], wm_q[l], wm_kv[l],
                   wm_o[l], ln2_g[l], ln2_b[l], w_router[l], router_bias[l], we_gate, we_up, we_down,
                   ws_gate[l], ws_up[l], ws_down[l], ln3_g[l], ln3_b[l], layer=l, alpha=alpha)
    return x
```

```python
import functools

import jax
import jax.numpy as jnp
from jax import lax
from jax.experimental import pallas as pl
from jax.experimental.pallas import tpu as pltpu
from jax.experimental.pallas import tpu_sc as plsc

f32 = jnp.float32
bf16 = jnp.bfloat16
i32 = jnp.int32

RWKV_HEADS = 8
HEAD_DIM = 64
RWKV_WIDTH = RWKV_HEADS * HEAD_DIM
DECAY_RANK = 64
AAA_RANK = 64
GATE_RANK = 128
RWKV_COLS = 3 * RWKV_WIDTH + DECAY_RANK + AAA_RANK + GATE_RANK
SWA_Q_HEADS = 8
SWA_KV_HEADS = 2
SWA_GROUP = SWA_Q_HEADS // SWA_KV_HEADS
SWA_WIDTH = SWA_Q_HEADS * HEAD_DIM
SWA_KV_WIDTH = SWA_KV_HEADS * HEAD_DIM
SWA_COLS = SWA_WIDTH + 2 * SWA_KV_WIDTH
WINDOW = 128
ROPE_THETA = 10000.0
MEM_HEADS = 4
N_EXPERTS = 256
TOP_K = 8
N_GROUPS = 8
GROUP_SIZE = N_EXPERTS // N_GROUPS
TOPK_GROUPS = 4
ROUTED_SCALE = 2.5
LN_EPS = 1e-5
GN_EPS = 64e-5
NEG_INF = -1e30

LANES = 128
SUBLANES = 8
WKV_CHUNK = 64
WKV_GROUP = 2
VMEM_LIMIT = 56 * 1024 * 1024

SC_CORES = 2
SC_SUBCORES = 16
SC_LANES = 16
SC_INDEX_GROUP = 128
SC_CHUNK_BYTES = 256 * 1024
MOE_PARTS = 2


def _sc_chunk_rows(nc, lanes, dtype):
    return min(SC_INDEX_GROUP, SC_CHUNK_BYTES // (nc * lanes * jnp.dtype(dtype).itemsize))


def _cparams(sem):
    return pltpu.CompilerParams(dimension_semantics=sem, vmem_limit_bytes=VMEM_LIMIT)


def _const_spec(shape):
    nd = len(shape)
    return pl.BlockSpec(shape, lambda *_: (0,) * nd)


def _dot(a, b):
    return jnp.dot(a, b, preferred_element_type=f32)


def _dot_nt(a, b):
    return lax.dot_general(a, b, (((1,), (1,)), ((), ())), preferred_element_type=f32)


def _dot_tn(a, b):
    return lax.dot_general(a, b, (((0,), (0,)), ((), ())), preferred_element_type=f32)


def _split2(x):
    hi = x.astype(bf16)
    lo = (x - hi.astype(f32)).astype(bf16)
    return hi, lo


def _seg_sums(xs, seg_b):
    parts = []
    for x in xs:
        parts.extend(_split2(x))
    out = _dot(jnp.concatenate(parts, axis=0), seg_b)
    t = xs[0].shape[0]
    return [out[2 * i * t:(2 * i + 1) * t] + out[(2 * i + 1) * t:(2 * i + 2) * t] for i in range(len(xs))]


def _dot_hp(a, b):
    ah, al = _split2(a)
    bh, bl = _split2(b)
    return _dot(ah, bh) + _dot(ah, bl) + _dot(al, bh)


def _dot_exact_lhs(m_bf16, x):
    hi, lo = _split2(x)
    return _dot(m_bf16, hi) + _dot(m_bf16, lo)


def _sigmoid(x):
    return 1.0 / (1.0 + jnp.exp(-x))


def _lane_chunk(n_rows, n_chunks, c):
    return (pl.ds(c, n_rows, stride=n_chunks), slice(None))


def _pack_bf16_pairs(x):
    chunks = []
    for j in range(x.shape[1] // (2 * LANES)):
        lo = x[:, 2 * j * LANES:(2 * j + 1) * LANES].astype(bf16).astype(f32)
        hi = x[:, (2 * j + 1) * LANES:(2 * j + 2) * LANES].astype(bf16).astype(f32)
        chunks.append(lax.bitcast_convert_type(hi, i32) | lax.shift_right_logical(lax.bitcast_convert_type(lo, i32), 16))
    return chunks


def _unpack_bf16_pairs(chunks):
    cols = []
    for w in chunks:
        cols.append(lax.bitcast_convert_type(lax.shift_left(w, 16), f32))
        cols.append(lax.bitcast_convert_type(w & jnp.int32(-65536), f32))
    return jnp.concatenate(cols, axis=1)


def _layer_norm(h, g, b):
    mu = jnp.mean(h, axis=-1, keepdims=True)
    d = h - mu
    var = jnp.mean(d * d, axis=-1, keepdims=True)
    return d * lax.rsqrt(var + LN_EPS) * g + b


def _proj_body(*refs, n_out):
    x_ref = refs[0]
    w_refs = refs[1:1 + n_out]
    o_refs = refs[1 + n_out:]
    xb = x_ref[...].astype(bf16)
    for w_ref, o_ref in zip(w_refs, o_refs):
        o_ref[...] = _dot(xb, w_ref[...]).astype(o_ref.dtype)


def _proj(x, ws, out_dtypes, tile):
    n, k = x.shape
    tile = min(tile, n)
    outs = pl.pallas_call(
        functools.partial(_proj_body, n_out=len(ws)),
        out_shape=[jax.ShapeDtypeStruct((n, w.shape[1]), dt) for w, dt in zip(ws, out_dtypes)],
        grid=(n // tile,),
        in_specs=[pl.BlockSpec((tile, k), lambda i: (i, 0))] + [_const_spec(w.shape) for w in ws],
        out_specs=[pl.BlockSpec((tile, w.shape[1]), lambda i: (i, 0)) for w in ws],
        compiler_params=_cparams(("parallel",)),
        name="proj",
    )(x, *ws)
    return outs


def _wkv_chunks(chains, states, masks):
    bd_b, bd, strict, incl, eye, eye_full = masks
    c, n = chains[0][1].shape
    nch = len(chains)

    def stack(x_b):
        return jnp.where(bd_b, jnp.concatenate([x_b] * WKV_GROUP, axis=0), jnp.zeros((), bf16))

    cast = [tuple(x.astype(bf16) for x in ch[1:6]) for ch in chains]
    v_s = [stack(cb[4]) for cb in cast]
    g = [_dot_nt(jnp.concatenate([cb[0], cb[3]], axis=0), jnp.concatenate([stack(cb[1]), stack(cb[2])], axis=0))
         for cb in cast]
    l_ak = [jnp.where(strict, gi[:c, n:], 0.0).astype(bf16) for gi in g]
    m_rb = [jnp.where(incl, gi[c:, :n], 0.0).astype(bf16) for gi in g]
    m_rk = [jnp.where(incl, gi[c:, n:], 0.0).astype(bf16) for gi in g]
    x = [jnp.where(strict, gi[:c, :n], 0.0) for gi in g]
    t = [eye + xi for xi in x]
    for _ in range(5):
        xb = [xi.astype(bf16) for xi in x]
        x = [_dot(xi, stack(xi)) for xi in xb]
        t = [ti + _dot(ti.astype(bf16), stack(xi.astype(bf16))) for ti, xi in zip(t, x)]
    lakv = [_dot(l_ak[i], v_s[i]).astype(bf16) for i in range(nch)]
    au = [_dot(t[i].astype(bf16), jnp.concatenate([stack(cast[i][0]), stack(lakv[i])], axis=1))
          for i in range(nch)]
    abar = [a[:, :n].astype(bf16) for a in au]
    ubar = [a[:, n:].astype(bf16) for a in au]
    ry = [_dot(m_rb[i], jnp.concatenate([stack(abar[i]), stack(ubar[i])], axis=1)) for i in range(nch)]
    r_bar = [(chains[i][4] + ry[i][:, :n]).astype(bf16) for i in range(nch)]
    y_bar = [ry[i][:, n:] + _dot(m_rk[i], v_s[i]) for i in range(nch)]
    p = [((eye_full + jnp.where(bd, _dot_tn(abar[i], cast[i][1]), 0.0)) * chains[i][6]).astype(bf16)
         for i in range(nch)]
    q = []
    for i in range(nch):
        q_bd = jnp.where(bd, _dot_tn(jnp.concatenate([ubar[i], cast[i][4]], axis=0),
                                     jnp.concatenate([cast[i][1], cast[i][2]], axis=0)), 0.0)
        qi = q_bd[0:c]
        for h in range(1, WKV_GROUP):
            qi = qi + q_bd[h * c:(h + 1) * c]
        q.append(qi * chains[i][6])
    states = list(states)
    ys = []
    for i in range(nch):
        gi = chains[i][0]
        s_b = states[gi].astype(bf16)
        ys.append(_dot_nt(r_bar[i], stack(s_b)) + y_bar[i])
        states[gi] = _dot(s_b, p[i]) + q[i]
    return ys, states


def _rwkv_body(u_ref, mu_ref, wdec_ref, w0_ref, aup_ref, a0_ref, gup_ref, kk_ref, ka_ref, rk_ref,
               lng_ref, lnb_ref, seg_ref, tri_ref, y_ref, state_ref, carry_ref, *, tt):
    j = pl.program_id(1)

    @pl.when(j == 0)
    def _():
        state_ref[...] = jnp.zeros_like(state_ref)
        carry_ref[...] = jnp.zeros_like(carry_ref)

    w = RWKV_WIDTH
    u = u_ref[...]
    row = lax.broadcasted_iota(i32, u.shape, 0)
    prev = jnp.where(row == 0, carry_ref[0:1, :], pltpu.roll(u, 1, axis=0))
    carry_ref[0:1, :] = u[tt - 1:tt, :]
    us = u + (prev - u) * mu_ref[...]
    r = us[:, 0:w]
    k = us[:, w:2 * w]
    v = us[:, 2 * w:3 * w]
    wa = us[:, 3 * w:3 * w + DECAY_RANK + AAA_RANK]
    gd = us[:, 3 * w + DECAY_RANK + AAA_RANK:]
    z = w0_ref[...] + _dot_hp(jnp.tanh(wa), wdec_ref[...])
    softplus_neg_z = jnp.maximum(-z, 0.0) + jnp.log(1.0 + jnp.exp(-jnp.abs(z)))
    lw = -jnp.exp(-softplus_neg_z - 0.5)
    a = _sigmoid(a0_ref[...] + _dot_hp(wa, aup_ref[...]))
    gate = _dot(_sigmoid(gd).astype(bf16), gup_ref[...].astype(bf16))
    seg = seg_ref[...]
    kk = k * kk_ref[...]
    kmod = k * (1.0 + (a - 1.0) * ka_ref[...])
    kk_sq, bonus_dot = _seg_sums([kk * kk, r * kmod * rk_ref[...]], seg)
    kk = kk / jnp.maximum(jnp.sqrt(kk_sq), 1e-12)
    cum = _dot_exact_lhs(tri_ref[...], lw)
    wc = jnp.exp(cum)
    iwc = jnp.exp(-cum)
    at = -kk * jnp.exp(cum - lw)
    bt = kk * a * iwc
    kt = kmod * iwc
    rt = r * wc

    n = WKV_GROUP * HEAD_DIM
    ri = lax.broadcasted_iota(i32, (n, n), 0)
    ci = lax.broadcasted_iota(i32, (n, n), 1)
    bd = (ri // WKV_CHUNK) == (ci // HEAD_DIM)
    bd_b = jnp.where(bd, 1.0, 0.0).astype(bf16) > 0
    eye_full = jnp.where(ri == ci, 1.0, 0.0).astype(f32)
    ti = lax.broadcasted_iota(i32, (WKV_CHUNK, n), 0)
    si = lax.broadcasted_iota(i32, (WKV_CHUNK, n), 1) % WKV_CHUNK
    masks = (bd_b, bd, ti > si, ti >= si, jnp.where(ti == si, 1.0, 0.0).astype(f32), eye_full)

    n_groups = w // n
    n_chunks = tt // WKV_CHUNK
    chains = []
    for c in range(n_chunks):
        rs = slice(c * WKV_CHUNK, (c + 1) * WKV_CHUNK)
        last = (c + 1) * WKV_CHUNK - 1
        for gi in range(n_groups):
            cs = slice(gi * n, (gi + 1) * n)
            chains.append((gi, at[rs, cs], bt[rs, cs], kt[rs, cs], rt[rs, cs], v[rs, cs], wc[last:last + 1, cs]))
    ys, states = _wkv_chunks(chains, [state_ref[gi] for gi in range(n_groups)], masks)
    for gi in range(n_groups):
        state_ref[gi] = states[gi]
    y = jnp.concatenate([jnp.concatenate(ys[c * n_groups:(c + 1) * n_groups], axis=1) for c in range(n_chunks)],
                        axis=0)

    inv_n = 1.0 / HEAD_DIM
    d = y - _seg_sums([y], seg)[0] * inv_n
    var = _seg_sums([d * d], seg)[0] * inv_n
    yn = d * lax.rsqrt(var + GN_EPS) * lng_ref[...] + lnb_ref[...]
    y_ref[...] = ((yn + bonus_dot * v) * gate).astype(y_ref.dtype)


def _rwkv(u_r, mu_shift, w_decay_up, w0, a_up, a0, g_up, k_k, k_a, r_k, lnx_g, lnx_b, tt):
    b, s, cols = u_r.shape
    tt = min(tt, s)
    w = RWKV_WIDTH
    row = lambda p: p.reshape(1, -1).astype(f32)
    wdec = jnp.concatenate([w_decay_up, jnp.zeros((AAA_RANK, w), f32)], axis=0)
    aup = jnp.concatenate([jnp.zeros((DECAY_RANK, w), f32), a_up], axis=0)
    hid = jnp.arange(w) // HEAD_DIM
    seg = (hid[:, None] == hid[None, :]).astype(bf16)
    ti = jnp.arange(tt)
    tri = ((ti[:, None] // WKV_CHUNK == ti[None, :] // WKV_CHUNK) & (ti[:, None] >= ti[None, :])).astype(bf16)
    params = [row(mu_shift), wdec, row(w0), aup, row(a0), g_up, row(k_k), row(k_a), row(r_k), row(lnx_g),
              row(lnx_b), seg, tri]
    n = WKV_GROUP * HEAD_DIM
    return pl.pallas_call(
        functools.partial(_rwkv_body, tt=tt),
        out_shape=jax.ShapeDtypeStruct((b, s, w), bf16),
        grid=(b, s // tt),
        in_specs=[pl.BlockSpec((None, tt, cols), lambda bi, j: (bi, j, 0))] + [_const_spec(p.shape) for p in params],
        out_specs=pl.BlockSpec((None, tt, w), lambda bi, j: (bi, j, 0)),
        scratch_shapes=[pltpu.VMEM((w // n, HEAD_DIM, n), f32), pltpu.VMEM((SUBLANES, cols), f32)],
        compiler_params=_cparams(("parallel", "arbitrary")),
        name="rwkv7",
    )(u_r, *params)


def _swa_body(sink_ref, u_ref, pos_ref, invf_ref, o_ref, kprev_ref, vprev_ref):
    j = pl.program_id(1)

    @pl.when(j == 0)
    def _():
        kprev_ref[...] = jnp.zeros_like(kprev_ref)
        vprev_ref[...] = jnp.zeros_like(vprev_ref)

    wq = SWA_WIDTH
    kvw = SWA_KV_WIDTH
    u = u_ref[...]
    ang = pos_ref[...].astype(f32) * invf_ref[...]
    cos = jnp.cos(ang)
    sin = jnp.sin(ang)

    def rope(x, c, s):
        n = x.shape[1]
        lane = lax.broadcasted_iota(i32, x.shape, 1)
        half = HEAD_DIM // 2
        rot = jnp.where((lane % HEAD_DIM) < half, -pltpu.roll(x, n - half, axis=1), pltpu.roll(x, half, axis=1))
        return x * c + rot * s

    q = rope(u[:, :wq] * (HEAD_DIM ** -0.5), jnp.concatenate([cos] * (wq // LANES), axis=1),
             jnp.concatenate([sin] * (wq // LANES), axis=1))
    k_cur = rope(u[:, wq:wq + kvw], cos, sin)
    v_cur = u[:, wq + kvw:]
    k_prev = kprev_ref[...]
    v_prev = vprev_ref[...]
    kprev_ref[...] = k_cur
    vprev_ref[...] = v_cur

    gw = SWA_GROUP * HEAD_DIM
    rows = SWA_GROUP * WINDOW
    lane_kv = lax.broadcasted_iota(i32, (WINDOW, kvw), 1)

    def rep(x, gi):
        sw = pltpu.roll(x, HEAD_DIM, axis=1)
        one = jnp.where((lane_kv // HEAD_DIM) == gi, x, sw)
        return jnp.concatenate([one] * (gw // kvw), axis=1).astype(bf16)

    ri = lax.broadcasted_iota(i32, (rows, WINDOW), 0)
    ci = lax.broadcasted_iota(i32, (rows, WINDOW), 1)
    t_idx = ri % WINDOW
    mask_prev = ci > t_idx + jnp.where(j > 0, 0, WINDOW)
    mask_cur = ci <= t_idx
    rb = lax.broadcasted_iota(i32, (rows, 1), 0) // WINDOW
    rbo = lax.broadcasted_iota(i32, (rows, gw), 0) // WINDOW
    cbo = lax.broadcasted_iota(i32, (rows, gw), 1) // HEAD_DIM
    bd = rbo == cbo
    outs = []
    for gi in range(SWA_KV_HEADS):
        qg = q[:, gi * gw:(gi + 1) * gw]
        q_bd = jnp.where(bd, jnp.concatenate([qg] * SWA_GROUP, axis=0), 0.0).astype(bf16)
        s_prev = jnp.where(mask_prev, _dot_nt(q_bd, rep(k_prev, gi)), NEG_INF)
        s_cur = jnp.where(mask_cur, _dot_nt(q_bd, rep(k_cur, gi)), NEG_INF)
        sink = jnp.zeros((rows, 1), f32)
        for h in range(SWA_GROUP):
            sink = jnp.where(rb == h, sink_ref[gi * SWA_GROUP + h], sink)
        m = jnp.maximum(jnp.maximum(jnp.max(s_prev, axis=-1, keepdims=True),
                                    jnp.max(s_cur, axis=-1, keepdims=True)), sink)
        p_prev = jnp.exp(s_prev - m)
        p_cur = jnp.exp(s_cur - m)
        denom = jnp.sum(p_prev, axis=-1, keepdims=True) + jnp.sum(p_cur, axis=-1, keepdims=True) + jnp.exp(sink - m)
        o_bd = _dot(p_prev.astype(bf16), rep(v_prev, gi)) + _dot(p_cur.astype(bf16), rep(v_cur, gi))
        o_bd = jnp.where(bd, o_bd / denom, 0.0)
        og = o_bd[0:WINDOW]
        for h in range(1, SWA_GROUP):
            og = og + o_bd[h * WINDOW:(h + 1) * WINDOW]
        outs.append(og)
    o_ref[...] = jnp.concatenate(outs, axis=1).astype(o_ref.dtype)


def _swa(u_s, positions, sinks):
    b, s, cols = u_s.shape
    half = HEAD_DIM // 2
    inv_freq = ROPE_THETA ** (-jnp.arange(0, HEAD_DIM, 2, dtype=f32) / HEAD_DIM)
    invf = jnp.tile(inv_freq, LANES // half).reshape(1, LANES)
    pos = positions.reshape(b, s, 1).astype(i32)
    return pl.pallas_call(
        _swa_body,
        out_shape=jax.ShapeDtypeStruct((b, s, SWA_WIDTH), bf16),
        grid=(b, s // WINDOW),
        in_specs=[pl.BlockSpec(memory_space=pltpu.SMEM),
                  pl.BlockSpec((None, WINDOW, cols), lambda bi, j: (bi, j, 0)),
                  pl.BlockSpec((None, WINDOW, 1), lambda bi, j: (bi, j, 0)),
                  _const_spec((1, LANES))],
        out_specs=pl.BlockSpec((None, WINDOW, SWA_WIDTH), lambda bi, j: (bi, j, 0)),
        scratch_shapes=[pltpu.VMEM((WINDOW, SWA_KV_WIDTH), f32), pltpu.VMEM((WINDOW, SWA_KV_WIDTH), f32)],
        compiler_params=_cparams(("parallel", "arbitrary")),
        name="swa",
    )(sinks.astype(f32), u_s, pos, invf)


def _mix_out_body(ya_ref, yb_ref, wa_ref, wb_ref, x_ref, g_ref, b_ref, o_ref, *, alpha):
    mix = _dot(ya_ref[...], wa_ref[...]) + _dot(yb_ref[...], wb_ref[...])
    o_ref[...] = _layer_norm(alpha * x_ref[...] + mix, g_ref[...], b_ref[...])


def _mix_out(ya, yb, w_o, x, g, b, alpha, tile):
    n, d = x.shape
    tile = min(tile, n)
    wa = w_o[:ya.shape[1]].astype(bf16)
    wb = w_o[ya.shape[1]:].astype(bf16)
    rows = lambda width: pl.BlockSpec((tile, width), lambda i: (i, 0))
    return pl.pallas_call(
        functools.partial(_mix_out_body, alpha=alpha),
        out_shape=jax.ShapeDtypeStruct((n, d), f32),
        grid=(n // tile,),
        in_specs=[rows(ya.shape[1]), rows(yb.shape[1]), _const_spec(wa.shape), _const_spec(wb.shape), rows(d),
                  _const_spec((1, d)), _const_spec((1, d))],
        out_specs=rows(d),
        compiler_params=_cparams(("parallel",)),
        name="mix_out_ln1",
    )(ya, yb, wa, wb, x, g.reshape(1, d), b.reshape(1, d))


def _xattn_body(x_ref, kv_ref, wq_ref, wo_ref, g_ref, b_ref, o_ref, o3_ref, *, alpha):
    x = x_ref[...]
    d = x.shape[1]
    hd = d // MEM_HEADS
    q = _dot(x.astype(bf16), wq_ref[...]) * (hd ** -0.5)
    kv = kv_ref[...]
    outs = []
    for h in range(MEM_HEADS):
        qh = q[:, h * hd:(h + 1) * hd].astype(bf16)
        kh = kv[:, h * hd:(h + 1) * hd]
        vh = kv[:, d + h * hd:d + (h + 1) * hd]
        s = _dot_nt(qh, kh)
        p = jnp.exp(s - jnp.max(s, axis=-1, keepdims=True))
        l = jnp.sum(p, axis=-1, keepdims=True)
        outs.append(_dot(p.astype(bf16), vh) / l)
    o = jnp.concatenate(outs, axis=1)
    xa = _dot(o.astype(bf16), wo_ref[...])
    y = _layer_norm(alpha * x + xa, g_ref[...], b_ref[...])
    o_ref[...] = y
    words = _pack_bf16_pairs(y)
    for j, w in enumerate(words):
        o3_ref[_lane_chunk(x.shape[0], len(words), j)] = w


def _xattn(x1, kv, wm_q, wm_o, g, b, alpha, tile):
    bsz, s, d = x1.shape
    m = kv.shape[1]
    tile = min(tile, s)
    nj = s // tile
    wq = wm_q.astype(bf16)
    wo = wm_o.astype(bf16)
    return pl.pallas_call(
        functools.partial(_xattn_body, alpha=alpha),
        out_shape=[jax.ShapeDtypeStruct((bsz, s, d), f32),
                   jax.ShapeDtypeStruct((bsz * s * (d // (2 * LANES)), LANES), i32)],
        grid=(bsz, nj),
        in_specs=[pl.BlockSpec((None, tile, d), lambda bi, j: (bi, j, 0)),
                  pl.BlockSpec((None, m, 2 * d), lambda bi, j: (bi, 0, 0)),
                  _const_spec(wq.shape), _const_spec(wo.shape), _const_spec((1, d)), _const_spec((1, d))],
        out_specs=[pl.BlockSpec((None, tile, d), lambda bi, j: (bi, j, 0)),
                   pl.BlockSpec((tile * (d // (2 * LANES)), LANES), lambda bi, j: (bi * nj + j, 0))],
        compiler_params=_cparams(("parallel", "parallel")),
        name="mem_xattn_ln2",
    )(x1, kv, wq, wo, g.reshape(1, d), b.reshape(1, d))


def _router_body(x_ref, wt_ref, bias_ref, upper_ref, e_ref, g_ref, r_ref, cnt_out_ref, cnt_ref, *, t):
    @pl.when(pl.program_id(0) == 0)
    def _():
        cnt_ref[...] = jnp.zeros_like(cnt_ref)

    xh, xl = _split2(x_ref[...])
    wh, wl = _split2(wt_ref[...])
    logits = _dot_nt(wh, xh) + _dot_nt(wh, xl) + _dot_nt(wl, xh)
    scores = _sigmoid(logits)
    biased = scores + bias_ref[...][:, 0:1]
    ne = N_EXPERTS
    neg = -jnp.inf

    def top1(vals):
        rows = lax.broadcasted_iota(i32, vals.shape, 0).astype(f32)
        m = jnp.max(vals, axis=0, keepdims=True)
        idx = jnp.min(jnp.where(vals == m, rows, float(vals.shape[0])), axis=0, keepdims=True)
        return m, idx, rows == idx

    gscores = []
    for gi in range(N_GROUPS):
        blk = biased[gi * GROUP_SIZE:(gi + 1) * GROUP_SIZE, :]
        m1, _, hit = top1(blk)
        m2 = jnp.max(jnp.where(hit, neg, blk), axis=0, keepdims=True)
        gscores.append(m1 + m2)
    gs = jnp.concatenate(gscores, axis=0)
    gsel = jnp.zeros(gs.shape, f32)
    for _ in range(TOPK_GROUPS):
        _, _, hit = top1(gs)
        gsel = jnp.where(hit, 1.0, gsel)
        gs = jnp.where(hit, neg, gs)
    emask = jnp.concatenate(
        [jnp.broadcast_to(gsel[gi:gi + 1, :], (GROUP_SIZE, t)) for gi in range(N_GROUPS)], axis=0) > 0.5
    cand = jnp.where(emask, biased, NEG_INF)
    idxs, sels = [], []
    chosen = jnp.zeros((ne, t), f32)
    for _ in range(TOP_K):
        _, idx, hit = top1(cand)
        idxs.append(idx)
        sels.append(jnp.sum(jnp.where(hit, scores, 0.0), axis=0, keepdims=True))
        chosen = chosen + jnp.where(hit, 1.0, 0.0)
        cand = jnp.where(hit, neg, cand)
    sel = jnp.concatenate(sels, axis=0)
    g_ref[...] = sel / jnp.sum(sel, axis=0, keepdims=True) * ROUTED_SCALE
    e_ref[...] = jnp.concatenate(idxs, axis=0).astype(i32)
    before = _dot(chosen.astype(bf16), upper_ref[...]) + cnt_ref[...][:, 0:1]
    rows = lax.broadcasted_iota(i32, (ne, t), 0).astype(f32)
    ranks = [jnp.sum(jnp.where(rows == idx, before, 0.0), axis=0, keepdims=True) for idx in idxs]
    r_ref[...] = jnp.concatenate(ranks, axis=0).astype(i32)
    cnt_ref[...] = cnt_ref[...] + jnp.sum(chosen, axis=1, keepdims=True)
    cnt_out_ref[...] = cnt_ref[...].astype(i32)


def _router(x2, w_router, router_bias, tile, row0=0, n=None):
    d = x2.shape[1]
    n = x2.shape[0] if n is None else n
    ne = N_EXPERTS
    t = min(tile, n)
    assert row0 % t == 0 and n % t == 0
    first = row0 // t
    wt = w_router.T
    bias = jnp.broadcast_to(router_bias.reshape(ne, 1).astype(f32), (ne, LANES))
    ti = jnp.arange(t)
    upper = (ti[:, None] < ti[None, :]).astype(bf16)
    cols = pl.BlockSpec((TOP_K, t), lambda i: (0, i))
    e_t, g_t, r_t, cnt = pl.pallas_call(
        functools.partial(_router_body, t=t),
        out_shape=[jax.ShapeDtypeStruct((TOP_K, n), i32), jax.ShapeDtypeStruct((TOP_K, n), f32),
                   jax.ShapeDtypeStruct((TOP_K, n), i32), jax.ShapeDtypeStruct((ne, LANES), i32)],
        grid=(n // t,),
        in_specs=[pl.BlockSpec((t, d), lambda i: (i + first, 0)), _const_spec((ne, d)), _const_spec((ne, LANES)),
                  _const_spec((t, t))],
        out_specs=[cols, cols, cols, _const_spec((ne, LANES))],
        scratch_shapes=[pltpu.VMEM((ne, LANES), f32)],
        compiler_params=_cparams(("arbitrary",)),
        name="router",
    )(x2, wt, bias, upper)
    return e_t, g_t, r_t, cnt[:, 0]


def _dispatch_sc(x3d, e_t, r_t, g_t, starts, tok0=0):
    _, nc, lanes = x3d.shape
    n = e_t.shape[1]
    workers = SC_CORES * SC_SUBCORES
    per_w = n // workers
    tc = _sc_chunk_rows(nc, lanes, x3d.dtype)
    assert n % (workers * SC_INDEX_GROUP) == 0 and SC_INDEX_GROUP % tc == 0
    mesh = plsc.VectorSubcoreMesh(core_axis_name="c", subcore_axis_name="s")

    @functools.partial(
        pl.kernel, mesh=mesh,
        out_type=[jax.ShapeDtypeStruct((n * TOP_K, nc, lanes), x3d.dtype),
                  jax.ShapeDtypeStruct((n * TOP_K,), f32)],
        scratch_types=[
            pltpu.VMEM((tc, nc, lanes), x3d.dtype),
            pltpu.VMEM((TOP_K, SC_INDEX_GROUP), i32),
            pltpu.VMEM((TOP_K, SC_INDEX_GROUP), i32),
            pltpu.VMEM((TOP_K, SC_INDEX_GROUP), f32),
            pltpu.VMEM((TOP_K, tc), i32),
            pltpu.VMEM((TOP_K, tc), f32),
            pltpu.VMEM((N_EXPERTS,), i32),
            pltpu.SemaphoreType.DMA,
        ],
        compiler_params=pltpu.CompilerParams(use_tc_tiling_on_sc=True, needs_layout_passes=False),
    )
    def dispatch(x_hbm, e_hbm, r_hbm, g_hbm, st_hbm, o_hbm, gs_hbm, rows_v, e_v, r_v, g_v, slot_v, gate_v, st_v, sem):
        wid = lax.axis_index("s") * SC_CORES + lax.axis_index("c")
        pltpu.sync_copy(st_hbm, st_v)

        @pl.loop(0, per_w // SC_INDEX_GROUP)
        def _(gi):
            base = wid * per_w + gi * SC_INDEX_GROUP
            pltpu.sync_copy(e_hbm.at[:, pl.ds(base, SC_INDEX_GROUP)], e_v)
            pltpu.sync_copy(r_hbm.at[:, pl.ds(base, SC_INDEX_GROUP)], r_v)
            pltpu.sync_copy(g_hbm.at[:, pl.ds(base, SC_INDEX_GROUP)], g_v)
            for h in range(SC_INDEX_GROUP // tc):
                off = h * tc
                pltpu.sync_copy(x_hbm.at[pl.ds(tok0 + base + off, tc)], rows_v)
                for kk in range(TOP_K):
                    for j in range(tc // SC_LANES):
                        src = pl.ds(off + j * SC_LANES, SC_LANES)
                        dst = pl.ds(j * SC_LANES, SC_LANES)
                        slot_v[kk, dst] = r_v[kk, src] + plsc.load_gather(st_v, [e_v[kk, src]])
                        gate_v[kk, dst] = g_v[kk, src]
                copies = [pltpu.async_copy(rows_v, o_hbm.at[slot_v.at[kk]], sem) for kk in range(TOP_K)]
                copies += [pltpu.async_copy(gate_v.at[kk], gs_hbm.at[slot_v.at[kk]], sem) for kk in range(TOP_K)]
                for cp in copies:
                    cp.wait()

    return dispatch(x3d, e_t, r_t, g_t, starts)


def _gmm_body(gid_ref, tid_ref, lo_ref, hi_ref, first_ref, newg_ref, nextg_ref, ord_ref, x_ref, g_ref,
              wg_hbm, wu_hbm, wd_hbm, o_ref, wg_b, wu_b, wd_b, wg_s, wu_s, wd_s, sems, *, tm, layer):
    v = pl.program_id(0)
    lo = lo_ref[v]
    hi = hi_ref[v]
    row0 = tid_ref[v] * tm
    nc = wg_b.shape[0] // LANES
    full = (lo <= row0) & (hi >= row0 + tm)

    def weight_copies(expert, slot):
        return [pltpu.make_async_copy(src.at[layer, expert], dst.at[slot], sems.at[slot])
                for src, dst in ((wg_hbm, wg_s), (wu_hbm, wu_s), (wd_hbm, wd_s))]

    @pl.when(newg_ref[v] == 1)
    def _():
        slot = ord_ref[v] % 2

        @pl.when(v == 0)
        def _():
            for cp in weight_copies(gid_ref[v], slot):
                cp.start()

        for cp in weight_copies(gid_ref[v], slot):
            cp.wait()
        wg_b[...] = wg_s[slot].astype(bf16)
        wu_b[...] = wu_s[slot].astype(bf16)
        wd_b[...] = wd_s[slot].astype(bf16)

        @pl.when(nextg_ref[v] >= 0)
        def _():
            for cp in weight_copies(nextg_ref[v], 1 - slot):
                cp.start()

    def ffn():
        nw = nc // 2
        x = _unpack_bf16_pairs([x_ref[_lane_chunk(tm, nw, j)] for j in range(nw)]).astype(bf16)
        hg = _dot(x, wg_b[...])
        h = hg * _sigmoid(hg) * _dot(x, wu_b[...])
        g = g_ref[...]
        g_cols = jnp.concatenate([g, jnp.zeros((SUBLANES - g.shape[0], LANES), f32)], axis=0).T
        h = jnp.concatenate([h[r * LANES:(r + 1) * LANES, :] * g_cols[:, r:r + 1] for r in range(tm // LANES)],
                            axis=0)
        return _dot(h.astype(bf16), wd_b[...])

    @pl.when(full)
    def _():
        y = ffn()
        for c in range(nc):
            o_ref[_lane_chunk(tm, nc, c)] = y[:, c * LANES:(c + 1) * LANES]

    @pl.when(jnp.logical_not(full) & (hi > lo))
    def _():
        @pl.when(first_ref[v] == 1)
        def _():
            o_ref[...] = jnp.zeros_like(o_ref)

        rows = row0 + lax.broadcasted_iota(i32, (tm, 1), 0)
        mask = (rows >= lo) & (rows < hi)
        y = ffn()
        for c in range(nc):
            idx = _lane_chunk(tm, nc, c)
            o_ref[idx] = jnp.where(mask, y[:, c * LANES:(c + 1) * LANES], o_ref[idx])


def _gmm(xs, gates, we_gate, we_up, we_down, layer, counts, tm):
    _, ne, d, de = we_gate.shape
    nc = d // LANES
    nw = nc // 2
    nk = xs.shape[0] // nw
    tm = min(tm, nk)
    assert tm % LANES == 0 and tm // LANES <= SUBLANES
    n_tiles = nk // tm
    gates3 = gates.reshape(n_tiles, tm // LANES, LANES)
    n_visits = n_tiles + ne - 1
    ends = jnp.cumsum(counts)
    starts = ends - counts
    tile_lo = starts // tm
    n_touch = jnp.where(counts > 0, (ends - 1) // tm - tile_lo + 1, 0)
    vis_end = jnp.cumsum(n_touch)
    vis_start = vis_end - n_touch
    vi = jnp.arange(n_visits, dtype=i32)
    valid = vi < vis_end[-1]
    gid = jnp.minimum(jnp.sum((vis_end[None, :] <= vi[:, None]).astype(i32), axis=1), ne - 1)
    gid = jnp.where(valid, gid, jnp.max(jnp.where(valid, gid, 0)))
    onehot = gid[:, None] == jnp.arange(ne, dtype=i32)[None, :]
    pick = lambda table: jnp.sum(jnp.where(onehot, table[None, :], 0), axis=1)
    tid = jnp.where(valid, pick(tile_lo) + vi - pick(vis_start), n_tiles - 1).astype(i32)
    lo = jnp.where(valid, pick(starts), 0).astype(i32)
    hi = jnp.where(valid, pick(ends), 0).astype(i32)
    one = jnp.ones((1,), i32)
    first = jnp.concatenate([one, (tid[1:] != tid[:-1]).astype(i32)])
    newg = jnp.concatenate([one, (gid[1:] != gid[:-1]).astype(i32)])
    later = gid[None, :] > gid[:, None]
    nextg = jnp.min(jnp.where(later, gid[None, :], ne), axis=1)
    nextg = jnp.where(nextg < ne, nextg, -1).astype(i32)
    order = (jnp.cumsum(newg) - 1).astype(i32)
    rows = lambda chunks: pl.BlockSpec((tm * chunks, LANES), lambda v, g, t, *_: (t[v], 0))
    hbm = pl.BlockSpec(memory_space=pl.ANY)
    return pl.pallas_call(
        functools.partial(_gmm_body, tm=tm, layer=layer),
        out_shape=jax.ShapeDtypeStruct((nk * nc, LANES), f32),
        grid_spec=pltpu.PrefetchScalarGridSpec(
            num_scalar_prefetch=8, grid=(n_visits,),
            in_specs=[rows(nw), pl.BlockSpec((None, tm // LANES, LANES), lambda v, g, t, *_: (t[v], 0, 0)),
                      hbm, hbm, hbm],
            out_specs=rows(nc),
            scratch_shapes=[pltpu.VMEM((d, de), bf16), pltpu.VMEM((d, de), bf16), pltpu.VMEM((de, d), bf16),
                            pltpu.VMEM((2, d, de), f32), pltpu.VMEM((2, d, de), f32), pltpu.VMEM((2, de, d), f32),
                            pltpu.SemaphoreType.DMA((2,))]),
        compiler_params=_cparams(("arbitrary",)),
        name="moe_experts",
    )(gid, tid, lo, hi, first, newg, nextg, order, xs, gates3, we_gate, we_up, we_down)


def _combine_sc(ys3d, e_t, r_t, starts, n):
    _, nc, lanes = ys3d.shape
    workers = SC_CORES * SC_SUBCORES
    per_w = n // workers
    tc = _sc_chunk_rows(nc, lanes, f32)
    assert n % (workers * SC_INDEX_GROUP) == 0 and SC_INDEX_GROUP % tc == 0
    mesh = plsc.VectorSubcoreMesh(core_axis_name="c", subcore_axis_name="s")

    @functools.partial(
        pl.kernel, mesh=mesh,
        out_type=jax.ShapeDtypeStruct((n, nc, lanes), f32),
        scratch_types=[
            pltpu.VMEM((tc, nc, lanes), f32),
            pltpu.VMEM((TOP_K, SC_INDEX_GROUP), i32),
            pltpu.VMEM((TOP_K, SC_INDEX_GROUP), i32),
            pltpu.VMEM((TOP_K, tc), i32),
            pltpu.VMEM((N_EXPERTS,), i32),
            pltpu.SemaphoreType.DMA,
        ],
        compiler_params=pltpu.CompilerParams(use_tc_tiling_on_sc=True, needs_layout_passes=False),
    )
    def combine(y_hbm, e_hbm, r_hbm, st_hbm, o_hbm, acc_v, e_v, r_v, slot_v, st_v, sem):
        wid = lax.axis_index("s") * SC_CORES + lax.axis_index("c")
        pltpu.sync_copy(st_hbm, st_v)

        @pl.loop(0, per_w // SC_INDEX_GROUP)
        def _(gi):
            base = wid * per_w + gi * SC_INDEX_GROUP
            pltpu.sync_copy(e_hbm.at[:, pl.ds(base, SC_INDEX_GROUP)], e_v)
            pltpu.sync_copy(r_hbm.at[:, pl.ds(base, SC_INDEX_GROUP)], r_v)
            for h in range(SC_INDEX_GROUP // tc):
                off = h * tc
                for kk in range(TOP_K):
                    for j in range(tc // SC_LANES):
                        src = pl.ds(off + j * SC_LANES, SC_LANES)
                        slot_v[kk, pl.ds(j * SC_LANES, SC_LANES)] = (
                            r_v[kk, src] + plsc.load_gather(st_v, [e_v[kk, src]]))
                pltpu.async_copy(y_hbm.at[slot_v.at[0]], acc_v, sem).wait()
                copies = [pltpu.async_copy(y_hbm.at[slot_v.at[kk]], acc_v, sem, add=True) for kk in range(1, TOP_K)]
                for cp in copies:
                    cp.wait()
                pltpu.sync_copy(acc_v, o_hbm.at[pl.ds(base + off, tc)])

    return combine(ys3d, e_t, r_t, starts)


def _ffn_out_body(x_ref, r_ref, wg_ref, wu_ref, wd_ref, g_ref, b_ref, *rest, alpha):
    o_ref = rest[-1]
    x = x_ref[...]
    xb = x.astype(bf16)
    hg = _dot(xb, wg_ref[...])
    h = hg * _sigmoid(hg) * _dot(xb, wu_ref[...])
    shared = _dot(h.astype(bf16), wd_ref[...])
    nc = x.shape[1] // LANES
    routed = jnp.concatenate([r_ref[_lane_chunk(x.shape[0], nc, c)] for c in range(nc)], axis=1)
    o_ref[...] = _layer_norm(alpha * x + routed + shared, g_ref[...], b_ref[...])


def _ffn_out(x2, routed_rows, ws_gate, ws_up, ws_down, g, b, alpha, tile, row0=0, prev=None):
    n_all, d = x2.shape
    n = routed_rows.shape[0] // (d // LANES)
    tile = min(tile, n)
    assert row0 % tile == 0 and n % tile == 0
    first = row0 // tile
    wg, wu, wd = ws_gate.astype(bf16), ws_up.astype(bf16), ws_down.astype(bf16)
    rows = pl.BlockSpec((tile, d), lambda i: (i + first, 0))
    in_specs = [rows, pl.BlockSpec((tile * (d // LANES), LANES), lambda i: (i, 0)), _const_spec(wg.shape),
                _const_spec(wu.shape), _const_spec(wd.shape), _const_spec((1, d)), _const_spec((1, d))]
    args = [x2, routed_rows, wg, wu, wd, g.reshape(1, d), b.reshape(1, d)]
    aliases = {}
    if prev is not None:
        in_specs.append(pl.BlockSpec(memory_space=pl.ANY))
        args.append(prev)
        aliases = {len(args) - 1: 0}
    return pl.pallas_call(
        functools.partial(_ffn_out_body, alpha=alpha),
        out_shape=jax.ShapeDtypeStruct((n_all, d), f32),
        grid=(n // tile,),
        in_specs=in_specs,
        out_specs=rows,
        input_output_aliases=aliases,
        compiler_params=_cparams(("parallel",)),
        name="ffn_out_ln3",
    )(*args)


def _layer(x, mem, positions, w_in, mu_shift, w_decay_up, w0, a_up, a0, g_up, k_k, k_a, r_k, lnx_g, lnx_b, sinks,
           w_o, ln1_g, ln1_b, wm_q, wm_kv, wm_o, ln2_g, ln2_b, w_router, router_bias, we_gate, we_up, we_down,
           ws_gate, ws_up, ws_down, ln3_g, ln3_b, *, layer, alpha):
    b, s, d = x.shape
    n = b * s
    xf = x.reshape(n, d)
    w_in_b = w_in.astype(bf16)
    u_r, u_s = _proj(xf, [w_in_b[:, :RWKV_COLS], w_in_b[:, RWKV_COLS:]], [f32, f32], tile=512)
    y_r = _rwkv(u_r.reshape(b, s, RWKV_COLS), mu_shift, w_decay_up, w0, a_up, a0, g_up, k_k, k_a, r_k, lnx_g, lnx_b,
                tt=256)
    y_s = _swa(u_s.reshape(b, s, SWA_COLS), positions, sinks)
    x1 = _mix_out(y_r.reshape(n, RWKV_WIDTH), y_s.reshape(n, SWA_WIDTH), w_o, xf, ln1_g, ln1_b, alpha, tile=512)
    m = mem.shape[1]
    (kv,) = _proj(mem.reshape(b * m, d), [wm_kv.astype(bf16)], [bf16], tile=512)
    x2, x2_rows = _xattn(x1.reshape(b, s, d), kv.reshape(b, m, 2 * d), wm_q, wm_o, ln2_g, ln2_b, alpha, tile=512)
    x2 = x2.reshape(n, d)
    nc = d // LANES
    nw = nc // 2
    parts = MOE_PARTS if n % (MOE_PARTS * SC_CORES * SC_SUBCORES * SC_INDEX_GROUP) == 0 else 1
    npart = n // parts
    x3 = None
    for part in range(parts):
        row0 = part * npart
        e_t, g_t, r_t, counts = _router(x2, w_router, router_bias, tile=512, row0=row0, n=npart)
        starts = (jnp.cumsum(counts) - counts).astype(i32)
        xs, gs = _dispatch_sc(x2_rows.reshape(n, nw, LANES), e_t, r_t, g_t, starts, tok0=row0)
        ys = _gmm(xs.reshape(npart * TOP_K * nw, LANES), gs, we_gate, we_up, we_down, layer, counts, tm=512)
        routed = _combine_sc(ys.reshape(npart * TOP_K, nc, LANES), e_t, r_t, starts, npart)
        x3 = _ffn_out(x2, routed.reshape(npart * nc, LANES), ws_gate, ws_up, ws_down, ln3_g, ln3_b, alpha, tile=512,
                      row0=row0, prev=x3)
    return x3.reshape(b, s, d)


def kernel(x, mem, positions, w_in, mu_shift, w_decay_up, w0, a_up, a0, g_up, k_k, k_a, r_k, lnx_g, lnx_b, sinks, w_o, ln1_g, ln1_b, wm_q, wm_kv, wm_o, ln2_g, ln2_b, w_router, router_bias, we_gate, we_up, we_down, ws_gate, ws_up, ws_down, ln3_g, ln3_b):
    depth = w_in.shape[0]
    alpha = (2 * depth) ** 0.25
    for l in range(depth):
        x = _layer(x, mem, positions, w_in[l], mu_shift[l], w_decay_up[l], w0[l], a_up[l], a0[l], g_up[l], k_k[l],
                   k_a[l], r_k[l], lnx_g[l], lnx_b[l], sinks[l], w_o[l], ln1_g[l], ln1_b[l], wm_q[l], wm_kv[l],
                   wm_o[l], ln2_g[l], ln2_b[l], w_router[l], router_bias[l], we_gate, we_up, we_down,
                   ws_gate[l], ws_up[l], ws_down[l], ln3_g[l], ln3_b[l], layer=l, alpha=alpha)
    return x
```

```python
import functools

import jax
import jax.numpy as jnp
from jax import lax
from jax.experimental import pallas as pl
from jax.experimental.pallas import tpu as pltpu
from jax.experimental.pallas import tpu_sc as plsc

f32 = jnp.float32
bf16 = jnp.bfloat16
i32 = jnp.int32

RWKV_HEADS = 8
HEAD_DIM = 64
RWKV_WIDTH = RWKV_HEADS * HEAD_DIM
DECAY_RANK = 64
AAA_RANK = 64
GATE_RANK = 128
RWKV_COLS = 3 * RWKV_WIDTH + DECAY_RANK + AAA_RANK + GATE_RANK
SWA_Q_HEADS = 8
SWA_KV_HEADS = 2
SWA_GROUP = SWA_Q_HEADS // SWA_KV_HEADS
SWA_WIDTH = SWA_Q_HEADS * HEAD_DIM
SWA_KV_WIDTH = SWA_KV_HEADS * HEAD_DIM
SWA_COLS = SWA_WIDTH + 2 * SWA_KV_WIDTH
WINDOW = 128
ROPE_THETA = 10000.0
MEM_HEADS = 4
N_EXPERTS = 256
TOP_K = 8
N_GROUPS = 8
GROUP_SIZE = N_EXPERTS // N_GROUPS
TOPK_GROUPS = 4
ROUTED_SCALE = 2.5
LN_EPS = 1e-5
GN_EPS = 64e-5
NEG_INF = -1e30

LANES = 128
SUBLANES = 8
WKV_CHUNK = 64
WKV_GROUP = 2
VMEM_LIMIT = 56 * 1024 * 1024

SC_CORES = 2
SC_SUBCORES = 16
SC_LANES = 16
SC_INDEX_GROUP = 128
SC_CHUNK_BYTES = 256 * 1024
MOE_PARTS = 2


def _sc_chunk_rows(nc, lanes, dtype):
    return min(SC_INDEX_GROUP, SC_CHUNK_BYTES // (nc * lanes * jnp.dtype(dtype).itemsize))


def _cparams(sem):
    return pltpu.CompilerParams(dimension_semantics=sem, vmem_limit_bytes=VMEM_LIMIT)


def _const_spec(shape):
    nd = len(shape)
    return pl.BlockSpec(shape, lambda *_: (0,) * nd)


def _dot(a, b):
    return jnp.dot(a, b, preferred_element_type=f32)


def _dot_nt(a, b):
    return lax.dot_general(a, b, (((1,), (1,)), ((), ())), preferred_element_type=f32)


def _dot_tn(a, b):
    return lax.dot_general(a, b, (((0,), (0,)), ((), ())), preferred_element_type=f32)


def _split2(x):
    hi = x.astype(bf16)
    lo = (x - hi.astype(f32)).astype(bf16)
    return hi, lo


def _seg_sums(xs, seg_b):
    parts = []
    for x in xs:
        parts.extend(_split2(x))
    out = _dot(jnp.concatenate(parts, axis=0), seg_b)
    t = xs[0].shape[0]
    return [out[2 * i * t:(2 * i + 1) * t] + out[(2 * i + 1) * t:(2 * i + 2) * t] for i in range(len(xs))]


def _dot_hp(a, b):
    ah, al = _split2(a)
    bh, bl = _split2(b)
    return _dot(ah, bh) + _dot(ah, bl) + _dot(al, bh)


def _dot_exact_lhs(m_bf16, x):
    hi, lo = _split2(x)
    return _dot(m_bf16, hi) + _dot(m_bf16, lo)


def _sigmoid(x):
    return 1.0 / (1.0 + jnp.exp(-x))


def _lane_chunk(n_rows, n_chunks, c):
    return (pl.ds(c, n_rows, stride=n_chunks), slice(None))


def _pack_bf16_pairs(x):
    chunks = []
    for j in range(x.shape[1] // (2 * LANES)):
        lo = x[:, 2 * j * LANES:(2 * j + 1) * LANES].astype(bf16).astype(f32)
        hi = x[:, (2 * j + 1) * LANES:(2 * j + 2) * LANES].astype(bf16).astype(f32)
        chunks.append(lax.bitcast_convert_type(hi, i32) | lax.shift_right_logical(lax.bitcast_convert_type(lo, i32), 16))
    return chunks


def _unpack_bf16_pairs(chunks):
    cols = []
    for w in chunks:
        cols.append(lax.bitcast_convert_type(lax.shift_left(w, 16), f32))
        cols.append(lax.bitcast_convert_type(w & jnp.int32(-65536), f32))
    return jnp.concatenate(cols, axis=1)


def _layer_norm(h, g, b):
    mu = jnp.mean(h, axis=-1, keepdims=True)
    d = h - mu
    var = jnp.mean(d * d, axis=-1, keepdims=True)
    return d * lax.rsqrt(var + LN_EPS) * g + b


def _proj_body(*refs, n_out):
    x_ref = refs[0]
    w_refs = refs[1:1 + n_out]
    o_refs = refs[1 + n_out:]
    xb = x_ref[...].astype(bf16)
    for w_ref, o_ref in zip(w_refs, o_refs):
        o_ref[...] = _dot(xb, w_ref[...]).astype(o_ref.dtype)


def _proj(x, ws, out_dtypes, tile):
    n, k = x.shape
    tile = min(tile, n)
    outs = pl.pallas_call(
        functools.partial(_proj_body, n_out=len(ws)),
        out_shape=[jax.ShapeDtypeStruct((n, w.shape[1]), dt) for w, dt in zip(ws, out_dtypes)],
        grid=(n // tile,),
        in_specs=[pl.BlockSpec((tile, k), lambda i: (i, 0))] + [_const_spec(w.shape) for w in ws],
        out_specs=[pl.BlockSpec((tile, w.shape[1]), lambda i: (i, 0)) for w in ws],
        compiler_params=_cparams(("parallel",)),
        name="proj",
    )(x, *ws)
    return outs


def _wkv_chunks(chains, states, masks):
    bd_b, bd, strict, incl, eye, eye_full = masks
    c, n = chains[0][1].shape
    nch = len(chains)

    def stack(x_b):
        return jnp.where(bd_b, jnp.concatenate([x_b] * WKV_GROUP, axis=0), jnp.zeros((), bf16))

    cast = [tuple(x.astype(bf16) for x in ch[1:6]) for ch in chains]
    v_s = [stack(cb[4]) for cb in cast]
    g = [_dot_nt(jnp.concatenate([cb[0], cb[3]], axis=0), jnp.concatenate([stack(cb[1]), stack(cb[2])], axis=0))
         for cb in cast]
    l_ak = [jnp.where(strict, gi[:c, n:], 0.0).astype(bf16) for gi in g]
    m_rb = [jnp.where(incl, gi[c:, :n], 0.0).astype(bf16) for gi in g]
    m_rk = [jnp.where(incl, gi[c:, n:], 0.0).astype(bf16) for gi in g]
    x = [jnp.where(strict, gi[:c, :n], 0.0) for gi in g]
    t = [eye + xi for xi in x]
    for _ in range(5):
        xb = [xi.astype(bf16) for xi in x]
        x = [_dot(xi, stack(xi)) for xi in xb]
        t = [ti + _dot(ti.astype(bf16), stack(xi.astype(bf16))) for ti, xi in zip(t, x)]
    lakv = [_dot(l_ak[i], v_s[i]).astype(bf16) for i in range(nch)]
    au = [_dot(t[i].astype(bf16), jnp.concatenate([stack(cast[i][0]), stack(lakv[i])], axis=1))
          for i in range(nch)]
    abar = [a[:, :n].astype(bf16) for a in au]
    ubar = [a[:, n:].astype(bf16) for a in au]
    ry = [_dot(m_rb[i], jnp.concatenate([stack(abar[i]), stack(ubar[i])], axis=1)) for i in range(nch)]
    r_bar = [(chains[i][4] + ry[i][:, :n]).astype(bf16) for i in range(nch)]
    y_bar = [ry[i][:, n:] + _dot(m_rk[i], v_s[i]) for i in range(nch)]
    p = [((eye_full + jnp.where(bd, _dot_tn(abar[i], cast[i][1]), 0.0)) * chains[i][6]).astype(bf16)
         for i in range(nch)]
    q = []
    for i in range(nch):
        q_bd = jnp.where(bd, _dot_tn(jnp.concatenate([ubar[i], cast[i][4]], axis=0),
                                     jnp.concatenate([cast[i][1], cast[i][2]], axis=0)), 0.0)
        qi = q_bd[0:c]
        for h in range(1, WKV_GROUP):
            qi = qi + q_bd[h * c:(h + 1) * c]
        q.append(qi * chains[i][6])
    states = list(states)
    ys = []
    for i in range(nch):
        gi = chains[i][0]
        s_b = states[gi].astype(bf16)
        ys.append(_dot_nt(r_bar[i], stack(s_b)) + y_bar[i])
        states[gi] = _dot(s_b, p[i]) + q[i]
    return ys, states


def _rwkv_body(u_ref, mu_ref, wdec_ref, w0_ref, aup_ref, a0_ref, gup_ref, kk_ref, ka_ref, rk_ref,
               lng_ref, lnb_ref, seg_ref, tri_ref, *rest, tt, n_side):
    side_in = rest[:n_side]
    y_ref = rest[n_side]
    side_out = rest[n_side + 1:2 * n_side + 1]
    state_ref, carry_ref = rest[2 * n_side + 1:]
    j = pl.program_id(1)
    for src, dst in zip(side_in, side_out):
        dst[...] = src[...].astype(dst.dtype)

    @pl.when(j == 0)
    def _():
        state_ref[...] = jnp.zeros_like(state_ref)
        carry_ref[...] = jnp.zeros_like(carry_ref)

    w = RWKV_WIDTH
    u = u_ref[...]
    row = lax.broadcasted_iota(i32, u.shape, 0)
    prev = jnp.where(row == 0, carry_ref[0:1, :], pltpu.roll(u, 1, axis=0))
    carry_ref[0:1, :] = u[tt - 1:tt, :]
    us = u + (prev - u) * mu_ref[...]
    r = us[:, 0:w]
    k = us[:, w:2 * w]
    v = us[:, 2 * w:3 * w]
    wa = us[:, 3 * w:3 * w + DECAY_RANK + AAA_RANK]
    gd = us[:, 3 * w + DECAY_RANK + AAA_RANK:]
    z = w0_ref[...] + _dot_hp(jnp.tanh(wa), wdec_ref[...])
    softplus_neg_z = jnp.maximum(-z, 0.0) + jnp.log(1.0 + jnp.exp(-jnp.abs(z)))
    lw = -jnp.exp(-softplus_neg_z - 0.5)
    a = _sigmoid(a0_ref[...] + _dot_hp(wa, aup_ref[...]))
    gate = _dot(_sigmoid(gd).astype(bf16), gup_ref[...].astype(bf16))
    seg = seg_ref[...]
    kk = k * kk_ref[...]
    kmod = k * (1.0 + (a - 1.0) * ka_ref[...])
    kk_sq, bonus_dot = _seg_sums([kk * kk, r * kmod * rk_ref[...]], seg)
    kk = kk / jnp.maximum(jnp.sqrt(kk_sq), 1e-12)
    cum = _dot_exact_lhs(tri_ref[...], lw)
    wc = jnp.exp(cum)
    iwc = jnp.exp(-cum)
    at = -kk * jnp.exp(cum - lw)
    bt = kk * a * iwc
    kt = kmod * iwc
    rt = r * wc

    n = WKV_GROUP * HEAD_DIM
    ri = lax.broadcasted_iota(i32, (n, n), 0)
    ci = lax.broadcasted_iota(i32, (n, n), 1)
    bd = (ri // WKV_CHUNK) == (ci // HEAD_DIM)
    bd_b = jnp.where(bd, 1.0, 0.0).astype(bf16) > 0
    eye_full = jnp.where(ri == ci, 1.0, 0.0).astype(f32)
    ti = lax.broadcasted_iota(i32, (WKV_CHUNK, n), 0)
    si = lax.broadcasted_iota(i32, (WKV_CHUNK, n), 1) % WKV_CHUNK
    masks = (bd_b, bd, ti > si, ti >= si, jnp.where(ti == si, 1.0, 0.0).astype(f32), eye_full)

    n_groups = w // n
    n_chunks = tt // WKV_CHUNK
    chains = []
    for c in range(n_chunks):
        rs = slice(c * WKV_CHUNK, (c + 1) * WKV_CHUNK)
        last = (c + 1) * WKV_CHUNK - 1
        for gi in range(n_groups):
            cs = slice(gi * n, (gi + 1) * n)
            chains.append((gi, at[rs, cs], bt[rs, cs], kt[rs, cs], rt[rs, cs], v[rs, cs], wc[last:last + 1, cs]))
    ys, states = _wkv_chunks(chains, [state_ref[gi] for gi in range(n_groups)], masks)
    for gi in range(n_groups):
        state_ref[gi] = states[gi]
    y = jnp.concatenate([jnp.concatenate(ys[c * n_groups:(c + 1) * n_groups], axis=1) for c in range(n_chunks)],
                        axis=0)

    inv_n = 1.0 / HEAD_DIM
    d = y - _seg_sums([y], seg)[0] * inv_n
    var = _seg_sums([d * d], seg)[0] * inv_n
    yn = d * lax.rsqrt(var + GN_EPS) * lng_ref[...] + lnb_ref[...]
    y_ref[...] = ((yn + bonus_dot * v) * gate).astype(y_ref.dtype)


def _rwkv(u_r, mu_shift, w_decay_up, w0, a_up, a0, g_up, k_k, k_a, r_k, lnx_g, lnx_b, tt, side=(), layer=0):
    b, s, cols = u_r.shape
    tt = min(tt, s)
    n_steps = b * (s // tt)
    assert all(a.shape[1] % n_steps == 0 for a in side)
    w = RWKV_WIDTH
    row = lambda p: p.reshape(1, -1).astype(f32)
    wdec = jnp.concatenate([w_decay_up, jnp.zeros((AAA_RANK, w), f32)], axis=0)
    aup = jnp.concatenate([jnp.zeros((DECAY_RANK, w), f32), a_up], axis=0)
    hid = jnp.arange(w) // HEAD_DIM
    seg = (hid[:, None] == hid[None, :]).astype(bf16)
    ti = jnp.arange(tt)
    tri = ((ti[:, None] // WKV_CHUNK == ti[None, :] // WKV_CHUNK) & (ti[:, None] >= ti[None, :])).astype(bf16)
    params = [row(mu_shift), wdec, row(w0), aup, row(a0), g_up, row(k_k), row(k_a), row(r_k), row(lnx_g),
              row(lnx_b), seg, tri]
    n = WKV_GROUP * HEAD_DIM
    nj = s // tt
    side_in = [pl.BlockSpec((None, a.shape[1] // n_steps) + a.shape[2:], lambda bi, j: (layer, bi * nj + j, 0, 0))
               for a in side]
    side_out = [pl.BlockSpec((a.shape[1] // n_steps,) + a.shape[2:], lambda bi, j: (bi * nj + j, 0, 0))
                for a in side]
    outs = pl.pallas_call(
        functools.partial(_rwkv_body, tt=tt, n_side=len(side)),
        out_shape=[jax.ShapeDtypeStruct((b, s, w), bf16)] + [jax.ShapeDtypeStruct(a.shape[1:], bf16) for a in side],
        grid=(b, nj),
        in_specs=([pl.BlockSpec((None, tt, cols), lambda bi, j: (bi, j, 0))] + [_const_spec(p.shape) for p in params]
                  + side_in),
        out_specs=[pl.BlockSpec((None, tt, w), lambda bi, j: (bi, j, 0))] + side_out,
        scratch_shapes=[pltpu.VMEM((w // n, HEAD_DIM, n), f32), pltpu.VMEM((SUBLANES, cols), f32)],
        compiler_params=_cparams(("parallel", "arbitrary")),
        name="rwkv7",
    )(u_r, *params, *side)
    return outs[0], outs[1:]


def _swa_body(sink_ref, u_ref, pos_ref, invf_ref, o_ref, kprev_ref, vprev_ref):
    j = pl.program_id(1)

    @pl.when(j == 0)
    def _():
        kprev_ref[...] = jnp.zeros_like(kprev_ref)
        vprev_ref[...] = jnp.zeros_like(vprev_ref)

    wq = SWA_WIDTH
    kvw = SWA_KV_WIDTH
    u = u_ref[...]
    ang = pos_ref[...].astype(f32) * invf_ref[...]
    cos = jnp.cos(ang)
    sin = jnp.sin(ang)

    def rope(x, c, s):
        n = x.shape[1]
        lane = lax.broadcasted_iota(i32, x.shape, 1)
        half = HEAD_DIM // 2
        rot = jnp.where((lane % HEAD_DIM) < half, -pltpu.roll(x, n - half, axis=1), pltpu.roll(x, half, axis=1))
        return x * c + rot * s

    q = rope(u[:, :wq] * (HEAD_DIM ** -0.5), jnp.concatenate([cos] * (wq // LANES), axis=1),
             jnp.concatenate([sin] * (wq // LANES), axis=1))
    k_cur = rope(u[:, wq:wq + kvw], cos, sin)
    v_cur = u[:, wq + kvw:]
    k_prev = kprev_ref[...]
    v_prev = vprev_ref[...]
    kprev_ref[...] = k_cur
    vprev_ref[...] = v_cur

    gw = SWA_GROUP * HEAD_DIM
    rows = SWA_GROUP * WINDOW
    lane_kv = lax.broadcasted_iota(i32, (WINDOW, kvw), 1)

    def rep(x, gi):
        sw = pltpu.roll(x, HEAD_DIM, axis=1)
        one = jnp.where((lane_kv // HEAD_DIM) == gi, x, sw)
        return jnp.concatenate([one] * (gw // kvw), axis=1).astype(bf16)

    ri = lax.broadcasted_iota(i32, (rows, WINDOW), 0)
    ci = lax.broadcasted_iota(i32, (rows, WINDOW), 1)
    t_idx = ri % WINDOW
    mask_prev = ci > t_idx + jnp.where(j > 0, 0, WINDOW)
    mask_cur = ci <= t_idx
    rb = lax.broadcasted_iota(i32, (rows, 1), 0) // WINDOW
    rbo = lax.broadcasted_iota(i32, (rows, gw), 0) // WINDOW
    cbo = lax.broadcasted_iota(i32, (rows, gw), 1) // HEAD_DIM
    bd = rbo == cbo
    outs = []
    for gi in range(SWA_KV_HEADS):
        qg = q[:, gi * gw:(gi + 1) * gw]
        q_bd = jnp.where(bd, jnp.concatenate([qg] * SWA_GROUP, axis=0), 0.0).astype(bf16)
        s_prev = jnp.where(mask_prev, _dot_nt(q_bd, rep(k_prev, gi)), NEG_INF)
        s_cur = jnp.where(mask_cur, _dot_nt(q_bd, rep(k_cur, gi)), NEG_INF)
        sink = jnp.zeros((rows, 1), f32)
        for h in range(SWA_GROUP):
            sink = jnp.where(rb == h, sink_ref[gi * SWA_GROUP + h], sink)
        m = jnp.maximum(jnp.maximum(jnp.max(s_prev, axis=-1, keepdims=True),
                                    jnp.max(s_cur, axis=-1, keepdims=True)), sink)
        p_prev = jnp.exp(s_prev - m)
        p_cur = jnp.exp(s_cur - m)
        denom = jnp.sum(p_prev, axis=-1, keepdims=True) + jnp.sum(p_cur, axis=-1, keepdims=True) + jnp.exp(sink - m)
        o_bd = _dot(p_prev.astype(bf16), rep(v_prev, gi)) + _dot(p_cur.astype(bf16), rep(v_cur, gi))
        o_bd = jnp.where(bd, o_bd / denom, 0.0)
        og = o_bd[0:WINDOW]
        for h in range(1, SWA_GROUP):
            og = og + o_bd[h * WINDOW:(h + 1) * WINDOW]
        outs.append(og)
    o_ref[...] = jnp.concatenate(outs, axis=1).astype(o_ref.dtype)


def _swa(u_s, positions, sinks):
    b, s, cols = u_s.shape
    half = HEAD_DIM // 2
    inv_freq = ROPE_THETA ** (-jnp.arange(0, HEAD_DIM, 2, dtype=f32) / HEAD_DIM)
    invf = jnp.tile(inv_freq, LANES // half).reshape(1, LANES)
    pos = positions.reshape(b, s, 1).astype(i32)
    return pl.pallas_call(
        _swa_body,
        out_shape=jax.ShapeDtypeStruct((b, s, SWA_WIDTH), bf16),
        grid=(b, s // WINDOW),
        in_specs=[pl.BlockSpec(memory_space=pltpu.SMEM),
                  pl.BlockSpec((None, WINDOW, cols), lambda bi, j: (bi, j, 0)),
                  pl.BlockSpec((None, WINDOW, 1), lambda bi, j: (bi, j, 0)),
                  _const_spec((1, LANES))],
        out_specs=pl.BlockSpec((None, WINDOW, SWA_WIDTH), lambda bi, j: (bi, j, 0)),
        scratch_shapes=[pltpu.VMEM((WINDOW, SWA_KV_WIDTH), f32), pltpu.VMEM((WINDOW, SWA_KV_WIDTH), f32)],
        compiler_params=_cparams(("parallel", "arbitrary")),
        name="swa",
    )(sinks.astype(f32), u_s, pos, invf)


def _mix_out_body(ya_ref, yb_ref, wa_ref, wb_ref, x_ref, g_ref, b_ref, o_ref, *, alpha):
    mix = _dot(ya_ref[...], wa_ref[...]) + _dot(yb_ref[...], wb_ref[...])
    o_ref[...] = _layer_norm(alpha * x_ref[...] + mix, g_ref[...], b_ref[...])


def _mix_out(ya, yb, w_o, x, g, b, alpha, tile):
    n, d = x.shape
    tile = min(tile, n)
    wa = w_o[:ya.shape[1]].astype(bf16)
    wb = w_o[ya.shape[1]:].astype(bf16)
    rows = lambda width: pl.BlockSpec((tile, width), lambda i: (i, 0))
    return pl.pallas_call(
        functools.partial(_mix_out_body, alpha=alpha),
        out_shape=jax.ShapeDtypeStruct((n, d), f32),
        grid=(n // tile,),
        in_specs=[rows(ya.shape[1]), rows(yb.shape[1]), _const_spec(wa.shape), _const_spec(wb.shape), rows(d),
                  _const_spec((1, d)), _const_spec((1, d))],
        out_specs=rows(d),
        compiler_params=_cparams(("parallel",)),
        name="mix_out_ln1",
    )(ya, yb, wa, wb, x, g.reshape(1, d), b.reshape(1, d))


def _xattn_body(x_ref, kv_ref, wq_ref, wo_ref, g_ref, b_ref, o_ref, o3_ref, *, alpha):
    x = x_ref[...]
    d = x.shape[1]
    hd = d // MEM_HEADS
    q = _dot(x.astype(bf16), wq_ref[...]) * (hd ** -0.5)
    kv = kv_ref[...]
    outs = []
    for h in range(MEM_HEADS):
        qh = q[:, h * hd:(h + 1) * hd].astype(bf16)
        kh = kv[:, h * hd:(h + 1) * hd]
        vh = kv[:, d + h * hd:d + (h + 1) * hd]
        s = _dot_nt(qh, kh)
        p = jnp.exp(s - jnp.max(s, axis=-1, keepdims=True))
        l = jnp.sum(p, axis=-1, keepdims=True)
        outs.append(_dot(p.astype(bf16), vh) / l)
    o = jnp.concatenate(outs, axis=1)
    xa = _dot(o.astype(bf16), wo_ref[...])
    y = _layer_norm(alpha * x + xa, g_ref[...], b_ref[...])
    o_ref[...] = y
    words = _pack_bf16_pairs(y)
    for j, w in enumerate(words):
        o3_ref[_lane_chunk(x.shape[0], len(words), j)] = w


def _xattn(x1, kv, wm_q, wm_o, g, b, alpha, tile):
    bsz, s, d = x1.shape
    m = kv.shape[1]
    tile = min(tile, s)
    nj = s // tile
    wq = wm_q.astype(bf16)
    wo = wm_o.astype(bf16)
    return pl.pallas_call(
        functools.partial(_xattn_body, alpha=alpha),
        out_shape=[jax.ShapeDtypeStruct((bsz, s, d), f32),
                   jax.ShapeDtypeStruct((bsz * s * (d // (2 * LANES)), LANES), i32)],
        grid=(bsz, nj),
        in_specs=[pl.BlockSpec((None, tile, d), lambda bi, j: (bi, j, 0)),
                  pl.BlockSpec((None, m, 2 * d), lambda bi, j: (bi, 0, 0)),
                  _const_spec(wq.shape), _const_spec(wo.shape), _const_spec((1, d)), _const_spec((1, d))],
        out_specs=[pl.BlockSpec((None, tile, d), lambda bi, j: (bi, j, 0)),
                   pl.BlockSpec((tile * (d // (2 * LANES)), LANES), lambda bi, j: (bi * nj + j, 0))],
        compiler_params=_cparams(("parallel", "parallel")),
        name="mem_xattn_ln2",
    )(x1, kv, wq, wo, g.reshape(1, d), b.reshape(1, d))


def _router_body(x_ref, wt_ref, bias_ref, upper_ref, e_ref, g_ref, r_ref, cnt_out_ref, cnt_ref, *, t):
    @pl.when(pl.program_id(0) == 0)
    def _():
        cnt_ref[...] = jnp.zeros_like(cnt_ref)

    xh, xl = _split2(x_ref[...])
    wh, wl = _split2(wt_ref[...])
    logits = _dot_nt(wh, xh) + _dot_nt(wh, xl) + _dot_nt(wl, xh)
    scores = _sigmoid(logits)
    biased = scores + bias_ref[...][:, 0:1]
    ne = N_EXPERTS
    neg = -jnp.inf

    def top1(vals):
        rows = lax.broadcasted_iota(i32, vals.shape, 0).astype(f32)
        m = jnp.max(vals, axis=0, keepdims=True)
        idx = jnp.min(jnp.where(vals == m, rows, float(vals.shape[0])), axis=0, keepdims=True)
        return m, idx, rows == idx

    gscores = []
    for gi in range(N_GROUPS):
        blk = biased[gi * GROUP_SIZE:(gi + 1) * GROUP_SIZE, :]
        m1, _, hit = top1(blk)
        m2 = jnp.max(jnp.where(hit, neg, blk), axis=0, keepdims=True)
        gscores.append(m1 + m2)
    gs = jnp.concatenate(gscores, axis=0)
    gsel = jnp.zeros(gs.shape, f32)
    for _ in range(TOPK_GROUPS):
        _, _, hit = top1(gs)
        gsel = jnp.where(hit, 1.0, gsel)
        gs = jnp.where(hit, neg, gs)
    emask = jnp.concatenate(
        [jnp.broadcast_to(gsel[gi:gi + 1, :], (GROUP_SIZE, t)) for gi in range(N_GROUPS)], axis=0) > 0.5
    cand = jnp.where(emask, biased, NEG_INF)
    idxs, sels = [], []
    chosen = jnp.zeros((ne, t), f32)
    for _ in range(TOP_K):
        _, idx, hit = top1(cand)
        idxs.append(idx)
        sels.append(jnp.sum(jnp.where(hit, scores, 0.0), axis=0, keepdims=True))
        chosen = chosen + jnp.where(hit, 1.0, 0.0)
        cand = jnp.where(hit, neg, cand)
    sel = jnp.concatenate(sels, axis=0)
    g_ref[...] = sel / jnp.sum(sel, axis=0, keepdims=True) * ROUTED_SCALE
    e_ref[...] = jnp.concatenate(idxs, axis=0).astype(i32)
    before = _dot(chosen.astype(bf16), upper_ref[...]) + cnt_ref[...][:, 0:1]
    rows = lax.broadcasted_iota(i32, (ne, t), 0).astype(f32)
    ranks = [jnp.sum(jnp.where(rows == idx, before, 0.0), axis=0, keepdims=True) for idx in idxs]
    r_ref[...] = jnp.concatenate(ranks, axis=0).astype(i32)
    cnt_ref[...] = cnt_ref[...] + jnp.sum(chosen, axis=1, keepdims=True)
    cnt_out_ref[...] = cnt_ref[...].astype(i32)


def _router(x2, w_router, router_bias, tile, row0=0, n=None):
    d = x2.shape[1]
    n = x2.shape[0] if n is None else n
    ne = N_EXPERTS
    t = min(tile, n)
    assert row0 % t == 0 and n % t == 0
    first = row0 // t
    wt = w_router.T
    bias = jnp.broadcast_to(router_bias.reshape(ne, 1).astype(f32), (ne, LANES))
    ti = jnp.arange(t)
    upper = (ti[:, None] < ti[None, :]).astype(bf16)
    cols = pl.BlockSpec((TOP_K, t), lambda i: (0, i))
    e_t, g_t, r_t, cnt = pl.pallas_call(
        functools.partial(_router_body, t=t),
        out_shape=[jax.ShapeDtypeStruct((TOP_K, n), i32), jax.ShapeDtypeStruct((TOP_K, n), f32),
                   jax.ShapeDtypeStruct((TOP_K, n), i32), jax.ShapeDtypeStruct((ne, LANES), i32)],
        grid=(n // t,),
        in_specs=[pl.BlockSpec((t, d), lambda i: (i + first, 0)), _const_spec((ne, d)), _const_spec((ne, LANES)),
                  _const_spec((t, t))],
        out_specs=[cols, cols, cols, _const_spec((ne, LANES))],
        scratch_shapes=[pltpu.VMEM((ne, LANES), f32)],
        compiler_params=_cparams(("arbitrary",)),
        name="router",
    )(x2, wt, bias, upper)
    return e_t, g_t, r_t, cnt[:, 0]


def _dispatch_sc(x3d, e_t, r_t, g_t, starts, tok0=0):
    _, nc, lanes = x3d.shape
    n = e_t.shape[1]
    workers = SC_CORES * SC_SUBCORES
    per_w = n // workers
    tc = _sc_chunk_rows(nc, lanes, x3d.dtype)
    assert n % (workers * SC_INDEX_GROUP) == 0 and SC_INDEX_GROUP % tc == 0
    mesh = plsc.VectorSubcoreMesh(core_axis_name="c", subcore_axis_name="s")

    @functools.partial(
        pl.kernel, mesh=mesh,
        out_type=[jax.ShapeDtypeStruct((n * TOP_K, nc, lanes), x3d.dtype),
                  jax.ShapeDtypeStruct((n * TOP_K,), f32)],
        scratch_types=[
            pltpu.VMEM((tc, nc, lanes), x3d.dtype),
            pltpu.VMEM((TOP_K, SC_INDEX_GROUP), i32),
            pltpu.VMEM((TOP_K, SC_INDEX_GROUP), i32),
            pltpu.VMEM((TOP_K, SC_INDEX_GROUP), f32),
            pltpu.VMEM((TOP_K, tc), i32),
            pltpu.VMEM((TOP_K, tc), f32),
            pltpu.VMEM((N_EXPERTS,), i32),
            pltpu.SemaphoreType.DMA,
        ],
        compiler_params=pltpu.CompilerParams(use_tc_tiling_on_sc=True, needs_layout_passes=False),
    )
    def dispatch(x_hbm, e_hbm, r_hbm, g_hbm, st_hbm, o_hbm, gs_hbm, rows_v, e_v, r_v, g_v, slot_v, gate_v, st_v, sem):
        wid = lax.axis_index("s") * SC_CORES + lax.axis_index("c")
        pltpu.sync_copy(st_hbm, st_v)

        @pl.loop(0, per_w // SC_INDEX_GROUP)
        def _(gi):
            base = wid * per_w + gi * SC_INDEX_GROUP
            pltpu.sync_copy(e_hbm.at[:, pl.ds(base, SC_INDEX_GROUP)], e_v)
            pltpu.sync_copy(r_hbm.at[:, pl.ds(base, SC_INDEX_GROUP)], r_v)
            pltpu.sync_copy(g_hbm.at[:, pl.ds(base, SC_INDEX_GROUP)], g_v)
            for h in range(SC_INDEX_GROUP // tc):
                off = h * tc
                pltpu.sync_copy(x_hbm.at[pl.ds(tok0 + base + off, tc)], rows_v)
                for kk in range(TOP_K):
                    for j in range(tc // SC_LANES):
                        src = pl.ds(off + j * SC_LANES, SC_LANES)
                        dst = pl.ds(j * SC_LANES, SC_LANES)
                        slot_v[kk, dst] = r_v[kk, src] + plsc.load_gather(st_v, [e_v[kk, src]])
                        gate_v[kk, dst] = g_v[kk, src]
                copies = [pltpu.async_copy(rows_v, o_hbm.at[slot_v.at[kk]], sem) for kk in range(TOP_K)]
                copies += [pltpu.async_copy(gate_v.at[kk], gs_hbm.at[slot_v.at[kk]], sem) for kk in range(TOP_K)]
                for cp in copies:
                    cp.wait()

    return dispatch(x3d, e_t, r_t, g_t, starts)


def _gmm_body(gid_ref, tid_ref, lo_ref, hi_ref, first_ref, newg_ref, nextg_ref, ord_ref, x_ref, g_ref,
              wg_hbm, wu_hbm, wd_hbm, o_ref, wg_b, wu_b, wd_b, sems, *, tm):
    v = pl.program_id(0)
    lo = lo_ref[v]
    hi = hi_ref[v]
    row0 = tid_ref[v] * tm
    nc = wg_b.shape[1] // LANES
    full = (lo <= row0) & (hi >= row0 + tm)
    slot = ord_ref[v] % 2

    def weight_copies(expert, dst_slot):
        return [pltpu.make_async_copy(src.at[expert], dst.at[dst_slot], sems.at[dst_slot])
                for src, dst in ((wg_hbm, wg_b), (wu_hbm, wu_b), (wd_hbm, wd_b))]

    @pl.when(newg_ref[v] == 1)
    def _():
        @pl.when(v == 0)
        def _():
            for cp in weight_copies(gid_ref[v], slot):
                cp.start()

        for cp in weight_copies(gid_ref[v], slot):
            cp.wait()

        @pl.when(nextg_ref[v] >= 0)
        def _():
            for cp in weight_copies(nextg_ref[v], 1 - slot):
                cp.start()

    def ffn():
        nw = nc // 2
        x = _unpack_bf16_pairs([x_ref[_lane_chunk(tm, nw, j)] for j in range(nw)]).astype(bf16)
        hg = _dot(x, wg_b[slot])
        h = hg * _sigmoid(hg) * _dot(x, wu_b[slot])
        g = g_ref[...]
        g_cols = jnp.concatenate([g, jnp.zeros((SUBLANES - g.shape[0], LANES), f32)], axis=0).T
        h = jnp.concatenate([h[r * LANES:(r + 1) * LANES, :] * g_cols[:, r:r + 1] for r in range(tm // LANES)],
                            axis=0)
        return _dot(h.astype(bf16), wd_b[slot])

    @pl.when(full)
    def _():
        y = ffn()
        for c in range(nc):
            o_ref[_lane_chunk(tm, nc, c)] = y[:, c * LANES:(c + 1) * LANES]

    @pl.when(jnp.logical_not(full) & (hi > lo))
    def _():
        @pl.when(first_ref[v] == 1)
        def _():
            o_ref[...] = jnp.zeros_like(o_ref)

        rows = row0 + lax.broadcasted_iota(i32, (tm, 1), 0)
        mask = (rows >= lo) & (rows < hi)
        y = ffn()
        for c in range(nc):
            idx = _lane_chunk(tm, nc, c)
            o_ref[idx] = jnp.where(mask, y[:, c * LANES:(c + 1) * LANES], o_ref[idx])


def _gmm(xs, gates, we_gate, we_up, we_down, counts, tm):
    ne, d, de = we_gate.shape
    nc = d // LANES
    nw = nc // 2
    nk = xs.shape[0] // nw
    tm = min(tm, nk)
    assert tm % LANES == 0 and tm // LANES <= SUBLANES
    n_tiles = nk // tm
    gates3 = gates.reshape(n_tiles, tm // LANES, LANES)
    n_visits = n_tiles + ne - 1
    ends = jnp.cumsum(counts)
    starts = ends - counts
    tile_lo = starts // tm
    n_touch = jnp.where(counts > 0, (ends - 1) // tm - tile_lo + 1, 0)
    vis_end = jnp.cumsum(n_touch)
    vis_start = vis_end - n_touch
    vi = jnp.arange(n_visits, dtype=i32)
    valid = vi < vis_end[-1]
    gid = jnp.minimum(jnp.sum((vis_end[None, :] <= vi[:, None]).astype(i32), axis=1), ne - 1)
    gid = jnp.where(valid, gid, jnp.max(jnp.where(valid, gid, 0)))
    onehot = gid[:, None] == jnp.arange(ne, dtype=i32)[None, :]
    pick = lambda table: jnp.sum(jnp.where(onehot, table[None, :], 0), axis=1)
    tid = jnp.where(valid, pick(tile_lo) + vi - pick(vis_start), n_tiles - 1).astype(i32)
    lo = jnp.where(valid, pick(starts), 0).astype(i32)
    hi = jnp.where(valid, pick(ends), 0).astype(i32)
    one = jnp.ones((1,), i32)
    first = jnp.concatenate([one, (tid[1:] != tid[:-1]).astype(i32)])
    newg = jnp.concatenate([one, (gid[1:] != gid[:-1]).astype(i32)])
    later = gid[None, :] > gid[:, None]
    nextg = jnp.min(jnp.where(later, gid[None, :], ne), axis=1)
    nextg = jnp.where(nextg < ne, nextg, -1).astype(i32)
    order = (jnp.cumsum(newg) - 1).astype(i32)
    rows = lambda chunks: pl.BlockSpec((tm * chunks, LANES), lambda v, g, t, *_: (t[v], 0))
    hbm = pl.BlockSpec(memory_space=pl.ANY)
    return pl.pallas_call(
        functools.partial(_gmm_body, tm=tm),
        out_shape=jax.ShapeDtypeStruct((nk * nc, LANES), f32),
        grid_spec=pltpu.PrefetchScalarGridSpec(
            num_scalar_prefetch=8, grid=(n_visits,),
            in_specs=[rows(nw), pl.BlockSpec((None, tm // LANES, LANES), lambda v, g, t, *_: (t[v], 0, 0)),
                      hbm, hbm, hbm],
            out_specs=rows(nc),
            scratch_shapes=[pltpu.VMEM((2, d, de), bf16), pltpu.VMEM((2, d, de), bf16), pltpu.VMEM((2, de, d), bf16),
                            pltpu.SemaphoreType.DMA((2,))]),
        compiler_params=_cparams(("arbitrary",)),
        name="moe_experts",
    )(gid, tid, lo, hi, first, newg, nextg, order, xs, gates3, we_gate, we_up, we_down)


def _combine_sc(ys3d, e_t, r_t, starts, n):
    _, nc, lanes = ys3d.shape
    workers = SC_CORES * SC_SUBCORES
    per_w = n // workers
    tc = _sc_chunk_rows(nc, lanes, f32)
    assert n % (workers * SC_INDEX_GROUP) == 0 and SC_INDEX_GROUP % tc == 0
    mesh = plsc.VectorSubcoreMesh(core_axis_name="c", subcore_axis_name="s")

    @functools.partial(
        pl.kernel, mesh=mesh,
        out_type=jax.ShapeDtypeStruct((n, nc, lanes), f32),
        scratch_types=[
            pltpu.VMEM((tc, nc, lanes), f32),
            pltpu.VMEM((TOP_K, SC_INDEX_GROUP), i32),
            pltpu.VMEM((TOP_K, SC_INDEX_GROUP), i32),
            pltpu.VMEM((TOP_K, tc), i32),
            pltpu.VMEM((N_EXPERTS,), i32),
            pltpu.SemaphoreType.DMA,
        ],
        compiler_params=pltpu.CompilerParams(use_tc_tiling_on_sc=True, needs_layout_passes=False),
    )
    def combine(y_hbm, e_hbm, r_hbm, st_hbm, o_hbm, acc_v, e_v, r_v, slot_v, st_v, sem):
        wid = lax.axis_index("s") * SC_CORES + lax.axis_index("c")
        pltpu.sync_copy(st_hbm, st_v)

        @pl.loop(0, per_w // SC_INDEX_GROUP)
        def _(gi):
            base = wid * per_w + gi * SC_INDEX_GROUP
            pltpu.sync_copy(e_hbm.at[:, pl.ds(base, SC_INDEX_GROUP)], e_v)
            pltpu.sync_copy(r_hbm.at[:, pl.ds(base, SC_INDEX_GROUP)], r_v)
            for h in range(SC_INDEX_GROUP // tc):
                off = h * tc
                for kk in range(TOP_K):
                    for j in range(tc // SC_LANES):
                        src = pl.ds(off + j * SC_LANES, SC_LANES)
                        slot_v[kk, pl.ds(j * SC_LANES, SC_LANES)] = (
                            r_v[kk, src] + plsc.load_gather(st_v, [e_v[kk, src]]))
                pltpu.async_copy(y_hbm.at[slot_v.at[0]], acc_v, sem).wait()
                copies = [pltpu.async_copy(y_hbm.at[slot_v.at[kk]], acc_v, sem, add=True) for kk in range(1, TOP_K)]
                for cp in copies:
                    cp.wait()
                pltpu.sync_copy(acc_v, o_hbm.at[pl.ds(base + off, tc)])

    return combine(ys3d, e_t, r_t, starts)


def _ffn_out_body(x_ref, r_ref, wg_ref, wu_ref, wd_ref, g_ref, b_ref, *rest, alpha):
    o_ref = rest[-1]
    x = x_ref[...]
    xb = x.astype(bf16)
    hg = _dot(xb, wg_ref[...])
    h = hg * _sigmoid(hg) * _dot(xb, wu_ref[...])
    shared = _dot(h.astype(bf16), wd_ref[...])
    nc = x.shape[1] // LANES
    routed = jnp.concatenate([r_ref[_lane_chunk(x.shape[0], nc, c)] for c in range(nc)], axis=1)
    o_ref[...] = _layer_norm(alpha * x + routed + shared, g_ref[...], b_ref[...])


def _ffn_out(x2, routed_rows, ws_gate, ws_up, ws_down, g, b, alpha, tile, row0=0, prev=None):
    n_all, d = x2.shape
    n = routed_rows.shape[0] // (d // LANES)
    tile = min(tile, n)
    assert row0 % tile == 0 and n % tile == 0
    first = row0 // tile
    wg, wu, wd = ws_gate.astype(bf16), ws_up.astype(bf16), ws_down.astype(bf16)
    rows = pl.BlockSpec((tile, d), lambda i: (i + first, 0))
    in_specs = [rows, pl.BlockSpec((tile * (d // LANES), LANES), lambda i: (i, 0)), _const_spec(wg.shape),
                _const_spec(wu.shape), _const_spec(wd.shape), _const_spec((1, d)), _const_spec((1, d))]
    args = [x2, routed_rows, wg, wu, wd, g.reshape(1, d), b.reshape(1, d)]
    aliases = {}
    if prev is not None:
        in_specs.append(pl.BlockSpec(memory_space=pl.ANY))
        args.append(prev)
        aliases = {len(args) - 1: 0}
    return pl.pallas_call(
        functools.partial(_ffn_out_body, alpha=alpha),
        out_shape=jax.ShapeDtypeStruct((n_all, d), f32),
        grid=(n // tile,),
        in_specs=in_specs,
        out_specs=rows,
        input_output_aliases=aliases,
        compiler_params=_cparams(("parallel",)),
        name="ffn_out_ln3",
    )(*args)


def _layer(x, mem, positions, w_in, mu_shift, w_decay_up, w0, a_up, a0, g_up, k_k, k_a, r_k, lnx_g, lnx_b, sinks,
           w_o, ln1_g, ln1_b, wm_q, wm_kv, wm_o, ln2_g, ln2_b, w_router, router_bias, we_gate, we_up, we_down,
           ws_gate, ws_up, ws_down, ln3_g, ln3_b, *, layer, alpha):
    b, s, d = x.shape
    n = b * s
    xf = x.reshape(n, d)
    w_in_b = w_in.astype(bf16)
    u_r, u_s = _proj(xf, [w_in_b[:, :RWKV_COLS], w_in_b[:, RWKV_COLS:]], [f32, f32], tile=512)
    rwkv_tt = 256
    experts = (we_gate, we_up, we_down)
    if N_EXPERTS % (b * (s // min(rwkv_tt, s))) == 0:
        y_r, experts_b = _rwkv(u_r.reshape(b, s, RWKV_COLS), mu_shift, w_decay_up, w0, a_up, a0, g_up, k_k, k_a, r_k,
                               lnx_g, lnx_b, tt=rwkv_tt, side=experts, layer=layer)
    else:
        y_r, _ = _rwkv(u_r.reshape(b, s, RWKV_COLS), mu_shift, w_decay_up, w0, a_up, a0, g_up, k_k, k_a, r_k,
                       lnx_g, lnx_b, tt=rwkv_tt)
        experts_b = [w[layer].astype(bf16) for w in experts]
    y_s = _swa(u_s.reshape(b, s, SWA_COLS), positions, sinks)
    x1 = _mix_out(y_r.reshape(n, RWKV_WIDTH), y_s.reshape(n, SWA_WIDTH), w_o, xf, ln1_g, ln1_b, alpha, tile=512)
    m = mem.shape[1]
    (kv,) = _proj(mem.reshape(b * m, d), [wm_kv.astype(bf16)], [bf16], tile=512)
    x2, x2_rows = _xattn(x1.reshape(b, s, d), kv.reshape(b, m, 2 * d), wm_q, wm_o, ln2_g, ln2_b, alpha, tile=512)
    x2 = x2.reshape(n, d)
    nc = d // LANES
    nw = nc // 2
    parts = MOE_PARTS if n % (MOE_PARTS * SC_CORES * SC_SUBCORES * SC_INDEX_GROUP) == 0 else 1
    npart = n // parts
    x3 = None
    for part in range(parts):
        row0 = part * npart
        e_t, g_t, r_t, counts = _router(x2, w_router, router_bias, tile=512, row0=row0, n=npart)
        starts = (jnp.cumsum(counts) - counts).astype(i32)
        xs, gs = _dispatch_sc(x2_rows.reshape(n, nw, LANES), e_t, r_t, g_t, starts, tok0=row0)
        ys = _gmm(xs.reshape(npart * TOP_K * nw, LANES), gs, *experts_b, counts, tm=512)
        routed = _combine_sc(ys.reshape(npart * TOP_K, nc, LANES), e_t, r_t, starts, npart)
        x3 = _ffn_out(x2, routed.reshape(npart * nc, LANES), ws_gate, ws_up, ws_down, ln3_g, ln3_b, alpha, tile=512,
                      row0=row0, prev=x3)
    return x3.reshape(b, s, d)


def kernel(x, mem, positions, w_in, mu_shift, w_decay_up, w0, a_up, a0, g_up, k_k, k_a, r_k, lnx_g, lnx_b, sinks, w_o, ln1_g, ln1_b, wm_q, wm_kv, wm_o, ln2_g, ln2_b, w_router, router_bias, we_gate, we_up, we_down, ws_gate, ws_up, ws_down, ln3_g, ln3_b):
    depth = w_in.shape[0]
    alpha = (2 * depth) ** 0.25
    for l in range(depth):
        x = _layer(x, mem, positions, w_in[l], mu_shift[l], w_decay_up[l], w0[l], a_up[l], a0[l], g_up[l], k_k[l],
                   k_a[l], r_k[l], lnx_g[l], lnx_b[l], sinks[l], w_o[l], ln1_g[l], ln1_b[l], wm_q[l], wm_kv[l],
                   wm_o[l], ln2_g[l], ln2_b[l], w_router[l], router_bias[l], we_gate, we_up, we_down,
                   ws_gate[l], ws_up[l], ws_down[l], ln3_g[l], ln3_b[l], layer=l, alpha=alpha)
    return x
```

```python
import functools

import jax
import jax.numpy as jnp
from jax import lax
from jax.experimental import pallas as pl
from jax.experimental.pallas import tpu as pltpu
from jax.experimental.pallas import tpu_sc as plsc

f32 = jnp.float32
bf16 = jnp.bfloat16
i32 = jnp.int32

RWKV_HEADS = 8
HEAD_DIM = 64
RWKV_WIDTH = RWKV_HEADS * HEAD_DIM
DECAY_RANK = 64
AAA_RANK = 64
GATE_RANK = 128
RWKV_COLS = 3 * RWKV_WIDTH + DECAY_RANK + AAA_RANK + GATE_RANK
SWA_Q_HEADS = 8
SWA_KV_HEADS = 2
SWA_GROUP = SWA_Q_HEADS // SWA_KV_HEADS
SWA_WIDTH = SWA_Q_HEADS * HEAD_DIM
SWA_KV_WIDTH = SWA_KV_HEADS * HEAD_DIM
SWA_COLS = SWA_WIDTH + 2 * SWA_KV_WIDTH
WINDOW = 128
ROPE_THETA = 10000.0
MEM_HEADS = 4
N_EXPERTS = 256
TOP_K = 8
N_GROUPS = 8
GROUP_SIZE = N_EXPERTS // N_GROUPS
TOPK_GROUPS = 4
ROUTED_SCALE = 2.5
LN_EPS = 1e-5
GN_EPS = 64e-5
NEG_INF = -1e30

LANES = 128
SUBLANES = 8
WKV_CHUNK = 64
WKV_GROUP = 2
VMEM_LIMIT = 56 * 1024 * 1024

SC_CORES = 2
SC_SUBCORES = 16
SC_LANES = 16
SC_INDEX_GROUP = 128
SC_CHUNK_BYTES = 256 * 1024
MOE_PARTS = 2


def _sc_chunk_rows(nc, lanes, dtype):
    return min(SC_INDEX_GROUP, SC_CHUNK_BYTES // (nc * lanes * jnp.dtype(dtype).itemsize))


def _cparams(sem):
    return pltpu.CompilerParams(dimension_semantics=sem, vmem_limit_bytes=VMEM_LIMIT)


def _const_spec(shape):
    nd = len(shape)
    return pl.BlockSpec(shape, lambda *_: (0,) * nd)


def _dot(a, b):
    return jnp.dot(a, b, preferred_element_type=f32)


def _dot_nt(a, b):
    return lax.dot_general(a, b, (((1,), (1,)), ((), ())), preferred_element_type=f32)


def _dot_tn(a, b):
    return lax.dot_general(a, b, (((0,), (0,)), ((), ())), preferred_element_type=f32)


def _split2(x):
    hi = x.astype(bf16)
    lo = (x - hi.astype(f32)).astype(bf16)
    return hi, lo


def _seg_sums(xs, seg_b):
    parts = []
    for x in xs:
        parts.extend(_split2(x))
    out = _dot(jnp.concatenate(parts, axis=0), seg_b)
    t = xs[0].shape[0]
    return [out[2 * i * t:(2 * i + 1) * t] + out[(2 * i + 1) * t:(2 * i + 2) * t] for i in range(len(xs))]


def _dot_hp(a, b):
    ah, al = _split2(a)
    bh, bl = _split2(b)
    return _dot(ah, bh) + _dot(ah, bl) + _dot(al, bh)


def _dot_exact_lhs(m_bf16, x):
    hi, lo = _split2(x)
    return _dot(m_bf16, hi) + _dot(m_bf16, lo)


def _sigmoid(x):
    return 1.0 / (1.0 + jnp.exp(-x))


def _lane_chunk(n_rows, n_chunks, c):
    return (pl.ds(c, n_rows, stride=n_chunks), slice(None))


def _pack_bf16_pairs(x):
    chunks = []
    for j in range(x.shape[1] // (2 * LANES)):
        lo = x[:, 2 * j * LANES:(2 * j + 1) * LANES].astype(bf16).astype(f32)
        hi = x[:, (2 * j + 1) * LANES:(2 * j + 2) * LANES].astype(bf16).astype(f32)
        chunks.append(lax.bitcast_convert_type(hi, i32) | lax.shift_right_logical(lax.bitcast_convert_type(lo, i32), 16))
    return chunks


def _unpack_bf16_pairs(chunks):
    cols = []
    for w in chunks:
        cols.append(lax.bitcast_convert_type(lax.shift_left(w, 16), f32))
        cols.append(lax.bitcast_convert_type(w & jnp.int32(-65536), f32))
    return jnp.concatenate(cols, axis=1)


def _layer_norm(h, g, b):
    mu = jnp.mean(h, axis=-1, keepdims=True)
    d = h - mu
    var = jnp.mean(d * d, axis=-1, keepdims=True)
    return d * lax.rsqrt(var + LN_EPS) * g + b


def _proj_body(*refs, n_out):
    x_ref = refs[0]
    w_refs = refs[1:1 + n_out]
    o_refs = refs[1 + n_out:]
    xb = x_ref[...].astype(bf16)
    for w_ref, o_ref in zip(w_refs, o_refs):
        o_ref[...] = _dot(xb, w_ref[...]).astype(o_ref.dtype)


def _proj(x, ws, out_dtypes, tile):
    n, k = x.shape
    tile = min(tile, n)
    outs = pl.pallas_call(
        functools.partial(_proj_body, n_out=len(ws)),
        out_shape=[jax.ShapeDtypeStruct((n, w.shape[1]), dt) for w, dt in zip(ws, out_dtypes)],
        grid=(n // tile,),
        in_specs=[pl.BlockSpec((tile, k), lambda i: (i, 0))] + [_const_spec(w.shape) for w in ws],
        out_specs=[pl.BlockSpec((tile, w.shape[1]), lambda i: (i, 0)) for w in ws],
        compiler_params=_cparams(("parallel",)),
        name="proj",
    )(x, *ws)
    return outs


def _wkv_chunks(chains, states, masks):
    bd_b, bd, strict, incl, eye, eye_full = masks
    c, n = chains[0][1].shape
    nch = len(chains)

    def stack(x_b):
        return jnp.where(bd_b, jnp.concatenate([x_b] * WKV_GROUP, axis=0), jnp.zeros((), bf16))

    cast = [tuple(x.astype(bf16) for x in ch[1:6]) for ch in chains]
    v_s = [stack(cb[4]) for cb in cast]
    g = [_dot_nt(jnp.concatenate([cb[0], cb[3]], axis=0), jnp.concatenate([stack(cb[1]), stack(cb[2])], axis=0))
         for cb in cast]
    l_ak = [jnp.where(strict, gi[:c, n:], 0.0).astype(bf16) for gi in g]
    m_rb = [jnp.where(incl, gi[c:, :n], 0.0).astype(bf16) for gi in g]
    m_rk = [jnp.where(incl, gi[c:, n:], 0.0).astype(bf16) for gi in g]
    x = [jnp.where(strict, gi[:c, :n], 0.0) for gi in g]
    t = [eye + xi for xi in x]
    for _ in range(5):
        xb = [xi.astype(bf16) for xi in x]
        x = [_dot(xi, stack(xi)) for xi in xb]
        t = [ti + _dot(ti.astype(bf16), stack(xi.astype(bf16))) for ti, xi in zip(t, x)]
    lakv = [_dot(l_ak[i], v_s[i]).astype(bf16) for i in range(nch)]
    au = [_dot(t[i].astype(bf16), jnp.concatenate([stack(cast[i][0]), stack(lakv[i])], axis=1))
          for i in range(nch)]
    abar = [a[:, :n].astype(bf16) for a in au]
    ubar = [a[:, n:].astype(bf16) for a in au]
    ry = [_dot(m_rb[i], jnp.concatenate([stack(abar[i]), stack(ubar[i])], axis=1)) for i in range(nch)]
    r_bar = [(chains[i][4] + ry[i][:, :n]).astype(bf16) for i in range(nch)]
    y_bar = [ry[i][:, n:] + _dot(m_rk[i], v_s[i]) for i in range(nch)]
    p = [((eye_full + jnp.where(bd, _dot_tn(abar[i], cast[i][1]), 0.0)) * chains[i][6]).astype(bf16)
         for i in range(nch)]
    q = []
    for i in range(nch):
        q_bd = jnp.where(bd, _dot_tn(jnp.concatenate([ubar[i], cast[i][4]], axis=0),
                                     jnp.concatenate([cast[i][1], cast[i][2]], axis=0)), 0.0)
        qi = q_bd[0:c]
        for h in range(1, WKV_GROUP):
            qi = qi + q_bd[h * c:(h + 1) * c]
        q.append(qi * chains[i][6])
    states = list(states)
    ys = []
    for i in range(nch):
        gi = chains[i][0]
        s_b = states[gi].astype(bf16)
        ys.append(_dot_nt(r_bar[i], stack(s_b)) + y_bar[i])
        states[gi] = _dot(s_b, p[i]) + q[i]
    return ys, states


def _rwkv_body(u_ref, mu_ref, wdec_ref, w0_ref, aup_ref, a0_ref, gup_ref, kk_ref, ka_ref, rk_ref,
               lng_ref, lnb_ref, seg_ref, tri_ref, *rest, tt, n_side):
    side_in = rest[:n_side]
    y_ref = rest[n_side]
    side_out = rest[n_side + 1:2 * n_side + 1]
    state_ref, carry_ref = rest[2 * n_side + 1:]
    j = pl.program_id(1)
    for src, dst in zip(side_in, side_out):
        dst[...] = src[...].astype(dst.dtype)

    @pl.when(j == 0)
    def _():
        state_ref[...] = jnp.zeros_like(state_ref)
        carry_ref[...] = jnp.zeros_like(carry_ref)

    w = RWKV_WIDTH
    u = u_ref[...]
    row = lax.broadcasted_iota(i32, u.shape, 0)
    prev = jnp.where(row == 0, carry_ref[0:1, :], pltpu.roll(u, 1, axis=0))
    carry_ref[0:1, :] = u[tt - 1:tt, :]
    us = u + (prev - u) * mu_ref[...]
    r = us[:, 0:w]
    k = us[:, w:2 * w]
    v = us[:, 2 * w:3 * w]
    wa = us[:, 3 * w:3 * w + DECAY_RANK + AAA_RANK]
    gd = us[:, 3 * w + DECAY_RANK + AAA_RANK:]
    z = w0_ref[...] + _dot_hp(jnp.tanh(wa), wdec_ref[...])
    softplus_neg_z = jnp.maximum(-z, 0.0) + jnp.log(1.0 + jnp.exp(-jnp.abs(z)))
    lw = -jnp.exp(-softplus_neg_z - 0.5)
    a = _sigmoid(a0_ref[...] + _dot_hp(wa, aup_ref[...]))
    gate = _dot(_sigmoid(gd).astype(bf16), gup_ref[...].astype(bf16))
    seg = seg_ref[...]
    kk = k * kk_ref[...]
    kmod = k * (1.0 + (a - 1.0) * ka_ref[...])
    kk_sq, bonus_dot = _seg_sums([kk * kk, r * kmod * rk_ref[...]], seg)
    kk = kk / jnp.maximum(jnp.sqrt(kk_sq), 1e-12)
    cum = _dot_exact_lhs(tri_ref[...], lw)
    wc = jnp.exp(cum)
    iwc = jnp.exp(-cum)
    at = -kk * jnp.exp(cum - lw)
    bt = kk * a * iwc
    kt = kmod * iwc
    rt = r * wc

    n = WKV_GROUP * HEAD_DIM
    ri = lax.broadcasted_iota(i32, (n, n), 0)
    ci = lax.broadcasted_iota(i32, (n, n), 1)
    bd = (ri // WKV_CHUNK) == (ci // HEAD_DIM)
    bd_b = jnp.where(bd, 1.0, 0.0).astype(bf16) > 0
    eye_full = jnp.where(ri == ci, 1.0, 0.0).astype(f32)
    ti = lax.broadcasted_iota(i32, (WKV_CHUNK, n), 0)
    si = lax.broadcasted_iota(i32, (WKV_CHUNK, n), 1) % WKV_CHUNK
    masks = (bd_b, bd, ti > si, ti >= si, jnp.where(ti == si, 1.0, 0.0).astype(f32), eye_full)

    n_groups = w // n
    n_chunks = tt // WKV_CHUNK
    chains = []
    for c in range(n_chunks):
        rs = slice(c * WKV_CHUNK, (c + 1) * WKV_CHUNK)
        last = (c + 1) * WKV_CHUNK - 1
        for gi in range(n_groups):
            cs = slice(gi * n, (gi + 1) * n)
            chains.append((gi, at[rs, cs], bt[rs, cs], kt[rs, cs], rt[rs, cs], v[rs, cs], wc[last:last + 1, cs]))
    ys, states = _wkv_chunks(chains, [state_ref[gi] for gi in range(n_groups)], masks)
    for gi in range(n_groups):
        state_ref[gi] = states[gi]
    y = jnp.concatenate([jnp.concatenate(ys[c * n_groups:(c + 1) * n_groups], axis=1) for c in range(n_chunks)],
                        axis=0)

    inv_n = 1.0 / HEAD_DIM
    d = y - _seg_sums([y], seg)[0] * inv_n
    var = _seg_sums([d * d], seg)[0] * inv_n
    yn = d * lax.rsqrt(var + GN_EPS) * lng_ref[...] + lnb_ref[...]
    y_ref[...] = ((yn + bonus_dot * v) * gate).astype(y_ref.dtype)


def _rwkv(u_r, mu_shift, w_decay_up, w0, a_up, a0, g_up, k_k, k_a, r_k, lnx_g, lnx_b, tt, side=(), layer=0):
    b, s, cols = u_r.shape
    tt = min(tt, s)
    n_steps = b * (s // tt)
    assert all(a.shape[1] % n_steps == 0 for a in side)
    w = RWKV_WIDTH
    row = lambda p: p.reshape(1, -1).astype(f32)
    wdec = jnp.concatenate([w_decay_up, jnp.zeros((AAA_RANK, w), f32)], axis=0)
    aup = jnp.concatenate([jnp.zeros((DECAY_RANK, w), f32), a_up], axis=0)
    hid = jnp.arange(w) // HEAD_DIM
    seg = (hid[:, None] == hid[None, :]).astype(bf16)
    ti = jnp.arange(tt)
    tri = ((ti[:, None] // WKV_CHUNK == ti[None, :] // WKV_CHUNK) & (ti[:, None] >= ti[None, :])).astype(bf16)
    params = [row(mu_shift), wdec, row(w0), aup, row(a0), g_up, row(k_k), row(k_a), row(r_k), row(lnx_g),
              row(lnx_b), seg, tri]
    n = WKV_GROUP * HEAD_DIM
    nj = s // tt
    side_in = [pl.BlockSpec((None, a.shape[1] // n_steps) + a.shape[2:], lambda bi, j: (layer, bi * nj + j, 0, 0))
               for a in side]
    side_out = [pl.BlockSpec((a.shape[1] // n_steps,) + a.shape[2:], lambda bi, j: (bi * nj + j, 0, 0))
                for a in side]
    outs = pl.pallas_call(
        functools.partial(_rwkv_body, tt=tt, n_side=len(side)),
        out_shape=[jax.ShapeDtypeStruct((b, s, w), bf16)] + [jax.ShapeDtypeStruct(a.shape[1:], bf16) for a in side],
        grid=(b, nj),
        in_specs=([pl.BlockSpec((None, tt, cols), lambda bi, j: (bi, j, 0))] + [_const_spec(p.shape) for p in params]
                  + side_in),
        out_specs=[pl.BlockSpec((None, tt, w), lambda bi, j: (bi, j, 0))] + side_out,
        scratch_shapes=[pltpu.VMEM((w // n, HEAD_DIM, n), f32), pltpu.VMEM((SUBLANES, cols), f32)],
        compiler_params=_cparams(("parallel", "arbitrary")),
        name="rwkv7",
    )(u_r, *params, *side)
    return outs[0], outs[1:]


def _swa_body(sink_ref, u_ref, pos_ref, invf_ref, o_ref, kprev_ref, vprev_ref, *, nwin):
    j = pl.program_id(1)

    @pl.when(j == 0)
    def _():
        kprev_ref[...] = jnp.zeros_like(kprev_ref)
        vprev_ref[...] = jnp.zeros_like(vprev_ref)

    wq = SWA_WIDTH
    kvw = SWA_KV_WIDTH
    half = HEAD_DIM // 2
    gw = SWA_GROUP * HEAD_DIM
    rows = SWA_GROUP * WINDOW
    lane_kv = lax.broadcasted_iota(i32, (WINDOW, kvw), 1)
    ri = lax.broadcasted_iota(i32, (rows, WINDOW), 0)
    ci = lax.broadcasted_iota(i32, (rows, WINDOW), 1)
    t_idx = ri % WINDOW
    mask_cur = ci <= t_idx
    rb = lax.broadcasted_iota(i32, (rows, 1), 0) // WINDOW
    bd = (lax.broadcasted_iota(i32, (rows, gw), 0) // WINDOW) == (lax.broadcasted_iota(i32, (rows, gw), 1) // HEAD_DIM)

    def rope(x, c, s):
        n = x.shape[1]
        lane = lax.broadcasted_iota(i32, x.shape, 1)
        rot = jnp.where((lane % HEAD_DIM) < half, -pltpu.roll(x, n - half, axis=1), pltpu.roll(x, half, axis=1))
        return x * c + rot * s

    def rep(x, gi):
        sw = pltpu.roll(x, HEAD_DIM, axis=1)
        one = jnp.where((lane_kv // HEAD_DIM) == gi, x, sw)
        return jnp.concatenate([one] * (gw // kvw), axis=1).astype(bf16)

    k_prev = kprev_ref[...]
    v_prev = vprev_ref[...]
    for wi in range(nwin):
        rs = slice(wi * WINDOW, (wi + 1) * WINDOW)
        u = u_ref[rs, :]
        ang_t = invf_ref[...] * pos_ref[wi].astype(f32)
        cos = jnp.concatenate([jnp.cos(ang_t)] * (LANES // half), axis=0).T
        sin = jnp.concatenate([jnp.sin(ang_t)] * (LANES // half), axis=0).T
        q = rope(u[:, :wq] * (HEAD_DIM ** -0.5), jnp.concatenate([cos] * (wq // LANES), axis=1),
                 jnp.concatenate([sin] * (wq // LANES), axis=1))
        k_cur = rope(u[:, wq:wq + kvw], cos, sin)
        v_cur = u[:, wq + kvw:]
        mask_prev = (ci > t_idx + jnp.where(j > 0, 0, WINDOW)) if wi == 0 else (ci > t_idx)
        outs = []
        for gi in range(SWA_KV_HEADS):
            qg = q[:, gi * gw:(gi + 1) * gw]
            q_bd = jnp.where(bd, jnp.concatenate([qg] * SWA_GROUP, axis=0), 0.0).astype(bf16)
            s_prev = jnp.where(mask_prev, _dot_nt(q_bd, rep(k_prev, gi)), NEG_INF)
            s_cur = jnp.where(mask_cur, _dot_nt(q_bd, rep(k_cur, gi)), NEG_INF)
            sink = jnp.zeros((rows, 1), f32)
            for h in range(SWA_GROUP):
                sink = jnp.where(rb == h, sink_ref[gi * SWA_GROUP + h], sink)
            m = jnp.maximum(jnp.maximum(jnp.max(s_prev, axis=-1, keepdims=True),
                                        jnp.max(s_cur, axis=-1, keepdims=True)), sink)
            p_prev = jnp.exp(s_prev - m)
            p_cur = jnp.exp(s_cur - m)
            denom = (jnp.sum(p_prev, axis=-1, keepdims=True) + jnp.sum(p_cur, axis=-1, keepdims=True)
                     + jnp.exp(sink - m))
            o_bd = _dot(p_prev.astype(bf16), rep(v_prev, gi)) + _dot(p_cur.astype(bf16), rep(v_cur, gi))
            o_bd = jnp.where(bd, o_bd * (1.0 / denom), 0.0)
            og = o_bd[0:WINDOW]
            for h in range(1, SWA_GROUP):
                og = og + o_bd[h * WINDOW:(h + 1) * WINDOW]
            outs.append(og)
        o_ref[rs, :] = jnp.concatenate(outs, axis=1).astype(o_ref.dtype)
        k_prev, v_prev = k_cur, v_cur
    kprev_ref[...] = k_prev
    vprev_ref[...] = v_prev


def _swa(u_s, positions, sinks, nwin=2):
    b, s, cols = u_s.shape
    half = HEAD_DIM // 2
    nwin = nwin if s % (nwin * WINDOW) == 0 else 1
    tile = nwin * WINDOW
    inv_freq = (ROPE_THETA ** (-jnp.arange(0, HEAD_DIM, 2, dtype=f32) / HEAD_DIM)).reshape(half, 1)
    pos = positions.reshape(b, s // WINDOW, 1, WINDOW).astype(i32)
    return pl.pallas_call(
        functools.partial(_swa_body, nwin=nwin),
        out_shape=jax.ShapeDtypeStruct((b, s, SWA_WIDTH), bf16),
        grid=(b, s // tile),
        in_specs=[pl.BlockSpec(memory_space=pltpu.SMEM),
                  pl.BlockSpec((None, tile, cols), lambda bi, j: (bi, j, 0)),
                  pl.BlockSpec((None, nwin, 1, WINDOW), lambda bi, j: (bi, j, 0, 0)),
                  _const_spec((half, 1))],
        out_specs=pl.BlockSpec((None, tile, SWA_WIDTH), lambda bi, j: (bi, j, 0)),
        scratch_shapes=[pltpu.VMEM((WINDOW, SWA_KV_WIDTH), f32), pltpu.VMEM((WINDOW, SWA_KV_WIDTH), f32)],
        compiler_params=_cparams(("parallel", "arbitrary")),
        name="swa",
    )(sinks.astype(f32), u_s, pos, inv_freq)


def _mix_xattn_body(ya_ref, yb_ref, wa_ref, wb_ref, xin_ref, g1_ref, b1_ref, kv_ref, wq_ref, wo_ref, g_ref, b_ref,
                    o_ref, o3_ref, *, alpha):
    mix = _dot(ya_ref[...], wa_ref[...]) + _dot(yb_ref[...], wb_ref[...])
    x = _layer_norm(alpha * xin_ref[...] + mix, g1_ref[...], b1_ref[...])
    d = x.shape[1]
    hd = d // MEM_HEADS
    q = _dot(x.astype(bf16), wq_ref[...]) * (hd ** -0.5)
    kv = kv_ref[...]
    outs = []
    for h in range(MEM_HEADS):
        qh = q[:, h * hd:(h + 1) * hd].astype(bf16)
        kh = kv[:, h * hd:(h + 1) * hd]
        vh = kv[:, d + h * hd:d + (h + 1) * hd]
        s = _dot_nt(qh, kh)
        p = jnp.exp(s - jnp.max(s, axis=-1, keepdims=True))
        l = jnp.sum(p, axis=-1, keepdims=True)
        outs.append(_dot(p.astype(bf16), vh) / l)
    o = jnp.concatenate(outs, axis=1)
    xa = _dot(o.astype(bf16), wo_ref[...])
    y = _layer_norm(alpha * x + xa, g_ref[...], b_ref[...])
    o_ref[...] = y
    words = _pack_bf16_pairs(y)
    for j, w in enumerate(words):
        o3_ref[_lane_chunk(x.shape[0], len(words), j)] = w


def _mix_xattn(ya, yb, w_o, x, g1, b1, kv, wm_q, wm_o, g2, b2, alpha, tile):
    bsz, s, d = x.shape
    m = kv.shape[1]
    tile = min(tile, s)
    nj = s // tile
    wa = w_o[:ya.shape[2]].astype(bf16)
    wb = w_o[ya.shape[2]:].astype(bf16)
    wq = wm_q.astype(bf16)
    wo = wm_o.astype(bf16)
    rows = lambda width: pl.BlockSpec((None, tile, width), lambda bi, j: (bi, j, 0))
    vec = _const_spec((1, d))
    return pl.pallas_call(
        functools.partial(_mix_xattn_body, alpha=alpha),
        out_shape=[jax.ShapeDtypeStruct((bsz, s, d), f32),
                   jax.ShapeDtypeStruct((bsz * s * (d // (2 * LANES)), LANES), i32)],
        grid=(bsz, nj),
        in_specs=[rows(ya.shape[2]), rows(yb.shape[2]), _const_spec(wa.shape), _const_spec(wb.shape), rows(d), vec, vec,
                  pl.BlockSpec((None, m, 2 * d), lambda bi, j: (bi, 0, 0)),
                  _const_spec(wq.shape), _const_spec(wo.shape), vec, vec],
        out_specs=[rows(d), pl.BlockSpec((tile * (d // (2 * LANES)), LANES), lambda bi, j: (bi * nj + j, 0))],
        compiler_params=_cparams(("parallel", "parallel")),
        name="mix_xattn",
    )(ya, yb, wa, wb, x, g1.reshape(1, d), b1.reshape(1, d), kv, wq, wo, g2.reshape(1, d), b2.reshape(1, d))


def _router_body(x_ref, wt_ref, bias_ref, upper_ref, e_ref, g_ref, r_ref, cnt_out_ref, cnt_ref, *, t):
    @pl.when(pl.program_id(0) == 0)
    def _():
        cnt_ref[...] = jnp.zeros_like(cnt_ref)

    xh, xl = _split2(x_ref[...])
    wh, wl = _split2(wt_ref[...])
    logits = _dot_nt(wh, xh) + _dot_nt(wh, xl) + _dot_nt(wl, xh)
    scores = _sigmoid(logits)
    biased = scores + bias_ref[...][:, 0:1]
    ne = N_EXPERTS
    neg = -jnp.inf

    def top1(vals):
        rows = lax.broadcasted_iota(i32, vals.shape, 0).astype(f32)
        m = jnp.max(vals, axis=0, keepdims=True)
        idx = jnp.min(jnp.where(vals == m, rows, float(vals.shape[0])), axis=0, keepdims=True)
        return m, idx, rows == idx

    gscores = []
    for gi in range(N_GROUPS):
        blk = biased[gi * GROUP_SIZE:(gi + 1) * GROUP_SIZE, :]
        m1, _, hit = top1(blk)
        m2 = jnp.max(jnp.where(hit, neg, blk), axis=0, keepdims=True)
        gscores.append(m1 + m2)
    gs = jnp.concatenate(gscores, axis=0)
    gsel = jnp.zeros(gs.shape, f32)
    for _ in range(TOPK_GROUPS):
        _, _, hit = top1(gs)
        gsel = jnp.where(hit, 1.0, gsel)
        gs = jnp.where(hit, neg, gs)
    emask = jnp.concatenate(
        [jnp.broadcast_to(gsel[gi:gi + 1, :], (GROUP_SIZE, t)) for gi in range(N_GROUPS)], axis=0) > 0.5
    cand = jnp.where(emask, biased, NEG_INF)
    idxs, sels = [], []
    chosen = jnp.zeros((ne, t), f32)
    for _ in range(TOP_K):
        _, idx, hit = top1(cand)
        idxs.append(idx)
        sels.append(jnp.sum(jnp.where(hit, scores, 0.0), axis=0, keepdims=True))
        chosen = chosen + jnp.where(hit, 1.0, 0.0)
        cand = jnp.where(hit, neg, cand)
    sel = jnp.concatenate(sels, axis=0)
    g_ref[...] = sel / jnp.sum(sel, axis=0, keepdims=True) * ROUTED_SCALE
    e_ref[...] = jnp.concatenate(idxs, axis=0).astype(i32)
    before = _dot(chosen.astype(bf16), upper_ref[...]) + cnt_ref[...][:, 0:1]
    rows = lax.broadcasted_iota(i32, (ne, t), 0).astype(f32)
    ranks = [jnp.sum(jnp.where(rows == idx, before, 0.0), axis=0, keepdims=True) for idx in idxs]
    r_ref[...] = jnp.concatenate(ranks, axis=0).astype(i32)
    cnt_ref[...] = cnt_ref[...] + jnp.sum(chosen, axis=1, keepdims=True)
    cnt_out_ref[...] = cnt_ref[...].astype(i32)


def _router(x2, w_router, router_bias, tile, row0=0, n=None):
    d = x2.shape[1]
    n = x2.shape[0] if n is None else n
    ne = N_EXPERTS
    t = min(tile, n)
    assert row0 % t == 0 and n % t == 0
    first = row0 // t
    wt = w_router.T
    bias = jnp.broadcast_to(router_bias.reshape(ne, 1).astype(f32), (ne, LANES))
    ti = jnp.arange(t)
    upper = (ti[:, None] < ti[None, :]).astype(bf16)
    cols = pl.BlockSpec((TOP_K, t), lambda i: (0, i))
    e_t, g_t, r_t, cnt = pl.pallas_call(
        functools.partial(_router_body, t=t),
        out_shape=[jax.ShapeDtypeStruct((TOP_K, n), i32), jax.ShapeDtypeStruct((TOP_K, n), f32),
                   jax.ShapeDtypeStruct((TOP_K, n), i32), jax.ShapeDtypeStruct((ne, LANES), i32)],
        grid=(n // t,),
        in_specs=[pl.BlockSpec((t, d), lambda i: (i + first, 0)), _const_spec((ne, d)), _const_spec((ne, LANES)),
                  _const_spec((t, t))],
        out_specs=[cols, cols, cols, _const_spec((ne, LANES))],
        scratch_shapes=[pltpu.VMEM((ne, LANES), f32)],
        compiler_params=_cparams(("arbitrary",)),
        name="router",
    )(x2, wt, bias, upper)
    return e_t, g_t, r_t, cnt[:, 0]


def _dispatch_sc(x3d, e_t, r_t, g_t, starts, tok0=0):
    _, nc, lanes = x3d.shape
    n = e_t.shape[1]
    workers = SC_CORES * SC_SUBCORES
    per_w = n // workers
    tc = _sc_chunk_rows(nc, lanes, x3d.dtype)
    assert n % (workers * SC_INDEX_GROUP) == 0 and SC_INDEX_GROUP % tc == 0
    mesh = plsc.VectorSubcoreMesh(core_axis_name="c", subcore_axis_name="s")

    @functools.partial(
        pl.kernel, mesh=mesh,
        out_type=[jax.ShapeDtypeStruct((n * TOP_K, nc, lanes), x3d.dtype),
                  jax.ShapeDtypeStruct((n * TOP_K,), f32)],
        scratch_types=[
            pltpu.VMEM((tc, nc, lanes), x3d.dtype),
            pltpu.VMEM((TOP_K, SC_INDEX_GROUP), i32),
            pltpu.VMEM((TOP_K, SC_INDEX_GROUP), i32),
            pltpu.VMEM((TOP_K, SC_INDEX_GROUP), f32),
            pltpu.VMEM((TOP_K, tc), i32),
            pltpu.VMEM((TOP_K, tc), f32),
            pltpu.VMEM((N_EXPERTS,), i32),
            pltpu.SemaphoreType.DMA,
        ],
        compiler_params=pltpu.CompilerParams(use_tc_tiling_on_sc=True, needs_layout_passes=False),
    )
    def dispatch(x_hbm, e_hbm, r_hbm, g_hbm, st_hbm, o_hbm, gs_hbm, rows_v, e_v, r_v, g_v, slot_v, gate_v, st_v, sem):
        wid = lax.axis_index("s") * SC_CORES + lax.axis_index("c")
        pltpu.sync_copy(st_hbm, st_v)

        @pl.loop(0, per_w // SC_INDEX_GROUP)
        def _(gi):
            base = wid * per_w + gi * SC_INDEX_GROUP
            pltpu.sync_copy(e_hbm.at[:, pl.ds(base, SC_INDEX_GROUP)], e_v)
            pltpu.sync_copy(r_hbm.at[:, pl.ds(base, SC_INDEX_GROUP)], r_v)
            pltpu.sync_copy(g_hbm.at[:, pl.ds(base, SC_INDEX_GROUP)], g_v)
            for h in range(SC_INDEX_GROUP // tc):
                off = h * tc
                pltpu.sync_copy(x_hbm.at[pl.ds(tok0 + base + off, tc)], rows_v)
                for kk in range(TOP_K):
                    for j in range(tc // SC_LANES):
                        src = pl.ds(off + j * SC_LANES, SC_LANES)
                        dst = pl.ds(j * SC_LANES, SC_LANES)
                        slot_v[kk, dst] = r_v[kk, src] + plsc.load_gather(st_v, [e_v[kk, src]])
                        gate_v[kk, dst] = g_v[kk, src]
                copies = [pltpu.async_copy(rows_v, o_hbm.at[slot_v.at[kk]], sem) for kk in range(TOP_K)]
                copies += [pltpu.async_copy(gate_v.at[kk], gs_hbm.at[slot_v.at[kk]], sem) for kk in range(TOP_K)]
                for cp in copies:
                    cp.wait()

    return dispatch(x3d, e_t, r_t, g_t, starts)


def _gmm_body(gid_ref, tid_ref, lo_ref, hi_ref, first_ref, newg_ref, nextg_ref, ord_ref, x_ref, g_ref,
              wg_hbm, wu_hbm, wd_hbm, o_ref, wg_b, wu_b, wd_b, sems, *, tm):
    v = pl.program_id(0)
    lo = lo_ref[v]
    hi = hi_ref[v]
    row0 = tid_ref[v] * tm
    nc = wg_b.shape[1] // LANES
    full = (lo <= row0) & (hi >= row0 + tm)
    slot = ord_ref[v] % 2

    def weight_copies(expert, dst_slot):
        return [pltpu.make_async_copy(src.at[expert], dst.at[dst_slot], sems.at[dst_slot])
                for src, dst in ((wg_hbm, wg_b), (wu_hbm, wu_b), (wd_hbm, wd_b))]

    @pl.when(newg_ref[v] == 1)
    def _():
        @pl.when(v == 0)
        def _():
            for cp in weight_copies(gid_ref[v], slot):
                cp.start()

        for cp in weight_copies(gid_ref[v], slot):
            cp.wait()

        @pl.when(nextg_ref[v] >= 0)
        def _():
            for cp in weight_copies(nextg_ref[v], 1 - slot):
                cp.start()

    def ffn():
        nw = nc // 2
        x = _unpack_bf16_pairs([x_ref[_lane_chunk(tm, nw, j)] for j in range(nw)]).astype(bf16)
        hg = _dot(x, wg_b[slot])
        h = hg * _sigmoid(hg) * _dot(x, wu_b[slot])
        g = g_ref[...]
        g_cols = jnp.concatenate([g, jnp.zeros((SUBLANES - g.shape[0], LANES), f32)], axis=0).T
        h = jnp.concatenate([h[r * LANES:(r + 1) * LANES, :] * g_cols[:, r:r + 1] for r in range(tm // LANES)],
                            axis=0)
        return _dot(h.astype(bf16), wd_b[slot])

    @pl.when(full)
    def _():
        y = ffn()
        for c in range(nc):
            o_ref[_lane_chunk(tm, nc, c)] = y[:, c * LANES:(c + 1) * LANES]

    @pl.when(jnp.logical_not(full) & (hi > lo))
    def _():
        @pl.when(first_ref[v] == 1)
        def _():
            o_ref[...] = jnp.zeros_like(o_ref)

        rows = row0 + lax.broadcasted_iota(i32, (tm, 1), 0)
        mask = (rows >= lo) & (rows < hi)
        y = ffn()
        for c in range(nc):
            idx = _lane_chunk(tm, nc, c)
            o_ref[idx] = jnp.where(mask, y[:, c * LANES:(c + 1) * LANES], o_ref[idx])


def _gmm(xs, gates, we_gate, we_up, we_down, counts, tm):
    ne, d, de = we_gate.shape
    nc = d // LANES
    nw = nc // 2
    nk = xs.shape[0] // nw
    tm = min(tm, nk)
    assert tm % LANES == 0 and tm // LANES <= SUBLANES
    n_tiles = nk // tm
    gates3 = gates.reshape(n_tiles, tm // LANES, LANES)
    n_visits = n_tiles + ne - 1
    ends = jnp.cumsum(counts)
    starts = ends - counts
    tile_lo = starts // tm
    n_touch = jnp.where(counts > 0, (ends - 1) // tm - tile_lo + 1, 0)
    vis_end = jnp.cumsum(n_touch)
    vis_start = vis_end - n_touch
    vi = jnp.arange(n_visits, dtype=i32)
    valid = vi < vis_end[-1]
    gid = jnp.minimum(jnp.sum((vis_end[None, :] <= vi[:, None]).astype(i32), axis=1), ne - 1)
    gid = jnp.where(valid, gid, jnp.max(jnp.where(valid, gid, 0)))
    onehot = gid[:, None] == jnp.arange(ne, dtype=i32)[None, :]
    pick = lambda table: jnp.sum(jnp.where(onehot, table[None, :], 0), axis=1)
    tid = jnp.where(valid, pick(tile_lo) + vi - pick(vis_start), n_tiles - 1).astype(i32)
    lo = jnp.where(valid, pick(starts), 0).astype(i32)
    hi = jnp.where(valid, pick(ends), 0).astype(i32)
    one = jnp.ones((1,), i32)
    first = jnp.concatenate([one, (tid[1:] != tid[:-1]).astype(i32)])
    newg = jnp.concatenate([one, (gid[1:] != gid[:-1]).astype(i32)])
    later = gid[None, :] > gid[:, None]
    nextg = jnp.min(jnp.where(later, gid[None, :], ne), axis=1)
    nextg = jnp.where(nextg < ne, nextg, -1).astype(i32)
    order = (jnp.cumsum(newg) - 1).astype(i32)
    rows = lambda chunks: pl.BlockSpec((tm * chunks, LANES), lambda v, g, t, *_: (t[v], 0))
    hbm = pl.BlockSpec(memory_space=pl.ANY)
    return pl.pallas_call(
        functools.partial(_gmm_body, tm=tm),
        out_shape=jax.ShapeDtypeStruct((nk * nc, LANES), f32),
        grid_spec=pltpu.PrefetchScalarGridSpec(
            num_scalar_prefetch=8, grid=(n_visits,),
            in_specs=[rows(nw), pl.BlockSpec((None, tm // LANES, LANES), lambda v, g, t, *_: (t[v], 0, 0)),
                      hbm, hbm, hbm],
            out_specs=rows(nc),
            scratch_shapes=[pltpu.VMEM((2, d, de), bf16), pltpu.VMEM((2, d, de), bf16), pltpu.VMEM((2, de, d), bf16),
                            pltpu.SemaphoreType.DMA((2,))]),
        compiler_params=_cparams(("arbitrary",)),
        name="moe_experts",
    )(gid, tid, lo, hi, first, newg, nextg, order, xs, gates3, we_gate, we_up, we_down)


def _combine_sc(ys3d, e_t, r_t, starts, n):
    _, nc, lanes = ys3d.shape
    workers = SC_CORES * SC_SUBCORES
    per_w = n // workers
    tc = _sc_chunk_rows(nc, lanes, f32)
    assert n % (workers * SC_INDEX_GROUP) == 0 and SC_INDEX_GROUP % tc == 0
    mesh = plsc.VectorSubcoreMesh(core_axis_name="c", subcore_axis_name="s")

    @functools.partial(
        pl.kernel, mesh=mesh,
        out_type=jax.ShapeDtypeStruct((n, nc, lanes), f32),
        scratch_types=[
            pltpu.VMEM((tc, nc, lanes), f32),
            pltpu.VMEM((TOP_K, SC_INDEX_GROUP), i32),
            pltpu.VMEM((TOP_K, SC_INDEX_GROUP), i32),
            pltpu.VMEM((TOP_K, tc), i32),
            pltpu.VMEM((N_EXPERTS,), i32),
            pltpu.SemaphoreType.DMA,
        ],
        compiler_params=pltpu.CompilerParams(use_tc_tiling_on_sc=True, needs_layout_passes=False),
    )
    def combine(y_hbm, e_hbm, r_hbm, st_hbm, o_hbm, acc_v, e_v, r_v, slot_v, st_v, sem):
        wid = lax.axis_index("s") * SC_CORES + lax.axis_index("c")
        pltpu.sync_copy(st_hbm, st_v)

        @pl.loop(0, per_w // SC_INDEX_GROUP)
        def _(gi):
            base = wid * per_w + gi * SC_INDEX_GROUP
            pltpu.sync_copy(e_hbm.at[:, pl.ds(base, SC_INDEX_GROUP)], e_v)
            pltpu.sync_copy(r_hbm.at[:, pl.ds(base, SC_INDEX_GROUP)], r_v)
            for h in range(SC_INDEX_GROUP // tc):
                off = h * tc
                for kk in range(TOP_K):
                    for j in range(tc // SC_LANES):
                        src = pl.ds(off + j * SC_LANES, SC_LANES)
                        slot_v[kk, pl.ds(j * SC_LANES, SC_LANES)] = (
                            r_v[kk, src] + plsc.load_gather(st_v, [e_v[kk, src]]))
                pltpu.async_copy(y_hbm.at[slot_v.at[0]], acc_v, sem).wait()
                copies = [pltpu.async_copy(y_hbm.at[slot_v.at[kk]], acc_v, sem, add=True) for kk in range(1, TOP_K)]
                for cp in copies:
                    cp.wait()
                pltpu.sync_copy(acc_v, o_hbm.at[pl.ds(base + off, tc)])

    return combine(ys3d, e_t, r_t, starts)


def _ffn_out_body(x_ref, r_ref, wg_ref, wu_ref, wd_ref, g_ref, b_ref, *rest, alpha):
    o_ref = rest[-1]
    x = x_ref[...]
    xb = x.astype(bf16)
    hg = _dot(xb, wg_ref[...])
    h = hg * _sigmoid(hg) * _dot(xb, wu_ref[...])
    shared = _dot(h.astype(bf16), wd_ref[...])
    nc = x.shape[1] // LANES
    routed = jnp.concatenate([r_ref[_lane_chunk(x.shape[0], nc, c)] for c in range(nc)], axis=1)
    o_ref[...] = _layer_norm(alpha * x + routed + shared, g_ref[...], b_ref[...])


def _ffn_out(x2, routed_rows, ws_gate, ws_up, ws_down, g, b, alpha, tile, row0=0, prev=None):
    n_all, d = x2.shape
    n = routed_rows.shape[0] // (d // LANES)
    tile = min(tile, n)
    assert row0 % tile == 0 and n % tile == 0
    first = row0 // tile
    wg, wu, wd = ws_gate.astype(bf16), ws_up.astype(bf16), ws_down.astype(bf16)
    rows = pl.BlockSpec((tile, d), lambda i: (i + first, 0))
    in_specs = [rows, pl.BlockSpec((tile * (d // LANES), LANES), lambda i: (i, 0)), _const_spec(wg.shape),
                _const_spec(wu.shape), _const_spec(wd.shape), _const_spec((1, d)), _const_spec((1, d))]
    args = [x2, routed_rows, wg, wu, wd, g.reshape(1, d), b.reshape(1, d)]
    aliases = {}
    if prev is not None:
        in_specs.append(pl.BlockSpec(memory_space=pl.ANY))
        args.append(prev)
        aliases = {len(args) - 1: 0}
    return pl.pallas_call(
        functools.partial(_ffn_out_body, alpha=alpha),
        out_shape=jax.ShapeDtypeStruct((n_all, d), f32),
        grid=(n // tile,),
        in_specs=in_specs,
        out_specs=rows,
        input_output_aliases=aliases,
        compiler_params=_cparams(("parallel",)),
        name="ffn_out_ln3",
    )(*args)


def _layer(x, mem, positions, w_in, mu_shift, w_decay_up, w0, a_up, a0, g_up, k_k, k_a, r_k, lnx_g, lnx_b, sinks,
           w_o, ln1_g, ln1_b, wm_q, wm_kv, wm_o, ln2_g, ln2_b, w_router, router_bias, we_gate, we_up, we_down,
           ws_gate, ws_up, ws_down, ln3_g, ln3_b, *, layer, alpha):
    b, s, d = x.shape
    n = b * s
    xf = x.reshape(n, d)
    w_in_b = w_in.astype(bf16)
    u_r, u_s = _proj(xf, [w_in_b[:, :RWKV_COLS], w_in_b[:, RWKV_COLS:]], [f32, f32], tile=512)
    rwkv_tt = 256
    experts = (we_gate, we_up, we_down)
    if N_EXPERTS % (b * (s // min(rwkv_tt, s))) == 0:
        y_r, experts_b = _rwkv(u_r.reshape(b, s, RWKV_COLS), mu_shift, w_decay_up, w0, a_up, a0, g_up, k_k, k_a, r_k,
                               lnx_g, lnx_b, tt=rwkv_tt, side=experts, layer=layer)
    else:
        y_r, _ = _rwkv(u_r.reshape(b, s, RWKV_COLS), mu_shift, w_decay_up, w0, a_up, a0, g_up, k_k, k_a, r_k,
                       lnx_g, lnx_b, tt=rwkv_tt)
        experts_b = [w[layer].astype(bf16) for w in experts]
    y_s = _swa(u_s.reshape(b, s, SWA_COLS), positions, sinks)
    m = mem.shape[1]
    (kv,) = _proj(mem.reshape(b * m, d), [wm_kv.astype(bf16)], [bf16], tile=512)
    x2, x2_rows = _mix_xattn(y_r, y_s, w_o, x, ln1_g, ln1_b, kv.reshape(b, m, 2 * d), wm_q, wm_o, ln2_g, ln2_b,
                             alpha, tile=512)
    x2 = x2.reshape(n, d)
    nc = d // LANES
    nw = nc // 2
    parts = MOE_PARTS if n % (MOE_PARTS * SC_CORES * SC_SUBCORES * SC_INDEX_GROUP) == 0 else 1
    npart = n // parts
    x3 = None
    for part in range(parts):
        row0 = part * npart
        e_t, g_t, r_t, counts = _router(x2, w_router, router_bias, tile=512, row0=row0, n=npart)
        starts = (jnp.cumsum(counts) - counts).astype(i32)
        xs, gs = _dispatch_sc(x2_rows.reshape(n, nw, LANES), e_t, r_t, g_t, starts, tok0=row0)
        ys = _gmm(xs.reshape(npart * TOP_K * nw, LANES), gs, *experts_b, counts, tm=512)
        routed = _combine_sc(ys.reshape(npart * TOP_K, nc, LANES), e_t, r_t, starts, npart)
        x3 = _ffn_out(x2, routed.reshape(npart * nc, LANES), ws_gate, ws_up, ws_down, ln3_g, ln3_b, alpha, tile=512,
                      row0=row0, prev=x3)
    return x3.reshape(b, s, d)


def kernel(x, mem, positions, w_in, mu_shift, w_decay_up, w0, a_up, a0, g_up, k_k, k_a, r_k, lnx_g, lnx_b, sinks, w_o, ln1_g, ln1_b, wm_q, wm_kv, wm_o, ln2_g, ln2_b, w_router, router_bias, we_gate, we_up, we_down, ws_gate, ws_up, ws_down, ln3_g, ln3_b):
    depth = w_in.shape[0]
    alpha = (2 * depth) ** 0.25
    for l in range(depth):
        x = _layer(x, mem, positions, w_in[l], mu_shift[l], w_decay_up[l], w0[l], a_up[l], a0[l], g_up[l], k_k[l],
                   k_a[l], r_k[l], lnx_g[l], lnx_b[l], sinks[l], w_o[l], ln1_g[l], ln1_b[l], wm_q[l], wm_kv[l],
                   wm_o[l], ln2_g[l], ln2_b[l], w_router[l], router_bias[l], we_gate, we_up, we_down,
                   ws_gate[l], ws_up[l], ws_down[l], ln3_g[l], ln3_b[l], layer=l, alpha=alpha)
    return x
```

```python
import functools

import jax
import jax.numpy as jnp
from jax import lax
from jax.experimental import pallas as pl
from jax.experimental.pallas import tpu as pltpu
from jax.experimental.pallas import tpu_sc as plsc

f32 = jnp.float32
bf16 = jnp.bfloat16
i32 = jnp.int32

RWKV_HEADS = 8
HEAD_DIM = 64
RWKV_WIDTH = RWKV_HEADS * HEAD_DIM
DECAY_RANK = 64
AAA_RANK = 64
GATE_RANK = 128
RWKV_COLS = 3 * RWKV_WIDTH + DECAY_RANK + AAA_RANK + GATE_RANK
SWA_Q_HEADS = 8
SWA_KV_HEADS = 2
SWA_GROUP = SWA_Q_HEADS // SWA_KV_HEADS
SWA_WIDTH = SWA_Q_HEADS * HEAD_DIM
SWA_KV_WIDTH = SWA_KV_HEADS * HEAD_DIM
SWA_COLS = SWA_WIDTH + 2 * SWA_KV_WIDTH
WINDOW = 128
ROPE_THETA = 10000.0
MEM_HEADS = 4
N_EXPERTS = 256
TOP_K = 8
N_GROUPS = 8
GROUP_SIZE = N_EXPERTS // N_GROUPS
TOPK_GROUPS = 4
ROUTED_SCALE = 2.5
LN_EPS = 1e-5
GN_EPS = 64e-5
NEG_INF = -1e30

LANES = 128
SUBLANES = 8
WKV_CHUNK = 64
WKV_GROUP = 2
VMEM_LIMIT = 56 * 1024 * 1024

SC_CORES = 2
SC_SUBCORES = 16
SC_LANES = 16
SC_INDEX_GROUP = 128
SC_CHUNK_BYTES = 256 * 1024
MOE_PARTS = 1


def _sc_chunk_rows(nc, lanes, dtype):
    return min(SC_INDEX_GROUP, SC_CHUNK_BYTES // (nc * lanes * jnp.dtype(dtype).itemsize))


def _cparams(sem):
    return pltpu.CompilerParams(dimension_semantics=sem, vmem_limit_bytes=VMEM_LIMIT)


def _const_spec(shape):
    nd = len(shape)
    return pl.BlockSpec(shape, lambda *_: (0,) * nd)


def _dot(a, b):
    return jnp.dot(a, b, preferred_element_type=f32)


def _dot_nt(a, b):
    return lax.dot_general(a, b, (((1,), (1,)), ((), ())), preferred_element_type=f32)


def _dot_tn(a, b):
    return lax.dot_general(a, b, (((0,), (0,)), ((), ())), preferred_element_type=f32)


def _split2(x):
    hi = x.astype(bf16)
    lo = (x - hi.astype(f32)).astype(bf16)
    return hi, lo


def _seg_sums(xs, seg_b):
    parts = []
    for x in xs:
        parts.extend(_split2(x))
    out = _dot(jnp.concatenate(parts, axis=0), seg_b)
    t = xs[0].shape[0]
    return [out[2 * i * t:(2 * i + 1) * t] + out[(2 * i + 1) * t:(2 * i + 2) * t] for i in range(len(xs))]


def _dot_hp(a, b):
    ah, al = _split2(a)
    bh, bl = _split2(b)
    return _dot(ah, bh) + _dot(ah, bl) + _dot(al, bh)


def _dot_exact_lhs(m_bf16, x):
    hi, lo = _split2(x)
    return _dot(m_bf16, hi) + _dot(m_bf16, lo)


def _sigmoid(x):
    return 1.0 / (1.0 + jnp.exp(-x))


def _lane_chunk(n_rows, n_chunks, c):
    return (pl.ds(c, n_rows, stride=n_chunks), slice(None))


def _pack_bf16_pairs(x):
    chunks = []
    for j in range(x.shape[1] // (2 * LANES)):
        lo = x[:, 2 * j * LANES:(2 * j + 1) * LANES].astype(bf16).astype(f32)
        hi = x[:, (2 * j + 1) * LANES:(2 * j + 2) * LANES].astype(bf16).astype(f32)
        chunks.append(lax.bitcast_convert_type(hi, i32) | lax.shift_right_logical(lax.bitcast_convert_type(lo, i32), 16))
    return chunks


def _unpack_bf16_pairs(chunks):
    cols = []
    for w in chunks:
        cols.append(lax.bitcast_convert_type(lax.shift_left(w, 16), f32))
        cols.append(lax.bitcast_convert_type(w & jnp.int32(-65536), f32))
    return jnp.concatenate(cols, axis=1)


def _layer_norm(h, g, b):
    mu = jnp.mean(h, axis=-1, keepdims=True)
    d = h - mu
    var = jnp.mean(d * d, axis=-1, keepdims=True)
    return d * lax.rsqrt(var + LN_EPS) * g + b


def _proj_body(*refs, n_out):
    x_ref = refs[0]
    w_refs = refs[1:1 + n_out]
    o_refs = refs[1 + n_out:]
    xb = x_ref[...].astype(bf16)
    for w_ref, o_ref in zip(w_refs, o_refs):
        o_ref[...] = _dot(xb, w_ref[...]).astype(o_ref.dtype)


def _proj(x, ws, out_dtypes, tile):
    n, k = x.shape
    tile = min(tile, n)
    outs = pl.pallas_call(
        functools.partial(_proj_body, n_out=len(ws)),
        out_shape=[jax.ShapeDtypeStruct((n, w.shape[1]), dt) for w, dt in zip(ws, out_dtypes)],
        grid=(n // tile,),
        in_specs=[pl.BlockSpec((tile, k), lambda i: (i, 0))] + [_const_spec(w.shape) for w in ws],
        out_specs=[pl.BlockSpec((tile, w.shape[1]), lambda i: (i, 0)) for w in ws],
        compiler_params=_cparams(("parallel",)),
        name="proj",
    )(x, *ws)
    return outs


def _wkv_chunks(chains, states, masks):
    bd_b, bd, strict, incl, eye, eye_full = masks
    c, n = chains[0][1].shape
    nch = len(chains)

    def stack(x_b):
        return jnp.where(bd_b, jnp.concatenate([x_b] * WKV_GROUP, axis=0), jnp.zeros((), bf16))

    cast = [tuple(x.astype(bf16) for x in ch[1:6]) for ch in chains]
    v_s = [stack(cb[4]) for cb in cast]
    g = [_dot_nt(jnp.concatenate([cb[0], cb[3]], axis=0), jnp.concatenate([stack(cb[1]), stack(cb[2])], axis=0))
         for cb in cast]
    l_ak = [jnp.where(strict, gi[:c, n:], 0.0).astype(bf16) for gi in g]
    m_rb = [jnp.where(incl, gi[c:, :n], 0.0).astype(bf16) for gi in g]
    m_rk = [jnp.where(incl, gi[c:, n:], 0.0).astype(bf16) for gi in g]
    x = [jnp.where(strict, gi[:c, :n], 0.0) for gi in g]
    t = [eye + xi for xi in x]
    for _ in range(5):
        xb = [xi.astype(bf16) for xi in x]
        x = [_dot(xi, stack(xi)) for xi in xb]
        t = [ti + _dot(ti.astype(bf16), stack(xi.astype(bf16))) for ti, xi in zip(t, x)]
    lakv = [_dot(l_ak[i], v_s[i]).astype(bf16) for i in range(nch)]
    au = [_dot(t[i].astype(bf16), jnp.concatenate([stack(cast[i][0]), stack(lakv[i])], axis=1))
          for i in range(nch)]
    abar = [a[:, :n].astype(bf16) for a in au]
    ubar = [a[:, n:].astype(bf16) for a in au]
    ry = [_dot(m_rb[i], jnp.concatenate([stack(abar[i]), stack(ubar[i])], axis=1)) for i in range(nch)]
    r_bar = [(chains[i][4] + ry[i][:, :n]).astype(bf16) for i in range(nch)]
    y_bar = [ry[i][:, n:] + _dot(m_rk[i], v_s[i]) for i in range(nch)]
    p = [((eye_full + jnp.where(bd, _dot_tn(abar[i], cast[i][1]), 0.0)) * chains[i][6]).astype(bf16)
         for i in range(nch)]
    q = []
    for i in range(nch):
        q_bd = jnp.where(bd, _dot_tn(jnp.concatenate([ubar[i], cast[i][4]], axis=0),
                                     jnp.concatenate([cast[i][1], cast[i][2]], axis=0)), 0.0)
        qi = q_bd[0:c]
        for h in range(1, WKV_GROUP):
            qi = qi + q_bd[h * c:(h + 1) * c]
        q.append(qi * chains[i][6])
    states = list(states)
    ys = []
    for i in range(nch):
        gi = chains[i][0]
        s_b = states[gi].astype(bf16)
        ys.append(_dot_nt(r_bar[i], stack(s_b)) + y_bar[i])
        states[gi] = _dot(s_b, p[i]) + q[i]
    return ys, states


def _rwkv_body(u_ref, mu_ref, wdec_ref, w0_ref, aup_ref, a0_ref, gup_ref, kk_ref, ka_ref, rk_ref,
               lng_ref, lnb_ref, seg_ref, tri_ref, *rest, tt, n_side):
    side_in = rest[:n_side]
    y_ref = rest[n_side]
    side_out = rest[n_side + 1:2 * n_side + 1]
    state_ref, carry_ref = rest[2 * n_side + 1:]
    j = pl.program_id(1)
    for src, dst in zip(side_in, side_out):
        dst[...] = src[...].astype(dst.dtype)

    @pl.when(j == 0)
    def _():
        state_ref[...] = jnp.zeros_like(state_ref)
        carry_ref[...] = jnp.zeros_like(carry_ref)

    w = RWKV_WIDTH
    u = u_ref[...]
    row = lax.broadcasted_iota(i32, u.shape, 0)
    prev = jnp.where(row == 0, carry_ref[0:1, :], pltpu.roll(u, 1, axis=0))
    carry_ref[0:1, :] = u[tt - 1:tt, :]
    us = u + (prev - u) * mu_ref[...]
    r = us[:, 0:w]
    k = us[:, w:2 * w]
    v = us[:, 2 * w:3 * w]
    wa = us[:, 3 * w:3 * w + DECAY_RANK + AAA_RANK]
    gd = us[:, 3 * w + DECAY_RANK + AAA_RANK:]
    z = w0_ref[...] + _dot_hp(jnp.tanh(wa), wdec_ref[...])
    softplus_neg_z = jnp.maximum(-z, 0.0) + jnp.log(1.0 + jnp.exp(-jnp.abs(z)))
    lw = -jnp.exp(-softplus_neg_z - 0.5)
    a = _sigmoid(a0_ref[...] + _dot_hp(wa, aup_ref[...]))
    gate = _dot(_sigmoid(gd).astype(bf16), gup_ref[...].astype(bf16))
    seg = seg_ref[...]
    kk = k * kk_ref[...]
    kmod = k * (1.0 + (a - 1.0) * ka_ref[...])
    kk_sq, bonus_dot = _seg_sums([kk * kk, r * kmod * rk_ref[...]], seg)
    kk = kk / jnp.maximum(jnp.sqrt(kk_sq), 1e-12)
    cum = _dot_exact_lhs(tri_ref[...], lw)
    wc = jnp.exp(cum)
    iwc = jnp.exp(-cum)
    at = -kk * jnp.exp(cum - lw)
    bt = kk * a * iwc
    kt = kmod * iwc
    rt = r * wc

    n = WKV_GROUP * HEAD_DIM
    ri = lax.broadcasted_iota(i32, (n, n), 0)
    ci = lax.broadcasted_iota(i32, (n, n), 1)
    bd = (ri // WKV_CHUNK) == (ci // HEAD_DIM)
    bd_b = jnp.where(bd, 1.0, 0.0).astype(bf16) > 0
    eye_full = jnp.where(ri == ci, 1.0, 0.0).astype(f32)
    ti = lax.broadcasted_iota(i32, (WKV_CHUNK, n), 0)
    si = lax.broadcasted_iota(i32, (WKV_CHUNK, n), 1) % WKV_CHUNK
    masks = (bd_b, bd, ti > si, ti >= si, jnp.where(ti == si, 1.0, 0.0).astype(f32), eye_full)

    n_groups = w // n
    n_chunks = tt // WKV_CHUNK
    chains = []
    for c in range(n_chunks):
        rs = slice(c * WKV_CHUNK, (c + 1) * WKV_CHUNK)
        last = (c + 1) * WKV_CHUNK - 1
        for gi in range(n_groups):
            cs = slice(gi * n, (gi + 1) * n)
            chains.append((gi, at[rs, cs], bt[rs, cs], kt[rs, cs], rt[rs, cs], v[rs, cs], wc[last:last + 1, cs]))
    ys, states = _wkv_chunks(chains, [state_ref[gi] for gi in range(n_groups)], masks)
    for gi in range(n_groups):
        state_ref[gi] = states[gi]
    y = jnp.concatenate([jnp.concatenate(ys[c * n_groups:(c + 1) * n_groups], axis=1) for c in range(n_chunks)],
                        axis=0)

    inv_n = 1.0 / HEAD_DIM
    d = y - _seg_sums([y], seg)[0] * inv_n
    var = _seg_sums([d * d], seg)[0] * inv_n
    yn = d * lax.rsqrt(var + GN_EPS) * lng_ref[...] + lnb_ref[...]
    y_ref[...] = ((yn + bonus_dot * v) * gate).astype(y_ref.dtype)


def _rwkv(u_r, mu_shift, w_decay_up, w0, a_up, a0, g_up, k_k, k_a, r_k, lnx_g, lnx_b, tt, side=(), layer=0):
    b, s, cols = u_r.shape
    tt = min(tt, s)
    n_steps = b * (s // tt)
    assert all(a.shape[1] % n_steps == 0 for a in side)
    w = RWKV_WIDTH
    row = lambda p: p.reshape(1, -1).astype(f32)
    wdec = jnp.concatenate([w_decay_up, jnp.zeros((AAA_RANK, w), f32)], axis=0)
    aup = jnp.concatenate([jnp.zeros((DECAY_RANK, w), f32), a_up], axis=0)
    hid = jnp.arange(w) // HEAD_DIM
    seg = (hid[:, None] == hid[None, :]).astype(bf16)
    ti = jnp.arange(tt)
    tri = ((ti[:, None] // WKV_CHUNK == ti[None, :] // WKV_CHUNK) & (ti[:, None] >= ti[None, :])).astype(bf16)
    params = [row(mu_shift), wdec, row(w0), aup, row(a0), g_up, row(k_k), row(k_a), row(r_k), row(lnx_g),
              row(lnx_b), seg, tri]
    n = WKV_GROUP * HEAD_DIM
    nj = s // tt
    side_in = [pl.BlockSpec((None, a.shape[1] // n_steps) + a.shape[2:], lambda bi, j: (layer, bi * nj + j, 0, 0))
               for a in side]
    side_out = [pl.BlockSpec((a.shape[1] // n_steps,) + a.shape[2:], lambda bi, j: (bi * nj + j, 0, 0))
                for a in side]
    outs = pl.pallas_call(
        functools.partial(_rwkv_body, tt=tt, n_side=len(side)),
        out_shape=[jax.ShapeDtypeStruct((b, s, w), bf16)] + [jax.ShapeDtypeStruct(a.shape[1:], bf16) for a in side],
        grid=(b, nj),
        in_specs=([pl.BlockSpec((None, tt, cols), lambda bi, j: (bi, j, 0))] + [_const_spec(p.shape) for p in params]
                  + side_in),
        out_specs=[pl.BlockSpec((None, tt, w), lambda bi, j: (bi, j, 0))] + side_out,
        scratch_shapes=[pltpu.VMEM((w // n, HEAD_DIM, n), f32), pltpu.VMEM((SUBLANES, cols), f32)],
        compiler_params=_cparams(("parallel", "arbitrary")),
        name="rwkv7",
    )(u_r, *params, *side)
    return outs[0], outs[1:]


def _swa_body(sink_ref, u_ref, pos_ref, invf_ref, o_ref, kprev_ref, vprev_ref, *, nwin):
    j = pl.program_id(1)

    @pl.when(j == 0)
    def _():
        kprev_ref[...] = jnp.zeros_like(kprev_ref)
        vprev_ref[...] = jnp.zeros_like(vprev_ref)

    wq = SWA_WIDTH
    kvw = SWA_KV_WIDTH
    half = HEAD_DIM // 2
    gw = SWA_GROUP * HEAD_DIM
    rows = SWA_GROUP * WINDOW
    lane_kv = lax.broadcasted_iota(i32, (WINDOW, kvw), 1)
    ri = lax.broadcasted_iota(i32, (rows, WINDOW), 0)
    ci = lax.broadcasted_iota(i32, (rows, WINDOW), 1)
    t_idx = ri % WINDOW
    mask_cur = ci <= t_idx
    rb = lax.broadcasted_iota(i32, (rows, 1), 0) // WINDOW
    bd = (lax.broadcasted_iota(i32, (rows, gw), 0) // WINDOW) == (lax.broadcasted_iota(i32, (rows, gw), 1) // HEAD_DIM)

    def rope(x, c, s):
        n = x.shape[1]
        lane = lax.broadcasted_iota(i32, x.shape, 1)
        rot = jnp.where((lane % HEAD_DIM) < half, -pltpu.roll(x, n - half, axis=1), pltpu.roll(x, half, axis=1))
        return x * c + rot * s

    def rep(x, gi):
        sw = pltpu.roll(x, HEAD_DIM, axis=1)
        one = jnp.where((lane_kv // HEAD_DIM) == gi, x, sw)
        return jnp.concatenate([one] * (gw // kvw), axis=1).astype(bf16)

    k_prev = kprev_ref[...]
    v_prev = vprev_ref[...]
    for wi in range(nwin):
        rs = slice(wi * WINDOW, (wi + 1) * WINDOW)
        u = u_ref[rs, :]
        ang_t = invf_ref[...] * pos_ref[wi].astype(f32)
        cos = jnp.concatenate([jnp.cos(ang_t)] * (LANES // half), axis=0).T
        sin = jnp.concatenate([jnp.sin(ang_t)] * (LANES // half), axis=0).T
        q = rope(u[:, :wq] * (HEAD_DIM ** -0.5), jnp.concatenate([cos] * (wq // LANES), axis=1),
                 jnp.concatenate([sin] * (wq // LANES), axis=1))
        k_cur = rope(u[:, wq:wq + kvw], cos, sin)
        v_cur = u[:, wq + kvw:]
        mask_prev = (ci > t_idx + jnp.where(j > 0, 0, WINDOW)) if wi == 0 else (ci > t_idx)
        outs = []
        for gi in range(SWA_KV_HEADS):
            qg = q[:, gi * gw:(gi + 1) * gw]
            q_bd = jnp.where(bd, jnp.concatenate([qg] * SWA_GROUP, axis=0), 0.0).astype(bf16)
            s_prev = jnp.where(mask_prev, _dot_nt(q_bd, rep(k_prev, gi)), NEG_INF)
            s_cur = jnp.where(mask_cur, _dot_nt(q_bd, rep(k_cur, gi)), NEG_INF)
            sink = jnp.zeros((rows, 1), f32)
            for h in range(SWA_GROUP):
                sink = jnp.where(rb == h, sink_ref[gi * SWA_GROUP + h], sink)
            m = jnp.maximum(jnp.maximum(jnp.max(s_prev, axis=-1, keepdims=True),
                                        jnp.max(s_cur, axis=-1, keepdims=True)), sink)
            p_prev = jnp.exp(s_prev - m)
            p_cur = jnp.exp(s_cur - m)
            denom = (jnp.sum(p_prev, axis=-1, keepdims=True) + jnp.sum(p_cur, axis=-1, keepdims=True)
                     + jnp.exp(sink - m))
            o_bd = _dot(p_prev.astype(bf16), rep(v_prev, gi)) + _dot(p_cur.astype(bf16), rep(v_cur, gi))
            o_bd = jnp.where(bd, o_bd * (1.0 / denom), 0.0)
            og = o_bd[0:WINDOW]
            for h in range(1, SWA_GROUP):
                og = og + o_bd[h * WINDOW:(h + 1) * WINDOW]
            outs.append(og)
        o_ref[rs, :] = jnp.concatenate(outs, axis=1).astype(o_ref.dtype)
        k_prev, v_prev = k_cur, v_cur
    kprev_ref[...] = k_prev
    vprev_ref[...] = v_prev


def _swa(u_s, positions, sinks, nwin=2):
    b, s, cols = u_s.shape
    half = HEAD_DIM // 2
    nwin = nwin if s % (nwin * WINDOW) == 0 else 1
    tile = nwin * WINDOW
    inv_freq = (ROPE_THETA ** (-jnp.arange(0, HEAD_DIM, 2, dtype=f32) / HEAD_DIM)).reshape(half, 1)
    pos = positions.reshape(b, s // WINDOW, 1, WINDOW).astype(i32)
    return pl.pallas_call(
        functools.partial(_swa_body, nwin=nwin),
        out_shape=jax.ShapeDtypeStruct((b, s, SWA_WIDTH), bf16),
        grid=(b, s // tile),
        in_specs=[pl.BlockSpec(memory_space=pltpu.SMEM),
                  pl.BlockSpec((None, tile, cols), lambda bi, j: (bi, j, 0)),
                  pl.BlockSpec((None, nwin, 1, WINDOW), lambda bi, j: (bi, j, 0, 0)),
                  _const_spec((half, 1))],
        out_specs=pl.BlockSpec((None, tile, SWA_WIDTH), lambda bi, j: (bi, j, 0)),
        scratch_shapes=[pltpu.VMEM((WINDOW, SWA_KV_WIDTH), f32), pltpu.VMEM((WINDOW, SWA_KV_WIDTH), f32)],
        compiler_params=_cparams(("parallel", "arbitrary")),
        name="swa",
    )(sinks.astype(f32), u_s, pos, inv_freq)


def _mix_xattn_body(ya_ref, yb_ref, wa_ref, wb_ref, xin_ref, g1_ref, b1_ref, kv_ref, wq_ref, wo_ref, g_ref, b_ref,
                    o_ref, o3_ref, *, alpha):
    mix = _dot(ya_ref[...], wa_ref[...]) + _dot(yb_ref[...], wb_ref[...])
    x = _layer_norm(alpha * xin_ref[...] + mix, g1_ref[...], b1_ref[...])
    d = x.shape[1]
    hd = d // MEM_HEADS
    q = _dot(x.astype(bf16), wq_ref[...]) * (hd ** -0.5)
    kv = kv_ref[...]
    outs = []
    for h in range(MEM_HEADS):
        qh = q[:, h * hd:(h + 1) * hd].astype(bf16)
        kh = kv[:, h * hd:(h + 1) * hd]
        vh = kv[:, d + h * hd:d + (h + 1) * hd]
        s = _dot_nt(qh, kh)
        p = jnp.exp(s - jnp.max(s, axis=-1, keepdims=True))
        l = jnp.sum(p, axis=-1, keepdims=True)
        outs.append(_dot(p.astype(bf16), vh) / l)
    o = jnp.concatenate(outs, axis=1)
    xa = _dot(o.astype(bf16), wo_ref[...])
    y = _layer_norm(alpha * x + xa, g_ref[...], b_ref[...])
    o_ref[...] = y
    words = _pack_bf16_pairs(y)
    for j, w in enumerate(words):
        o3_ref[_lane_chunk(x.shape[0], len(words), j)] = w


def _mix_xattn(ya, yb, w_o, x, g1, b1, kv, wm_q, wm_o, g2, b2, alpha, tile):
    bsz, s, d = x.shape
    m = kv.shape[1]
    tile = min(tile, s)
    nj = s // tile
    wa = w_o[:ya.shape[2]].astype(bf16)
    wb = w_o[ya.shape[2]:].astype(bf16)
    wq = wm_q.astype(bf16)
    wo = wm_o.astype(bf16)
    rows = lambda width: pl.BlockSpec((None, tile, width), lambda bi, j: (bi, j, 0))
    vec = _const_spec((1, d))
    return pl.pallas_call(
        functools.partial(_mix_xattn_body, alpha=alpha),
        out_shape=[jax.ShapeDtypeStruct((bsz, s, d), f32),
                   jax.ShapeDtypeStruct((bsz * s * (d // (2 * LANES)), LANES), i32)],
        grid=(bsz, nj),
        in_specs=[rows(ya.shape[2]), rows(yb.shape[2]), _const_spec(wa.shape), _const_spec(wb.shape), rows(d), vec, vec,
                  pl.BlockSpec((None, m, 2 * d), lambda bi, j: (bi, 0, 0)),
                  _const_spec(wq.shape), _const_spec(wo.shape), vec, vec],
        out_specs=[rows(d), pl.BlockSpec((tile * (d // (2 * LANES)), LANES), lambda bi, j: (bi * nj + j, 0))],
        compiler_params=_cparams(("parallel", "parallel")),
        name="mix_xattn",
    )(ya, yb, wa, wb, x, g1.reshape(1, d), b1.reshape(1, d), kv, wq, wo, g2.reshape(1, d), b2.reshape(1, d))


def _router_body(x_ref, wt_ref, bias_ref, upper_ref, e_ref, g_ref, r_ref, cnt_out_ref, cnt_ref, *, t):
    @pl.when(pl.program_id(0) == 0)
    def _():
        cnt_ref[...] = jnp.zeros_like(cnt_ref)

    xh, xl = _split2(x_ref[...])
    wh, wl = _split2(wt_ref[...])
    logits = _dot_nt(wh, xh) + _dot_nt(wh, xl) + _dot_nt(wl, xh)
    scores = _sigmoid(logits)
    biased = scores + bias_ref[...][:, 0:1]
    ne = N_EXPERTS
    neg = -jnp.inf

    def top1(vals):
        rows = lax.broadcasted_iota(i32, vals.shape, 0).astype(f32)
        m = jnp.max(vals, axis=0, keepdims=True)
        idx = jnp.min(jnp.where(vals == m, rows, float(vals.shape[0])), axis=0, keepdims=True)
        return m, idx, rows == idx

    gscores = []
    for gi in range(N_GROUPS):
        blk = biased[gi * GROUP_SIZE:(gi + 1) * GROUP_SIZE, :]
        m1, _, hit = top1(blk)
        m2 = jnp.max(jnp.where(hit, neg, blk), axis=0, keepdims=True)
        gscores.append(m1 + m2)
    gs = jnp.concatenate(gscores, axis=0)
    gsel = jnp.zeros(gs.shape, f32)
    for _ in range(TOPK_GROUPS):
        _, _, hit = top1(gs)
        gsel = jnp.where(hit, 1.0, gsel)
        gs = jnp.where(hit, neg, gs)
    emask = jnp.concatenate(
        [jnp.broadcast_to(gsel[gi:gi + 1, :], (GROUP_SIZE, t)) for gi in range(N_GROUPS)], axis=0) > 0.5
    cand = jnp.where(emask, biased, NEG_INF)
    idxs, sels = [], []
    chosen = jnp.zeros((ne, t), f32)
    for _ in range(TOP_K):
        _, idx, hit = top1(cand)
        idxs.append(idx)
        sels.append(jnp.sum(jnp.where(hit, scores, 0.0), axis=0, keepdims=True))
        chosen = chosen + jnp.where(hit, 1.0, 0.0)
        cand = jnp.where(hit, neg, cand)
    sel = jnp.concatenate(sels, axis=0)
    g_ref[...] = sel / jnp.sum(sel, axis=0, keepdims=True) * ROUTED_SCALE
    e_ref[...] = jnp.concatenate(idxs, axis=0).astype(i32)
    before = _dot(chosen.astype(bf16), upper_ref[...]) + cnt_ref[...][:, 0:1]
    rows = lax.broadcasted_iota(i32, (ne, t), 0).astype(f32)
    ranks = [jnp.sum(jnp.where(rows == idx, before, 0.0), axis=0, keepdims=True) for idx in idxs]
    r_ref[...] = jnp.concatenate(ranks, axis=0).astype(i32)
    cnt_ref[...] = cnt_ref[...] + jnp.sum(chosen, axis=1, keepdims=True)
    cnt_out_ref[...] = cnt_ref[...].astype(i32)


def _router(x2, w_router, router_bias, tile, row0=0, n=None):
    d = x2.shape[1]
    n = x2.shape[0] if n is None else n
    ne = N_EXPERTS
    t = min(tile, n)
    assert row0 % t == 0 and n % t == 0
    first = row0 // t
    wt = w_router.T
    bias = jnp.broadcast_to(router_bias.reshape(ne, 1).astype(f32), (ne, LANES))
    ti = jnp.arange(t)
    upper = (ti[:, None] < ti[None, :]).astype(bf16)
    cols = pl.BlockSpec((TOP_K, t), lambda i: (0, i))
    e_t, g_t, r_t, cnt = pl.pallas_call(
        functools.partial(_router_body, t=t),
        out_shape=[jax.ShapeDtypeStruct((TOP_K, n), i32), jax.ShapeDtypeStruct((TOP_K, n), f32),
                   jax.ShapeDtypeStruct((TOP_K, n), i32), jax.ShapeDtypeStruct((ne, LANES), i32)],
        grid=(n // t,),
        in_specs=[pl.BlockSpec((t, d), lambda i: (i + first, 0)), _const_spec((ne, d)), _const_spec((ne, LANES)),
                  _const_spec((t, t))],
        out_specs=[cols, cols, cols, _const_spec((ne, LANES))],
        scratch_shapes=[pltpu.VMEM((ne, LANES), f32)],
        compiler_params=_cparams(("arbitrary",)),
        name="router",
    )(x2, wt, bias, upper)
    return e_t, g_t, r_t, cnt[:, 0]


def _dispatch_sc(x3d, e_t, r_t, g_t, starts, tok0=0):
    _, nc, lanes = x3d.shape
    n = e_t.shape[1]
    workers = SC_CORES * SC_SUBCORES
    per_w = n // workers
    tc = _sc_chunk_rows(nc, lanes, x3d.dtype)
    assert n % (workers * SC_INDEX_GROUP) == 0 and SC_INDEX_GROUP % tc == 0
    mesh = plsc.VectorSubcoreMesh(core_axis_name="c", subcore_axis_name="s")

    @functools.partial(
        pl.kernel, mesh=mesh,
        out_type=[jax.ShapeDtypeStruct((n * TOP_K, nc, lanes), x3d.dtype),
                  jax.ShapeDtypeStruct((n * TOP_K,), f32)],
        scratch_types=[
            pltpu.VMEM((tc, nc, lanes), x3d.dtype),
            pltpu.VMEM((TOP_K, SC_INDEX_GROUP), i32),
            pltpu.VMEM((TOP_K, SC_INDEX_GROUP), i32),
            pltpu.VMEM((TOP_K, SC_INDEX_GROUP), f32),
            pltpu.VMEM((TOP_K, tc), i32),
            pltpu.VMEM((TOP_K, tc), f32),
            pltpu.VMEM((N_EXPERTS,), i32),
            pltpu.SemaphoreType.DMA,
        ],
        compiler_params=pltpu.CompilerParams(use_tc_tiling_on_sc=True, needs_layout_passes=False),
    )
    def dispatch(x_hbm, e_hbm, r_hbm, g_hbm, st_hbm, o_hbm, gs_hbm, rows_v, e_v, r_v, g_v, slot_v, gate_v, st_v, sem):
        wid = lax.axis_index("s") * SC_CORES + lax.axis_index("c")
        pltpu.sync_copy(st_hbm, st_v)

        @pl.loop(0, per_w // SC_INDEX_GROUP)
        def _(gi):
            base = wid * per_w + gi * SC_INDEX_GROUP
            pltpu.sync_copy(e_hbm.at[:, pl.ds(base, SC_INDEX_GROUP)], e_v)
            pltpu.sync_copy(r_hbm.at[:, pl.ds(base, SC_INDEX_GROUP)], r_v)
            pltpu.sync_copy(g_hbm.at[:, pl.ds(base, SC_INDEX_GROUP)], g_v)
            for h in range(SC_INDEX_GROUP // tc):
                off = h * tc
                pltpu.sync_copy(x_hbm.at[pl.ds(tok0 + base + off, tc)], rows_v)
                for kk in range(TOP_K):
                    for j in range(tc // SC_LANES):
                        src = pl.ds(off + j * SC_LANES, SC_LANES)
                        dst = pl.ds(j * SC_LANES, SC_LANES)
                        slot_v[kk, dst] = r_v[kk, src] + plsc.load_gather(st_v, [e_v[kk, src]])
                        gate_v[kk, dst] = g_v[kk, src]
                copies = [pltpu.async_copy(rows_v, o_hbm.at[slot_v.at[kk]], sem) for kk in range(TOP_K)]
                copies += [pltpu.async_copy(gate_v.at[kk], gs_hbm.at[slot_v.at[kk]], sem) for kk in range(TOP_K)]
                for cp in copies:
                    cp.wait()

    return dispatch(x3d, e_t, r_t, g_t, starts)


def _gmm_body(gid_ref, tid_ref, lo_ref, hi_ref, first_ref, newg_ref, nextg_ref, ord_ref, x_ref, g_ref,
              wg_hbm, wu_hbm, wd_hbm, o_ref, wg_b, wu_b, wd_b, sems, *, tm):
    v = pl.program_id(0)
    lo = lo_ref[v]
    hi = hi_ref[v]
    row0 = tid_ref[v] * tm
    nc = wg_b.shape[1] // LANES
    full = (lo <= row0) & (hi >= row0 + tm)
    slot = ord_ref[v] % 2

    def weight_copies(expert, dst_slot):
        return [pltpu.make_async_copy(src.at[expert], dst.at[dst_slot], sems.at[dst_slot])
                for src, dst in ((wg_hbm, wg_b), (wu_hbm, wu_b), (wd_hbm, wd_b))]

    @pl.when(newg_ref[v] == 1)
    def _():
        @pl.when(v == 0)
        def _():
            for cp in weight_copies(gid_ref[v], slot):
                cp.start()

        for cp in weight_copies(gid_ref[v], slot):
            cp.wait()

        @pl.when(nextg_ref[v] >= 0)
        def _():
            for cp in weight_copies(nextg_ref[v], 1 - slot):
                cp.start()

    def ffn():
        nw = nc // 2
        x = _unpack_bf16_pairs([x_ref[_lane_chunk(tm, nw, j)] for j in range(nw)]).astype(bf16)
        hg = _dot(x, wg_b[slot])
        h = hg * _sigmoid(hg) * _dot(x, wu_b[slot])
        g = g_ref[...]
        g_cols = jnp.concatenate([g, jnp.zeros((SUBLANES - g.shape[0], LANES), f32)], axis=0).T
        h = jnp.concatenate([h[r * LANES:(r + 1) * LANES, :] * g_cols[:, r:r + 1] for r in range(tm // LANES)],
                            axis=0)
        return _dot(h.astype(bf16), wd_b[slot])

    @pl.when(full)
    def _():
        y = ffn()
        for c in range(nc):
            o_ref[_lane_chunk(tm, nc, c)] = y[:, c * LANES:(c + 1) * LANES]

    @pl.when(jnp.logical_not(full) & (hi > lo))
    def _():
        @pl.when(first_ref[v] == 1)
        def _():
            o_ref[...] = jnp.zeros_like(o_ref)

        rows = row0 + lax.broadcasted_iota(i32, (tm, 1), 0)
        mask = (rows >= lo) & (rows < hi)
        y = ffn()
        for c in range(nc):
            idx = _lane_chunk(tm, nc, c)
            o_ref[idx] = jnp.where(mask, y[:, c * LANES:(c + 1) * LANES], o_ref[idx])


def _gmm(xs, gates, we_gate, we_up, we_down, counts, tm):
    ne, d, de = we_gate.shape
    nc = d // LANES
    nw = nc // 2
    nk = xs.shape[0] // nw
    tm = min(tm, nk)
    assert tm % LANES == 0 and tm // LANES <= SUBLANES
    n_tiles = nk // tm
    gates3 = gates.reshape(n_tiles, tm // LANES, LANES)
    n_visits = n_tiles + ne - 1
    ends = jnp.cumsum(counts)
    starts = ends - counts
    tile_lo = starts // tm
    n_touch = jnp.where(counts > 0, (ends - 1) // tm - tile_lo + 1, 0)
    vis_end = jnp.cumsum(n_touch)
    vis_start = vis_end - n_touch
    vi = jnp.arange(n_visits, dtype=i32)
    valid = vi < vis_end[-1]
    gid = jnp.minimum(jnp.sum((vis_end[None, :] <= vi[:, None]).astype(i32), axis=1), ne - 1)
    gid = jnp.where(valid, gid, jnp.max(jnp.where(valid, gid, 0)))
    onehot = gid[:, None] == jnp.arange(ne, dtype=i32)[None, :]
    pick = lambda table: jnp.sum(jnp.where(onehot, table[None, :], 0), axis=1)
    tid = jnp.where(valid, pick(tile_lo) + vi - pick(vis_start), n_tiles - 1).astype(i32)
    lo = jnp.where(valid, pick(starts), 0).astype(i32)
    hi = jnp.where(valid, pick(ends), 0).astype(i32)
    one = jnp.ones((1,), i32)
    first = jnp.concatenate([one, (tid[1:] != tid[:-1]).astype(i32)])
    newg = jnp.concatenate([one, (gid[1:] != gid[:-1]).astype(i32)])
    later = gid[None, :] > gid[:, None]
    nextg = jnp.min(jnp.where(later, gid[None, :], ne), axis=1)
    nextg = jnp.where(nextg < ne, nextg, -1).astype(i32)
    order = (jnp.cumsum(newg) - 1).astype(i32)
    rows = lambda chunks: pl.BlockSpec((tm * chunks, LANES), lambda v, g, t, *_: (t[v], 0))
    hbm = pl.BlockSpec(memory_space=pl.ANY)
    return pl.pallas_call(
        functools.partial(_gmm_body, tm=tm),
        out_shape=jax.ShapeDtypeStruct((nk * nc, LANES), f32),
        grid_spec=pltpu.PrefetchScalarGridSpec(
            num_scalar_prefetch=8, grid=(n_visits,),
            in_specs=[rows(nw), pl.BlockSpec((None, tm // LANES, LANES), lambda v, g, t, *_: (t[v], 0, 0)),
                      hbm, hbm, hbm],
            out_specs=rows(nc),
            scratch_shapes=[pltpu.VMEM((2, d, de), bf16), pltpu.VMEM((2, d, de), bf16), pltpu.VMEM((2, de, d), bf16),
                            pltpu.SemaphoreType.DMA((2,))]),
        compiler_params=_cparams(("arbitrary",)),
        name="moe_experts",
    )(gid, tid, lo, hi, first, newg, nextg, order, xs, gates3, we_gate, we_up, we_down)


def _combine_sc(ys3d, e_t, r_t, starts, n):
    _, nc, lanes = ys3d.shape
    workers = SC_CORES * SC_SUBCORES
    per_w = n // workers
    tc = _sc_chunk_rows(nc, lanes, f32)
    assert n % (workers * SC_INDEX_GROUP) == 0 and SC_INDEX_GROUP % tc == 0
    mesh = plsc.VectorSubcoreMesh(core_axis_name="c", subcore_axis_name="s")

    @functools.partial(
        pl.kernel, mesh=mesh,
        out_type=jax.ShapeDtypeStruct((n, nc, lanes), f32),
        scratch_types=[
            pltpu.VMEM((tc, nc, lanes), f32),
            pltpu.VMEM((TOP_K, SC_INDEX_GROUP), i32),
            pltpu.VMEM((TOP_K, SC_INDEX_GROUP), i32),
            pltpu.VMEM((TOP_K, tc), i32),
            pltpu.VMEM((N_EXPERTS,), i32),
            pltpu.SemaphoreType.DMA,
        ],
        compiler_params=pltpu.CompilerParams(use_tc_tiling_on_sc=True, needs_layout_passes=False),
    )
    def combine(y_hbm, e_hbm, r_hbm, st_hbm, o_hbm, acc_v, e_v, r_v, slot_v, st_v, sem):
        wid = lax.axis_index("s") * SC_CORES + lax.axis_index("c")
        pltpu.sync_copy(st_hbm, st_v)

        @pl.loop(0, per_w // SC_INDEX_GROUP)
        def _(gi):
            base = wid * per_w + gi * SC_INDEX_GROUP
            pltpu.sync_copy(e_hbm.at[:, pl.ds(base, SC_INDEX_GROUP)], e_v)
            pltpu.sync_copy(r_hbm.at[:, pl.ds(base, SC_INDEX_GROUP)], r_v)
            for h in range(SC_INDEX_GROUP // tc):
                off = h * tc
                for kk in range(TOP_K):
                    for j in range(tc // SC_LANES):
                        src = pl.ds(off + j * SC_LANES, SC_LANES)
                        slot_v[kk, pl.ds(j * SC_LANES, SC_LANES)] = (
                            r_v[kk, src] + plsc.load_gather(st_v, [e_v[kk, src]]))
                pltpu.async_copy(y_hbm.at[slot_v.at[0]], acc_v, sem).wait()
                copies = [pltpu.async_copy(y_hbm.at[slot_v.at[kk]], acc_v, sem, add=True) for kk in range(1, TOP_K)]
                for cp in copies:
                    cp.wait()
                pltpu.sync_copy(acc_v, o_hbm.at[pl.ds(base + off, tc)])

    return combine(ys3d, e_t, r_t, starts)


def _ffn_out_body(x_ref, r_ref, wg_ref, wu_ref, wd_ref, g_ref, b_ref, *rest, alpha):
    o_ref = rest[-1]
    x = x_ref[...]
    xb = x.astype(bf16)
    hg = _dot(xb, wg_ref[...])
    h = hg * _sigmoid(hg) * _dot(xb, wu_ref[...])
    shared = _dot(h.astype(bf16), wd_ref[...])
    nc = x.shape[1] // LANES
    routed = jnp.concatenate([r_ref[_lane_chunk(x.shape[0], nc, c)] for c in range(nc)], axis=1)
    o_ref[...] = _layer_norm(alpha * x + routed + shared, g_ref[...], b_ref[...])


def _ffn_out(x2, routed_rows, ws_gate, ws_up, ws_down, g, b, alpha, tile, row0=0, prev=None):
    n_all, d = x2.shape
    n = routed_rows.shape[0] // (d // LANES)
    tile = min(tile, n)
    assert row0 % tile == 0 and n % tile == 0
    first = row0 // tile
    wg, wu, wd = ws_gate.astype(bf16), ws_up.astype(bf16), ws_down.astype(bf16)
    rows = pl.BlockSpec((tile, d), lambda i: (i + first, 0))
    in_specs = [rows, pl.BlockSpec((tile * (d // LANES), LANES), lambda i: (i, 0)), _const_spec(wg.shape),
                _const_spec(wu.shape), _const_spec(wd.shape), _const_spec((1, d)), _const_spec((1, d))]
    args = [x2, routed_rows, wg, wu, wd, g.reshape(1, d), b.reshape(1, d)]
    aliases = {}
    if prev is not None:
        in_specs.append(pl.BlockSpec(memory_space=pl.ANY))
        args.append(prev)
        aliases = {len(args) - 1: 0}
    return pl.pallas_call(
        functools.partial(_ffn_out_body, alpha=alpha),
        out_shape=jax.ShapeDtypeStruct((n_all, d), f32),
        grid=(n // tile,),
        in_specs=in_specs,
        out_specs=rows,
        input_output_aliases=aliases,
        compiler_params=_cparams(("parallel",)),
        name="ffn_out_ln3",
    )(*args)


def _layer(x, mem, positions, w_in, mu_shift, w_decay_up, w0, a_up, a0, g_up, k_k, k_a, r_k, lnx_g, lnx_b, sinks,
           w_o, ln1_g, ln1_b, wm_q, wm_kv, wm_o, ln2_g, ln2_b, w_router, router_bias, we_gate, we_up, we_down,
           ws_gate, ws_up, ws_down, ln3_g, ln3_b, *, layer, alpha):
    b, s, d = x.shape
    n = b * s
    xf = x.reshape(n, d)
    w_in_b = w_in.astype(bf16)
    u_r, u_s = _proj(xf, [w_in_b[:, :RWKV_COLS], w_in_b[:, RWKV_COLS:]], [f32, f32], tile=512)
    rwkv_tt = 256
    experts = (we_gate, we_up, we_down)
    if N_EXPERTS % (b * (s // min(rwkv_tt, s))) == 0:
        y_r, experts_b = _rwkv(u_r.reshape(b, s, RWKV_COLS), mu_shift, w_decay_up, w0, a_up, a0, g_up, k_k, k_a, r_k,
                               lnx_g, lnx_b, tt=rwkv_tt, side=experts, layer=layer)
    else:
        y_r, _ = _rwkv(u_r.reshape(b, s, RWKV_COLS), mu_shift, w_decay_up, w0, a_up, a0, g_up, k_k, k_a, r_k,
                       lnx_g, lnx_b, tt=rwkv_tt)
        experts_b = [w[layer].astype(bf16) for w in experts]
    y_s = _swa(u_s.reshape(b, s, SWA_COLS), positions, sinks)
    m = mem.shape[1]
    (kv,) = _proj(mem.reshape(b * m, d), [wm_kv.astype(bf16)], [bf16], tile=512)
    x2, x2_rows = _mix_xattn(y_r, y_s, w_o, x, ln1_g, ln1_b, kv.reshape(b, m, 2 * d), wm_q, wm_o, ln2_g, ln2_b,
                             alpha, tile=512)
    x2 = x2.reshape(n, d)
    nc = d // LANES
    nw = nc // 2
    parts = MOE_PARTS if n % (MOE_PARTS * SC_CORES * SC_SUBCORES * SC_INDEX_GROUP) == 0 else 1
    npart = n // parts
    x3 = None
    for part in range(parts):
        row0 = part * npart
        e_t, g_t, r_t, counts = _router(x2, w_router, router_bias, tile=512, row0=row0, n=npart)
        starts = (jnp.cumsum(counts) - counts).astype(i32)
        xs, gs = _dispatch_sc(x2_rows.reshape(n, nw, LANES), e_t, r_t, g_t, starts, tok0=row0)
        ys = _gmm(xs.reshape(npart * TOP_K * nw, LANES), gs, *experts_b, counts, tm=512)
        routed = _combine_sc(ys.reshape(npart * TOP_K, nc, LANES), e_t, r_t, starts, npart)
        x3 = _ffn_out(x2, routed.reshape(npart * nc, LANES), ws_gate, ws_up, ws_down, ln3_g, ln3_b, alpha, tile=512,
                      row0=row0, prev=x3)
    return x3.reshape(b, s, d)


def kernel(x, mem, positions, w_in, mu_shift, w_decay_up, w0, a_up, a0, g_up, k_k, k_a, r_k, lnx_g, lnx_b, sinks, w_o, ln1_g, ln1_b, wm_q, wm_kv, wm_o, ln2_g, ln2_b, w_router, router_bias, we_gate, we_up, we_down, ws_gate, ws_up, ws_down, ln3_g, ln3_b):
    depth = w_in.shape[0]
    alpha = (2 * depth) ** 0.25
    for l in range(depth):
        x = _layer(x, mem, positions, w_in[l], mu_shift[l], w_decay_up[l], w0[l], a_up[l], a0[l], g_up[l], k_k[l],
                   k_a[l], r_k[l], lnx_g[l], lnx_b[l], sinks[l], w_o[l], ln1_g[l], ln1_b[l], wm_q[l], wm_kv[l],
                   wm_o[l], ln2_g[l], ln2_b[l], w_router[l], router_bias[l], we_gate, we_up, we_down,
                   ws_gate[l], ws_up[l], ws_down[l], ln3_g[l], ln3_b[l], layer=l, alpha=alpha)
    return x
```

```python
import functools

import jax
import jax.numpy as jnp
from jax import lax
from jax.experimental import pallas as pl
from jax.experimental.pallas import tpu as pltpu
from jax.experimental.pallas import tpu_sc as plsc

f32 = jnp.float32
bf16 = jnp.bfloat16
i32 = jnp.int32

RWKV_HEADS = 8
HEAD_DIM = 64
RWKV_WIDTH = RWKV_HEADS * HEAD_DIM
DECAY_RANK = 64
AAA_RANK = 64
GATE_RANK = 128
RWKV_COLS = 3 * RWKV_WIDTH + DECAY_RANK + AAA_RANK + GATE_RANK
SWA_Q_HEADS = 8
SWA_KV_HEADS = 2
SWA_GROUP = SWA_Q_HEADS // SWA_KV_HEADS
SWA_WIDTH = SWA_Q_HEADS * HEAD_DIM
SWA_KV_WIDTH = SWA_KV_HEADS * HEAD_DIM
SWA_COLS = SWA_WIDTH + 2 * SWA_KV_WIDTH
WINDOW = 128
ROPE_THETA = 10000.0
MEM_HEADS = 4
N_EXPERTS = 256
TOP_K = 8
N_GROUPS = 8
GROUP_SIZE = N_EXPERTS // N_GROUPS
TOPK_GROUPS = 4
ROUTED_SCALE = 2.5
LN_EPS = 1e-5
GN_EPS = 64e-5
NEG_INF = -1e30

LANES = 128
SUBLANES = 8
WKV_CHUNK = 64
WKV_GROUP = 2
GMM_TILE = 512
VMEM_LIMIT = 56 * 1024 * 1024

SC_CORES = 2
SC_SUBCORES = 16
SC_LANES = 16
SC_INDEX_GROUP = 128
SC_CHUNK_BYTES = 256 * 1024
MOE_PARTS = 2


def _sc_chunk_rows(nc, lanes, dtype):
    return min(SC_INDEX_GROUP, SC_CHUNK_BYTES // (nc * lanes * jnp.dtype(dtype).itemsize))


def _cparams(sem):
    return pltpu.CompilerParams(dimension_semantics=sem, vmem_limit_bytes=VMEM_LIMIT)


def _const_spec(shape):
    nd = len(shape)
    return pl.BlockSpec(shape, lambda *_: (0,) * nd)


def _dot(a, b):
    return jnp.dot(a, b, preferred_element_type=f32)


def _dot_nt(a, b):
    return lax.dot_general(a, b, (((1,), (1,)), ((), ())), preferred_element_type=f32)


def _dot_tn(a, b):
    return lax.dot_general(a, b, (((0,), (0,)), ((), ())), preferred_element_type=f32)


def _split2(x):
    hi = x.astype(bf16)
    lo = (x - hi.astype(f32)).astype(bf16)
    return hi, lo


def _seg_sums(xs, seg_b):
    parts = []
    for x in xs:
        parts.extend(_split2(x))
    out = _dot(jnp.concatenate(parts, axis=0), seg_b)
    t = xs[0].shape[0]
    return [out[2 * i * t:(2 * i + 1) * t] + out[(2 * i + 1) * t:(2 * i + 2) * t] for i in range(len(xs))]


def _dot_hp(a, b):
    ah, al = _split2(a)
    bh, bl = _split2(b)
    return _dot(ah, bh) + _dot(ah, bl) + _dot(al, bh)


def _dot_exact_lhs(m_bf16, x):
    hi, lo = _split2(x)
    return _dot(m_bf16, hi) + _dot(m_bf16, lo)


def _sigmoid(x):
    return 1.0 / (1.0 + jnp.exp(-x))


def _lane_chunk(n_rows, n_chunks, c, row0=0):
    return (pl.ds(row0 * n_chunks + c, n_rows, stride=n_chunks), slice(None))


def _pack_bf16_pairs(x):
    chunks = []
    for j in range(x.shape[1] // (2 * LANES)):
        lo = x[:, 2 * j * LANES:(2 * j + 1) * LANES].astype(bf16).astype(f32)
        hi = x[:, (2 * j + 1) * LANES:(2 * j + 2) * LANES].astype(bf16).astype(f32)
        chunks.append(lax.bitcast_convert_type(hi, i32) | lax.shift_right_logical(lax.bitcast_convert_type(lo, i32), 16))
    return chunks


def _unpack_bf16_pairs(chunks):
    cols = []
    for w in chunks:
        cols.append(lax.bitcast_convert_type(lax.shift_left(w, 16), f32))
        cols.append(lax.bitcast_convert_type(w & jnp.int32(-65536), f32))
    return jnp.concatenate(cols, axis=1)


def _layer_norm(h, g, b):
    mu = jnp.mean(h, axis=-1, keepdims=True)
    d = h - mu
    var = jnp.mean(d * d, axis=-1, keepdims=True)
    return d * lax.rsqrt(var + LN_EPS) * g + b


def _proj_body(*refs, n_out):
    x_ref = refs[0]
    w_refs = refs[1:1 + n_out]
    o_refs = refs[1 + n_out:]
    xb = x_ref[...].astype(bf16)
    for w_ref, o_ref in zip(w_refs, o_refs):
        o_ref[...] = _dot(xb, w_ref[...]).astype(o_ref.dtype)


def _proj(x, ws, out_dtypes, tile):
    n, k = x.shape
    tile = min(tile, n)
    outs = pl.pallas_call(
        functools.partial(_proj_body, n_out=len(ws)),
        out_shape=[jax.ShapeDtypeStruct((n, w.shape[1]), dt) for w, dt in zip(ws, out_dtypes)],
        grid=(n // tile,),
        in_specs=[pl.BlockSpec((tile, k), lambda i: (i, 0))] + [_const_spec(w.shape) for w in ws],
        out_specs=[pl.BlockSpec((tile, w.shape[1]), lambda i: (i, 0)) for w in ws],
        compiler_params=_cparams(("parallel",)),
        name="proj",
    )(x, *ws)
    return outs


def _wkv_chunks(chains, states, masks):
    bd_b, bd, strict, incl, eye, eye_full = masks
    c, n = chains[0][1].shape
    nch = len(chains)

    def stack(x_b):
        return jnp.where(bd_b, jnp.concatenate([x_b] * WKV_GROUP, axis=0), jnp.zeros((), bf16))

    cast = [tuple(x.astype(bf16) for x in ch[1:6]) for ch in chains]
    v_s = [stack(cb[4]) for cb in cast]
    g = [_dot_nt(jnp.concatenate([cb[0], cb[3]], axis=0), jnp.concatenate([stack(cb[1]), stack(cb[2])], axis=0))
         for cb in cast]
    l_ak = [jnp.where(strict, gi[:c, n:], 0.0).astype(bf16) for gi in g]
    m_rb = [jnp.where(incl, gi[c:, :n], 0.0).astype(bf16) for gi in g]
    m_rk = [jnp.where(incl, gi[c:, n:], 0.0).astype(bf16) for gi in g]
    x = [jnp.where(strict, gi[:c, :n], 0.0) for gi in g]
    t = [eye + xi for xi in x]
    for _ in range(5):
        xb = [xi.astype(bf16) for xi in x]
        x = [_dot(xi, stack(xi)) for xi in xb]
        t = [ti + _dot(ti.astype(bf16), stack(xi.astype(bf16))) for ti, xi in zip(t, x)]
    lakv = [_dot(l_ak[i], v_s[i]).astype(bf16) for i in range(nch)]
    au = [_dot(t[i].astype(bf16), jnp.concatenate([stack(cast[i][0]), stack(lakv[i])], axis=1))
          for i in range(nch)]
    abar = [a[:, :n].astype(bf16) for a in au]
    ubar = [a[:, n:].astype(bf16) for a in au]
    ry = [_dot(m_rb[i], jnp.concatenate([stack(abar[i]), stack(ubar[i])], axis=1)) for i in range(nch)]
    r_bar = [(chains[i][4] + ry[i][:, :n]).astype(bf16) for i in range(nch)]
    y_bar = [ry[i][:, n:] + _dot(m_rk[i], v_s[i]) for i in range(nch)]
    p = [((eye_full + jnp.where(bd, _dot_tn(abar[i], cast[i][1]), 0.0)) * chains[i][6]).astype(bf16)
         for i in range(nch)]
    q = []
    for i in range(nch):
        q_bd = jnp.where(bd, _dot_tn(jnp.concatenate([ubar[i], cast[i][4]], axis=0),
                                     jnp.concatenate([cast[i][1], cast[i][2]], axis=0)), 0.0)
        qi = q_bd[0:c]
        for h in range(1, WKV_GROUP):
            qi = qi + q_bd[h * c:(h + 1) * c]
        q.append(qi * chains[i][6])
    states = list(states)
    ys = []
    for i in range(nch):
        gi = chains[i][0]
        s_b = states[gi].astype(bf16)
        ys.append(_dot_nt(r_bar[i], stack(s_b)) + y_bar[i])
        states[gi] = _dot(s_b, p[i]) + q[i]
    return ys, states


def _rwkv_body(u_ref, mu_ref, wdec_ref, w0_ref, aup_ref, a0_ref, gup_ref, kk_ref, ka_ref, rk_ref,
               lng_ref, lnb_ref, seg_ref, tri_ref, *rest, tt, n_side):
    side_in = rest[:n_side]
    y_ref = rest[n_side]
    side_out = rest[n_side + 1:2 * n_side + 1]
    state_ref, carry_ref = rest[2 * n_side + 1:]
    j = pl.program_id(1)
    for src, dst in zip(side_in, side_out):
        dst[...] = src[...].astype(dst.dtype)

    @pl.when(j == 0)
    def _():
        state_ref[...] = jnp.zeros_like(state_ref)
        carry_ref[...] = jnp.zeros_like(carry_ref)

    w = RWKV_WIDTH
    u = u_ref[...]
    row = lax.broadcasted_iota(i32, u.shape, 0)
    prev = jnp.where(row == 0, carry_ref[0:1, :], pltpu.roll(u, 1, axis=0))
    carry_ref[0:1, :] = u[tt - 1:tt, :]
    us = u + (prev - u) * mu_ref[...]
    r = us[:, 0:w]
    k = us[:, w:2 * w]
    v = us[:, 2 * w:3 * w]
    wa = us[:, 3 * w:3 * w + DECAY_RANK + AAA_RANK]
    gd = us[:, 3 * w + DECAY_RANK + AAA_RANK:]
    z = w0_ref[...] + _dot_hp(jnp.tanh(wa), wdec_ref[...])
    softplus_neg_z = jnp.maximum(-z, 0.0) + jnp.log(1.0 + jnp.exp(-jnp.abs(z)))
    lw = -jnp.exp(-softplus_neg_z - 0.5)
    a = _sigmoid(a0_ref[...] + _dot_hp(wa, aup_ref[...]))
    gate = _dot(_sigmoid(gd).astype(bf16), gup_ref[...].astype(bf16))
    seg = seg_ref[...]
    kk = k * kk_ref[...]
    kmod = k * (1.0 + (a - 1.0) * ka_ref[...])
    kk_sq, bonus_dot = _seg_sums([kk * kk, r * kmod * rk_ref[...]], seg)
    kk = kk / jnp.maximum(jnp.sqrt(kk_sq), 1e-12)
    cum = _dot_exact_lhs(tri_ref[...], lw)
    wc = jnp.exp(cum)
    iwc = jnp.exp(-cum)
    at = -kk * jnp.exp(cum - lw)
    bt = kk * a * iwc
    kt = kmod * iwc
    rt = r * wc

    n = WKV_GROUP * HEAD_DIM
    ri = lax.broadcasted_iota(i32, (n, n), 0)
    ci = lax.broadcasted_iota(i32, (n, n), 1)
    bd = (ri // WKV_CHUNK) == (ci // HEAD_DIM)
    bd_b = jnp.where(bd, 1.0, 0.0).astype(bf16) > 0
    eye_full = jnp.where(ri == ci, 1.0, 0.0).astype(f32)
    ti = lax.broadcasted_iota(i32, (WKV_CHUNK, n), 0)
    si = lax.broadcasted_iota(i32, (WKV_CHUNK, n), 1) % WKV_CHUNK
    masks = (bd_b, bd, ti > si, ti >= si, jnp.where(ti == si, 1.0, 0.0).astype(f32), eye_full)

    n_groups = w // n
    n_chunks = tt // WKV_CHUNK
    chains = []
    for c in range(n_chunks):
        rs = slice(c * WKV_CHUNK, (c + 1) * WKV_CHUNK)
        last = (c + 1) * WKV_CHUNK - 1
        for gi in range(n_groups):
            cs = slice(gi * n, (gi + 1) * n)
            chains.append((gi, at[rs, cs], bt[rs, cs], kt[rs, cs], rt[rs, cs], v[rs, cs], wc[last:last + 1, cs]))
    ys, states = _wkv_chunks(chains, [state_ref[gi] for gi in range(n_groups)], masks)
    for gi in range(n_groups):
        state_ref[gi] = states[gi]
    y = jnp.concatenate([jnp.concatenate(ys[c * n_groups:(c + 1) * n_groups], axis=1) for c in range(n_chunks)],
                        axis=0)

    inv_n = 1.0 / HEAD_DIM
    d = y - _seg_sums([y], seg)[0] * inv_n
    var = _seg_sums([d * d], seg)[0] * inv_n
    yn = d * lax.rsqrt(var + GN_EPS) * lng_ref[...] + lnb_ref[...]
    y_ref[...] = ((yn + bonus_dot * v) * gate).astype(y_ref.dtype)


def _rwkv(u_r, mu_shift, w_decay_up, w0, a_up, a0, g_up, k_k, k_a, r_k, lnx_g, lnx_b, tt, side=(), layer=0):
    b, s, cols = u_r.shape
    tt = min(tt, s)
    n_steps = b * (s // tt)
    assert all(a.shape[1] % n_steps == 0 for a in side)
    w = RWKV_WIDTH
    row = lambda p: p.reshape(1, -1).astype(f32)
    wdec = jnp.concatenate([w_decay_up, jnp.zeros((AAA_RANK, w), f32)], axis=0)
    aup = jnp.concatenate([jnp.zeros((DECAY_RANK, w), f32), a_up], axis=0)
    hid = jnp.arange(w) // HEAD_DIM
    seg = (hid[:, None] == hid[None, :]).astype(bf16)
    ti = jnp.arange(tt)
    tri = ((ti[:, None] // WKV_CHUNK == ti[None, :] // WKV_CHUNK) & (ti[:, None] >= ti[None, :])).astype(bf16)
    params = [row(mu_shift), wdec, row(w0), aup, row(a0), g_up, row(k_k), row(k_a), row(r_k), row(lnx_g),
              row(lnx_b), seg, tri]
    n = WKV_GROUP * HEAD_DIM
    nj = s // tt
    side_in = [pl.BlockSpec((None, a.shape[1] // n_steps) + a.shape[2:], lambda bi, j: (layer, bi * nj + j, 0, 0))
               for a in side]
    side_out = [pl.BlockSpec((a.shape[1] // n_steps,) + a.shape[2:], lambda bi, j: (bi * nj + j, 0, 0))
                for a in side]
    outs = pl.pallas_call(
        functools.partial(_rwkv_body, tt=tt, n_side=len(side)),
        out_shape=[jax.ShapeDtypeStruct((b, s, w), bf16)] + [jax.ShapeDtypeStruct(a.shape[1:], bf16) for a in side],
        grid=(b, nj),
        in_specs=([pl.BlockSpec((None, tt, cols), lambda bi, j: (bi, j, 0))] + [_const_spec(p.shape) for p in params]
                  + side_in),
        out_specs=[pl.BlockSpec((None, tt, w), lambda bi, j: (bi, j, 0))] + side_out,
        scratch_shapes=[pltpu.VMEM((w // n, HEAD_DIM, n), f32), pltpu.VMEM((SUBLANES, cols), f32)],
        compiler_params=_cparams(("parallel", "arbitrary")),
        name="rwkv7",
    )(u_r, *params, *side)
    return outs[0], outs[1:]


def _swa_body(sink_ref, u_ref, pos_ref, invf_ref, o_ref, kprev_ref, vprev_ref, *, nwin):
    j = pl.program_id(1)

    @pl.when(j == 0)
    def _():
        kprev_ref[...] = jnp.zeros_like(kprev_ref)
        vprev_ref[...] = jnp.zeros_like(vprev_ref)

    wq = SWA_WIDTH
    kvw = SWA_KV_WIDTH
    half = HEAD_DIM // 2
    gw = SWA_GROUP * HEAD_DIM
    rows = SWA_GROUP * WINDOW
    lane_kv = lax.broadcasted_iota(i32, (WINDOW, kvw), 1)
    ri = lax.broadcasted_iota(i32, (rows, WINDOW), 0)
    ci = lax.broadcasted_iota(i32, (rows, WINDOW), 1)
    t_idx = ri % WINDOW
    mask_cur = ci <= t_idx
    rb = lax.broadcasted_iota(i32, (rows, 1), 0) // WINDOW
    bd = (lax.broadcasted_iota(i32, (rows, gw), 0) // WINDOW) == (lax.broadcasted_iota(i32, (rows, gw), 1) // HEAD_DIM)

    def rope(x, c, s):
        n = x.shape[1]
        lane = lax.broadcasted_iota(i32, x.shape, 1)
        rot = jnp.where((lane % HEAD_DIM) < half, -pltpu.roll(x, n - half, axis=1), pltpu.roll(x, half, axis=1))
        return x * c + rot * s

    def rep(x, gi):
        sw = pltpu.roll(x, HEAD_DIM, axis=1)
        one = jnp.where((lane_kv // HEAD_DIM) == gi, x, sw)
        return jnp.concatenate([one] * (gw // kvw), axis=1).astype(bf16)

    k_prev = kprev_ref[...]
    v_prev = vprev_ref[...]
    for wi in range(nwin):
        rs = slice(wi * WINDOW, (wi + 1) * WINDOW)
        u = u_ref[rs, :]
        ang_t = invf_ref[...] * pos_ref[wi].astype(f32)
        cos = jnp.concatenate([jnp.cos(ang_t)] * (LANES // half), axis=0).T
        sin = jnp.concatenate([jnp.sin(ang_t)] * (LANES // half), axis=0).T
        q = rope(u[:, :wq] * (HEAD_DIM ** -0.5), jnp.concatenate([cos] * (wq // LANES), axis=1),
                 jnp.concatenate([sin] * (wq // LANES), axis=1))
        k_cur = rope(u[:, wq:wq + kvw], cos, sin)
        v_cur = u[:, wq + kvw:]
        mask_prev = (ci > t_idx + jnp.where(j > 0, 0, WINDOW)) if wi == 0 else (ci > t_idx)
        outs = []
        for gi in range(SWA_KV_HEADS):
            qg = q[:, gi * gw:(gi + 1) * gw]
            q_bd = jnp.where(bd, jnp.concatenate([qg] * SWA_GROUP, axis=0), 0.0).astype(bf16)
            s_prev = jnp.where(mask_prev, _dot_nt(q_bd, rep(k_prev, gi)), NEG_INF)
            s_cur = jnp.where(mask_cur, _dot_nt(q_bd, rep(k_cur, gi)), NEG_INF)
            sink = jnp.zeros((rows, 1), f32)
            for h in range(SWA_GROUP):
                sink = jnp.where(rb == h, sink_ref[gi * SWA_GROUP + h], sink)
            m = jnp.maximum(jnp.maximum(jnp.max(s_prev, axis=-1, keepdims=True),
                                        jnp.max(s_cur, axis=-1, keepdims=True)), sink)
            p_prev = jnp.exp(s_prev - m)
            p_cur = jnp.exp(s_cur - m)
            denom = (jnp.sum(p_prev, axis=-1, keepdims=True) + jnp.sum(p_cur, axis=-1, keepdims=True)
                     + jnp.exp(sink - m))
            o_bd = _dot(p_prev.astype(bf16), rep(v_prev, gi)) + _dot(p_cur.astype(bf16), rep(v_cur, gi))
            o_bd = jnp.where(bd, o_bd * (1.0 / denom), 0.0)
            og = o_bd[0:WINDOW]
            for h in range(1, SWA_GROUP):
                og = og + o_bd[h * WINDOW:(h + 1) * WINDOW]
            outs.append(og)
        o_ref[rs, :] = jnp.concatenate(outs, axis=1).astype(o_ref.dtype)
        k_prev, v_prev = k_cur, v_cur
    kprev_ref[...] = k_prev
    vprev_ref[...] = v_prev


def _swa(u_s, positions, sinks, nwin=2):
    b, s, cols = u_s.shape
    half = HEAD_DIM // 2
    nwin = nwin if s % (nwin * WINDOW) == 0 else 1
    tile = nwin * WINDOW
    inv_freq = (ROPE_THETA ** (-jnp.arange(0, HEAD_DIM, 2, dtype=f32) / HEAD_DIM)).reshape(half, 1)
    pos = positions.reshape(b, s // WINDOW, 1, WINDOW).astype(i32)
    return pl.pallas_call(
        functools.partial(_swa_body, nwin=nwin),
        out_shape=jax.ShapeDtypeStruct((b, s, SWA_WIDTH), bf16),
        grid=(b, s // tile),
        in_specs=[pl.BlockSpec(memory_space=pltpu.SMEM),
                  pl.BlockSpec((None, tile, cols), lambda bi, j: (bi, j, 0)),
                  pl.BlockSpec((None, nwin, 1, WINDOW), lambda bi, j: (bi, j, 0, 0)),
                  _const_spec((half, 1))],
        out_specs=pl.BlockSpec((None, tile, SWA_WIDTH), lambda bi, j: (bi, j, 0)),
        scratch_shapes=[pltpu.VMEM((WINDOW, SWA_KV_WIDTH), f32), pltpu.VMEM((WINDOW, SWA_KV_WIDTH), f32)],
        compiler_params=_cparams(("parallel", "arbitrary")),
        name="swa",
    )(sinks.astype(f32), u_s, pos, inv_freq)


def _mix_xattn_body(ya_ref, yb_ref, wa_ref, wb_ref, xin_ref, g1_ref, b1_ref, kv_ref, wq_ref, wo_ref, g_ref, b_ref,
                    o_ref, o3_ref, *, alpha):
    mix = _dot(ya_ref[...], wa_ref[...]) + _dot(yb_ref[...], wb_ref[...])
    x = _layer_norm(alpha * xin_ref[...] + mix, g1_ref[...], b1_ref[...])
    d = x.shape[1]
    hd = d // MEM_HEADS
    q = _dot(x.astype(bf16), wq_ref[...]) * (hd ** -0.5)
    kv = kv_ref[...]
    outs = []
    for h in range(MEM_HEADS):
        qh = q[:, h * hd:(h + 1) * hd].astype(bf16)
        kh = kv[:, h * hd:(h + 1) * hd]
        vh = kv[:, d + h * hd:d + (h + 1) * hd]
        s = _dot_nt(qh, kh)
        p = jnp.exp(s - jnp.max(s, axis=-1, keepdims=True))
        l = jnp.sum(p, axis=-1, keepdims=True)
        outs.append(_dot(p.astype(bf16), vh) / l)
    o = jnp.concatenate(outs, axis=1)
    xa = _dot(o.astype(bf16), wo_ref[...])
    y = _layer_norm(alpha * x + xa, g_ref[...], b_ref[...])
    o_ref[...] = y
    words = _pack_bf16_pairs(y)
    for j, w in enumerate(words):
        o3_ref[_lane_chunk(x.shape[0], len(words), j)] = w


def _mix_xattn(ya, yb, w_o, x, g1, b1, kv, wm_q, wm_o, g2, b2, alpha, tile):
    bsz, s, d = x.shape
    m = kv.shape[1]
    tile = min(tile, s)
    nj = s // tile
    wa = w_o[:ya.shape[2]].astype(bf16)
    wb = w_o[ya.shape[2]:].astype(bf16)
    wq = wm_q.astype(bf16)
    wo = wm_o.astype(bf16)
    rows = lambda width: pl.BlockSpec((None, tile, width), lambda bi, j: (bi, j, 0))
    vec = _const_spec((1, d))
    return pl.pallas_call(
        functools.partial(_mix_xattn_body, alpha=alpha),
        out_shape=[jax.ShapeDtypeStruct((bsz, s, d), f32),
                   jax.ShapeDtypeStruct((bsz * s * (d // (2 * LANES)), LANES), i32)],
        grid=(bsz, nj),
        in_specs=[rows(ya.shape[2]), rows(yb.shape[2]), _const_spec(wa.shape), _const_spec(wb.shape), rows(d), vec, vec,
                  pl.BlockSpec((None, m, 2 * d), lambda bi, j: (bi, 0, 0)),
                  _const_spec(wq.shape), _const_spec(wo.shape), vec, vec],
        out_specs=[rows(d), pl.BlockSpec((tile * (d // (2 * LANES)), LANES), lambda bi, j: (bi * nj + j, 0))],
        compiler_params=_cparams(("parallel", "parallel")),
        name="mix_xattn",
    )(ya, yb, wa, wb, x, g1.reshape(1, d), b1.reshape(1, d), kv, wq, wo, g2.reshape(1, d), b2.reshape(1, d))


def _router_body(x_ref, wt_ref, bias_ref, upper_ref, e_ref, g_ref, r_ref, cnt_out_ref, cnt_ref, *, t):
    @pl.when(pl.program_id(0) == 0)
    def _():
        cnt_ref[...] = jnp.zeros_like(cnt_ref)

    xh, xl = _split2(x_ref[...])
    wh, wl = _split2(wt_ref[...])
    logits = _dot_nt(wh, xh) + _dot_nt(wh, xl) + _dot_nt(wl, xh)
    scores = _sigmoid(logits)
    biased = scores + bias_ref[...][:, 0:1]
    ne = N_EXPERTS
    neg = -jnp.inf

    def top1(vals):
        rows = lax.broadcasted_iota(i32, vals.shape, 0).astype(f32)
        m = jnp.max(vals, axis=0, keepdims=True)
        idx = jnp.min(jnp.where(vals == m, rows, float(vals.shape[0])), axis=0, keepdims=True)
        return m, idx, rows == idx

    gscores = []
    for gi in range(N_GROUPS):
        blk = biased[gi * GROUP_SIZE:(gi + 1) * GROUP_SIZE, :]
        m1, _, hit = top1(blk)
        m2 = jnp.max(jnp.where(hit, neg, blk), axis=0, keepdims=True)
        gscores.append(m1 + m2)
    gs = jnp.concatenate(gscores, axis=0)
    gsel = jnp.zeros(gs.shape, f32)
    for _ in range(TOPK_GROUPS):
        _, _, hit = top1(gs)
        gsel = jnp.where(hit, 1.0, gsel)
        gs = jnp.where(hit, neg, gs)
    emask = jnp.concatenate(
        [jnp.broadcast_to(gsel[gi:gi + 1, :], (GROUP_SIZE, t)) for gi in range(N_GROUPS)], axis=0) > 0.5
    cand = jnp.where(emask, biased, NEG_INF)
    idxs, sels = [], []
    chosen = jnp.zeros((ne, t), f32)
    for _ in range(TOP_K):
        _, idx, hit = top1(cand)
        idxs.append(idx)
        sels.append(jnp.sum(jnp.where(hit, scores, 0.0), axis=0, keepdims=True))
        chosen = chosen + jnp.where(hit, 1.0, 0.0)
        cand = jnp.where(hit, neg, cand)
    sel = jnp.concatenate(sels, axis=0)
    g_ref[...] = sel / jnp.sum(sel, axis=0, keepdims=True) * ROUTED_SCALE
    e_ref[...] = jnp.concatenate(idxs, axis=0).astype(i32)
    before = _dot(chosen.astype(bf16), upper_ref[...]) + cnt_ref[...][:, 0:1]
    rows = lax.broadcasted_iota(i32, (ne, t), 0).astype(f32)
    ranks = [jnp.sum(jnp.where(rows == idx, before, 0.0), axis=0, keepdims=True) for idx in idxs]
    r_ref[...] = jnp.concatenate(ranks, axis=0).astype(i32)
    cnt_ref[...] = cnt_ref[...] + jnp.sum(chosen, axis=1, keepdims=True)
    cnt_out_ref[...] = cnt_ref[...].astype(i32)


def _router(x2, w_router, router_bias, tile, row0=0, n=None):
    d = x2.shape[1]
    n = x2.shape[0] if n is None else n
    ne = N_EXPERTS
    t = min(tile, n)
    assert row0 % t == 0 and n % t == 0
    first = row0 // t
    wt = w_router.T
    bias = jnp.broadcast_to(router_bias.reshape(ne, 1).astype(f32), (ne, LANES))
    ti = jnp.arange(t)
    upper = (ti[:, None] < ti[None, :]).astype(bf16)
    cols = pl.BlockSpec((TOP_K, t), lambda i: (0, i))
    e_t, g_t, r_t, cnt = pl.pallas_call(
        functools.partial(_router_body, t=t),
        out_shape=[jax.ShapeDtypeStruct((TOP_K, n), i32), jax.ShapeDtypeStruct((TOP_K, n), f32),
                   jax.ShapeDtypeStruct((TOP_K, n), i32), jax.ShapeDtypeStruct((ne, LANES), i32)],
        grid=(n // t,),
        in_specs=[pl.BlockSpec((t, d), lambda i: (i + first, 0)), _const_spec((ne, d)), _const_spec((ne, LANES)),
                  _const_spec((t, t))],
        out_specs=[cols, cols, cols, _const_spec((ne, LANES))],
        scratch_shapes=[pltpu.VMEM((ne, LANES), f32)],
        compiler_params=_cparams(("arbitrary",)),
        name="router",
    )(x2, wt, bias, upper)
    return e_t, g_t, r_t, cnt[:, 0]


def _dispatch_sc(x3d, e_t, r_t, g_t, starts, n_slots, tok0=0):
    _, nc, lanes = x3d.shape
    n = e_t.shape[1]
    workers = SC_CORES * SC_SUBCORES
    per_w = n // workers
    tc = _sc_chunk_rows(nc, lanes, x3d.dtype)
    assert n % (workers * SC_INDEX_GROUP) == 0 and SC_INDEX_GROUP % tc == 0
    mesh = plsc.VectorSubcoreMesh(core_axis_name="c", subcore_axis_name="s")

    @functools.partial(
        pl.kernel, mesh=mesh,
        out_type=[jax.ShapeDtypeStruct((n_slots, nc, lanes), x3d.dtype),
                  jax.ShapeDtypeStruct((n_slots,), f32)],
        scratch_types=[
            pltpu.VMEM((tc, nc, lanes), x3d.dtype),
            pltpu.VMEM((TOP_K, SC_INDEX_GROUP), i32),
            pltpu.VMEM((TOP_K, SC_INDEX_GROUP), i32),
            pltpu.VMEM((TOP_K, SC_INDEX_GROUP), f32),
            pltpu.VMEM((TOP_K, tc), i32),
            pltpu.VMEM((TOP_K, tc), f32),
            pltpu.VMEM((N_EXPERTS,), i32),
            pltpu.SemaphoreType.DMA,
        ],
        compiler_params=pltpu.CompilerParams(use_tc_tiling_on_sc=True, needs_layout_passes=False),
    )
    def dispatch(x_hbm, e_hbm, r_hbm, g_hbm, st_hbm, o_hbm, gs_hbm, rows_v, e_v, r_v, g_v, slot_v, gate_v, st_v, sem):
        wid = lax.axis_index("s") * SC_CORES + lax.axis_index("c")
        pltpu.sync_copy(st_hbm, st_v)

        @pl.loop(0, per_w // SC_INDEX_GROUP)
        def _(gi):
            base = wid * per_w + gi * SC_INDEX_GROUP
            pltpu.sync_copy(e_hbm.at[:, pl.ds(base, SC_INDEX_GROUP)], e_v)
            pltpu.sync_copy(r_hbm.at[:, pl.ds(base, SC_INDEX_GROUP)], r_v)
            pltpu.sync_copy(g_hbm.at[:, pl.ds(base, SC_INDEX_GROUP)], g_v)
            for h in range(SC_INDEX_GROUP // tc):
                off = h * tc
                pltpu.sync_copy(x_hbm.at[pl.ds(tok0 + base + off, tc)], rows_v)
                for kk in range(TOP_K):
                    for j in range(tc // SC_LANES):
                        src = pl.ds(off + j * SC_LANES, SC_LANES)
                        dst = pl.ds(j * SC_LANES, SC_LANES)
                        slot_v[kk, dst] = r_v[kk, src] + plsc.load_gather(st_v, [e_v[kk, src]])
                        gate_v[kk, dst] = g_v[kk, src]
                copies = [pltpu.async_copy(rows_v, o_hbm.at[slot_v.at[kk]], sem) for kk in range(TOP_K)]
                copies += [pltpu.async_copy(gate_v.at[kk], gs_hbm.at[slot_v.at[kk]], sem) for kk in range(TOP_K)]
                for cp in copies:
                    cp.wait()

    return dispatch(x3d, e_t, r_t, g_t, starts)


def _gmm_body(gid_ref, tid_ref, nrows_ref, newg_ref, nextg_ref, ord_ref, x_ref, g_ref,
              wg_hbm, wu_hbm, wd_hbm, o_ref, wg_b, wu_b, wd_b, sems, *, tm):
    v = pl.program_id(0)
    n_real = nrows_ref[v]
    nc = wg_b.shape[1] // LANES
    slot = ord_ref[v] % 2

    def weight_copies(expert, dst_slot):
        return [pltpu.make_async_copy(src.at[expert], dst.at[dst_slot], sems.at[dst_slot])
                for src, dst in ((wg_hbm, wg_b), (wu_hbm, wu_b), (wd_hbm, wd_b))]

    @pl.when(newg_ref[v] == 1)
    def _():
        @pl.when(v == 0)
        def _():
            for cp in weight_copies(gid_ref[v], slot):
                cp.start()

        for cp in weight_copies(gid_ref[v], slot):
            cp.wait()

        @pl.when(nextg_ref[v] >= 0)
        def _():
            for cp in weight_copies(nextg_ref[v], 1 - slot):
                cp.start()

    @pl.when(n_real > 0)
    def _():
        nw = nc // 2
        real = lax.broadcasted_iota(i32, (tm, 1), 0) < n_real
        x = _unpack_bf16_pairs([x_ref[_lane_chunk(tm, nw, j)] for j in range(nw)])
        x = jnp.where(real, x, 0.0).astype(bf16)
        hg = _dot(x, wg_b[slot])
        h = hg * _sigmoid(hg) * _dot(x, wu_b[slot])
        g = g_ref[...]
        g_cols = jnp.concatenate([g, jnp.zeros((SUBLANES - g.shape[0], LANES), f32)], axis=0).T
        g_col = jnp.concatenate([g_cols[:, r:r + 1] for r in range(tm // LANES)], axis=0)
        h = h * jnp.where(real, g_col, 0.0)
        y = _dot(h.astype(bf16), wd_b[slot])
        for c in range(nc):
            o_ref[_lane_chunk(tm, nc, c)] = y[:, c * LANES:(c + 1) * LANES]


def _expert_layout(counts, n_assign, tm):
    tiles = (counts + tm - 1) // tm
    starts = ((jnp.cumsum(tiles) - tiles) * tm).astype(i32)
    capacity = (n_assign // tm + counts.shape[0]) * tm
    return starts, tiles, capacity


def _gmm(xs, gates, we_gate, we_up, we_down, counts, tm):
    ne, d, de = we_gate.shape
    nc = d // LANES
    nw = nc // 2
    n_tiles = xs.shape[0] // (nw * tm)
    assert tm % LANES == 0 and tm // LANES <= SUBLANES
    gates3 = gates.reshape(n_tiles, tm // LANES, LANES)
    tiles = (counts + tm - 1) // tm
    tile_end = jnp.cumsum(tiles)
    tile_start = tile_end - tiles
    vi = jnp.arange(n_tiles, dtype=i32)
    valid = vi < tile_end[-1]
    gid = jnp.minimum(jnp.sum((tile_end[None, :] <= vi[:, None]).astype(i32), axis=1), ne - 1)
    gid = jnp.where(valid, gid, jnp.max(jnp.where(valid, gid, 0)))
    onehot = gid[:, None] == jnp.arange(ne, dtype=i32)[None, :]
    pick = lambda table: jnp.sum(jnp.where(onehot, table[None, :], 0), axis=1)
    tid = jnp.minimum(vi, tile_end[-1] - 1).astype(i32)
    n_real = jnp.where(valid, jnp.clip(pick(counts) - (vi - pick(tile_start)) * tm, 0, tm), 0).astype(i32)
    one = jnp.ones((1,), i32)
    newg = jnp.concatenate([one, (gid[1:] != gid[:-1]).astype(i32)])
    later = gid[None, :] > gid[:, None]
    nextg = jnp.min(jnp.where(later, gid[None, :], ne), axis=1)
    nextg = jnp.where(nextg < ne, nextg, -1).astype(i32)
    order = (jnp.cumsum(newg) - 1).astype(i32)
    rows = lambda chunks: pl.BlockSpec((tm * chunks, LANES), lambda v, g, t, *_: (t[v], 0))
    hbm = pl.BlockSpec(memory_space=pl.ANY)
    return pl.pallas_call(
        functools.partial(_gmm_body, tm=tm),
        out_shape=jax.ShapeDtypeStruct((n_tiles * tm * nc, LANES), f32),
        grid_spec=pltpu.PrefetchScalarGridSpec(
            num_scalar_prefetch=6, grid=(n_tiles,),
            in_specs=[rows(nw), pl.BlockSpec((None, tm // LANES, LANES), lambda v, g, t, *_: (t[v], 0, 0)),
                      hbm, hbm, hbm],
            out_specs=rows(nc),
            scratch_shapes=[pltpu.VMEM((2, d, de), bf16), pltpu.VMEM((2, d, de), bf16), pltpu.VMEM((2, de, d), bf16),
                            pltpu.SemaphoreType.DMA((2,))]),
        compiler_params=_cparams(("arbitrary",)),
        name="moe_experts",
    )(gid, tid, n_real, newg, nextg, order, xs, gates3, we_gate, we_up, we_down)


def _combine_sc(ys3d, e_t, r_t, starts, n):
    _, nc, lanes = ys3d.shape
    workers = SC_CORES * SC_SUBCORES
    per_w = n // workers
    tc = _sc_chunk_rows(nc, lanes, f32)
    assert n % (workers * SC_INDEX_GROUP) == 0 and SC_INDEX_GROUP % tc == 0
    mesh = plsc.VectorSubcoreMesh(core_axis_name="c", subcore_axis_name="s")

    @functools.partial(
        pl.kernel, mesh=mesh,
        out_type=jax.ShapeDtypeStruct((n, nc, lanes), f32),
        scratch_types=[
            pltpu.VMEM((tc, nc, lanes), f32),
            pltpu.VMEM((TOP_K, SC_INDEX_GROUP), i32),
            pltpu.VMEM((TOP_K, SC_INDEX_GROUP), i32),
            pltpu.VMEM((TOP_K, tc), i32),
            pltpu.VMEM((N_EXPERTS,), i32),
            pltpu.SemaphoreType.DMA,
        ],
        compiler_params=pltpu.CompilerParams(use_tc_tiling_on_sc=True, needs_layout_passes=False),
    )
    def combine(y_hbm, e_hbm, r_hbm, st_hbm, o_hbm, acc_v, e_v, r_v, slot_v, st_v, sem):
        wid = lax.axis_index("s") * SC_CORES + lax.axis_index("c")
        pltpu.sync_copy(st_hbm, st_v)

        @pl.loop(0, per_w // SC_INDEX_GROUP)
        def _(gi):
            base = wid * per_w + gi * SC_INDEX_GROUP
            pltpu.sync_copy(e_hbm.at[:, pl.ds(base, SC_INDEX_GROUP)], e_v)
            pltpu.sync_copy(r_hbm.at[:, pl.ds(base, SC_INDEX_GROUP)], r_v)
            for h in range(SC_INDEX_GROUP // tc):
                off = h * tc
                for kk in range(TOP_K):
                    for j in range(tc // SC_LANES):
                        src = pl.ds(off + j * SC_LANES, SC_LANES)
                        slot_v[kk, pl.ds(j * SC_LANES, SC_LANES)] = (
                            r_v[kk, src] + plsc.load_gather(st_v, [e_v[kk, src]]))
                pltpu.async_copy(y_hbm.at[slot_v.at[0]], acc_v, sem).wait()
                copies = [pltpu.async_copy(y_hbm.at[slot_v.at[kk]], acc_v, sem, add=True) for kk in range(1, TOP_K)]
                for cp in copies:
                    cp.wait()
                pltpu.sync_copy(acc_v, o_hbm.at[pl.ds(base + off, tc)])

    return combine(ys3d, e_t, r_t, starts)


def _ffn_out_body(x_ref, r_ref, wg_ref, wu_ref, wd_ref, g_ref, b_ref, *rest, alpha):
    o_ref = rest[-1]
    x = x_ref[...]
    xb = x.astype(bf16)
    hg = _dot(xb, wg_ref[...])
    h = hg * _sigmoid(hg) * _dot(xb, wu_ref[...])
    shared = _dot(h.astype(bf16), wd_ref[...])
    nc = x.shape[1] // LANES
    routed = jnp.concatenate([r_ref[_lane_chunk(x.shape[0], nc, c)] for c in range(nc)], axis=1)
    o_ref[...] = _layer_norm(alpha * x + routed + shared, g_ref[...], b_ref[...])


def _ffn_out(x2, routed_rows, ws_gate, ws_up, ws_down, g, b, alpha, tile, row0=0, prev=None):
    n_all, d = x2.shape
    n = routed_rows.shape[0] // (d // LANES)
    tile = min(tile, n)
    assert row0 % tile == 0 and n % tile == 0
    first = row0 // tile
    wg, wu, wd = ws_gate.astype(bf16), ws_up.astype(bf16), ws_down.astype(bf16)
    rows = pl.BlockSpec((tile, d), lambda i: (i + first, 0))
    in_specs = [rows, pl.BlockSpec((tile * (d // LANES), LANES), lambda i: (i, 0)), _const_spec(wg.shape),
                _const_spec(wu.shape), _const_spec(wd.shape), _const_spec((1, d)), _const_spec((1, d))]
    args = [x2, routed_rows, wg, wu, wd, g.reshape(1, d), b.reshape(1, d)]
    aliases = {}
    if prev is not None:
        in_specs.append(pl.BlockSpec(memory_space=pl.ANY))
        args.append(prev)
        aliases = {len(args) - 1: 0}
    return pl.pallas_call(
        functools.partial(_ffn_out_body, alpha=alpha),
        out_shape=jax.ShapeDtypeStruct((n_all, d), f32),
        grid=(n // tile,),
        in_specs=in_specs,
        out_specs=rows,
        input_output_aliases=aliases,
        compiler_params=_cparams(("parallel",)),
        name="ffn_out_ln3",
    )(*args)


def _layer(x, mem, positions, w_in, mu_shift, w_decay_up, w0, a_up, a0, g_up, k_k, k_a, r_k, lnx_g, lnx_b, sinks,
           w_o, ln1_g, ln1_b, wm_q, wm_kv, wm_o, ln2_g, ln2_b, w_router, router_bias, we_gate, we_up, we_down,
           ws_gate, ws_up, ws_down, ln3_g, ln3_b, *, layer, alpha):
    b, s, d = x.shape
    n = b * s
    xf = x.reshape(n, d)
    w_in_b = w_in.astype(bf16)
    u_r, u_s = _proj(xf, [w_in_b[:, :RWKV_COLS], w_in_b[:, RWKV_COLS:]], [f32, f32], tile=512)
    rwkv_tt = 256
    experts = (we_gate, we_up, we_down)
    if N_EXPERTS % (b * (s // min(rwkv_tt, s))) == 0:
        y_r, experts_b = _rwkv(u_r.reshape(b, s, RWKV_COLS), mu_shift, w_decay_up, w0, a_up, a0, g_up, k_k, k_a, r_k,
                               lnx_g, lnx_b, tt=rwkv_tt, side=experts, layer=layer)
    else:
        y_r, _ = _rwkv(u_r.reshape(b, s, RWKV_COLS), mu_shift, w_decay_up, w0, a_up, a0, g_up, k_k, k_a, r_k,
                       lnx_g, lnx_b, tt=rwkv_tt)
        experts_b = [w[layer].astype(bf16) for w in experts]
    y_s = _swa(u_s.reshape(b, s, SWA_COLS), positions, sinks)
    m = mem.shape[1]
    (kv,) = _proj(mem.reshape(b * m, d), [wm_kv.astype(bf16)], [bf16], tile=512)
    x2, x2_rows = _mix_xattn(y_r, y_s, w_o, x, ln1_g, ln1_b, kv.reshape(b, m, 2 * d), wm_q, wm_o, ln2_g, ln2_b,
                             alpha, tile=512)
    x2 = x2.reshape(n, d)
    nc = d // LANES
    nw = nc // 2
    parts = MOE_PARTS if n % (MOE_PARTS * SC_CORES * SC_SUBCORES * SC_INDEX_GROUP) == 0 else 1
    npart = n // parts
    x3 = None
    for part in range(parts):
        row0 = part * npart
        e_t, g_t, r_t, counts = _router(x2, w_router, router_bias, tile=512, row0=row0, n=npart)
        starts, _, n_slots = _expert_layout(counts, npart * TOP_K, GMM_TILE)
        xs, gs = _dispatch_sc(x2_rows.reshape(n, nw, LANES), e_t, r_t, g_t, starts, n_slots, tok0=row0)
        ys = _gmm(xs.reshape(n_slots * nw, LANES), gs, *experts_b, counts, tm=GMM_TILE)
        routed = _combine_sc(ys.reshape(n_slots, nc, LANES), e_t, r_t, starts, npart)
        x3 = _ffn_out(x2, routed.reshape(npart * nc, LANES), ws_gate, ws_up, ws_down, ln3_g, ln3_b, alpha, tile=512,
                      row0=row0, prev=x3)
    return x3.reshape(b, s, d)


def kernel(x, mem, positions, w_in, mu_shift, w_decay_up, w0, a_up, a0, g_up, k_k, k_a, r_k, lnx_g, lnx_b, sinks, w_o, ln1_g, ln1_b, wm_q, wm_kv, wm_o, ln2_g, ln2_b, w_router, router_bias, we_gate, we_up, we_down, ws_gate, ws_up, ws_down, ln3_g, ln3_b):
    depth = w_in.shape[0]
    alpha = (2 * depth) ** 0.25
    for l in range(depth):
        x = _layer(x, mem, positions, w_in[l], mu_shift[l], w_decay_up[l], w0[l], a_up[l], a0[l], g_up[l], k_k[l],
                   k_a[l], r_k[l], lnx_g[l], lnx_b[l], sinks[l], w_o[l], ln1_g[l], ln1_b[l], wm_q[l], wm_kv[l],
                   wm_o[l], ln2_g[l], ln2_b[l], w_router[l], router_bias[l], we_gate, we_up, we_down,
                   ws_gate[l], ws_up[l], ws_down[l], ln3_g[l], ln3_b[l], layer=l, alpha=alpha)
    return x
```

```python
import functools

import jax
import jax.numpy as jnp
from jax import lax
from jax.experimental import pallas as pl
from jax.experimental.pallas import tpu as pltpu
from jax.experimental.pallas import tpu_sc as plsc

f32 = jnp.float32
bf16 = jnp.bfloat16
i32 = jnp.int32

RWKV_HEADS = 8
HEAD_DIM = 64
RWKV_WIDTH = RWKV_HEADS * HEAD_DIM
DECAY_RANK = 64
AAA_RANK = 64
GATE_RANK = 128
RWKV_COLS = 3 * RWKV_WIDTH + DECAY_RANK + AAA_RANK + GATE_RANK
SWA_Q_HEADS = 8
SWA_KV_HEADS = 2
SWA_GROUP = SWA_Q_HEADS // SWA_KV_HEADS
SWA_WIDTH = SWA_Q_HEADS * HEAD_DIM
SWA_KV_WIDTH = SWA_KV_HEADS * HEAD_DIM
SWA_COLS = SWA_WIDTH + 2 * SWA_KV_WIDTH
WINDOW = 128
ROPE_THETA = 10000.0
MEM_HEADS = 4
N_EXPERTS = 256
TOP_K = 8
N_GROUPS = 8
GROUP_SIZE = N_EXPERTS // N_GROUPS
TOPK_GROUPS = 4
ROUTED_SCALE = 2.5
LN_EPS = 1e-5
GN_EPS = 64e-5
NEG_INF = -1e30

LANES = 128
SUBLANES = 8
WKV_CHUNK = 64
WKV_GROUP = 2
GMM_TILE = 512
VMEM_LIMIT = 56 * 1024 * 1024

SC_CORES = 2
SC_SUBCORES = 16
SC_LANES = 16
SC_INDEX_GROUP = 128
SC_CHUNK_BYTES = 256 * 1024
MOE_PARTS = 2


def _sc_chunk_rows(nc, lanes, dtype):
    return min(SC_INDEX_GROUP, SC_CHUNK_BYTES // (nc * lanes * jnp.dtype(dtype).itemsize))


def _cparams(sem):
    return pltpu.CompilerParams(dimension_semantics=sem, vmem_limit_bytes=VMEM_LIMIT)


def _const_spec(shape):
    nd = len(shape)
    return pl.BlockSpec(shape, lambda *_: (0,) * nd)


def _dot(a, b):
    return jnp.dot(a, b, preferred_element_type=f32)


def _dot_nt(a, b):
    return lax.dot_general(a, b, (((1,), (1,)), ((), ())), preferred_element_type=f32)


def _dot_tn(a, b):
    return lax.dot_general(a, b, (((0,), (0,)), ((), ())), preferred_element_type=f32)


def _split2(x):
    hi = x.astype(bf16)
    lo = (x - hi.astype(f32)).astype(bf16)
    return hi, lo


def _seg_sums(xs, seg_b):
    parts = []
    for x in xs:
        parts.extend(_split2(x))
    out = _dot(jnp.concatenate(parts, axis=0), seg_b)
    t = xs[0].shape[0]
    return [out[2 * i * t:(2 * i + 1) * t] + out[(2 * i + 1) * t:(2 * i + 2) * t] for i in range(len(xs))]


def _dot_hp(a, b):
    ah, al = _split2(a)
    bh, bl = _split2(b)
    return _dot(ah, bh) + _dot(ah, bl) + _dot(al, bh)


def _dot_exact_lhs(m_bf16, x):
    hi, lo = _split2(x)
    return _dot(m_bf16, hi) + _dot(m_bf16, lo)


def _sigmoid(x):
    return 1.0 / (1.0 + jnp.exp(-x))


def _lane_chunk(n_rows, n_chunks, c, row0=0):
    return (pl.ds(row0 * n_chunks + c, n_rows, stride=n_chunks), slice(None))


def _pack_bf16_pairs(x):
    chunks = []
    for j in range(x.shape[1] // (2 * LANES)):
        lo = x[:, 2 * j * LANES:(2 * j + 1) * LANES].astype(bf16).astype(f32)
        hi = x[:, (2 * j + 1) * LANES:(2 * j + 2) * LANES].astype(bf16).astype(f32)
        chunks.append(lax.bitcast_convert_type(hi, i32) | lax.shift_right_logical(lax.bitcast_convert_type(lo, i32), 16))
    return chunks


def _unpack_bf16_pairs(chunks):
    cols = []
    for w in chunks:
        cols.append(lax.bitcast_convert_type(lax.shift_left(w, 16), f32))
        cols.append(lax.bitcast_convert_type(w & jnp.int32(-65536), f32))
    return jnp.concatenate(cols, axis=1)


def _layer_norm(h, g, b):
    mu = jnp.mean(h, axis=-1, keepdims=True)
    d = h - mu
    var = jnp.mean(d * d, axis=-1, keepdims=True)
    return d * lax.rsqrt(var + LN_EPS) * g + b


def _proj_body(*refs, n_out):
    x_ref = refs[0]
    w_refs = refs[1:1 + n_out]
    o_refs = refs[1 + n_out:]
    xb = x_ref[...].astype(bf16)
    for w_ref, o_ref in zip(w_refs, o_refs):
        o_ref[...] = _dot(xb, w_ref[...]).astype(o_ref.dtype)


def _proj(x, ws, out_dtypes, tile):
    n, k = x.shape
    tile = min(tile, n)
    outs = pl.pallas_call(
        functools.partial(_proj_body, n_out=len(ws)),
        out_shape=[jax.ShapeDtypeStruct((n, w.shape[1]), dt) for w, dt in zip(ws, out_dtypes)],
        grid=(n // tile,),
        in_specs=[pl.BlockSpec((tile, k), lambda i: (i, 0))] + [_const_spec(w.shape) for w in ws],
        out_specs=[pl.BlockSpec((tile, w.shape[1]), lambda i: (i, 0)) for w in ws],
        compiler_params=_cparams(("parallel",)),
        name="proj",
    )(x, *ws)
    return outs


def _wkv_chunks(chains, states, masks):
    bd_b, bd, strict, incl, eye, eye_full = masks
    c, n = chains[0][1].shape
    nch = len(chains)

    def stack(x_b):
        return jnp.where(bd_b, jnp.concatenate([x_b] * WKV_GROUP, axis=0), jnp.zeros((), bf16))

    cast = [tuple(x.astype(bf16) for x in ch[1:6]) for ch in chains]
    v_s = [stack(cb[4]) for cb in cast]
    g = [_dot_nt(jnp.concatenate([cb[0], cb[3]], axis=0), jnp.concatenate([stack(cb[1]), stack(cb[2])], axis=0))
         for cb in cast]
    l_ak = [jnp.where(strict, gi[:c, n:], 0.0).astype(bf16) for gi in g]
    m_rb = [jnp.where(incl, gi[c:, :n], 0.0).astype(bf16) for gi in g]
    m_rk = [jnp.where(incl, gi[c:, n:], 0.0).astype(bf16) for gi in g]
    x = [jnp.where(strict, gi[:c, :n], 0.0) for gi in g]
    t = [eye + xi for xi in x]
    for _ in range(5):
        xb = [xi.astype(bf16) for xi in x]
        x = [_dot(xi, stack(xi)) for xi in xb]
        t = [ti + _dot(ti.astype(bf16), stack(xi.astype(bf16))) for ti, xi in zip(t, x)]
    lakv = [_dot(l_ak[i], v_s[i]).astype(bf16) for i in range(nch)]
    au = [_dot(t[i].astype(bf16), jnp.concatenate([stack(cast[i][0]), stack(lakv[i])], axis=1))
          for i in range(nch)]
    abar = [a[:, :n].astype(bf16) for a in au]
    ubar = [a[:, n:].astype(bf16) for a in au]
    ry = [_dot(m_rb[i], jnp.concatenate([stack(abar[i]), stack(ubar[i])], axis=1)) for i in range(nch)]
    r_bar = [(chains[i][4] + ry[i][:, :n]).astype(bf16) for i in range(nch)]
    y_bar = [ry[i][:, n:] + _dot(m_rk[i], v_s[i]) for i in range(nch)]
    p = [((eye_full + jnp.where(bd, _dot_tn(abar[i], cast[i][1]), 0.0)) * chains[i][6]).astype(bf16)
         for i in range(nch)]
    q = []
    for i in range(nch):
        q_bd = jnp.where(bd, _dot_tn(jnp.concatenate([ubar[i], cast[i][4]], axis=0),
                                     jnp.concatenate([cast[i][1], cast[i][2]], axis=0)), 0.0)
        qi = q_bd[0:c]
        for h in range(1, WKV_GROUP):
            qi = qi + q_bd[h * c:(h + 1) * c]
        q.append(qi * chains[i][6])
    states = list(states)
    ys = []
    for i in range(nch):
        gi = chains[i][0]
        s_b = states[gi].astype(bf16)
        ys.append(_dot_nt(r_bar[i], stack(s_b)) + y_bar[i])
        states[gi] = _dot(s_b, p[i]) + q[i]
    return ys, states


def _rwkv_body(u_ref, mu_ref, wdec_ref, w0_ref, aup_ref, a0_ref, gup_ref, kk_ref, ka_ref, rk_ref,
               lng_ref, lnb_ref, seg_ref, tri_ref, *rest, tt, n_side):
    side_in = rest[:n_side]
    y_ref = rest[n_side]
    side_out = rest[n_side + 1:2 * n_side + 1]
    state_ref, carry_ref = rest[2 * n_side + 1:]
    j = pl.program_id(1)
    for src, dst in zip(side_in, side_out):
        dst[...] = src[...].astype(dst.dtype)

    @pl.when(j == 0)
    def _():
        state_ref[...] = jnp.zeros_like(state_ref)
        carry_ref[...] = jnp.zeros_like(carry_ref)

    w = RWKV_WIDTH
    u = u_ref[...]
    row = lax.broadcasted_iota(i32, u.shape, 0)
    prev = jnp.where(row == 0, carry_ref[0:1, :], pltpu.roll(u, 1, axis=0))
    carry_ref[0:1, :] = u[tt - 1:tt, :]
    us = u + (prev - u) * mu_ref[...]
    r = us[:, 0:w]
    k = us[:, w:2 * w]
    v = us[:, 2 * w:3 * w]
    wa = us[:, 3 * w:3 * w + DECAY_RANK + AAA_RANK]
    gd = us[:, 3 * w + DECAY_RANK + AAA_RANK:]
    z = w0_ref[...] + _dot_hp(jnp.tanh(wa), wdec_ref[...])
    softplus_neg_z = jnp.maximum(-z, 0.0) + jnp.log(1.0 + jnp.exp(-jnp.abs(z)))
    lw = -jnp.exp(-softplus_neg_z - 0.5)
    a = _sigmoid(a0_ref[...] + _dot_hp(wa, aup_ref[...]))
    gate = _dot(_sigmoid(gd).astype(bf16), gup_ref[...].astype(bf16))
    seg = seg_ref[...]
    kk = k * kk_ref[...]
    kmod = k * (1.0 + (a - 1.0) * ka_ref[...])
    kk_sq, bonus_dot = _seg_sums([kk * kk, r * kmod * rk_ref[...]], seg)
    kk = kk / jnp.maximum(jnp.sqrt(kk_sq), 1e-12)
    cum = _dot_exact_lhs(tri_ref[...], lw)
    wc = jnp.exp(cum)
    iwc = jnp.exp(-cum)
    at = -kk * jnp.exp(cum - lw)
    bt = kk * a * iwc
    kt = kmod * iwc
    rt = r * wc

    n = WKV_GROUP * HEAD_DIM
    ri = lax.broadcasted_iota(i32, (n, n), 0)
    ci = lax.broadcasted_iota(i32, (n, n), 1)
    bd = (ri // WKV_CHUNK) == (ci // HEAD_DIM)
    bd_b = jnp.where(bd, 1.0, 0.0).astype(bf16) > 0
    eye_full = jnp.where(ri == ci, 1.0, 0.0).astype(f32)
    ti = lax.broadcasted_iota(i32, (WKV_CHUNK, n), 0)
    si = lax.broadcasted_iota(i32, (WKV_CHUNK, n), 1) % WKV_CHUNK
    masks = (bd_b, bd, ti > si, ti >= si, jnp.where(ti == si, 1.0, 0.0).astype(f32), eye_full)

    n_groups = w // n
    n_chunks = tt // WKV_CHUNK
    chains = []
    for c in range(n_chunks):
        rs = slice(c * WKV_CHUNK, (c + 1) * WKV_CHUNK)
        last = (c + 1) * WKV_CHUNK - 1
        for gi in range(n_groups):
            cs = slice(gi * n, (gi + 1) * n)
            chains.append((gi, at[rs, cs], bt[rs, cs], kt[rs, cs], rt[rs, cs], v[rs, cs], wc[last:last + 1, cs]))
    ys, states = _wkv_chunks(chains, [state_ref[gi] for gi in range(n_groups)], masks)
    for gi in range(n_groups):
        state_ref[gi] = states[gi]
    y = jnp.concatenate([jnp.concatenate(ys[c * n_groups:(c + 1) * n_groups], axis=1) for c in range(n_chunks)],
                        axis=0)

    inv_n = 1.0 / HEAD_DIM
    d = y - _seg_sums([y], seg)[0] * inv_n
    var = _seg_sums([d * d], seg)[0] * inv_n
    yn = d * lax.rsqrt(var + GN_EPS) * lng_ref[...] + lnb_ref[...]
    y_ref[...] = ((yn + bonus_dot * v) * gate).astype(y_ref.dtype)


def _rwkv(u_r, mu_shift, w_decay_up, w0, a_up, a0, g_up, k_k, k_a, r_k, lnx_g, lnx_b, tt, side=(), layer=0):
    b, s, cols = u_r.shape
    tt = min(tt, s)
    n_steps = b * (s // tt)
    assert all(a.shape[1] % n_steps == 0 for a in side)
    w = RWKV_WIDTH
    row = lambda p: p.reshape(1, -1).astype(f32)
    wdec = jnp.concatenate([w_decay_up, jnp.zeros((AAA_RANK, w), f32)], axis=0)
    aup = jnp.concatenate([jnp.zeros((DECAY_RANK, w), f32), a_up], axis=0)
    hid = jnp.arange(w) // HEAD_DIM
    seg = (hid[:, None] == hid[None, :]).astype(bf16)
    ti = jnp.arange(tt)
    tri = ((ti[:, None] // WKV_CHUNK == ti[None, :] // WKV_CHUNK) & (ti[:, None] >= ti[None, :])).astype(bf16)
    params = [row(mu_shift), wdec, row(w0), aup, row(a0), g_up, row(k_k), row(k_a), row(r_k), row(lnx_g),
              row(lnx_b), seg, tri]
    n = WKV_GROUP * HEAD_DIM
    nj = s // tt
    side_in = [pl.BlockSpec((None, a.shape[1] // n_steps) + a.shape[2:], lambda bi, j: (layer, bi * nj + j, 0, 0))
               for a in side]
    side_out = [pl.BlockSpec((a.shape[1] // n_steps,) + a.shape[2:], lambda bi, j: (bi * nj + j, 0, 0))
                for a in side]
    outs = pl.pallas_call(
        functools.partial(_rwkv_body, tt=tt, n_side=len(side)),
        out_shape=[jax.ShapeDtypeStruct((b, s, w), bf16)] + [jax.ShapeDtypeStruct(a.shape[1:], bf16) for a in side],
        grid=(b, nj),
        in_specs=([pl.BlockSpec((None, tt, cols), lambda bi, j: (bi, j, 0))] + [_const_spec(p.shape) for p in params]
                  + side_in),
        out_specs=[pl.BlockSpec((None, tt, w), lambda bi, j: (bi, j, 0))] + side_out,
        scratch_shapes=[pltpu.VMEM((w // n, HEAD_DIM, n), f32), pltpu.VMEM((SUBLANES, cols), f32)],
        compiler_params=_cparams(("parallel", "arbitrary")),
        name="rwkv7",
    )(u_r, *params, *side)
    return outs[0], outs[1:]


def _swa_body(sink_ref, u_ref, pos_ref, invf_ref, o_ref, kprev_ref, vprev_ref, *, nwin):
    j = pl.program_id(1)

    @pl.when(j == 0)
    def _():
        kprev_ref[...] = jnp.zeros_like(kprev_ref)
        vprev_ref[...] = jnp.zeros_like(vprev_ref)

    wq = SWA_WIDTH
    kvw = SWA_KV_WIDTH
    half = HEAD_DIM // 2
    gw = SWA_GROUP * HEAD_DIM
    rows = SWA_GROUP * WINDOW
    lane_kv = lax.broadcasted_iota(i32, (WINDOW, kvw), 1)
    ri = lax.broadcasted_iota(i32, (rows, WINDOW), 0)
    ci = lax.broadcasted_iota(i32, (rows, WINDOW), 1)
    t_idx = ri % WINDOW
    mask_cur = ci <= t_idx
    mask_cur_b = jnp.where(mask_cur, 1.0, 0.0).astype(bf16) > 0
    rb = lax.broadcasted_iota(i32, (rows, 1), 0) // WINDOW
    bd = (lax.broadcasted_iota(i32, (rows, gw), 0) // WINDOW) == (lax.broadcasted_iota(i32, (rows, gw), 1) // HEAD_DIM)

    def rope(x, c, s):
        n = x.shape[1]
        lane = lax.broadcasted_iota(i32, x.shape, 1)
        rot = jnp.where((lane % HEAD_DIM) < half, -pltpu.roll(x, n - half, axis=1), pltpu.roll(x, half, axis=1))
        return x * c + rot * s

    def rep(x, gi):
        sw = pltpu.roll(x, HEAD_DIM, axis=1)
        one = jnp.where((lane_kv // HEAD_DIM) == gi, x, sw)
        return jnp.concatenate([one] * (gw // kvw), axis=1).astype(bf16)

    k_prev = kprev_ref[...]
    v_prev = vprev_ref[...]
    for wi in range(nwin):
        rs = slice(wi * WINDOW, (wi + 1) * WINDOW)
        u = u_ref[rs, :]
        ang_t = invf_ref[...] * pos_ref[wi].astype(f32)
        cos = jnp.concatenate([jnp.cos(ang_t)] * (LANES // half), axis=0).T
        sin = jnp.concatenate([jnp.sin(ang_t)] * (LANES // half), axis=0).T
        q = rope(u[:, :wq] * (HEAD_DIM ** -0.5), jnp.concatenate([cos] * (wq // LANES), axis=1),
                 jnp.concatenate([sin] * (wq // LANES), axis=1))
        k_cur = rope(u[:, wq:wq + kvw], cos, sin)
        v_cur = u[:, wq + kvw:]
        prev_bias = jnp.where(j > 0, 0.0, NEG_INF) if wi == 0 else 0.0
        outs = []
        for gi in range(SWA_KV_HEADS):
            qg = q[:, gi * gw:(gi + 1) * gw]
            q_bd = jnp.where(bd, jnp.concatenate([qg] * SWA_GROUP, axis=0), 0.0).astype(bf16)
            s = jnp.where(mask_cur, _dot_nt(q_bd, rep(k_cur, gi)), _dot_nt(q_bd, rep(k_prev, gi)) + prev_bias)
            sink = jnp.zeros((rows, 1), f32)
            for h in range(SWA_GROUP):
                sink = jnp.where(rb == h, sink_ref[gi * SWA_GROUP + h], sink)
            m = jnp.maximum(jnp.max(s, axis=-1, keepdims=True), sink)
            p = jnp.exp(s - m)
            denom = jnp.sum(p, axis=-1, keepdims=True) + jnp.exp(sink - m)
            p_b = p.astype(bf16)
            zero_b = jnp.zeros((), bf16)
            o_bd = (_dot(jnp.where(mask_cur_b, p_b, zero_b), rep(v_cur, gi))
                    + _dot(jnp.where(mask_cur_b, zero_b, p_b), rep(v_prev, gi)))
            o_bd = jnp.where(bd, o_bd * (1.0 / denom), 0.0)
            og = o_bd[0:WINDOW]
            for h in range(1, SWA_GROUP):
                og = og + o_bd[h * WINDOW:(h + 1) * WINDOW]
            outs.append(og)
        o_ref[rs, :] = jnp.concatenate(outs, axis=1).astype(o_ref.dtype)
        k_prev, v_prev = k_cur, v_cur
    kprev_ref[...] = k_prev
    vprev_ref[...] = v_prev


def _swa(u_s, positions, sinks, nwin=2):
    b, s, cols = u_s.shape
    half = HEAD_DIM // 2
    nwin = nwin if s % (nwin * WINDOW) == 0 else 1
    tile = nwin * WINDOW
    inv_freq = (ROPE_THETA ** (-jnp.arange(0, HEAD_DIM, 2, dtype=f32) / HEAD_DIM)).reshape(half, 1)
    pos = positions.reshape(b, s // WINDOW, 1, WINDOW).astype(i32)
    return pl.pallas_call(
        functools.partial(_swa_body, nwin=nwin),
        out_shape=jax.ShapeDtypeStruct((b, s, SWA_WIDTH), bf16),
        grid=(b, s // tile),
        in_specs=[pl.BlockSpec(memory_space=pltpu.SMEM),
                  pl.BlockSpec((None, tile, cols), lambda bi, j: (bi, j, 0)),
                  pl.BlockSpec((None, nwin, 1, WINDOW), lambda bi, j: (bi, j, 0, 0)),
                  _const_spec((half, 1))],
        out_specs=pl.BlockSpec((None, tile, SWA_WIDTH), lambda bi, j: (bi, j, 0)),
        scratch_shapes=[pltpu.VMEM((WINDOW, SWA_KV_WIDTH), f32), pltpu.VMEM((WINDOW, SWA_KV_WIDTH), f32)],
        compiler_params=_cparams(("parallel", "arbitrary")),
        name="swa",
    )(sinks.astype(f32), u_s, pos, inv_freq)


def _mix_xattn_body(ya_ref, yb_ref, wa_ref, wb_ref, xin_ref, g1_ref, b1_ref, kv_ref, wq_ref, wo_ref, g_ref, b_ref,
                    o_ref, o3_ref, *, alpha):
    mix = _dot(ya_ref[...], wa_ref[...]) + _dot(yb_ref[...], wb_ref[...])
    x = _layer_norm(alpha * xin_ref[...] + mix, g1_ref[...], b1_ref[...])
    d = x.shape[1]
    hd = d // MEM_HEADS
    q = _dot(x.astype(bf16), wq_ref[...]) * (hd ** -0.5)
    kv = kv_ref[...]
    outs = []
    for h in range(MEM_HEADS):
        qh = q[:, h * hd:(h + 1) * hd].astype(bf16)
        kh = kv[:, h * hd:(h + 1) * hd]
        vh = kv[:, d + h * hd:d + (h + 1) * hd]
        s = _dot_nt(qh, kh)
        p = jnp.exp(s - jnp.max(s, axis=-1, keepdims=True))
        l = jnp.sum(p, axis=-1, keepdims=True)
        outs.append(_dot(p.astype(bf16), vh) / l)
    o = jnp.concatenate(outs, axis=1)
    xa = _dot(o.astype(bf16), wo_ref[...])
    y = _layer_norm(alpha * x + xa, g_ref[...], b_ref[...])
    o_ref[...] = y
    words = _pack_bf16_pairs(y)
    for j, w in enumerate(words):
        o3_ref[_lane_chunk(x.shape[0], len(words), j)] = w


def _mix_xattn(ya, yb, w_o, x, g1, b1, kv, wm_q, wm_o, g2, b2, alpha, tile):
    bsz, s, d = x.shape
    m = kv.shape[1]
    tile = min(tile, s)
    nj = s // tile
    wa = w_o[:ya.shape[2]].astype(bf16)
    wb = w_o[ya.shape[2]:].astype(bf16)
    wq = wm_q.astype(bf16)
    wo = wm_o.astype(bf16)
    rows = lambda width: pl.BlockSpec((None, tile, width), lambda bi, j: (bi, j, 0))
    vec = _const_spec((1, d))
    return pl.pallas_call(
        functools.partial(_mix_xattn_body, alpha=alpha),
        out_shape=[jax.ShapeDtypeStruct((bsz, s, d), f32),
                   jax.ShapeDtypeStruct((bsz * s * (d // (2 * LANES)), LANES), i32)],
        grid=(bsz, nj),
        in_specs=[rows(ya.shape[2]), rows(yb.shape[2]), _const_spec(wa.shape), _const_spec(wb.shape), rows(d), vec, vec,
                  pl.BlockSpec((None, m, 2 * d), lambda bi, j: (bi, 0, 0)),
                  _const_spec(wq.shape), _const_spec(wo.shape), vec, vec],
        out_specs=[rows(d), pl.BlockSpec((tile * (d // (2 * LANES)), LANES), lambda bi, j: (bi * nj + j, 0))],
        compiler_params=_cparams(("parallel", "parallel")),
        name="mix_xattn",
    )(ya, yb, wa, wb, x, g1.reshape(1, d), b1.reshape(1, d), kv, wq, wo, g2.reshape(1, d), b2.reshape(1, d))


def _router_body(x_ref, wt_ref, bias_ref, upper_ref, e_ref, g_ref, r_ref, cnt_out_ref, cnt_ref, *, t):
    @pl.when(pl.program_id(0) == 0)
    def _():
        cnt_ref[...] = jnp.zeros_like(cnt_ref)

    xh, xl = _split2(x_ref[...])
    wh, wl = _split2(wt_ref[...])
    logits = _dot_nt(wh, xh) + _dot_nt(wh, xl) + _dot_nt(wl, xh)
    scores = _sigmoid(logits)
    biased = scores + bias_ref[...][:, 0:1]
    ne = N_EXPERTS
    neg = -jnp.inf

    def top1(vals):
        rows = lax.broadcasted_iota(i32, vals.shape, 0).astype(f32)
        m = jnp.max(vals, axis=0, keepdims=True)
        idx = jnp.min(jnp.where(vals == m, rows, float(vals.shape[0])), axis=0, keepdims=True)
        return m, idx, rows == idx

    gscores = []
    for gi in range(N_GROUPS):
        blk = biased[gi * GROUP_SIZE:(gi + 1) * GROUP_SIZE, :]
        m1, _, hit = top1(blk)
        m2 = jnp.max(jnp.where(hit, neg, blk), axis=0, keepdims=True)
        gscores.append(m1 + m2)
    gs = jnp.concatenate(gscores, axis=0)
    gsel = jnp.zeros(gs.shape, f32)
    for _ in range(TOPK_GROUPS):
        _, _, hit = top1(gs)
        gsel = jnp.where(hit, 1.0, gsel)
        gs = jnp.where(hit, neg, gs)
    emask = jnp.concatenate(
        [jnp.broadcast_to(gsel[gi:gi + 1, :], (GROUP_SIZE, t)) for gi in range(N_GROUPS)], axis=0) > 0.5
    cand = jnp.where(emask, biased, NEG_INF)
    idxs, sels = [], []
    chosen = jnp.zeros((ne, t), f32)
    for _ in range(TOP_K):
        _, idx, hit = top1(cand)
        idxs.append(idx)
        sels.append(jnp.sum(jnp.where(hit, scores, 0.0), axis=0, keepdims=True))
        chosen = chosen + jnp.where(hit, 1.0, 0.0)
        cand = jnp.where(hit, neg, cand)
    sel = jnp.concatenate(sels, axis=0)
    g_ref[...] = sel / jnp.sum(sel, axis=0, keepdims=True) * ROUTED_SCALE
    e_ref[...] = jnp.concatenate(idxs, axis=0).astype(i32)
    before = _dot(chosen.astype(bf16), upper_ref[...]) + cnt_ref[...][:, 0:1]
    rows = lax.broadcasted_iota(i32, (ne, t), 0).astype(f32)
    ranks = [jnp.sum(jnp.where(rows == idx, before, 0.0), axis=0, keepdims=True) for idx in idxs]
    r_ref[...] = jnp.concatenate(ranks, axis=0).astype(i32)
    cnt_ref[...] = cnt_ref[...] + jnp.sum(chosen, axis=1, keepdims=True)
    cnt_out_ref[...] = cnt_ref[...].astype(i32)


def _router(x2, w_router, router_bias, tile, row0=0, n=None):
    d = x2.shape[1]
    n = x2.shape[0] if n is None else n
    ne = N_EXPERTS
    t = min(tile, n)
    assert row0 % t == 0 and n % t == 0
    first = row0 // t
    wt = w_router.T
    bias = jnp.broadcast_to(router_bias.reshape(ne, 1).astype(f32), (ne, LANES))
    ti = jnp.arange(t)
    upper = (ti[:, None] < ti[None, :]).astype(bf16)
    cols = pl.BlockSpec((TOP_K, t), lambda i: (0, i))
    e_t, g_t, r_t, cnt = pl.pallas_call(
        functools.partial(_router_body, t=t),
        out_shape=[jax.ShapeDtypeStruct((TOP_K, n), i32), jax.ShapeDtypeStruct((TOP_K, n), f32),
                   jax.ShapeDtypeStruct((TOP_K, n), i32), jax.ShapeDtypeStruct((ne, LANES), i32)],
        grid=(n // t,),
        in_specs=[pl.BlockSpec((t, d), lambda i: (i + first, 0)), _const_spec((ne, d)), _const_spec((ne, LANES)),
                  _const_spec((t, t))],
        out_specs=[cols, cols, cols, _const_spec((ne, LANES))],
        scratch_shapes=[pltpu.VMEM((ne, LANES), f32)],
        compiler_params=_cparams(("arbitrary",)),
        name="router",
    )(x2, wt, bias, upper)
    return e_t, g_t, r_t, cnt[:, 0]


def _dispatch_sc(x3d, e_t, r_t, g_t, starts, n_slots, tok0=0):
    _, nc, lanes = x3d.shape
    n = e_t.shape[1]
    workers = SC_CORES * SC_SUBCORES
    per_w = n // workers
    tc = _sc_chunk_rows(nc, lanes, x3d.dtype)
    assert n % (workers * SC_INDEX_GROUP) == 0 and SC_INDEX_GROUP % tc == 0
    mesh = plsc.VectorSubcoreMesh(core_axis_name="c", subcore_axis_name="s")

    @functools.partial(
        pl.kernel, mesh=mesh,
        out_type=[jax.ShapeDtypeStruct((n_slots, nc, lanes), x3d.dtype),
                  jax.ShapeDtypeStruct((n_slots,), f32)],
        scratch_types=[
            pltpu.VMEM((tc, nc, lanes), x3d.dtype),
            pltpu.VMEM((TOP_K, SC_INDEX_GROUP), i32),
            pltpu.VMEM((TOP_K, SC_INDEX_GROUP), i32),
            pltpu.VMEM((TOP_K, SC_INDEX_GROUP), f32),
            pltpu.VMEM((TOP_K, tc), i32),
            pltpu.VMEM((TOP_K, tc), f32),
            pltpu.VMEM((N_EXPERTS,), i32),
            pltpu.SemaphoreType.DMA,
        ],
        compiler_params=pltpu.CompilerParams(use_tc_tiling_on_sc=True, needs_layout_passes=False),
    )
    def dispatch(x_hbm, e_hbm, r_hbm, g_hbm, st_hbm, o_hbm, gs_hbm, rows_v, e_v, r_v, g_v, slot_v, gate_v, st_v, sem):
        wid = lax.axis_index("s") * SC_CORES + lax.axis_index("c")
        pltpu.sync_copy(st_hbm, st_v)

        @pl.loop(0, per_w // SC_INDEX_GROUP)
        def _(gi):
            base = wid * per_w + gi * SC_INDEX_GROUP
            pltpu.sync_copy(e_hbm.at[:, pl.ds(base, SC_INDEX_GROUP)], e_v)
            pltpu.sync_copy(r_hbm.at[:, pl.ds(base, SC_INDEX_GROUP)], r_v)
            pltpu.sync_copy(g_hbm.at[:, pl.ds(base, SC_INDEX_GROUP)], g_v)
            for h in range(SC_INDEX_GROUP // tc):
                off = h * tc
                pltpu.sync_copy(x_hbm.at[pl.ds(tok0 + base + off, tc)], rows_v)
                for kk in range(TOP_K):
                    for j in range(tc // SC_LANES):
                        src = pl.ds(off + j * SC_LANES, SC_LANES)
                        dst = pl.ds(j * SC_LANES, SC_LANES)
                        slot_v[kk, dst] = r_v[kk, src] + plsc.load_gather(st_v, [e_v[kk, src]])
                        gate_v[kk, dst] = g_v[kk, src]
                copies = [pltpu.async_copy(rows_v, o_hbm.at[slot_v.at[kk]], sem) for kk in range(TOP_K)]
                copies += [pltpu.async_copy(gate_v.at[kk], gs_hbm.at[slot_v.at[kk]], sem) for kk in range(TOP_K)]
                for cp in copies:
                    cp.wait()

    return dispatch(x3d, e_t, r_t, g_t, starts)


def _gmm_body(gid_ref, tid_ref, nrows_ref, newg_ref, nextg_ref, ord_ref, x_ref, g_ref,
              wg_hbm, wu_hbm, wd_hbm, o_ref, wg_b, wu_b, wd_b, sems, *, tm):
    v = pl.program_id(0)
    n_real = nrows_ref[v]
    nc = wg_b.shape[1] // LANES
    slot = ord_ref[v] % 2

    def weight_copies(expert, dst_slot):
        return [pltpu.make_async_copy(src.at[expert], dst.at[dst_slot], sems.at[dst_slot])
                for src, dst in ((wg_hbm, wg_b), (wu_hbm, wu_b), (wd_hbm, wd_b))]

    @pl.when(newg_ref[v] == 1)
    def _():
        @pl.when(v == 0)
        def _():
            for cp in weight_copies(gid_ref[v], slot):
                cp.start()

        for cp in weight_copies(gid_ref[v], slot):
            cp.wait()

        @pl.when(nextg_ref[v] >= 0)
        def _():
            for cp in weight_copies(nextg_ref[v], 1 - slot):
                cp.start()

    @pl.when(n_real > 0)
    def _():
        nw = nc // 2
        real = lax.broadcasted_iota(i32, (tm, 1), 0) < n_real
        x = _unpack_bf16_pairs([x_ref[_lane_chunk(tm, nw, j)] for j in range(nw)])
        x = jnp.where(real, x, 0.0).astype(bf16)
        hg = _dot(x, wg_b[slot])
        h = hg * _sigmoid(hg) * _dot(x, wu_b[slot])
        g = g_ref[...]
        g_cols = jnp.concatenate([g, jnp.zeros((SUBLANES - g.shape[0], LANES), f32)], axis=0).T
        g_col = jnp.concatenate([g_cols[:, r:r + 1] for r in range(tm // LANES)], axis=0)
        h = h * jnp.where(real, g_col, 0.0)
        y = _dot(h.astype(bf16), wd_b[slot])
        for c in range(nc):
            o_ref[_lane_chunk(tm, nc, c)] = y[:, c * LANES:(c + 1) * LANES]


def _expert_layout(counts, n_assign, tm):
    tiles = (counts + tm - 1) // tm
    starts = ((jnp.cumsum(tiles) - tiles) * tm).astype(i32)
    capacity = (n_assign // tm + counts.shape[0]) * tm
    return starts, tiles, capacity


def _gmm(xs, gates, we_gate, we_up, we_down, counts, tm):
    ne, d, de = we_gate.shape
    nc = d // LANES
    nw = nc // 2
    n_tiles = xs.shape[0] // (nw * tm)
    assert tm % LANES == 0 and tm // LANES <= SUBLANES
    gates3 = gates.reshape(n_tiles, tm // LANES, LANES)
    tiles = (counts + tm - 1) // tm
    tile_end = jnp.cumsum(tiles)
    tile_start = tile_end - tiles
    vi = jnp.arange(n_tiles, dtype=i32)
    valid = vi < tile_end[-1]
    gid = jnp.minimum(jnp.sum((tile_end[None, :] <= vi[:, None]).astype(i32), axis=1), ne - 1)
    gid = jnp.where(valid, gid, jnp.max(jnp.where(valid, gid, 0)))
    onehot = gid[:, None] == jnp.arange(ne, dtype=i32)[None, :]
    pick = lambda table: jnp.sum(jnp.where(onehot, table[None, :], 0), axis=1)
    tid = jnp.minimum(vi, tile_end[-1] - 1).astype(i32)
    n_real = jnp.where(valid, jnp.clip(pick(counts) - (vi - pick(tile_start)) * tm, 0, tm), 0).astype(i32)
    one = jnp.ones((1,), i32)
    newg = jnp.concatenate([one, (gid[1:] != gid[:-1]).astype(i32)])
    later = gid[None, :] > gid[:, None]
    nextg = jnp.min(jnp.where(later, gid[None, :], ne), axis=1)
    nextg = jnp.where(nextg < ne, nextg, -1).astype(i32)
    order = (jnp.cumsum(newg) - 1).astype(i32)
    rows = lambda chunks: pl.BlockSpec((tm * chunks, LANES), lambda v, g, t, *_: (t[v], 0))
    hbm = pl.BlockSpec(memory_space=pl.ANY)
    return pl.pallas_call(
        functools.partial(_gmm_body, tm=tm),
        out_shape=jax.ShapeDtypeStruct((n_tiles * tm * nc, LANES), f32),
        grid_spec=pltpu.PrefetchScalarGridSpec(
            num_scalar_prefetch=6, grid=(n_tiles,),
            in_specs=[rows(nw), pl.BlockSpec((None, tm // LANES, LANES), lambda v, g, t, *_: (t[v], 0, 0)),
                      hbm, hbm, hbm],
            out_specs=rows(nc),
            scratch_shapes=[pltpu.VMEM((2, d, de), bf16), pltpu.VMEM((2, d, de), bf16), pltpu.VMEM((2, de, d), bf16),
                            pltpu.SemaphoreType.DMA((2,))]),
        compiler_params=_cparams(("arbitrary",)),
        name="moe_experts",
    )(gid, tid, n_real, newg, nextg, order, xs, gates3, we_gate, we_up, we_down)


def _combine_sc(ys3d, e_t, r_t, starts, n):
    _, nc, lanes = ys3d.shape
    workers = SC_CORES * SC_SUBCORES
    per_w = n // workers
    tc = _sc_chunk_rows(nc, lanes, f32)
    assert n % (workers * SC_INDEX_GROUP) == 0 and SC_INDEX_GROUP % tc == 0
    mesh = plsc.VectorSubcoreMesh(core_axis_name="c", subcore_axis_name="s")

    @functools.partial(
        pl.kernel, mesh=mesh,
        out_type=jax.ShapeDtypeStruct((n, nc, lanes), f32),
        scratch_types=[
            pltpu.VMEM((tc, nc, lanes), f32),
            pltpu.VMEM((TOP_K, SC_INDEX_GROUP), i32),
            pltpu.VMEM((TOP_K, SC_INDEX_GROUP), i32),
            pltpu.VMEM((TOP_K, tc), i32),
            pltpu.VMEM((N_EXPERTS,), i32),
            pltpu.SemaphoreType.DMA,
        ],
        compiler_params=pltpu.CompilerParams(use_tc_tiling_on_sc=True, needs_layout_passes=False),
    )
    def combine(y_hbm, e_hbm, r_hbm, st_hbm, o_hbm, acc_v, e_v, r_v, slot_v, st_v, sem):
        wid = lax.axis_index("s") * SC_CORES + lax.axis_index("c")
        pltpu.sync_copy(st_hbm, st_v)

        @pl.loop(0, per_w // SC_INDEX_GROUP)
        def _(gi):
            base = wid * per_w + gi * SC_INDEX_GROUP
            pltpu.sync_copy(e_hbm.at[:, pl.ds(base, SC_INDEX_GROUP)], e_v)
            pltpu.sync_copy(r_hbm.at[:, pl.ds(base, SC_INDEX_GROUP)], r_v)
            for h in range(SC_INDEX_GROUP // tc):
                off = h * tc
                for kk in range(TOP_K):
                    for j in range(tc // SC_LANES):
                        src = pl.ds(off + j * SC_LANES, SC_LANES)
                        slot_v[kk, pl.ds(j * SC_LANES, SC_LANES)] = (
                            r_v[kk, src] + plsc.load_gather(st_v, [e_v[kk, src]]))
                pltpu.async_copy(y_hbm.at[slot_v.at[0]], acc_v, sem).wait()
                copies = [pltpu.async_copy(y_hbm.at[slot_v.at[kk]], acc_v, sem, add=True) for kk in range(1, TOP_K)]
                for cp in copies:
                    cp.wait()
                pltpu.sync_copy(acc_v, o_hbm.at[pl.ds(base + off, tc)])

    return combine(ys3d, e_t, r_t, starts)


def _ffn_out_body(x_ref, r_ref, wg_ref, wu_ref, wd_ref, g_ref, b_ref, *rest, alpha):
    o_ref = rest[-1]
    x = x_ref[...]
    xb = x.astype(bf16)
    hg = _dot(xb, wg_ref[...])
    h = hg * _sigmoid(hg) * _dot(xb, wu_ref[...])
    shared = _dot(h.astype(bf16), wd_ref[...])
    nc = x.shape[1] // LANES
    routed = jnp.concatenate([r_ref[_lane_chunk(x.shape[0], nc, c)] for c in range(nc)], axis=1)
    o_ref[...] = _layer_norm(alpha * x + routed + shared, g_ref[...], b_ref[...])


def _ffn_out(x2, routed_rows, ws_gate, ws_up, ws_down, g, b, alpha, tile, row0=0, prev=None):
    n_all, d = x2.shape
    n = routed_rows.shape[0] // (d // LANES)
    tile = min(tile, n)
    assert row0 % tile == 0 and n % tile == 0
    first = row0 // tile
    wg, wu, wd = ws_gate.astype(bf16), ws_up.astype(bf16), ws_down.astype(bf16)
    rows = pl.BlockSpec((tile, d), lambda i: (i + first, 0))
    in_specs = [rows, pl.BlockSpec((tile * (d // LANES), LANES), lambda i: (i, 0)), _const_spec(wg.shape),
                _const_spec(wu.shape), _const_spec(wd.shape), _const_spec((1, d)), _const_spec((1, d))]
    args = [x2, routed_rows, wg, wu, wd, g.reshape(1, d), b.reshape(1, d)]
    aliases = {}
    if prev is not None:
        in_specs.append(pl.BlockSpec(memory_space=pl.ANY))
        args.append(prev)
        aliases = {len(args) - 1: 0}
    return pl.pallas_call(
        functools.partial(_ffn_out_body, alpha=alpha),
        out_shape=jax.ShapeDtypeStruct((n_all, d), f32),
        grid=(n // tile,),
        in_specs=in_specs,
        out_specs=rows,
        input_output_aliases=aliases,
        compiler_params=_cparams(("parallel",)),
        name="ffn_out_ln3",
    )(*args)


def _layer(x, mem, positions, w_in, mu_shift, w_decay_up, w0, a_up, a0, g_up, k_k, k_a, r_k, lnx_g, lnx_b, sinks,
           w_o, ln1_g, ln1_b, wm_q, wm_kv, wm_o, ln2_g, ln2_b, w_router, router_bias, we_gate, we_up, we_down,
           ws_gate, ws_up, ws_down, ln3_g, ln3_b, *, layer, alpha):
    b, s, d = x.shape
    n = b * s
    xf = x.reshape(n, d)
    w_in_b = w_in.astype(bf16)
    u_r, u_s = _proj(xf, [w_in_b[:, :RWKV_COLS], w_in_b[:, RWKV_COLS:]], [f32, f32], tile=512)
    rwkv_tt = 256
    experts = (we_gate, we_up, we_down)
    if N_EXPERTS % (b * (s // min(rwkv_tt, s))) == 0:
        y_r, experts_b = _rwkv(u_r.reshape(b, s, RWKV_COLS), mu_shift, w_decay_up, w0, a_up, a0, g_up, k_k, k_a, r_k,
                               lnx_g, lnx_b, tt=rwkv_tt, side=experts, layer=layer)
    else:
        y_r, _ = _rwkv(u_r.reshape(b, s, RWKV_COLS), mu_shift, w_decay_up, w0, a_up, a0, g_up, k_k, k_a, r_k,
                       lnx_g, lnx_b, tt=rwkv_tt)
        experts_b = [w[layer].astype(bf16) for w in experts]
    y_s = _swa(u_s.reshape(b, s, SWA_COLS), positions, sinks)
    m = mem.shape[1]
    (kv,) = _proj(mem.reshape(b * m, d), [wm_kv.astype(bf16)], [bf16], tile=512)
    x2, x2_rows = _mix_xattn(y_r, y_s, w_o, x, ln1_g, ln1_b, kv.reshape(b, m, 2 * d), wm_q, wm_o, ln2_g, ln2_b,
                             alpha, tile=512)
    x2 = x2.reshape(n, d)
    nc = d // LANES
    nw = nc // 2
    parts = MOE_PARTS if n % (MOE_PARTS * SC_CORES * SC_SUBCORES * SC_INDEX_GROUP) == 0 else 1
    npart = n // parts
    x3 = None
    for part in range(parts):
        row0 = part * npart
        e_t, g_t, r_t, counts = _router(x2, w_router, router_bias, tile=512, row0=row0, n=npart)
        starts, _, n_slots = _expert_layout(counts, npart * TOP_K, GMM_TILE)
        xs, gs = _dispatch_sc(x2_rows.reshape(n, nw, LANES), e_t, r_t, g_t, starts, n_slots, tok0=row0)
        ys = _gmm(xs.reshape(n_slots * nw, LANES), gs, *experts_b, counts, tm=GMM_TILE)
        routed = _combine_sc(ys.reshape(n_slots, nc, LANES), e_t, r_t, starts, npart)
        x3 = _ffn_out(x2, routed.reshape(npart * nc, LANES), ws_gate, ws_up, ws_down, ln3_g, ln3_b, alpha, tile=512,
                      row0=row0, prev=x3)
    return x3.reshape(b, s, d)


def kernel(x, mem, positions, w_in, mu_shift, w_decay_up, w0, a_up, a0, g_up, k_k, k_a, r_k, lnx_g, lnx_b, sinks, w_o, ln1_g, ln1_b, wm_q, wm_kv, wm_o, ln2_g, ln2_b, w_router, router_bias, we_gate, we_up, we_down, ws_gate, ws_up, ws_down, ln3_g, ln3_b):
    depth = w_in.shape[0]
    alpha = (2 * depth) ** 0.25
    for l in range(depth):
        x = _layer(x, mem, positions, w_in[l], mu_shift[l], w_decay_up[l], w0[l], a_up[l], a0[l], g_up[l], k_k[l],
                   k_a[l], r_k[l], lnx_g[l], lnx_b[l], sinks[l], w_o[l], ln1_g[l], ln1_b[l], wm_q[l], wm_kv[l],
                   wm_o[l], ln2_g[l], ln2_b[l], w_router[l], router_bias[l], we_gate, we_up, we_down,
                   ws_gate[l], ws_up[l], ws_down[l], ln3_g[l], ln3_b[l], layer=l, alpha=alpha)
    return x
```

```python
import functools

import jax
import jax.numpy as jnp
from jax import lax
from jax.experimental import pallas as pl
from jax.experimental.pallas import tpu as pltpu
from jax.experimental.pallas import tpu_sc as plsc

f32 = jnp.float32
bf16 = jnp.bfloat16
i32 = jnp.int32

RWKV_HEADS = 8
HEAD_DIM = 64
RWKV_WIDTH = RWKV_HEADS * HEAD_DIM
DECAY_RANK = 64
AAA_RANK = 64
GATE_RANK = 128
RWKV_COLS = 3 * RWKV_WIDTH + DECAY_RANK + AAA_RANK + GATE_RANK
SWA_Q_HEADS = 8
SWA_KV_HEADS = 2
SWA_GROUP = SWA_Q_HEADS // SWA_KV_HEADS
SWA_WIDTH = SWA_Q_HEADS * HEAD_DIM
SWA_KV_WIDTH = SWA_KV_HEADS * HEAD_DIM
SWA_COLS = SWA_WIDTH + 2 * SWA_KV_WIDTH
WINDOW = 128
ROPE_THETA = 10000.0
MEM_HEADS = 4
N_EXPERTS = 256
TOP_K = 8
N_GROUPS = 8
GROUP_SIZE = N_EXPERTS // N_GROUPS
TOPK_GROUPS = 4
ROUTED_SCALE = 2.5
LN_EPS = 1e-5
GN_EPS = 64e-5
NEG_INF = -1e30

LANES = 128
SUBLANES = 8
WKV_CHUNK = 64
WKV_GROUP = 2
GMM_TILE = 512
VMEM_LIMIT = 56 * 1024 * 1024

SC_CORES = 2
SC_SUBCORES = 16
SC_LANES = 16
SC_INDEX_GROUP = 128
SC_CHUNK_BYTES = 256 * 1024
MOE_PARTS = 2


def _sc_chunk_rows(nc, lanes, dtype):
    return min(SC_INDEX_GROUP, SC_CHUNK_BYTES // (nc * lanes * jnp.dtype(dtype).itemsize))


def _cparams(sem):
    return pltpu.CompilerParams(dimension_semantics=sem, vmem_limit_bytes=VMEM_LIMIT)


def _const_spec(shape):
    nd = len(shape)
    return pl.BlockSpec(shape, lambda *_: (0,) * nd)


def _dot(a, b):
    return jnp.dot(a, b, preferred_element_type=f32)


def _dot_nt(a, b):
    return lax.dot_general(a, b, (((1,), (1,)), ((), ())), preferred_element_type=f32)


def _dot_tn(a, b):
    return lax.dot_general(a, b, (((0,), (0,)), ((), ())), preferred_element_type=f32)


def _split2(x):
    hi = x.astype(bf16)
    lo = (x - hi.astype(f32)).astype(bf16)
    return hi, lo


def _seg_sums(xs, seg_b):
    parts = []
    for x in xs:
        parts.extend(_split2(x))
    out = _dot(jnp.concatenate(parts, axis=0), seg_b)
    t = xs[0].shape[0]
    return [out[2 * i * t:(2 * i + 1) * t] + out[(2 * i + 1) * t:(2 * i + 2) * t] for i in range(len(xs))]


def _dot_hp(a, b):
    ah, al = _split2(a)
    bh, bl = _split2(b)
    return _dot(ah, bh) + _dot(ah, bl) + _dot(al, bh)


def _dot_exact_lhs(m_bf16, x):
    hi, lo = _split2(x)
    return _dot(m_bf16, hi) + _dot(m_bf16, lo)


def _sigmoid(x):
    return 1.0 / (1.0 + jnp.exp(-x))


def _lane_chunk(n_rows, n_chunks, c, row0=0):
    return (pl.ds(row0 * n_chunks + c, n_rows, stride=n_chunks), slice(None))


def _pack_bf16_pairs(x):
    chunks = []
    for j in range(x.shape[1] // (2 * LANES)):
        lo = x[:, 2 * j * LANES:(2 * j + 1) * LANES].astype(bf16).astype(f32)
        hi = x[:, (2 * j + 1) * LANES:(2 * j + 2) * LANES].astype(bf16).astype(f32)
        chunks.append(lax.bitcast_convert_type(hi, i32) | lax.shift_right_logical(lax.bitcast_convert_type(lo, i32), 16))
    return chunks


def _unpack_bf16_pairs(chunks):
    cols = []
    for w in chunks:
        cols.append(lax.bitcast_convert_type(lax.shift_left(w, 16), f32))
        cols.append(lax.bitcast_convert_type(w & jnp.int32(-65536), f32))
    return jnp.concatenate(cols, axis=1)


def _layer_norm(h, g, b):
    mu = jnp.mean(h, axis=-1, keepdims=True)
    d = h - mu
    var = jnp.mean(d * d, axis=-1, keepdims=True)
    return d * lax.rsqrt(var + LN_EPS) * g + b


def _proj_body(*refs, n_out):
    x_ref = refs[0]
    w_refs = refs[1:1 + n_out]
    o_refs = refs[1 + n_out:]
    xb = x_ref[...].astype(bf16)
    for w_ref, o_ref in zip(w_refs, o_refs):
        o_ref[...] = _dot(xb, w_ref[...]).astype(o_ref.dtype)


def _proj(x, ws, out_dtypes, tile):
    n, k = x.shape
    tile = min(tile, n)
    outs = pl.pallas_call(
        functools.partial(_proj_body, n_out=len(ws)),
        out_shape=[jax.ShapeDtypeStruct((n, w.shape[1]), dt) for w, dt in zip(ws, out_dtypes)],
        grid=(n // tile,),
        in_specs=[pl.BlockSpec((tile, k), lambda i: (i, 0))] + [_const_spec(w.shape) for w in ws],
        out_specs=[pl.BlockSpec((tile, w.shape[1]), lambda i: (i, 0)) for w in ws],
        compiler_params=_cparams(("parallel",)),
        name="proj",
    )(x, *ws)
    return outs


def _wkv_chunks(chains, states, masks):
    bd_b, bd, strict, incl, eye, eye_full = masks
    c, n = chains[0][1].shape
    nch = len(chains)

    def stack(x_b):
        return jnp.where(bd_b, jnp.concatenate([x_b] * WKV_GROUP, axis=0), jnp.zeros((), bf16))

    cast = [tuple(x.astype(bf16) for x in ch[1:6]) for ch in chains]
    v_s = [stack(cb[4]) for cb in cast]
    g = [_dot_nt(jnp.concatenate([cb[0], cb[3]], axis=0), jnp.concatenate([stack(cb[1]), stack(cb[2])], axis=0))
         for cb in cast]
    l_ak = [jnp.where(strict, gi[:c, n:], 0.0).astype(bf16) for gi in g]
    m_rb = [jnp.where(incl, gi[c:, :n], 0.0).astype(bf16) for gi in g]
    m_rk = [jnp.where(incl, gi[c:, n:], 0.0).astype(bf16) for gi in g]
    x = [jnp.where(strict, gi[:c, :n], 0.0) for gi in g]
    t = [eye + xi for xi in x]
    for _ in range(5):
        xb = [xi.astype(bf16) for xi in x]
        x = [_dot(xi, stack(xi)) for xi in xb]
        t = [ti + _dot(ti.astype(bf16), stack(xi.astype(bf16))) for ti, xi in zip(t, x)]
    lakv = [_dot(l_ak[i], v_s[i]).astype(bf16) for i in range(nch)]
    au = [_dot(t[i].astype(bf16), jnp.concatenate([stack(cast[i][0]), stack(lakv[i])], axis=1))
          for i in range(nch)]
    abar = [a[:, :n].astype(bf16) for a in au]
    ubar = [a[:, n:].astype(bf16) for a in au]
    ry = [_dot(m_rb[i], jnp.concatenate([stack(abar[i]), stack(ubar[i])], axis=1)) for i in range(nch)]
    r_bar = [(chains[i][4] + ry[i][:, :n]).astype(bf16) for i in range(nch)]
    y_bar = [ry[i][:, n:] + _dot(m_rk[i], v_s[i]) for i in range(nch)]
    p = [((eye_full + jnp.where(bd, _dot_tn(abar[i], cast[i][1]), 0.0)) * chains[i][6]).astype(bf16)
         for i in range(nch)]
    q = []
    for i in range(nch):
        q_bd = jnp.where(bd, _dot_tn(jnp.concatenate([ubar[i], cast[i][4]], axis=0),
                                     jnp.concatenate([cast[i][1], cast[i][2]], axis=0)), 0.0)
        qi = q_bd[0:c]
        for h in range(1, WKV_GROUP):
            qi = qi + q_bd[h * c:(h + 1) * c]
        q.append(qi * chains[i][6])
    states = list(states)
    ys = []
    for i in range(nch):
        gi = chains[i][0]
        s_b = states[gi].astype(bf16)
        ys.append(_dot_nt(r_bar[i], stack(s_b)) + y_bar[i])
        states[gi] = _dot(s_b, p[i]) + q[i]
    return ys, states


def _rwkv_body(u_ref, mu_ref, wdec_ref, w0_ref, aup_ref, a0_ref, gup_ref, kk_ref, ka_ref, rk_ref,
               lng_ref, lnb_ref, seg_ref, tri_ref, *rest, tt, n_side):
    side_in = rest[:n_side]
    y_ref = rest[n_side]
    side_out = rest[n_side + 1:2 * n_side + 1]
    state_ref, carry_ref = rest[2 * n_side + 1:]
    j = pl.program_id(1)
    for src, dst in zip(side_in, side_out):
        dst[...] = src[...].astype(dst.dtype)

    @pl.when(j == 0)
    def _():
        state_ref[...] = jnp.zeros_like(state_ref)
        carry_ref[...] = jnp.zeros_like(carry_ref)

    w = RWKV_WIDTH
    u = u_ref[...]
    row = lax.broadcasted_iota(i32, u.shape, 0)
    prev = jnp.where(row == 0, carry_ref[0:1, :], pltpu.roll(u, 1, axis=0))
    carry_ref[0:1, :] = u[tt - 1:tt, :]
    us = u + (prev - u) * mu_ref[...]
    r = us[:, 0:w]
    k = us[:, w:2 * w]
    v = us[:, 2 * w:3 * w]
    wa = us[:, 3 * w:3 * w + DECAY_RANK + AAA_RANK]
    gd = us[:, 3 * w + DECAY_RANK + AAA_RANK:]
    z = w0_ref[...] + _dot_hp(jnp.tanh(wa), wdec_ref[...])
    softplus_neg_z = jnp.maximum(-z, 0.0) + jnp.log(1.0 + jnp.exp(-jnp.abs(z)))
    lw = -jnp.exp(-softplus_neg_z - 0.5)
    a = _sigmoid(a0_ref[...] + _dot_hp(wa, aup_ref[...]))
    gate = _dot(_sigmoid(gd).astype(bf16), gup_ref[...].astype(bf16))
    seg = seg_ref[...]
    kk = k * kk_ref[...]
    kmod = k * (1.0 + (a - 1.0) * ka_ref[...])
    kk_sq, bonus_dot = _seg_sums([kk * kk, r * kmod * rk_ref[...]], seg)
    kk = kk / jnp.maximum(jnp.sqrt(kk_sq), 1e-12)
    cum = _dot_exact_lhs(tri_ref[...], lw)
    wc = jnp.exp(cum)
    iwc = jnp.exp(-cum)
    at = -kk * jnp.exp(cum - lw)
    bt = kk * a * iwc
    kt = kmod * iwc
    rt = r * wc

    n = WKV_GROUP * HEAD_DIM
    ri = lax.broadcasted_iota(i32, (n, n), 0)
    ci = lax.broadcasted_iota(i32, (n, n), 1)
    bd = (ri // WKV_CHUNK) == (ci // HEAD_DIM)
    bd_b = jnp.where(bd, 1.0, 0.0).astype(bf16) > 0
    eye_full = jnp.where(ri == ci, 1.0, 0.0).astype(f32)
    ti = lax.broadcasted_iota(i32, (WKV_CHUNK, n), 0)
    si = lax.broadcasted_iota(i32, (WKV_CHUNK, n), 1) % WKV_CHUNK
    masks = (bd_b, bd, ti > si, ti >= si, jnp.where(ti == si, 1.0, 0.0).astype(f32), eye_full)

    n_groups = w // n
    n_chunks = tt // WKV_CHUNK
    chains = []
    for c in range(n_chunks):
        rs = slice(c * WKV_CHUNK, (c + 1) * WKV_CHUNK)
        last = (c + 1) * WKV_CHUNK - 1
        for gi in range(n_groups):
            cs = slice(gi * n, (gi + 1) * n)
            chains.append((gi, at[rs, cs], bt[rs, cs], kt[rs, cs], rt[rs, cs], v[rs, cs], wc[last:last + 1, cs]))
    ys, states = _wkv_chunks(chains, [state_ref[gi] for gi in range(n_groups)], masks)
    for gi in range(n_groups):
        state_ref[gi] = states[gi]
    y = jnp.concatenate([jnp.concatenate(ys[c * n_groups:(c + 1) * n_groups], axis=1) for c in range(n_chunks)],
                        axis=0)

    inv_n = 1.0 / HEAD_DIM
    d = y - _seg_sums([y], seg)[0] * inv_n
    var = _seg_sums([d * d], seg)[0] * inv_n
    yn = d * lax.rsqrt(var + GN_EPS) * lng_ref[...] + lnb_ref[...]
    y_ref[...] = ((yn + bonus_dot * v) * gate).astype(y_ref.dtype)


def _rwkv(u_r, mu_shift, w_decay_up, w0, a_up, a0, g_up, k_k, k_a, r_k, lnx_g, lnx_b, tt, side=(), layer=0):
    b, s, cols = u_r.shape
    tt = min(tt, s)
    n_steps = b * (s // tt)
    assert all(a.shape[1] % n_steps == 0 for a in side)
    w = RWKV_WIDTH
    row = lambda p: p.reshape(1, -1).astype(f32)
    wdec = jnp.concatenate([w_decay_up, jnp.zeros((AAA_RANK, w), f32)], axis=0)
    aup = jnp.concatenate([jnp.zeros((DECAY_RANK, w), f32), a_up], axis=0)
    hid = jnp.arange(w) // HEAD_DIM
    seg = (hid[:, None] == hid[None, :]).astype(bf16)
    ti = jnp.arange(tt)
    tri = ((ti[:, None] // WKV_CHUNK == ti[None, :] // WKV_CHUNK) & (ti[:, None] >= ti[None, :])).astype(bf16)
    params = [row(mu_shift), wdec, row(w0), aup, row(a0), g_up, row(k_k), row(k_a), row(r_k), row(lnx_g),
              row(lnx_b), seg, tri]
    n = WKV_GROUP * HEAD_DIM
    nj = s // tt
    side_in = [pl.BlockSpec((None, a.shape[1] // n_steps) + a.shape[2:], lambda bi, j: (layer, bi * nj + j, 0, 0))
               for a in side]
    side_out = [pl.BlockSpec((a.shape[1] // n_steps,) + a.shape[2:], lambda bi, j: (bi * nj + j, 0, 0))
                for a in side]
    outs = pl.pallas_call(
        functools.partial(_rwkv_body, tt=tt, n_side=len(side)),
        out_shape=[jax.ShapeDtypeStruct((b, s, w), bf16)] + [jax.ShapeDtypeStruct(a.shape[1:], bf16) for a in side],
        grid=(b, nj),
        in_specs=([pl.BlockSpec((None, tt, cols), lambda bi, j: (bi, j, 0))] + [_const_spec(p.shape) for p in params]
                  + side_in),
        out_specs=[pl.BlockSpec((None, tt, w), lambda bi, j: (bi, j, 0))] + side_out,
        scratch_shapes=[pltpu.VMEM((w // n, HEAD_DIM, n), f32), pltpu.VMEM((SUBLANES, cols), f32)],
        compiler_params=_cparams(("parallel", "arbitrary")),
        name="rwkv7",
    )(u_r, *params, *side)
    return outs[0], outs[1:]


def _swa_body(sink_ref, u_ref, pos_ref, invf_ref, o_ref, kprev_ref, vprev_ref, *, nwin):
    j = pl.program_id(1)

    @pl.when(j == 0)
    def _():
        kprev_ref[...] = jnp.zeros_like(kprev_ref)
        vprev_ref[...] = jnp.zeros_like(vprev_ref)

    wq = SWA_WIDTH
    kvw = SWA_KV_WIDTH
    half = HEAD_DIM // 2
    gw = SWA_GROUP * HEAD_DIM
    rows = SWA_GROUP * WINDOW
    lane_kv = lax.broadcasted_iota(i32, (WINDOW, kvw), 1)
    ri = lax.broadcasted_iota(i32, (rows, WINDOW), 0)
    ci = lax.broadcasted_iota(i32, (rows, WINDOW), 1)
    t_idx = ri % WINDOW
    mask_cur = ci <= t_idx
    mask_cur_b = jnp.where(mask_cur, 1.0, 0.0).astype(bf16) > 0
    rb = lax.broadcasted_iota(i32, (rows, 1), 0) // WINDOW
    bd = (lax.broadcasted_iota(i32, (rows, gw), 0) // WINDOW) == (lax.broadcasted_iota(i32, (rows, gw), 1) // HEAD_DIM)

    def rope(x, c, s):
        n = x.shape[1]
        lane = lax.broadcasted_iota(i32, x.shape, 1)
        rot = jnp.where((lane % HEAD_DIM) < half, -pltpu.roll(x, n - half, axis=1), pltpu.roll(x, half, axis=1))
        return x * c + rot * s

    def rep(x, gi):
        sw = pltpu.roll(x, HEAD_DIM, axis=1)
        one = jnp.where((lane_kv // HEAD_DIM) == gi, x, sw)
        return jnp.concatenate([one] * (gw // kvw), axis=1).astype(bf16)

    k_prev = kprev_ref[...]
    v_prev = vprev_ref[...]
    for wi in range(nwin):
        rs = slice(wi * WINDOW, (wi + 1) * WINDOW)
        u = u_ref[rs, :]
        ang_t = invf_ref[...] * pos_ref[wi].astype(f32)
        cos = jnp.concatenate([jnp.cos(ang_t)] * (LANES // half), axis=0).T
        sin = jnp.concatenate([jnp.sin(ang_t)] * (LANES // half), axis=0).T
        q = rope(u[:, :wq] * (HEAD_DIM ** -0.5), jnp.concatenate([cos] * (wq // LANES), axis=1),
                 jnp.concatenate([sin] * (wq // LANES), axis=1))
        k_cur = rope(u[:, wq:wq + kvw], cos, sin)
        v_cur = u[:, wq + kvw:]
        prev_bias = jnp.where(j > 0, 0.0, NEG_INF) if wi == 0 else 0.0
        outs = []
        for gi in range(SWA_KV_HEADS):
            qg = q[:, gi * gw:(gi + 1) * gw]
            q_bd = jnp.where(bd, jnp.concatenate([qg] * SWA_GROUP, axis=0), 0.0).astype(bf16)
            s = jnp.where(mask_cur, _dot_nt(q_bd, rep(k_cur, gi)), _dot_nt(q_bd, rep(k_prev, gi)) + prev_bias)
            sink = jnp.zeros((rows, 1), f32)
            for h in range(SWA_GROUP):
                sink = jnp.where(rb == h, sink_ref[gi * SWA_GROUP + h], sink)
            m = jnp.maximum(jnp.max(s, axis=-1, keepdims=True), sink)
            p = jnp.exp(s - m)
            denom = jnp.sum(p, axis=-1, keepdims=True) + jnp.exp(sink - m)
            p_b = p.astype(bf16)
            zero_b = jnp.zeros((), bf16)
            o_bd = (_dot(jnp.where(mask_cur_b, p_b, zero_b), rep(v_cur, gi))
                    + _dot(jnp.where(mask_cur_b, zero_b, p_b), rep(v_prev, gi)))
            o_bd = jnp.where(bd, o_bd * (1.0 / denom), 0.0)
            og = o_bd[0:WINDOW]
            for h in range(1, SWA_GROUP):
                og = og + o_bd[h * WINDOW:(h + 1) * WINDOW]
            outs.append(og)
        o_ref[rs, :] = jnp.concatenate(outs, axis=1).astype(o_ref.dtype)
        k_prev, v_prev = k_cur, v_cur
    kprev_ref[...] = k_prev
    vprev_ref[...] = v_prev


def _swa(u_s, positions, sinks, nwin=2):
    b, s, cols = u_s.shape
    half = HEAD_DIM // 2
    nwin = nwin if s % (nwin * WINDOW) == 0 else 1
    tile = nwin * WINDOW
    inv_freq = (ROPE_THETA ** (-jnp.arange(0, HEAD_DIM, 2, dtype=f32) / HEAD_DIM)).reshape(half, 1)
    pos = positions.reshape(b, s // WINDOW, 1, WINDOW).astype(i32)
    return pl.pallas_call(
        functools.partial(_swa_body, nwin=nwin),
        out_shape=jax.ShapeDtypeStruct((b, s, SWA_WIDTH), bf16),
        grid=(b, s // tile),
        in_specs=[pl.BlockSpec(memory_space=pltpu.SMEM),
                  pl.BlockSpec((None, tile, cols), lambda bi, j: (bi, j, 0)),
                  pl.BlockSpec((None, nwin, 1, WINDOW), lambda bi, j: (bi, j, 0, 0)),
                  _const_spec((half, 1))],
        out_specs=pl.BlockSpec((None, tile, SWA_WIDTH), lambda bi, j: (bi, j, 0)),
        scratch_shapes=[pltpu.VMEM((WINDOW, SWA_KV_WIDTH), f32), pltpu.VMEM((WINDOW, SWA_KV_WIDTH), f32)],
        compiler_params=_cparams(("parallel", "arbitrary")),
        name="swa",
    )(sinks.astype(f32), u_s, pos, inv_freq)


def _mix_xattn_body(ya_ref, yb_ref, wa_ref, wb_ref, xin_ref, g1_ref, b1_ref, kv_ref, wq_ref, wo_ref, g_ref, b_ref,
                    wt_ref, bias_ref, upper_ref, o_ref, o3_ref, e_ref, gate_ref, r_ref, cnt_out_ref, cnt_ref,
                    *, alpha, batches_per_part):
    @pl.when((pl.program_id(0) % batches_per_part == 0) & (pl.program_id(1) == 0))
    def _():
        cnt_ref[...] = jnp.zeros_like(cnt_ref)

    mix = _dot(ya_ref[...], wa_ref[...]) + _dot(yb_ref[...], wb_ref[...])
    x = _layer_norm(alpha * xin_ref[...] + mix, g1_ref[...], b1_ref[...])
    d = x.shape[1]
    hd = d // MEM_HEADS
    q = _dot(x.astype(bf16), wq_ref[...]) * (hd ** -0.5)
    kv = kv_ref[...]
    outs = []
    for h in range(MEM_HEADS):
        qh = q[:, h * hd:(h + 1) * hd].astype(bf16)
        kh = kv[:, h * hd:(h + 1) * hd]
        vh = kv[:, d + h * hd:d + (h + 1) * hd]
        s = _dot_nt(qh, kh)
        p = jnp.exp(s - jnp.max(s, axis=-1, keepdims=True))
        l = jnp.sum(p, axis=-1, keepdims=True)
        outs.append(_dot(p.astype(bf16), vh) / l)
    o = jnp.concatenate(outs, axis=1)
    xa = _dot(o.astype(bf16), wo_ref[...])
    y = _layer_norm(alpha * x + xa, g_ref[...], b_ref[...])
    o_ref[...] = y
    words = _pack_bf16_pairs(y)
    for j, w in enumerate(words):
        o3_ref[_lane_chunk(x.shape[0], len(words), j)] = w
    _route(y, wt_ref, bias_ref, upper_ref, e_ref, gate_ref, r_ref, cnt_out_ref, cnt_ref)


def _mix_xattn(ya, yb, w_o, x, g1, b1, kv, wm_q, wm_o, g2, b2, w_router, router_bias, alpha, tile, parts):
    bsz, s, d = x.shape
    m = kv.shape[1]
    tile = min(tile, s)
    nj = s // tile
    assert bsz % parts == 0
    bpp = bsz // parts
    ne = N_EXPERTS
    wt = w_router.T
    bias = jnp.broadcast_to(router_bias.reshape(ne, 1).astype(f32), (ne, LANES))
    ti = jnp.arange(tile)
    upper = (ti[:, None] < ti[None, :]).astype(bf16)
    cols = pl.BlockSpec((TOP_K, tile), lambda bi, j: (0, bi * nj + j))
    n = bsz * s
    wa = w_o[:ya.shape[2]].astype(bf16)
    wb = w_o[ya.shape[2]:].astype(bf16)
    wq = wm_q.astype(bf16)
    wo = wm_o.astype(bf16)
    rows = lambda width: pl.BlockSpec((None, tile, width), lambda bi, j: (bi, j, 0))
    vec = _const_spec((1, d))
    x2, x2_rows, e_t, g_t, r_t, cnt = pl.pallas_call(
        functools.partial(_mix_xattn_body, alpha=alpha, batches_per_part=bpp),
        out_shape=[jax.ShapeDtypeStruct((bsz, s, d), f32),
                   jax.ShapeDtypeStruct((bsz * s * (d // (2 * LANES)), LANES), i32),
                   jax.ShapeDtypeStruct((TOP_K, n), i32), jax.ShapeDtypeStruct((TOP_K, n), f32),
                   jax.ShapeDtypeStruct((TOP_K, n), i32), jax.ShapeDtypeStruct((parts, ne, LANES), i32)],
        grid=(bsz, nj),
        in_specs=[rows(ya.shape[2]), rows(yb.shape[2]), _const_spec(wa.shape), _const_spec(wb.shape), rows(d), vec, vec,
                  pl.BlockSpec((None, m, 2 * d), lambda bi, j: (bi, 0, 0)),
                  _const_spec(wq.shape), _const_spec(wo.shape), vec, vec,
                  _const_spec((ne, d)), _const_spec((ne, LANES)), _const_spec((tile, tile))],
        out_specs=[rows(d), pl.BlockSpec((tile * (d // (2 * LANES)), LANES), lambda bi, j: (bi * nj + j, 0)),
                   cols, cols, cols, pl.BlockSpec((None, ne, LANES), lambda bi, j: (bi // bpp, 0, 0))],
        scratch_shapes=[pltpu.VMEM((ne, LANES), f32)],
        compiler_params=_cparams(("arbitrary", "arbitrary")),
        name="mix_xattn",
    )(ya, yb, wa, wb, x, g1.reshape(1, d), b1.reshape(1, d), kv, wq, wo, g2.reshape(1, d), b2.reshape(1, d),
      wt, bias, upper)
    return x2, x2_rows, e_t, g_t, r_t, cnt[:, :, 0]


def _router_body(x_ref, wt_ref, bias_ref, upper_ref, e_ref, g_ref, r_ref, cnt_out_ref, cnt_ref):
    @pl.when(pl.program_id(0) == 0)
    def _():
        cnt_ref[...] = jnp.zeros_like(cnt_ref)

    _route(x_ref[...], wt_ref, bias_ref, upper_ref, e_ref, g_ref, r_ref, cnt_out_ref, cnt_ref)


def _route(x, wt_ref, bias_ref, upper_ref, e_ref, g_ref, r_ref, cnt_out_ref, cnt_ref):
    t = x.shape[0]
    xh, xl = _split2(x)
    wh, wl = _split2(wt_ref[...])
    logits = _dot_nt(wh, xh) + _dot_nt(wh, xl) + _dot_nt(wl, xh)
    scores = _sigmoid(logits)
    biased = scores + bias_ref[...][:, 0:1]
    ne = N_EXPERTS
    neg = -jnp.inf

    def top1(vals):
        rows = lax.broadcasted_iota(i32, vals.shape, 0).astype(f32)
        m = jnp.max(vals, axis=0, keepdims=True)
        idx = jnp.min(jnp.where(vals == m, rows, float(vals.shape[0])), axis=0, keepdims=True)
        return m, idx, rows == idx

    gscores = []
    for gi in range(N_GROUPS):
        blk = biased[gi * GROUP_SIZE:(gi + 1) * GROUP_SIZE, :]
        m1, _, hit = top1(blk)
        m2 = jnp.max(jnp.where(hit, neg, blk), axis=0, keepdims=True)
        gscores.append(m1 + m2)
    gs = jnp.concatenate(gscores, axis=0)
    gsel = jnp.zeros(gs.shape, f32)
    for _ in range(TOPK_GROUPS):
        _, _, hit = top1(gs)
        gsel = jnp.where(hit, 1.0, gsel)
        gs = jnp.where(hit, neg, gs)
    emask = jnp.concatenate(
        [jnp.broadcast_to(gsel[gi:gi + 1, :], (GROUP_SIZE, t)) for gi in range(N_GROUPS)], axis=0) > 0.5
    cand = jnp.where(emask, biased, NEG_INF)
    idxs, sels = [], []
    chosen = jnp.zeros((ne, t), f32)
    for _ in range(TOP_K):
        _, idx, hit = top1(cand)
        idxs.append(idx)
        sels.append(jnp.sum(jnp.where(hit, scores, 0.0), axis=0, keepdims=True))
        chosen = chosen + jnp.where(hit, 1.0, 0.0)
        cand = jnp.where(hit, neg, cand)
    sel = jnp.concatenate(sels, axis=0)
    g_ref[...] = sel / jnp.sum(sel, axis=0, keepdims=True) * ROUTED_SCALE
    e_ref[...] = jnp.concatenate(idxs, axis=0).astype(i32)
    before = _dot(chosen.astype(bf16), upper_ref[...]) + cnt_ref[...][:, 0:1]
    rows = lax.broadcasted_iota(i32, (ne, t), 0).astype(f32)
    ranks = [jnp.sum(jnp.where(rows == idx, before, 0.0), axis=0, keepdims=True) for idx in idxs]
    r_ref[...] = jnp.concatenate(ranks, axis=0).astype(i32)
    cnt_ref[...] = cnt_ref[...] + jnp.sum(chosen, axis=1, keepdims=True)
    cnt_out_ref[...] = cnt_ref[...].astype(i32)


def _router(x2, w_router, router_bias, tile, row0=0, n=None):
    d = x2.shape[1]
    n = x2.shape[0] if n is None else n
    ne = N_EXPERTS
    t = min(tile, n)
    assert row0 % t == 0 and n % t == 0
    first = row0 // t
    wt = w_router.T
    bias = jnp.broadcast_to(router_bias.reshape(ne, 1).astype(f32), (ne, LANES))
    ti = jnp.arange(t)
    upper = (ti[:, None] < ti[None, :]).astype(bf16)
    cols = pl.BlockSpec((TOP_K, t), lambda i: (0, i))
    e_t, g_t, r_t, cnt = pl.pallas_call(
        _router_body,
        out_shape=[jax.ShapeDtypeStruct((TOP_K, n), i32), jax.ShapeDtypeStruct((TOP_K, n), f32),
                   jax.ShapeDtypeStruct((TOP_K, n), i32), jax.ShapeDtypeStruct((ne, LANES), i32)],
        grid=(n // t,),
        in_specs=[pl.BlockSpec((t, d), lambda i: (i + first, 0)), _const_spec((ne, d)), _const_spec((ne, LANES)),
                  _const_spec((t, t))],
        out_specs=[cols, cols, cols, _const_spec((ne, LANES))],
        scratch_shapes=[pltpu.VMEM((ne, LANES), f32)],
        compiler_params=_cparams(("arbitrary",)),
        name="router",
    )(x2, wt, bias, upper)
    return e_t, g_t, r_t, cnt[:, 0]


def _dispatch_sc(x3d, e_t, r_t, g_t, starts, n_slots, tok0=0):
    _, nc, lanes = x3d.shape
    n = e_t.shape[1]
    workers = SC_CORES * SC_SUBCORES
    per_w = n // workers
    tc = _sc_chunk_rows(nc, lanes, x3d.dtype)
    assert n % (workers * SC_INDEX_GROUP) == 0 and SC_INDEX_GROUP % tc == 0
    mesh = plsc.VectorSubcoreMesh(core_axis_name="c", subcore_axis_name="s")

    @functools.partial(
        pl.kernel, mesh=mesh,
        out_type=[jax.ShapeDtypeStruct((n_slots, nc, lanes), x3d.dtype),
                  jax.ShapeDtypeStruct((n_slots,), f32)],
        scratch_types=[
            pltpu.VMEM((tc, nc, lanes), x3d.dtype),
            pltpu.VMEM((TOP_K, SC_INDEX_GROUP), i32),
            pltpu.VMEM((TOP_K, SC_INDEX_GROUP), i32),
            pltpu.VMEM((TOP_K, SC_INDEX_GROUP), f32),
            pltpu.VMEM((TOP_K, tc), i32),
            pltpu.VMEM((TOP_K, tc), f32),
            pltpu.VMEM((N_EXPERTS,), i32),
            pltpu.SemaphoreType.DMA,
        ],
        compiler_params=pltpu.CompilerParams(use_tc_tiling_on_sc=True, needs_layout_passes=False),
    )
    def dispatch(x_hbm, e_hbm, r_hbm, g_hbm, st_hbm, o_hbm, gs_hbm, rows_v, e_v, r_v, g_v, slot_v, gate_v, st_v, sem):
        wid = lax.axis_index("s") * SC_CORES + lax.axis_index("c")
        pltpu.sync_copy(st_hbm, st_v)

        @pl.loop(0, per_w // SC_INDEX_GROUP)
        def _(gi):
            base = wid * per_w + gi * SC_INDEX_GROUP
            pltpu.sync_copy(e_hbm.at[:, pl.ds(base, SC_INDEX_GROUP)], e_v)
            pltpu.sync_copy(r_hbm.at[:, pl.ds(base, SC_INDEX_GROUP)], r_v)
            pltpu.sync_copy(g_hbm.at[:, pl.ds(base, SC_INDEX_GROUP)], g_v)
            for h in range(SC_INDEX_GROUP // tc):
                off = h * tc
                pltpu.sync_copy(x_hbm.at[pl.ds(tok0 + base + off, tc)], rows_v)
                for kk in range(TOP_K):
                    for j in range(tc // SC_LANES):
                        src = pl.ds(off + j * SC_LANES, SC_LANES)
                        dst = pl.ds(j * SC_LANES, SC_LANES)
                        slot_v[kk, dst] = r_v[kk, src] + plsc.load_gather(st_v, [e_v[kk, src]])
                        gate_v[kk, dst] = g_v[kk, src]
                copies = [pltpu.async_copy(rows_v, o_hbm.at[slot_v.at[kk]], sem) for kk in range(TOP_K)]
                copies += [pltpu.async_copy(gate_v.at[kk], gs_hbm.at[slot_v.at[kk]], sem) for kk in range(TOP_K)]
                for cp in copies:
                    cp.wait()

    return dispatch(x3d, e_t, r_t, g_t, starts)


def _gmm_body(gid_ref, tid_ref, nrows_ref, newg_ref, nextg_ref, ord_ref, x_ref, g_ref,
              wg_hbm, wu_hbm, wd_hbm, o_ref, wg_b, wu_b, wd_b, sems, *, tm):
    v = pl.program_id(0)
    n_real = nrows_ref[v]
    nc = wg_b.shape[1] // LANES
    slot = ord_ref[v] % 2

    def weight_copies(expert, dst_slot):
        return [pltpu.make_async_copy(src.at[expert], dst.at[dst_slot], sems.at[dst_slot])
                for src, dst in ((wg_hbm, wg_b), (wu_hbm, wu_b), (wd_hbm, wd_b))]

    @pl.when(newg_ref[v] == 1)
    def _():
        @pl.when(v == 0)
        def _():
            for cp in weight_copies(gid_ref[v], slot):
                cp.start()

        for cp in weight_copies(gid_ref[v], slot):
            cp.wait()

        @pl.when(nextg_ref[v] >= 0)
        def _():
            for cp in weight_copies(nextg_ref[v], 1 - slot):
                cp.start()

    @pl.when(n_real > 0)
    def _():
        nw = nc // 2
        real = lax.broadcasted_iota(i32, (tm, 1), 0) < n_real
        x = _unpack_bf16_pairs([x_ref[_lane_chunk(tm, nw, j)] for j in range(nw)])
        x = jnp.where(real, x, 0.0).astype(bf16)
        hg = _dot(x, wg_b[slot])
        h = hg * _sigmoid(hg) * _dot(x, wu_b[slot])
        g = g_ref[...]
        g_cols = jnp.concatenate([g, jnp.zeros((SUBLANES - g.shape[0], LANES), f32)], axis=0).T
        g_col = jnp.concatenate([g_cols[:, r:r + 1] for r in range(tm // LANES)], axis=0)
        h = h * jnp.where(real, g_col, 0.0)
        y = _dot(h.astype(bf16), wd_b[slot])
        for c in range(nc):
            o_ref[_lane_chunk(tm, nc, c)] = y[:, c * LANES:(c + 1) * LANES]


def _expert_layout(counts, n_assign, tm):
    tiles = (counts + tm - 1) // tm
    starts = ((jnp.cumsum(tiles) - tiles) * tm).astype(i32)
    capacity = (n_assign // tm + counts.shape[0]) * tm
    return starts, tiles, capacity


def _gmm(xs, gates, we_gate, we_up, we_down, counts, tm):
    ne, d, de = we_gate.shape
    nc = d // LANES
    nw = nc // 2
    n_tiles = xs.shape[0] // (nw * tm)
    assert tm % LANES == 0 and tm // LANES <= SUBLANES
    gates3 = gates.reshape(n_tiles, tm // LANES, LANES)
    tiles = (counts + tm - 1) // tm
    tile_end = jnp.cumsum(tiles)
    tile_start = tile_end - tiles
    vi = jnp.arange(n_tiles, dtype=i32)
    valid = vi < tile_end[-1]
    gid = jnp.minimum(jnp.sum((tile_end[None, :] <= vi[:, None]).astype(i32), axis=1), ne - 1)
    gid = jnp.where(valid, gid, jnp.max(jnp.where(valid, gid, 0)))
    onehot = gid[:, None] == jnp.arange(ne, dtype=i32)[None, :]
    pick = lambda table: jnp.sum(jnp.where(onehot, table[None, :], 0), axis=1)
    tid = jnp.minimum(vi, tile_end[-1] - 1).astype(i32)
    n_real = jnp.where(valid, jnp.clip(pick(counts) - (vi - pick(tile_start)) * tm, 0, tm), 0).astype(i32)
    one = jnp.ones((1,), i32)
    newg = jnp.concatenate([one, (gid[1:] != gid[:-1]).astype(i32)])
    later = gid[None, :] > gid[:, None]
    nextg = jnp.min(jnp.where(later, gid[None, :], ne), axis=1)
    nextg = jnp.where(nextg < ne, nextg, -1).astype(i32)
    order = (jnp.cumsum(newg) - 1).astype(i32)
    rows = lambda chunks: pl.BlockSpec((tm * chunks, LANES), lambda v, g, t, *_: (t[v], 0))
    hbm = pl.BlockSpec(memory_space=pl.ANY)
    return pl.pallas_call(
        functools.partial(_gmm_body, tm=tm),
        out_shape=jax.ShapeDtypeStruct((n_tiles * tm * nc, LANES), f32),
        grid_spec=pltpu.PrefetchScalarGridSpec(
            num_scalar_prefetch=6, grid=(n_tiles,),
            in_specs=[rows(nw), pl.BlockSpec((None, tm // LANES, LANES), lambda v, g, t, *_: (t[v], 0, 0)),
                      hbm, hbm, hbm],
            out_specs=rows(nc),
            scratch_shapes=[pltpu.VMEM((2, d, de), bf16), pltpu.VMEM((2, d, de), bf16), pltpu.VMEM((2, de, d), bf16),
                            pltpu.SemaphoreType.DMA((2,))]),
        compiler_params=_cparams(("arbitrary",)),
        name="moe_experts",
    )(gid, tid, n_real, newg, nextg, order, xs, gates3, we_gate, we_up, we_down)


def _combine_sc(ys3d, e_t, r_t, starts, n):
    _, nc, lanes = ys3d.shape
    workers = SC_CORES * SC_SUBCORES
    per_w = n // workers
    tc = _sc_chunk_rows(nc, lanes, f32)
    assert n % (workers * SC_INDEX_GROUP) == 0 and SC_INDEX_GROUP % tc == 0
    mesh = plsc.VectorSubcoreMesh(core_axis_name="c", subcore_axis_name="s")

    @functools.partial(
        pl.kernel, mesh=mesh,
        out_type=jax.ShapeDtypeStruct((n, nc, lanes), f32),
        scratch_types=[
            pltpu.VMEM((tc, nc, lanes), f32),
            pltpu.VMEM((TOP_K, SC_INDEX_GROUP), i32),
            pltpu.VMEM((TOP_K, SC_INDEX_GROUP), i32),
            pltpu.VMEM((TOP_K, tc), i32),
            pltpu.VMEM((N_EXPERTS,), i32),
            pltpu.SemaphoreType.DMA,
        ],
        compiler_params=pltpu.CompilerParams(use_tc_tiling_on_sc=True, needs_layout_passes=False),
    )
    def combine(y_hbm, e_hbm, r_hbm, st_hbm, o_hbm, acc_v, e_v, r_v, slot_v, st_v, sem):
        wid = lax.axis_index("s") * SC_CORES + lax.axis_index("c")
        pltpu.sync_copy(st_hbm, st_v)

        @pl.loop(0, per_w // SC_INDEX_GROUP)
        def _(gi):
            base = wid * per_w + gi * SC_INDEX_GROUP
            pltpu.sync_copy(e_hbm.at[:, pl.ds(base, SC_INDEX_GROUP)], e_v)
            pltpu.sync_copy(r_hbm.at[:, pl.ds(base, SC_INDEX_GROUP)], r_v)
            for h in range(SC_INDEX_GROUP // tc):
                off = h * tc
                for kk in range(TOP_K):
                    for j in range(tc // SC_LANES):
                        src = pl.ds(off + j * SC_LANES, SC_LANES)
                        slot_v[kk, pl.ds(j * SC_LANES, SC_LANES)] = (
                            r_v[kk, src] + plsc.load_gather(st_v, [e_v[kk, src]]))
                pltpu.async_copy(y_hbm.at[slot_v.at[0]], acc_v, sem).wait()
                copies = [pltpu.async_copy(y_hbm.at[slot_v.at[kk]], acc_v, sem, add=True) for kk in range(1, TOP_K)]
                for cp in copies:
                    cp.wait()
                pltpu.sync_copy(acc_v, o_hbm.at[pl.ds(base + off, tc)])

    return combine(ys3d, e_t, r_t, starts)


def _ffn_out_body(x_ref, r_ref, wg_ref, wu_ref, wd_ref, g_ref, b_ref, *rest, alpha):
    o_ref = rest[-1]
    x = x_ref[...]
    xb = x.astype(bf16)
    hg = _dot(xb, wg_ref[...])
    h = hg * _sigmoid(hg) * _dot(xb, wu_ref[...])
    shared = _dot(h.astype(bf16), wd_ref[...])
    nc = x.shape[1] // LANES
    routed = jnp.concatenate([r_ref[_lane_chunk(x.shape[0], nc, c)] for c in range(nc)], axis=1)
    o_ref[...] = _layer_norm(alpha * x + routed + shared, g_ref[...], b_ref[...])


def _ffn_out(x2, routed_rows, ws_gate, ws_up, ws_down, g, b, alpha, tile, row0=0, prev=None):
    n_all, d = x2.shape
    n = routed_rows.shape[0] // (d // LANES)
    tile = min(tile, n)
    assert row0 % tile == 0 and n % tile == 0
    first = row0 // tile
    wg, wu, wd = ws_gate.astype(bf16), ws_up.astype(bf16), ws_down.astype(bf16)
    rows = pl.BlockSpec((tile, d), lambda i: (i + first, 0))
    in_specs = [rows, pl.BlockSpec((tile * (d // LANES), LANES), lambda i: (i, 0)), _const_spec(wg.shape),
                _const_spec(wu.shape), _const_spec(wd.shape), _const_spec((1, d)), _const_spec((1, d))]
    args = [x2, routed_rows, wg, wu, wd, g.reshape(1, d), b.reshape(1, d)]
    aliases = {}
    if prev is not None:
        in_specs.append(pl.BlockSpec(memory_space=pl.ANY))
        args.append(prev)
        aliases = {len(args) - 1: 0}
    return pl.pallas_call(
        functools.partial(_ffn_out_body, alpha=alpha),
        out_shape=jax.ShapeDtypeStruct((n_all, d), f32),
        grid=(n // tile,),
        in_specs=in_specs,
        out_specs=rows,
        input_output_aliases=aliases,
        compiler_params=_cparams(("parallel",)),
        name="ffn_out_ln3",
    )(*args)


def _layer(x, mem, positions, w_in, mu_shift, w_decay_up, w0, a_up, a0, g_up, k_k, k_a, r_k, lnx_g, lnx_b, sinks,
           w_o, ln1_g, ln1_b, wm_q, wm_kv, wm_o, ln2_g, ln2_b, w_router, router_bias, we_gate, we_up, we_down,
           ws_gate, ws_up, ws_down, ln3_g, ln3_b, *, layer, alpha):
    b, s, d = x.shape
    n = b * s
    xf = x.reshape(n, d)
    w_in_b = w_in.astype(bf16)
    u_r, u_s = _proj(xf, [w_in_b[:, :RWKV_COLS], w_in_b[:, RWKV_COLS:]], [f32, f32], tile=512)
    rwkv_tt = 256
    experts = (we_gate, we_up, we_down)
    if N_EXPERTS % (b * (s // min(rwkv_tt, s))) == 0:
        y_r, experts_b = _rwkv(u_r.reshape(b, s, RWKV_COLS), mu_shift, w_decay_up, w0, a_up, a0, g_up, k_k, k_a, r_k,
                               lnx_g, lnx_b, tt=rwkv_tt, side=experts, layer=layer)
    else:
        y_r, _ = _rwkv(u_r.reshape(b, s, RWKV_COLS), mu_shift, w_decay_up, w0, a_up, a0, g_up, k_k, k_a, r_k,
                       lnx_g, lnx_b, tt=rwkv_tt)
        experts_b = [w[layer].astype(bf16) for w in experts]
    y_s = _swa(u_s.reshape(b, s, SWA_COLS), positions, sinks)
    m = mem.shape[1]
    (kv,) = _proj(mem.reshape(b * m, d), [wm_kv.astype(bf16)], [bf16], tile=512)
    parts = MOE_PARTS
    if b % parts or (n // parts) % (SC_CORES * SC_SUBCORES * SC_INDEX_GROUP):
        parts = 1
    npart = n // parts
    x2, x2_rows, e_all, g_all, r_all, counts_all = _mix_xattn(
        y_r, y_s, w_o, x, ln1_g, ln1_b, kv.reshape(b, m, 2 * d), wm_q, wm_o, ln2_g, ln2_b, w_router, router_bias,
        alpha, tile=512, parts=parts)
    x2 = x2.reshape(n, d)
    nc = d // LANES
    nw = nc // 2
    x3 = None
    for part in range(parts):
        row0 = part * npart
        e_t, g_t, r_t = (a[:, row0:row0 + npart] for a in (e_all, g_all, r_all))
        counts = counts_all[part]
        starts, _, n_slots = _expert_layout(counts, npart * TOP_K, GMM_TILE)
        xs, gs = _dispatch_sc(x2_rows.reshape(n, nw, LANES), e_t, r_t, g_t, starts, n_slots, tok0=row0)
        ys = _gmm(xs.reshape(n_slots * nw, LANES), gs, *experts_b, counts, tm=GMM_TILE)
        routed = _combine_sc(ys.reshape(n_slots, nc, LANES), e_t, r_t, starts, npart)
        x3 = _ffn_out(x2, routed.reshape(npart * nc, LANES), ws_gate, ws_up, ws_down, ln3_g, ln3_b, alpha, tile=512,
                      row0=row0, prev=x3)
    return x3.reshape(b, s, d)


def kernel(x, mem, positions, w_in, mu_shift, w_decay_up, w0, a_up, a0, g_up, k_k, k_a, r_k, lnx_g, lnx_b, sinks, w_o, ln1_g, ln1_b, wm_q, wm_kv, wm_o, ln2_g, ln2_b, w_router, router_bias, we_gate, we_up, we_down, ws_gate, ws_up, ws_down, ln3_g, ln3_b):
    depth = w_in.shape[0]
    alpha = (2 * depth) ** 0.25
    for l in range(depth):
        x = _layer(x, mem, positions, w_in[l], mu_shift[l], w_decay_up[l], w0[l], a_up[l], a0[l], g_up[l], k_k[l],
                   k_a[l], r_k[l], lnx_g[l], lnx_b[l], sinks[l], w_o[l], ln1_g[l], ln1_b[l], wm_q[l], wm_kv[l],
                   wm_o[l], ln2_g[l], ln2_b[l], w_router[l], router_bias[l], we_gate, we_up, we_down,
                   ws_gate[l], ws_up[l], ws_down[l], ln3_g[l], ln3_b[l], layer=l, alpha=alpha)
    return x
```

```python
import functools

import jax
import jax.numpy as jnp
from jax import lax
from jax.experimental import pallas as pl
from jax.experimental.pallas import tpu as pltpu
from jax.experimental.pallas import tpu_sc as plsc

f32 = jnp.float32
bf16 = jnp.bfloat16
i32 = jnp.int32

RWKV_HEADS = 8
HEAD_DIM = 64
RWKV_WIDTH = RWKV_HEADS * HEAD_DIM
DECAY_RANK = 64
AAA_RANK = 64
GATE_RANK = 128
RWKV_COLS = 3 * RWKV_WIDTH + DECAY_RANK + AAA_RANK + GATE_RANK
SWA_Q_HEADS = 8
SWA_KV_HEADS = 2
SWA_GROUP = SWA_Q_HEADS // SWA_KV_HEADS
SWA_WIDTH = SWA_Q_HEADS * HEAD_DIM
SWA_KV_WIDTH = SWA_KV_HEADS * HEAD_DIM
SWA_COLS = SWA_WIDTH + 2 * SWA_KV_WIDTH
WINDOW = 128
ROPE_THETA = 10000.0
MEM_HEADS = 4
N_EXPERTS = 256
TOP_K = 8
N_GROUPS = 8
GROUP_SIZE = N_EXPERTS // N_GROUPS
TOPK_GROUPS = 4
ROUTED_SCALE = 2.5
LN_EPS = 1e-5
GN_EPS = 64e-5
NEG_INF = -1e30

LANES = 128
SUBLANES = 8
WKV_CHUNK = 64
WKV_GROUP = 2
GMM_TILE = 512
VMEM_LIMIT = 56 * 1024 * 1024

SC_CORES = 2
SC_SUBCORES = 16
SC_LANES = 16
SC_INDEX_GROUP = 128
SC_CHUNK_BYTES = 256 * 1024
MOE_PARTS = 2


def _sc_chunk_rows(nc, lanes, dtype):
    return min(SC_INDEX_GROUP, SC_CHUNK_BYTES // (nc * lanes * jnp.dtype(dtype).itemsize))


def _cparams(sem):
    return pltpu.CompilerParams(dimension_semantics=sem, vmem_limit_bytes=VMEM_LIMIT)


def _const_spec(shape):
    nd = len(shape)
    return pl.BlockSpec(shape, lambda *_: (0,) * nd)


def _dot(a, b):
    return jnp.dot(a, b, preferred_element_type=f32)


def _dot_nt(a, b):
    return lax.dot_general(a, b, (((1,), (1,)), ((), ())), preferred_element_type=f32)


def _dot_tn(a, b):
    return lax.dot_general(a, b, (((0,), (0,)), ((), ())), preferred_element_type=f32)


def _split2(x):
    hi = x.astype(bf16)
    lo = (x - hi.astype(f32)).astype(bf16)
    return hi, lo


def _seg_sums(xs, seg_b):
    parts = []
    for x in xs:
        parts.extend(_split2(x))
    out = _dot(jnp.concatenate(parts, axis=0), seg_b)
    t = xs[0].shape[0]
    return [out[2 * i * t:(2 * i + 1) * t] + out[(2 * i + 1) * t:(2 * i + 2) * t] for i in range(len(xs))]


def _dot_hp(a, b):
    ah, al = _split2(a)
    bh, bl = _split2(b)
    return _dot(ah, bh) + _dot(ah, bl) + _dot(al, bh)


def _dot_exact_lhs(m_bf16, x):
    hi, lo = _split2(x)
    return _dot(m_bf16, hi) + _dot(m_bf16, lo)


def _sigmoid(x):
    return 1.0 / (1.0 + jnp.exp(-x))


def _lane_chunk(n_rows, n_chunks, c, row0=0):
    return (pl.ds(row0 * n_chunks + c, n_rows, stride=n_chunks), slice(None))


def _pack_bf16_pairs(x):
    chunks = []
    for j in range(x.shape[1] // (2 * LANES)):
        lo = x[:, 2 * j * LANES:(2 * j + 1) * LANES].astype(bf16).astype(f32)
        hi = x[:, (2 * j + 1) * LANES:(2 * j + 2) * LANES].astype(bf16).astype(f32)
        chunks.append(lax.bitcast_convert_type(hi, i32) | lax.shift_right_logical(lax.bitcast_convert_type(lo, i32), 16))
    return chunks


def _unpack_bf16_pairs(chunks):
    cols = []
    for w in chunks:
        cols.append(lax.bitcast_convert_type(lax.shift_left(w, 16), f32))
        cols.append(lax.bitcast_convert_type(w & jnp.int32(-65536), f32))
    return jnp.concatenate(cols, axis=1)


def _layer_norm(h, g, b):
    mu = jnp.mean(h, axis=-1, keepdims=True)
    d = h - mu
    var = jnp.mean(d * d, axis=-1, keepdims=True)
    return d * lax.rsqrt(var + LN_EPS) * g + b


def _proj_body(*refs, n_out):
    x_ref = refs[0]
    w_refs = refs[1:1 + n_out]
    o_refs = refs[1 + n_out:]
    xb = x_ref[...].astype(bf16)
    for w_ref, o_ref in zip(w_refs, o_refs):
        o_ref[...] = _dot(xb, w_ref[...]).astype(o_ref.dtype)


def _proj(x, ws, out_dtypes, tile):
    n, k = x.shape
    tile = min(tile, n)
    outs = pl.pallas_call(
        functools.partial(_proj_body, n_out=len(ws)),
        out_shape=[jax.ShapeDtypeStruct((n, w.shape[1]), dt) for w, dt in zip(ws, out_dtypes)],
        grid=(n // tile,),
        in_specs=[pl.BlockSpec((tile, k), lambda i: (i, 0))] + [_const_spec(w.shape) for w in ws],
        out_specs=[pl.BlockSpec((tile, w.shape[1]), lambda i: (i, 0)) for w in ws],
        compiler_params=_cparams(("parallel",)),
        name="proj",
    )(x, *ws)
    return outs


def _wkv_chunks(chains, states, masks):
    bd_b, bd, strict, incl, eye, eye_full = masks
    c, n = chains[0][1].shape
    nch = len(chains)

    def stack(x_b):
        return jnp.where(bd_b, jnp.concatenate([x_b] * WKV_GROUP, axis=0), jnp.zeros((), bf16))

    cast = [tuple(x.astype(bf16) for x in ch[1:6]) for ch in chains]
    v_s = [stack(cb[4]) for cb in cast]
    g = [_dot_nt(jnp.concatenate([cb[0], cb[3]], axis=0), jnp.concatenate([stack(cb[1]), stack(cb[2])], axis=0))
         for cb in cast]
    l_ak = [jnp.where(strict, gi[:c, n:], 0.0).astype(bf16) for gi in g]
    m_rb = [jnp.where(incl, gi[c:, :n], 0.0).astype(bf16) for gi in g]
    m_rk = [jnp.where(incl, gi[c:, n:], 0.0).astype(bf16) for gi in g]
    x = [jnp.where(strict, gi[:c, :n], 0.0) for gi in g]
    t = [eye + xi for xi in x]
    for _ in range(5):
        xb = [xi.astype(bf16) for xi in x]
        x = [_dot(xi, stack(xi)) for xi in xb]
        t = [ti + _dot(ti.astype(bf16), stack(xi.astype(bf16))) for ti, xi in zip(t, x)]
    lakv = [_dot(l_ak[i], v_s[i]).astype(bf16) for i in range(nch)]
    au = [_dot(t[i].astype(bf16), jnp.concatenate([stack(cast[i][0]), stack(lakv[i])], axis=1))
          for i in range(nch)]
    abar = [a[:, :n].astype(bf16) for a in au]
    ubar = [a[:, n:].astype(bf16) for a in au]
    ry = [_dot(m_rb[i], jnp.concatenate([stack(abar[i]), stack(ubar[i])], axis=1)) for i in range(nch)]
    r_bar = [(chains[i][4] + ry[i][:, :n]).astype(bf16) for i in range(nch)]
    y_bar = [ry[i][:, n:] + _dot(m_rk[i], v_s[i]) for i in range(nch)]
    p = [((eye_full + jnp.where(bd, _dot_tn(abar[i], cast[i][1]), 0.0)) * chains[i][6]).astype(bf16)
         for i in range(nch)]
    q = []
    for i in range(nch):
        q_bd = jnp.where(bd, _dot_tn(jnp.concatenate([ubar[i], cast[i][4]], axis=0),
                                     jnp.concatenate([cast[i][1], cast[i][2]], axis=0)), 0.0)
        qi = q_bd[0:c]
        for h in range(1, WKV_GROUP):
            qi = qi + q_bd[h * c:(h + 1) * c]
        q.append(qi * chains[i][6])
    states = list(states)
    ys = []
    for i in range(nch):
        gi = chains[i][0]
        s_b = states[gi].astype(bf16)
        ys.append(_dot_nt(r_bar[i], stack(s_b)) + y_bar[i])
        states[gi] = _dot(s_b, p[i]) + q[i]
    return ys, states


def _rwkv_body(u_ref, mu_ref, wdec_ref, w0_ref, aup_ref, a0_ref, gup_ref, kk_ref, ka_ref, rk_ref,
               lng_ref, lnb_ref, seg_ref, tri_ref, *rest, tt, n_side):
    side_in = rest[:n_side]
    y_ref = rest[n_side]
    side_out = rest[n_side + 1:2 * n_side + 1]
    state_ref, carry_ref = rest[2 * n_side + 1:]
    j = pl.program_id(1)
    for src, dst in zip(side_in, side_out):
        dst[...] = src[...].astype(dst.dtype)

    @pl.when(j == 0)
    def _():
        state_ref[...] = jnp.zeros_like(state_ref)
        carry_ref[...] = jnp.zeros_like(carry_ref)

    w = RWKV_WIDTH
    u = u_ref[...]
    row = lax.broadcasted_iota(i32, u.shape, 0)
    prev = jnp.where(row == 0, carry_ref[0:1, :], pltpu.roll(u, 1, axis=0))
    carry_ref[0:1, :] = u[tt - 1:tt, :]
    us = u + (prev - u) * mu_ref[...]
    r = us[:, 0:w]
    k = us[:, w:2 * w]
    v = us[:, 2 * w:3 * w]
    wa = us[:, 3 * w:3 * w + DECAY_RANK + AAA_RANK]
    gd = us[:, 3 * w + DECAY_RANK + AAA_RANK:]
    z = w0_ref[...] + _dot_hp(jnp.tanh(wa), wdec_ref[...])
    softplus_neg_z = jnp.maximum(-z, 0.0) + jnp.log(1.0 + jnp.exp(-jnp.abs(z)))
    lw = -jnp.exp(-softplus_neg_z - 0.5)
    a = _sigmoid(a0_ref[...] + _dot_hp(wa, aup_ref[...]))
    gate = _dot(_sigmoid(gd).astype(bf16), gup_ref[...].astype(bf16))
    seg = seg_ref[...]
    kk = k * kk_ref[...]
    kmod = k * (1.0 + (a - 1.0) * ka_ref[...])
    kk_sq, bonus_dot = _seg_sums([kk * kk, r * kmod * rk_ref[...]], seg)
    kk = kk / jnp.maximum(jnp.sqrt(kk_sq), 1e-12)
    cum = _dot_exact_lhs(tri_ref[...], lw)
    wc = jnp.exp(cum)
    iwc = jnp.exp(-cum)
    at = -kk * jnp.exp(cum - lw)
    bt = kk * a * iwc
    kt = kmod * iwc
    rt = r * wc

    n = WKV_GROUP * HEAD_DIM
    ri = lax.broadcasted_iota(i32, (n, n), 0)
    ci = lax.broadcasted_iota(i32, (n, n), 1)
    bd = (ri // WKV_CHUNK) == (ci // HEAD_DIM)
    bd_b = jnp.where(bd, 1.0, 0.0).astype(bf16) > 0
    eye_full = jnp.where(ri == ci, 1.0, 0.0).astype(f32)
    ti = lax.broadcasted_iota(i32, (WKV_CHUNK, n), 0)
    si = lax.broadcasted_iota(i32, (WKV_CHUNK, n), 1) % WKV_CHUNK
    masks = (bd_b, bd, ti > si, ti >= si, jnp.where(ti == si, 1.0, 0.0).astype(f32), eye_full)

    n_groups = w // n
    n_chunks = tt // WKV_CHUNK
    chains = []
    for c in range(n_chunks):
        rs = slice(c * WKV_CHUNK, (c + 1) * WKV_CHUNK)
        last = (c + 1) * WKV_CHUNK - 1
        for gi in range(n_groups):
            cs = slice(gi * n, (gi + 1) * n)
            chains.append((gi, at[rs, cs], bt[rs, cs], kt[rs, cs], rt[rs, cs], v[rs, cs], wc[last:last + 1, cs]))
    ys, states = _wkv_chunks(chains, [state_ref[gi] for gi in range(n_groups)], masks)
    for gi in range(n_groups):
        state_ref[gi] = states[gi]
    y = jnp.concatenate([jnp.concatenate(ys[c * n_groups:(c + 1) * n_groups], axis=1) for c in range(n_chunks)],
                        axis=0)

    inv_n = 1.0 / HEAD_DIM
    d = y - _seg_sums([y], seg)[0] * inv_n
    var = _seg_sums([d * d], seg)[0] * inv_n
    yn = d * lax.rsqrt(var + GN_EPS) * lng_ref[...] + lnb_ref[...]
    y_ref[...] = ((yn + bonus_dot * v) * gate).astype(y_ref.dtype)


def _rwkv(u_r, mu_shift, w_decay_up, w0, a_up, a0, g_up, k_k, k_a, r_k, lnx_g, lnx_b, tt, side=(), layer=0):
    b, s, cols = u_r.shape
    tt = min(tt, s)
    n_steps = b * (s // tt)
    assert all(a.shape[1] % n_steps == 0 for a in side)
    w = RWKV_WIDTH
    row = lambda p: p.reshape(1, -1).astype(f32)
    wdec = jnp.concatenate([w_decay_up, jnp.zeros((AAA_RANK, w), f32)], axis=0)
    aup = jnp.concatenate([jnp.zeros((DECAY_RANK, w), f32), a_up], axis=0)
    hid = jnp.arange(w) // HEAD_DIM
    seg = (hid[:, None] == hid[None, :]).astype(bf16)
    ti = jnp.arange(tt)
    tri = ((ti[:, None] // WKV_CHUNK == ti[None, :] // WKV_CHUNK) & (ti[:, None] >= ti[None, :])).astype(bf16)
    params = [row(mu_shift), wdec, row(w0), aup, row(a0), g_up, row(k_k), row(k_a), row(r_k), row(lnx_g),
              row(lnx_b), seg, tri]
    n = WKV_GROUP * HEAD_DIM
    nj = s // tt
    side_in = [pl.BlockSpec((None, a.shape[1] // n_steps) + a.shape[2:], lambda bi, j: (layer, bi * nj + j, 0, 0))
               for a in side]
    side_out = [pl.BlockSpec((a.shape[1] // n_steps,) + a.shape[2:], lambda bi, j: (bi * nj + j, 0, 0))
                for a in side]
    outs = pl.pallas_call(
        functools.partial(_rwkv_body, tt=tt, n_side=len(side)),
        out_shape=[jax.ShapeDtypeStruct((b, s, w), bf16)] + [jax.ShapeDtypeStruct(a.shape[1:], bf16) for a in side],
        grid=(b, nj),
        in_specs=([pl.BlockSpec((None, tt, cols), lambda bi, j: (bi, j, 0))] + [_const_spec(p.shape) for p in params]
                  + side_in),
        out_specs=[pl.BlockSpec((None, tt, w), lambda bi, j: (bi, j, 0))] + side_out,
        scratch_shapes=[pltpu.VMEM((w // n, HEAD_DIM, n), f32), pltpu.VMEM((SUBLANES, cols), f32)],
        compiler_params=_cparams(("parallel", "arbitrary")),
        name="rwkv7",
    )(u_r, *params, *side)
    return outs[0], outs[1:]


def _swa_body(sink_ref, u_ref, pos_ref, invf_ref, o_ref, kprev_ref, vprev_ref, *, nwin):
    j = pl.program_id(1)

    @pl.when(j == 0)
    def _():
        kprev_ref[...] = jnp.zeros_like(kprev_ref)
        vprev_ref[...] = jnp.zeros_like(vprev_ref)

    wq = SWA_WIDTH
    kvw = SWA_KV_WIDTH
    half = HEAD_DIM // 2
    gw = SWA_GROUP * HEAD_DIM
    rows = SWA_GROUP * WINDOW
    lane_kv = lax.broadcasted_iota(i32, (WINDOW, kvw), 1)
    ri = lax.broadcasted_iota(i32, (rows, WINDOW), 0)
    ci = lax.broadcasted_iota(i32, (rows, WINDOW), 1)
    t_idx = ri % WINDOW
    mask_cur = ci <= t_idx
    mask_cur_b = jnp.where(mask_cur, 1.0, 0.0).astype(bf16) > 0
    rb = lax.broadcasted_iota(i32, (rows, 1), 0) // WINDOW
    bd = (lax.broadcasted_iota(i32, (rows, gw), 0) // WINDOW) == (lax.broadcasted_iota(i32, (rows, gw), 1) // HEAD_DIM)

    def rope(x, c, s):
        n = x.shape[1]
        lane = lax.broadcasted_iota(i32, x.shape, 1)
        rot = jnp.where((lane % HEAD_DIM) < half, -pltpu.roll(x, n - half, axis=1), pltpu.roll(x, half, axis=1))
        return x * c + rot * s

    def rep(x, gi):
        sw = pltpu.roll(x, HEAD_DIM, axis=1)
        one = jnp.where((lane_kv // HEAD_DIM) == gi, x, sw)
        return jnp.concatenate([one] * (gw // kvw), axis=1).astype(bf16)

    k_prev = kprev_ref[...]
    v_prev = vprev_ref[...]
    for wi in range(nwin):
        rs = slice(wi * WINDOW, (wi + 1) * WINDOW)
        u = u_ref[rs, :]
        ang_t = invf_ref[...] * pos_ref[wi].astype(f32)
        cos = jnp.concatenate([jnp.cos(ang_t)] * (LANES // half), axis=0).T
        sin = jnp.concatenate([jnp.sin(ang_t)] * (LANES // half), axis=0).T
        q = rope(u[:, :wq] * (HEAD_DIM ** -0.5), jnp.concatenate([cos] * (wq // LANES), axis=1),
                 jnp.concatenate([sin] * (wq // LANES), axis=1))
        k_cur = rope(u[:, wq:wq + kvw], cos, sin)
        v_cur = u[:, wq + kvw:]
        prev_bias = jnp.where(j > 0, 0.0, NEG_INF) if wi == 0 else 0.0
        outs = []
        for gi in range(SWA_KV_HEADS):
            qg = q[:, gi * gw:(gi + 1) * gw]
            q_bd = jnp.where(bd, jnp.concatenate([qg] * SWA_GROUP, axis=0), 0.0).astype(bf16)
            s = jnp.where(mask_cur, _dot_nt(q_bd, rep(k_cur, gi)), _dot_nt(q_bd, rep(k_prev, gi)) + prev_bias)
            sink = jnp.zeros((rows, 1), f32)
            for h in range(SWA_GROUP):
                sink = jnp.where(rb == h, sink_ref[gi * SWA_GROUP + h], sink)
            m = jnp.maximum(jnp.max(s, axis=-1, keepdims=True), sink)
            p = jnp.exp(s - m)
            denom = jnp.sum(p, axis=-1, keepdims=True) + jnp.exp(sink - m)
            p_b = p.astype(bf16)
            zero_b = jnp.zeros((), bf16)
            o_bd = (_dot(jnp.where(mask_cur_b, p_b, zero_b), rep(v_cur, gi))
                    + _dot(jnp.where(mask_cur_b, zero_b, p_b), rep(v_prev, gi)))
            o_bd = jnp.where(bd, o_bd * (1.0 / denom), 0.0)
            og = o_bd[0:WINDOW]
            for h in range(1, SWA_GROUP):
                og = og + o_bd[h * WINDOW:(h + 1) * WINDOW]
            outs.append(og)
        o_ref[rs, :] = jnp.concatenate(outs, axis=1).astype(o_ref.dtype)
        k_prev, v_prev = k_cur, v_cur
    kprev_ref[...] = k_prev
    vprev_ref[...] = v_prev


def _swa(u_s, positions, sinks, nwin=2):
    b, s, cols = u_s.shape
    half = HEAD_DIM // 2
    nwin = nwin if s % (nwin * WINDOW) == 0 else 1
    tile = nwin * WINDOW
    inv_freq = (ROPE_THETA ** (-jnp.arange(0, HEAD_DIM, 2, dtype=f32) / HEAD_DIM)).reshape(half, 1)
    pos = positions.reshape(b, s // WINDOW, 1, WINDOW).astype(i32)
    return pl.pallas_call(
        functools.partial(_swa_body, nwin=nwin),
        out_shape=jax.ShapeDtypeStruct((b, s, SWA_WIDTH), bf16),
        grid=(b, s // tile),
        in_specs=[pl.BlockSpec(memory_space=pltpu.SMEM),
                  pl.BlockSpec((None, tile, cols), lambda bi, j: (bi, j, 0)),
                  pl.BlockSpec((None, nwin, 1, WINDOW), lambda bi, j: (bi, j, 0, 0)),
                  _const_spec((half, 1))],
        out_specs=pl.BlockSpec((None, tile, SWA_WIDTH), lambda bi, j: (bi, j, 0)),
        scratch_shapes=[pltpu.VMEM((WINDOW, SWA_KV_WIDTH), f32), pltpu.VMEM((WINDOW, SWA_KV_WIDTH), f32)],
        compiler_params=_cparams(("parallel", "arbitrary")),
        name="swa",
    )(sinks.astype(f32), u_s, pos, inv_freq)


def _mix_xattn_body(ya_ref, yb_ref, wa_ref, wb_ref, xin_ref, g1_ref, b1_ref, kv_ref, wq_ref, wo_ref, g_ref, b_ref,
                    o_ref, o3_ref, *, alpha):
    mix = _dot(ya_ref[...], wa_ref[...]) + _dot(yb_ref[...], wb_ref[...])
    x = _layer_norm(alpha * xin_ref[...] + mix, g1_ref[...], b1_ref[...])
    d = x.shape[1]
    hd = d // MEM_HEADS
    q = _dot(x.astype(bf16), wq_ref[...]) * (hd ** -0.5)
    kv = kv_ref[...]
    outs = []
    for h in range(MEM_HEADS):
        qh = q[:, h * hd:(h + 1) * hd].astype(bf16)
        kh = kv[:, h * hd:(h + 1) * hd]
        vh = kv[:, d + h * hd:d + (h + 1) * hd]
        s = _dot_nt(qh, kh)
        p = jnp.exp(s - jnp.max(s, axis=-1, keepdims=True))
        l = jnp.sum(p, axis=-1, keepdims=True)
        outs.append(_dot(p.astype(bf16), vh) / l)
    o = jnp.concatenate(outs, axis=1)
    xa = _dot(o.astype(bf16), wo_ref[...])
    y = _layer_norm(alpha * x + xa, g_ref[...], b_ref[...])
    o_ref[...] = y
    words = _pack_bf16_pairs(y)
    for j, w in enumerate(words):
        o3_ref[_lane_chunk(x.shape[0], len(words), j)] = w


def _mix_xattn(ya, yb, w_o, x, g1, b1, kv, wm_q, wm_o, g2, b2, alpha, tile, batch0=0, n_batch=None):
    bsz, s, d = x.shape
    nb = bsz if n_batch is None else n_batch
    m = kv.shape[1]
    tile = min(tile, s)
    nj = s // tile
    wa = w_o[:ya.shape[2]].astype(bf16)
    wb = w_o[ya.shape[2]:].astype(bf16)
    wq = wm_q.astype(bf16)
    wo = wm_o.astype(bf16)
    rows_in = lambda width: pl.BlockSpec((None, tile, width), lambda bi, j: (bi + batch0, j, 0))
    vec = _const_spec((1, d))
    return pl.pallas_call(
        functools.partial(_mix_xattn_body, alpha=alpha),
        out_shape=[jax.ShapeDtypeStruct((nb, s, d), f32),
                   jax.ShapeDtypeStruct((nb * s * (d // (2 * LANES)), LANES), i32)],
        grid=(nb, nj),
        in_specs=[rows_in(ya.shape[2]), rows_in(yb.shape[2]), _const_spec(wa.shape), _const_spec(wb.shape), rows_in(d),
                  vec, vec, pl.BlockSpec((None, m, 2 * d), lambda bi, j: (bi + batch0, 0, 0)),
                  _const_spec(wq.shape), _const_spec(wo.shape), vec, vec],
        out_specs=[pl.BlockSpec((None, tile, d), lambda bi, j: (bi, j, 0)),
                   pl.BlockSpec((tile * (d // (2 * LANES)), LANES), lambda bi, j: (bi * nj + j, 0))],
        compiler_params=_cparams(("parallel", "parallel")),
        name="mix_xattn",
    )(ya, yb, wa, wb, x, g1.reshape(1, d), b1.reshape(1, d), kv, wq, wo, g2.reshape(1, d), b2.reshape(1, d))


def _router_body(x_ref, wt_ref, bias_ref, upper_ref, e_ref, g_ref, r_ref, cnt_out_ref, cnt_ref, *, t):
    @pl.when(pl.program_id(0) == 0)
    def _():
        cnt_ref[...] = jnp.zeros_like(cnt_ref)

    xh, xl = _split2(x_ref[...])
    wh, wl = _split2(wt_ref[...])
    logits = _dot_nt(wh, xh) + _dot_nt(wh, xl) + _dot_nt(wl, xh)
    scores = _sigmoid(logits)
    biased = scores + bias_ref[...][:, 0:1]
    ne = N_EXPERTS
    neg = -jnp.inf

    def top1(vals):
        rows = lax.broadcasted_iota(i32, vals.shape, 0).astype(f32)
        m = jnp.max(vals, axis=0, keepdims=True)
        idx = jnp.min(jnp.where(vals == m, rows, float(vals.shape[0])), axis=0, keepdims=True)
        return m, idx, rows == idx

    gscores = []
    for gi in range(N_GROUPS):
        blk = biased[gi * GROUP_SIZE:(gi + 1) * GROUP_SIZE, :]
        m1, _, hit = top1(blk)
        m2 = jnp.max(jnp.where(hit, neg, blk), axis=0, keepdims=True)
        gscores.append(m1 + m2)
    gs = jnp.concatenate(gscores, axis=0)
    gsel = jnp.zeros(gs.shape, f32)
    for _ in range(TOPK_GROUPS):
        _, _, hit = top1(gs)
        gsel = jnp.where(hit, 1.0, gsel)
        gs = jnp.where(hit, neg, gs)
    emask = jnp.concatenate(
        [jnp.broadcast_to(gsel[gi:gi + 1, :], (GROUP_SIZE, t)) for gi in range(N_GROUPS)], axis=0) > 0.5
    cand = jnp.where(emask, biased, NEG_INF)
    idxs, sels = [], []
    chosen = jnp.zeros((ne, t), f32)
    for _ in range(TOP_K):
        _, idx, hit = top1(cand)
        idxs.append(idx)
        sels.append(jnp.sum(jnp.where(hit, scores, 0.0), axis=0, keepdims=True))
        chosen = chosen + jnp.where(hit, 1.0, 0.0)
        cand = jnp.where(hit, neg, cand)
    sel = jnp.concatenate(sels, axis=0)
    g_ref[...] = sel / jnp.sum(sel, axis=0, keepdims=True) * ROUTED_SCALE
    e_ref[...] = jnp.concatenate(idxs, axis=0).astype(i32)
    before = _dot(chosen.astype(bf16), upper_ref[...]) + cnt_ref[...][:, 0:1]
    rows = lax.broadcasted_iota(i32, (ne, t), 0).astype(f32)
    ranks = [jnp.sum(jnp.where(rows == idx, before, 0.0), axis=0, keepdims=True) for idx in idxs]
    r_ref[...] = jnp.concatenate(ranks, axis=0).astype(i32)
    cnt_ref[...] = cnt_ref[...] + jnp.sum(chosen, axis=1, keepdims=True)
    cnt_out_ref[...] = cnt_ref[...].astype(i32)


def _router(x2, w_router, router_bias, tile, row0=0, n=None):
    d = x2.shape[1]
    n = x2.shape[0] if n is None else n
    ne = N_EXPERTS
    t = min(tile, n)
    assert row0 % t == 0 and n % t == 0
    first = row0 // t
    wt = w_router.T
    bias = jnp.broadcast_to(router_bias.reshape(ne, 1).astype(f32), (ne, LANES))
    ti = jnp.arange(t)
    upper = (ti[:, None] < ti[None, :]).astype(bf16)
    cols = pl.BlockSpec((TOP_K, t), lambda i: (0, i))
    e_t, g_t, r_t, cnt = pl.pallas_call(
        functools.partial(_router_body, t=t),
        out_shape=[jax.ShapeDtypeStruct((TOP_K, n), i32), jax.ShapeDtypeStruct((TOP_K, n), f32),
                   jax.ShapeDtypeStruct((TOP_K, n), i32), jax.ShapeDtypeStruct((ne, LANES), i32)],
        grid=(n // t,),
        in_specs=[pl.BlockSpec((t, d), lambda i: (i + first, 0)), _const_spec((ne, d)), _const_spec((ne, LANES)),
                  _const_spec((t, t))],
        out_specs=[cols, cols, cols, _const_spec((ne, LANES))],
        scratch_shapes=[pltpu.VMEM((ne, LANES), f32)],
        compiler_params=_cparams(("arbitrary",)),
        name="router",
    )(x2, wt, bias, upper)
    return e_t, g_t, r_t, cnt[:, 0]


def _dispatch_sc(x3d, e_t, r_t, g_t, starts, n_slots, tok0=0):
    _, nc, lanes = x3d.shape
    n = e_t.shape[1]
    workers = SC_CORES * SC_SUBCORES
    per_w = n // workers
    tc = _sc_chunk_rows(nc, lanes, x3d.dtype)
    assert n % (workers * SC_INDEX_GROUP) == 0 and SC_INDEX_GROUP % tc == 0
    mesh = plsc.VectorSubcoreMesh(core_axis_name="c", subcore_axis_name="s")

    @functools.partial(
        pl.kernel, mesh=mesh,
        out_type=[jax.ShapeDtypeStruct((n_slots, nc, lanes), x3d.dtype),
                  jax.ShapeDtypeStruct((n_slots,), f32)],
        scratch_types=[
            pltpu.VMEM((tc, nc, lanes), x3d.dtype),
            pltpu.VMEM((TOP_K, SC_INDEX_GROUP), i32),
            pltpu.VMEM((TOP_K, SC_INDEX_GROUP), i32),
            pltpu.VMEM((TOP_K, SC_INDEX_GROUP), f32),
            pltpu.VMEM((TOP_K, tc), i32),
            pltpu.VMEM((TOP_K, tc), f32),
            pltpu.VMEM((N_EXPERTS,), i32),
            pltpu.SemaphoreType.DMA,
        ],
        compiler_params=pltpu.CompilerParams(use_tc_tiling_on_sc=True, needs_layout_passes=False),
    )
    def dispatch(x_hbm, e_hbm, r_hbm, g_hbm, st_hbm, o_hbm, gs_hbm, rows_v, e_v, r_v, g_v, slot_v, gate_v, st_v, sem):
        wid = lax.axis_index("s") * SC_CORES + lax.axis_index("c")
        pltpu.sync_copy(st_hbm, st_v)

        @pl.loop(0, per_w // SC_INDEX_GROUP)
        def _(gi):
            base = wid * per_w + gi * SC_INDEX_GROUP
            pltpu.sync_copy(e_hbm.at[:, pl.ds(base, SC_INDEX_GROUP)], e_v)
            pltpu.sync_copy(r_hbm.at[:, pl.ds(base, SC_INDEX_GROUP)], r_v)
            pltpu.sync_copy(g_hbm.at[:, pl.ds(base, SC_INDEX_GROUP)], g_v)
            for h in range(SC_INDEX_GROUP // tc):
                off = h * tc
                pltpu.sync_copy(x_hbm.at[pl.ds(tok0 + base + off, tc)], rows_v)
                for kk in range(TOP_K):
                    for j in range(tc // SC_LANES):
                        src = pl.ds(off + j * SC_LANES, SC_LANES)
                        dst = pl.ds(j * SC_LANES, SC_LANES)
                        slot_v[kk, dst] = r_v[kk, src] + plsc.load_gather(st_v, [e_v[kk, src]])
                        gate_v[kk, dst] = g_v[kk, src]
                copies = [pltpu.async_copy(rows_v, o_hbm.at[slot_v.at[kk]], sem) for kk in range(TOP_K)]
                copies += [pltpu.async_copy(gate_v.at[kk], gs_hbm.at[slot_v.at[kk]], sem) for kk in range(TOP_K)]
                for cp in copies:
                    cp.wait()

    return dispatch(x3d, e_t, r_t, g_t, starts)


def _gmm_body(gid_ref, tid_ref, nrows_ref, newg_ref, nextg_ref, ord_ref, x_ref, g_ref,
              wg_hbm, wu_hbm, wd_hbm, o_ref, wg_b, wu_b, wd_b, sems, *, tm):
    v = pl.program_id(0)
    n_real = nrows_ref[v]
    nc = wg_b.shape[1] // LANES
    slot = ord_ref[v] % 2

    def weight_copies(expert, dst_slot):
        return [pltpu.make_async_copy(src.at[expert], dst.at[dst_slot], sems.at[dst_slot])
                for src, dst in ((wg_hbm, wg_b), (wu_hbm, wu_b), (wd_hbm, wd_b))]

    @pl.when(newg_ref[v] == 1)
    def _():
        @pl.when(v == 0)
        def _():
            for cp in weight_copies(gid_ref[v], slot):
                cp.start()

        for cp in weight_copies(gid_ref[v], slot):
            cp.wait()

        @pl.when(nextg_ref[v] >= 0)
        def _():
            for cp in weight_copies(nextg_ref[v], 1 - slot):
                cp.start()

    @pl.when(n_real > 0)
    def _():
        nw = nc // 2
        real = lax.broadcasted_iota(i32, (tm, 1), 0) < n_real
        x = _unpack_bf16_pairs([x_ref[_lane_chunk(tm, nw, j)] for j in range(nw)])
        x = jnp.where(real, x, 0.0).astype(bf16)
        hg = _dot(x, wg_b[slot])
        h = hg * _sigmoid(hg) * _dot(x, wu_b[slot])
        g = g_ref[...]
        g_cols = jnp.concatenate([g, jnp.zeros((SUBLANES - g.shape[0], LANES), f32)], axis=0).T
        g_col = jnp.concatenate([g_cols[:, r:r + 1] for r in range(tm // LANES)], axis=0)
        h = h * jnp.where(real, g_col, 0.0)
        y = _dot(h.astype(bf16), wd_b[slot])
        for c in range(nc):
            o_ref[_lane_chunk(tm, nc, c)] = y[:, c * LANES:(c + 1) * LANES]


def _expert_layout(counts, n_assign, tm):
    tiles = (counts + tm - 1) // tm
    starts = ((jnp.cumsum(tiles) - tiles) * tm).astype(i32)
    capacity = (n_assign // tm + counts.shape[0]) * tm
    return starts, tiles, capacity


def _gmm(xs, gates, we_gate, we_up, we_down, counts, tm):
    ne, d, de = we_gate.shape
    nc = d // LANES
    nw = nc // 2
    n_tiles = xs.shape[0] // (nw * tm)
    assert tm % LANES == 0 and tm // LANES <= SUBLANES
    gates3 = gates.reshape(n_tiles, tm // LANES, LANES)
    tiles = (counts + tm - 1) // tm
    tile_end = jnp.cumsum(tiles)
    tile_start = tile_end - tiles
    vi = jnp.arange(n_tiles, dtype=i32)
    valid = vi < tile_end[-1]
    gid = jnp.minimum(jnp.sum((tile_end[None, :] <= vi[:, None]).astype(i32), axis=1), ne - 1)
    gid = jnp.where(valid, gid, jnp.max(jnp.where(valid, gid, 0)))
    onehot = gid[:, None] == jnp.arange(ne, dtype=i32)[None, :]
    pick = lambda table: jnp.sum(jnp.where(onehot, table[None, :], 0), axis=1)
    tid = jnp.minimum(vi, tile_end[-1] - 1).astype(i32)
    n_real = jnp.where(valid, jnp.clip(pick(counts) - (vi - pick(tile_start)) * tm, 0, tm), 0).astype(i32)
    one = jnp.ones((1,), i32)
    newg = jnp.concatenate([one, (gid[1:] != gid[:-1]).astype(i32)])
    later = gid[None, :] > gid[:, None]
    nextg = jnp.min(jnp.where(later, gid[None, :], ne), axis=1)
    nextg = jnp.where(nextg < ne, nextg, -1).astype(i32)
    order = (jnp.cumsum(newg) - 1).astype(i32)
    rows = lambda chunks: pl.BlockSpec((tm * chunks, LANES), lambda v, g, t, *_: (t[v], 0))
    hbm = pl.BlockSpec(memory_space=pl.ANY)
    return pl.pallas_call(
        functools.partial(_gmm_body, tm=tm),
        out_shape=jax.ShapeDtypeStruct((n_tiles * tm * nc, LANES), f32),
        grid_spec=pltpu.PrefetchScalarGridSpec(
            num_scalar_prefetch=6, grid=(n_tiles,),
            in_specs=[rows(nw), pl.BlockSpec((None, tm // LANES, LANES), lambda v, g, t, *_: (t[v], 0, 0)),
                      hbm, hbm, hbm],
            out_specs=rows(nc),
            scratch_shapes=[pltpu.VMEM((2, d, de), bf16), pltpu.VMEM((2, d, de), bf16), pltpu.VMEM((2, de, d), bf16),
                            pltpu.SemaphoreType.DMA((2,))]),
        compiler_params=_cparams(("arbitrary",)),
        name="moe_experts",
    )(gid, tid, n_real, newg, nextg, order, xs, gates3, we_gate, we_up, we_down)


def _combine_sc(ys3d, e_t, r_t, starts, n):
    _, nc, lanes = ys3d.shape
    workers = SC_CORES * SC_SUBCORES
    per_w = n // workers
    tc = _sc_chunk_rows(nc, lanes, f32)
    assert n % (workers * SC_INDEX_GROUP) == 0 and SC_INDEX_GROUP % tc == 0
    mesh = plsc.VectorSubcoreMesh(core_axis_name="c", subcore_axis_name="s")

    @functools.partial(
        pl.kernel, mesh=mesh,
        out_type=jax.ShapeDtypeStruct((n, nc, lanes), f32),
        scratch_types=[
            pltpu.VMEM((tc, nc, lanes), f32),
            pltpu.VMEM((TOP_K, SC_INDEX_GROUP), i32),
            pltpu.VMEM((TOP_K, SC_INDEX_GROUP), i32),
            pltpu.VMEM((TOP_K, tc), i32),
            pltpu.VMEM((N_EXPERTS,), i32),
            pltpu.SemaphoreType.DMA,
        ],
        compiler_params=pltpu.CompilerParams(use_tc_tiling_on_sc=True, needs_layout_passes=False),
    )
    def combine(y_hbm, e_hbm, r_hbm, st_hbm, o_hbm, acc_v, e_v, r_v, slot_v, st_v, sem):
        wid = lax.axis_index("s") * SC_CORES + lax.axis_index("c")
        pltpu.sync_copy(st_hbm, st_v)

        @pl.loop(0, per_w // SC_INDEX_GROUP)
        def _(gi):
            base = wid * per_w + gi * SC_INDEX_GROUP
            pltpu.sync_copy(e_hbm.at[:, pl.ds(base, SC_INDEX_GROUP)], e_v)
            pltpu.sync_copy(r_hbm.at[:, pl.ds(base, SC_INDEX_GROUP)], r_v)
            for h in range(SC_INDEX_GROUP // tc):
                off = h * tc
                for kk in range(TOP_K):
                    for j in range(tc // SC_LANES):
                        src = pl.ds(off + j * SC_LANES, SC_LANES)
                        slot_v[kk, pl.ds(j * SC_LANES, SC_LANES)] = (
                            r_v[kk, src] + plsc.load_gather(st_v, [e_v[kk, src]]))
                pltpu.async_copy(y_hbm.at[slot_v.at[0]], acc_v, sem).wait()
                copies = [pltpu.async_copy(y_hbm.at[slot_v.at[kk]], acc_v, sem, add=True) for kk in range(1, TOP_K)]
                for cp in copies:
                    cp.wait()
                pltpu.sync_copy(acc_v, o_hbm.at[pl.ds(base + off, tc)])

    return combine(ys3d, e_t, r_t, starts)


def _ffn_out_body(x_ref, r_ref, wg_ref, wu_ref, wd_ref, g_ref, b_ref, *rest, alpha):
    o_ref = rest[-1]
    x = x_ref[...]
    xb = x.astype(bf16)
    hg = _dot(xb, wg_ref[...])
    h = hg * _sigmoid(hg) * _dot(xb, wu_ref[...])
    shared = _dot(h.astype(bf16), wd_ref[...])
    nc = x.shape[1] // LANES
    routed = jnp.concatenate([r_ref[_lane_chunk(x.shape[0], nc, c)] for c in range(nc)], axis=1)
    o_ref[...] = _layer_norm(alpha * x + routed + shared, g_ref[...], b_ref[...])


def _ffn_out(x2, routed_rows, ws_gate, ws_up, ws_down, g, b, alpha, tile, n_all=None, row0=0, prev=None):
    n, d = x2.shape
    n_all = n if n_all is None else n_all
    tile = min(tile, n)
    assert row0 % tile == 0 and n % tile == 0
    first = row0 // tile
    wg, wu, wd = ws_gate.astype(bf16), ws_up.astype(bf16), ws_down.astype(bf16)
    rows = pl.BlockSpec((tile, d), lambda i: (i + first, 0))
    in_specs = [pl.BlockSpec((tile, d), lambda i: (i, 0)),
                pl.BlockSpec((tile * (d // LANES), LANES), lambda i: (i, 0)), _const_spec(wg.shape),
                _const_spec(wu.shape), _const_spec(wd.shape), _const_spec((1, d)), _const_spec((1, d))]
    args = [x2, routed_rows, wg, wu, wd, g.reshape(1, d), b.reshape(1, d)]
    aliases = {}
    if prev is not None:
        in_specs.append(pl.BlockSpec(memory_space=pl.ANY))
        args.append(prev)
        aliases = {len(args) - 1: 0}
    return pl.pallas_call(
        functools.partial(_ffn_out_body, alpha=alpha),
        out_shape=jax.ShapeDtypeStruct((n_all, d), f32),
        grid=(n // tile,),
        in_specs=in_specs,
        out_specs=rows,
        input_output_aliases=aliases,
        compiler_params=_cparams(("parallel",)),
        name="ffn_out_ln3",
    )(*args)


def _layer(x, mem, positions, w_in, mu_shift, w_decay_up, w0, a_up, a0, g_up, k_k, k_a, r_k, lnx_g, lnx_b, sinks,
           w_o, ln1_g, ln1_b, wm_q, wm_kv, wm_o, ln2_g, ln2_b, w_router, router_bias, we_gate, we_up, we_down,
           ws_gate, ws_up, ws_down, ln3_g, ln3_b, *, layer, alpha):
    b, s, d = x.shape
    n = b * s
    xf = x.reshape(n, d)
    w_in_b = w_in.astype(bf16)
    u_r, u_s = _proj(xf, [w_in_b[:, :RWKV_COLS], w_in_b[:, RWKV_COLS:]], [f32, f32], tile=512)
    rwkv_tt = 256
    experts = (we_gate, we_up, we_down)
    if N_EXPERTS % (b * (s // min(rwkv_tt, s))) == 0:
        y_r, experts_b = _rwkv(u_r.reshape(b, s, RWKV_COLS), mu_shift, w_decay_up, w0, a_up, a0, g_up, k_k, k_a, r_k,
                               lnx_g, lnx_b, tt=rwkv_tt, side=experts, layer=layer)
    else:
        y_r, _ = _rwkv(u_r.reshape(b, s, RWKV_COLS), mu_shift, w_decay_up, w0, a_up, a0, g_up, k_k, k_a, r_k,
                       lnx_g, lnx_b, tt=rwkv_tt)
        experts_b = [w[layer].astype(bf16) for w in experts]
    y_s = _swa(u_s.reshape(b, s, SWA_COLS), positions, sinks)
    m = mem.shape[1]
    (kv,) = _proj(mem.reshape(b * m, d), [wm_kv.astype(bf16)], [bf16], tile=512)
    nc = d // LANES
    nw = nc // 2
    parts = MOE_PARTS
    if b % parts or (n // parts) % (SC_CORES * SC_SUBCORES * SC_INDEX_GROUP):
        parts = 1
    npart = n // parts
    bpp = b // parts
    x3 = None
    for part in range(parts):
        x2, x2_rows = _mix_xattn(y_r, y_s, w_o, x, ln1_g, ln1_b, kv.reshape(b, m, 2 * d), wm_q, wm_o, ln2_g, ln2_b,
                                 alpha, tile=512, batch0=part * bpp, n_batch=bpp)
        x2 = x2.reshape(npart, d)
        e_t, g_t, r_t, counts = _router(x2, w_router, router_bias, tile=512)
        starts, _, n_slots = _expert_layout(counts, npart * TOP_K, GMM_TILE)
        xs, gs = _dispatch_sc(x2_rows.reshape(npart, nw, LANES), e_t, r_t, g_t, starts, n_slots)
        ys = _gmm(xs.reshape(n_slots * nw, LANES), gs, *experts_b, counts, tm=GMM_TILE)
        routed = _combine_sc(ys.reshape(n_slots, nc, LANES), e_t, r_t, starts, npart)
        x3 = _ffn_out(x2, routed.reshape(npart * nc, LANES), ws_gate, ws_up, ws_down, ln3_g, ln3_b, alpha, tile=512,
                      n_all=n, row0=part * npart, prev=x3)
    return x3.reshape(b, s, d)


def kernel(x, mem, positions, w_in, mu_shift, w_decay_up, w0, a_up, a0, g_up, k_k, k_a, r_k, lnx_g, lnx_b, sinks, w_o, ln1_g, ln1_b, wm_q, wm_kv, wm_o, ln2_g, ln2_b, w_router, router_bias, we_gate, we_up, we_down, ws_gate, ws_up, ws_down, ln3_g, ln3_b):
    depth = w_in.shape[0]
    alpha = (2 * depth) ** 0.25
    for l in range(depth):
        x = _layer(x, mem, positions, w_in[l], mu_shift[l], w_decay_up[l], w0[l], a_up[l], a0[l], g_up[l], k_k[l],
                   k_a[l], r_k[l], lnx_g[l], lnx_b[l], sinks[l], w_o[l], ln1_g[l], ln1_b[l], wm_q[l], wm_kv[l],
                   wm_o[l], ln2_g[l], ln2_b[l], w_router[l], router_bias[l], we_gate, we_up, we_down,
                   ws_gate[l], ws_up[l], ws_down[l], ln3_g[l], ln3_b[l], layer=l, alpha=alpha)
    return x
```

```python
import functools

import jax
import jax.numpy as jnp
from jax import lax
from jax.experimental import pallas as pl
from jax.experimental.pallas import tpu as pltpu
from jax.experimental.pallas import tpu_sc as plsc

f32 = jnp.float32
bf16 = jnp.bfloat16
i32 = jnp.int32

RWKV_HEADS = 8
HEAD_DIM = 64
RWKV_WIDTH = RWKV_HEADS * HEAD_DIM
DECAY_RANK = 64
AAA_RANK = 64
GATE_RANK = 128
RWKV_COLS = 3 * RWKV_WIDTH + DECAY_RANK + AAA_RANK + GATE_RANK
SWA_Q_HEADS = 8
SWA_KV_HEADS = 2
SWA_GROUP = SWA_Q_HEADS // SWA_KV_HEADS
SWA_WIDTH = SWA_Q_HEADS * HEAD_DIM
SWA_KV_WIDTH = SWA_KV_HEADS * HEAD_DIM
SWA_COLS = SWA_WIDTH + 2 * SWA_KV_WIDTH
WINDOW = 128
ROPE_THETA = 10000.0
MEM_HEADS = 4
N_EXPERTS = 256
TOP_K = 8
N_GROUPS = 8
GROUP_SIZE = N_EXPERTS // N_GROUPS
TOPK_GROUPS = 4
ROUTED_SCALE = 2.5
LN_EPS = 1e-5
GN_EPS = 64e-5
NEG_INF = -1e30

LANES = 128
SUBLANES = 8
WKV_CHUNK = 64
WKV_GROUP = 2
ROW_TILE = 512
RWKV_TILE = 256
GMM_TILE = 512
VMEM_LIMIT = 56 * 1024 * 1024

SC_CORES = 2
SC_SUBCORES = 16
SC_LANES = 16
SC_INDEX_GROUP = 128
SC_CHUNK_BYTES = 256 * 1024
MOE_PARTS = 2


def _sc_chunk_rows(nc, lanes, dtype):
    return min(SC_INDEX_GROUP, SC_CHUNK_BYTES // (nc * lanes * jnp.dtype(dtype).itemsize))


def _cparams(sem):
    return pltpu.CompilerParams(dimension_semantics=sem, vmem_limit_bytes=VMEM_LIMIT)


def _const_spec(shape):
    nd = len(shape)
    return pl.BlockSpec(shape, lambda *_: (0,) * nd)


def _dot(a, b):
    return jnp.dot(a, b, preferred_element_type=f32)


def _dot_nt(a, b):
    return lax.dot_general(a, b, (((1,), (1,)), ((), ())), preferred_element_type=f32)


def _dot_tn(a, b):
    return lax.dot_general(a, b, (((0,), (0,)), ((), ())), preferred_element_type=f32)


def _split2(x):
    hi = x.astype(bf16)
    lo = (x - hi.astype(f32)).astype(bf16)
    return hi, lo


def _seg_sums(xs, seg_b):
    parts = []
    for x in xs:
        parts.extend(_split2(x))
    out = _dot(jnp.concatenate(parts, axis=0), seg_b)
    t = xs[0].shape[0]
    return [out[2 * i * t:(2 * i + 1) * t] + out[(2 * i + 1) * t:(2 * i + 2) * t] for i in range(len(xs))]


def _dot_hp(a, b):
    ah, al = _split2(a)
    bh, bl = _split2(b)
    return _dot(ah, bh) + _dot(ah, bl) + _dot(al, bh)


def _dot_exact_lhs(m_bf16, x):
    hi, lo = _split2(x)
    return _dot(m_bf16, hi) + _dot(m_bf16, lo)


def _sigmoid(x):
    return 1.0 / (1.0 + jnp.exp(-x))


def _lane_chunk(n_rows, n_chunks, c, row0=0):
    return (pl.ds(row0 * n_chunks + c, n_rows, stride=n_chunks), slice(None))


def _pack_bf16_pairs(x):
    chunks = []
    for j in range(x.shape[1] // (2 * LANES)):
        lo = x[:, 2 * j * LANES:(2 * j + 1) * LANES].astype(bf16).astype(f32)
        hi = x[:, (2 * j + 1) * LANES:(2 * j + 2) * LANES].astype(bf16).astype(f32)
        chunks.append(lax.bitcast_convert_type(hi, i32) | lax.shift_right_logical(lax.bitcast_convert_type(lo, i32), 16))
    return chunks


def _unpack_bf16_pairs(chunks):
    cols = []
    for w in chunks:
        cols.append(lax.bitcast_convert_type(lax.shift_left(w, 16), f32))
        cols.append(lax.bitcast_convert_type(w & jnp.int32(-65536), f32))
    return jnp.concatenate(cols, axis=1)


def _layer_norm(h, g, b):
    mu = jnp.mean(h, axis=-1, keepdims=True)
    d = h - mu
    var = jnp.mean(d * d, axis=-1, keepdims=True)
    return d * lax.rsqrt(var + LN_EPS) * g + b


def _proj_body(*refs, n_out):
    x_ref = refs[0]
    w_refs = refs[1:1 + n_out]
    o_refs = refs[1 + n_out:]
    xb = x_ref[...].astype(bf16)
    for w_ref, o_ref in zip(w_refs, o_refs):
        o_ref[...] = _dot(xb, w_ref[...]).astype(o_ref.dtype)


def _proj(x, ws, out_dtypes, tile):
    n, k = x.shape
    tile = min(tile, n)
    outs = pl.pallas_call(
        functools.partial(_proj_body, n_out=len(ws)),
        out_shape=[jax.ShapeDtypeStruct((n, w.shape[1]), dt) for w, dt in zip(ws, out_dtypes)],
        grid=(n // tile,),
        in_specs=[pl.BlockSpec((tile, k), lambda i: (i, 0))] + [_const_spec(w.shape) for w in ws],
        out_specs=[pl.BlockSpec((tile, w.shape[1]), lambda i: (i, 0)) for w in ws],
        compiler_params=_cparams(("parallel",)),
        name="proj",
    )(x, *ws)
    return outs


def _wkv_chunks(chains, states, masks):
    bd_b, bd, strict, incl, eye, eye_full = masks
    c, n = chains[0][1].shape
    nch = len(chains)

    def stack(x_b):
        return jnp.where(bd_b, jnp.concatenate([x_b] * WKV_GROUP, axis=0), jnp.zeros((), bf16))

    cast = [tuple(x.astype(bf16) for x in ch[1:6]) for ch in chains]
    v_s = [stack(cb[4]) for cb in cast]
    g = [_dot_nt(jnp.concatenate([cb[0], cb[3]], axis=0), jnp.concatenate([stack(cb[1]), stack(cb[2])], axis=0))
         for cb in cast]
    l_ak = [jnp.where(strict, gi[:c, n:], 0.0).astype(bf16) for gi in g]
    m_rb = [jnp.where(incl, gi[c:, :n], 0.0).astype(bf16) for gi in g]
    m_rk = [jnp.where(incl, gi[c:, n:], 0.0).astype(bf16) for gi in g]
    x = [jnp.where(strict, gi[:c, :n], 0.0) for gi in g]
    t = [eye + xi for xi in x]
    for _ in range(5):
        xb = [xi.astype(bf16) for xi in x]
        x = [_dot(xi, stack(xi)) for xi in xb]
        t = [ti + _dot(ti.astype(bf16), stack(xi.astype(bf16))) for ti, xi in zip(t, x)]
    lakv = [_dot(l_ak[i], v_s[i]).astype(bf16) for i in range(nch)]
    au = [_dot(t[i].astype(bf16), jnp.concatenate([stack(cast[i][0]), stack(lakv[i])], axis=1))
          for i in range(nch)]
    abar = [a[:, :n].astype(bf16) for a in au]
    ubar = [a[:, n:].astype(bf16) for a in au]
    ry = [_dot(m_rb[i], jnp.concatenate([stack(abar[i]), stack(ubar[i])], axis=1)) for i in range(nch)]
    r_bar = [(chains[i][4] + ry[i][:, :n]).astype(bf16) for i in range(nch)]
    y_bar = [ry[i][:, n:] + _dot(m_rk[i], v_s[i]) for i in range(nch)]
    p = [((eye_full + jnp.where(bd, _dot_tn(abar[i], cast[i][1]), 0.0)) * chains[i][6]).astype(bf16)
         for i in range(nch)]
    q = []
    for i in range(nch):
        q_bd = jnp.where(bd, _dot_tn(jnp.concatenate([ubar[i], cast[i][4]], axis=0),
                                     jnp.concatenate([cast[i][1], cast[i][2]], axis=0)), 0.0)
        qi = q_bd[0:c]
        for h in range(1, WKV_GROUP):
            qi = qi + q_bd[h * c:(h + 1) * c]
        q.append(qi * chains[i][6])
    states = list(states)
    ys = []
    for i in range(nch):
        gi = chains[i][0]
        s_b = states[gi].astype(bf16)
        ys.append(_dot_nt(r_bar[i], stack(s_b)) + y_bar[i])
        states[gi] = _dot(s_b, p[i]) + q[i]
    return ys, states


def _rwkv_body(u_ref, mu_ref, wdec_ref, w0_ref, aup_ref, a0_ref, gup_ref, kk_ref, ka_ref, rk_ref,
               lng_ref, lnb_ref, seg_ref, tri_ref, *rest, tt, n_side):
    side_in = rest[:n_side]
    y_ref = rest[n_side]
    side_out = rest[n_side + 1:2 * n_side + 1]
    state_ref, carry_ref = rest[2 * n_side + 1:]
    j = pl.program_id(1)
    for src, dst in zip(side_in, side_out):
        dst[...] = src[...].astype(dst.dtype)

    @pl.when(j == 0)
    def _():
        state_ref[...] = jnp.zeros_like(state_ref)
        carry_ref[...] = jnp.zeros_like(carry_ref)

    w = RWKV_WIDTH
    u = u_ref[...]
    row = lax.broadcasted_iota(i32, u.shape, 0)
    prev = jnp.where(row == 0, carry_ref[0:1, :], pltpu.roll(u, 1, axis=0))
    carry_ref[0:1, :] = u[tt - 1:tt, :]
    us = u + (prev - u) * mu_ref[...]
    r = us[:, 0:w]
    k = us[:, w:2 * w]
    v = us[:, 2 * w:3 * w]
    wa = us[:, 3 * w:3 * w + DECAY_RANK + AAA_RANK]
    gd = us[:, 3 * w + DECAY_RANK + AAA_RANK:]
    z = w0_ref[...] + _dot_hp(jnp.tanh(wa), wdec_ref[...])
    softplus_neg_z = jnp.maximum(-z, 0.0) + jnp.log(1.0 + jnp.exp(-jnp.abs(z)))
    lw = -jnp.exp(-softplus_neg_z - 0.5)
    a = _sigmoid(a0_ref[...] + _dot_hp(wa, aup_ref[...]))
    gate = _dot(_sigmoid(gd).astype(bf16), gup_ref[...].astype(bf16))
    seg = seg_ref[...]
    kk = k * kk_ref[...]
    kmod = k * (1.0 + (a - 1.0) * ka_ref[...])
    kk_sq, bonus_dot = _seg_sums([kk * kk, r * kmod * rk_ref[...]], seg)
    kk = kk / jnp.maximum(jnp.sqrt(kk_sq), 1e-12)
    cum = _dot_exact_lhs(tri_ref[...], lw)
    wc = jnp.exp(cum)
    iwc = jnp.exp(-cum)
    at = -kk * jnp.exp(cum - lw)
    bt = kk * a * iwc
    kt = kmod * iwc
    rt = r * wc

    n = WKV_GROUP * HEAD_DIM
    ri = lax.broadcasted_iota(i32, (n, n), 0)
    ci = lax.broadcasted_iota(i32, (n, n), 1)
    bd = (ri // WKV_CHUNK) == (ci // HEAD_DIM)
    bd_b = jnp.where(bd, 1.0, 0.0).astype(bf16) > 0
    eye_full = jnp.where(ri == ci, 1.0, 0.0).astype(f32)
    ti = lax.broadcasted_iota(i32, (WKV_CHUNK, n), 0)
    si = lax.broadcasted_iota(i32, (WKV_CHUNK, n), 1) % WKV_CHUNK
    masks = (bd_b, bd, ti > si, ti >= si, jnp.where(ti == si, 1.0, 0.0).astype(f32), eye_full)

    n_groups = w // n
    n_chunks = tt // WKV_CHUNK
    chains = []
    for c in range(n_chunks):
        rs = slice(c * WKV_CHUNK, (c + 1) * WKV_CHUNK)
        last = (c + 1) * WKV_CHUNK - 1
        for gi in range(n_groups):
            cs = slice(gi * n, (gi + 1) * n)
            chains.append((gi, at[rs, cs], bt[rs, cs], kt[rs, cs], rt[rs, cs], v[rs, cs], wc[last:last + 1, cs]))
    ys, states = _wkv_chunks(chains, [state_ref[gi] for gi in range(n_groups)], masks)
    for gi in range(n_groups):
        state_ref[gi] = states[gi]
    y = jnp.concatenate([jnp.concatenate(ys[c * n_groups:(c + 1) * n_groups], axis=1) for c in range(n_chunks)],
                        axis=0)

    inv_n = 1.0 / HEAD_DIM
    d = y - _seg_sums([y], seg)[0] * inv_n
    var = _seg_sums([d * d], seg)[0] * inv_n
    yn = d * lax.rsqrt(var + GN_EPS) * lng_ref[...] + lnb_ref[...]
    y_ref[...] = ((yn + bonus_dot * v) * gate).astype(y_ref.dtype)


def _rwkv(u_r, mu_shift, w_decay_up, w0, a_up, a0, g_up, k_k, k_a, r_k, lnx_g, lnx_b, tt, side=(), layer=0):
    b, s, cols = u_r.shape
    tt = min(tt, s)
    n_steps = b * (s // tt)
    assert all(a.shape[1] % n_steps == 0 for a in side)
    w = RWKV_WIDTH
    row = lambda p: p.reshape(1, -1).astype(f32)
    wdec = jnp.concatenate([w_decay_up, jnp.zeros((AAA_RANK, w), f32)], axis=0)
    aup = jnp.concatenate([jnp.zeros((DECAY_RANK, w), f32), a_up], axis=0)
    hid = jnp.arange(w) // HEAD_DIM
    seg = (hid[:, None] == hid[None, :]).astype(bf16)
    ti = jnp.arange(tt)
    tri = ((ti[:, None] // WKV_CHUNK == ti[None, :] // WKV_CHUNK) & (ti[:, None] >= ti[None, :])).astype(bf16)
    params = [row(mu_shift), wdec, row(w0), aup, row(a0), g_up, row(k_k), row(k_a), row(r_k), row(lnx_g),
              row(lnx_b), seg, tri]
    n = WKV_GROUP * HEAD_DIM
    nj = s // tt
    side_in = [pl.BlockSpec((None, a.shape[1] // n_steps) + a.shape[2:], lambda bi, j: (layer, bi * nj + j, 0, 0))
               for a in side]
    side_out = [pl.BlockSpec((a.shape[1] // n_steps,) + a.shape[2:], lambda bi, j: (bi * nj + j, 0, 0))
                for a in side]
    outs = pl.pallas_call(
        functools.partial(_rwkv_body, tt=tt, n_side=len(side)),
        out_shape=[jax.ShapeDtypeStruct((b, s, w), bf16)] + [jax.ShapeDtypeStruct(a.shape[1:], bf16) for a in side],
        grid=(b, nj),
        in_specs=([pl.BlockSpec((None, tt, cols), lambda bi, j: (bi, j, 0))] + [_const_spec(p.shape) for p in params]
                  + side_in),
        out_specs=[pl.BlockSpec((None, tt, w), lambda bi, j: (bi, j, 0))] + side_out,
        scratch_shapes=[pltpu.VMEM((w // n, HEAD_DIM, n), f32), pltpu.VMEM((SUBLANES, cols), f32)],
        compiler_params=_cparams(("parallel", "arbitrary")),
        name="rwkv7",
    )(u_r, *params, *side)
    return outs[0], outs[1:]


def _swa_body(sink_ref, u_ref, pos_ref, invf_ref, o_ref, kprev_ref, vprev_ref, *, nwin):
    j = pl.program_id(1)

    @pl.when(j == 0)
    def _():
        kprev_ref[...] = jnp.zeros_like(kprev_ref)
        vprev_ref[...] = jnp.zeros_like(vprev_ref)

    wq = SWA_WIDTH
    kvw = SWA_KV_WIDTH
    half = HEAD_DIM // 2
    gw = SWA_GROUP * HEAD_DIM
    rows = SWA_GROUP * WINDOW
    lane_kv = lax.broadcasted_iota(i32, (WINDOW, kvw), 1)
    ri = lax.broadcasted_iota(i32, (rows, WINDOW), 0)
    ci = lax.broadcasted_iota(i32, (rows, WINDOW), 1)
    t_idx = ri % WINDOW
    mask_cur = ci <= t_idx
    mask_cur_b = jnp.where(mask_cur, 1.0, 0.0).astype(bf16) > 0
    rb = lax.broadcasted_iota(i32, (rows, 1), 0) // WINDOW
    bd = (lax.broadcasted_iota(i32, (rows, gw), 0) // WINDOW) == (lax.broadcasted_iota(i32, (rows, gw), 1) // HEAD_DIM)

    def rope(x, c, s):
        n = x.shape[1]
        lane = lax.broadcasted_iota(i32, x.shape, 1)
        rot = jnp.where((lane % HEAD_DIM) < half, -pltpu.roll(x, n - half, axis=1), pltpu.roll(x, half, axis=1))
        return x * c + rot * s

    def rep(x, gi):
        sw = pltpu.roll(x, HEAD_DIM, axis=1)
        one = jnp.where((lane_kv // HEAD_DIM) == gi, x, sw)
        return jnp.concatenate([one] * (gw // kvw), axis=1).astype(bf16)

    k_prev = kprev_ref[...]
    v_prev = vprev_ref[...]
    for wi in range(nwin):
        rs = slice(wi * WINDOW, (wi + 1) * WINDOW)
        u = u_ref[rs, :]
        ang_t = invf_ref[...] * pos_ref[wi].astype(f32)
        cos = jnp.concatenate([jnp.cos(ang_t)] * (LANES // half), axis=0).T
        sin = jnp.concatenate([jnp.sin(ang_t)] * (LANES // half), axis=0).T
        q = rope(u[:, :wq] * (HEAD_DIM ** -0.5), jnp.concatenate([cos] * (wq // LANES), axis=1),
                 jnp.concatenate([sin] * (wq // LANES), axis=1))
        k_cur = rope(u[:, wq:wq + kvw], cos, sin)
        v_cur = u[:, wq + kvw:]
        prev_bias = jnp.where(j > 0, 0.0, NEG_INF) if wi == 0 else 0.0
        outs = []
        for gi in range(SWA_KV_HEADS):
            qg = q[:, gi * gw:(gi + 1) * gw]
            q_bd = jnp.where(bd, jnp.concatenate([qg] * SWA_GROUP, axis=0), 0.0).astype(bf16)
            s = jnp.where(mask_cur, _dot_nt(q_bd, rep(k_cur, gi)), _dot_nt(q_bd, rep(k_prev, gi)) + prev_bias)
            sink = jnp.zeros((rows, 1), f32)
            for h in range(SWA_GROUP):
                sink = jnp.where(rb == h, sink_ref[gi * SWA_GROUP + h], sink)
            m = jnp.maximum(jnp.max(s, axis=-1, keepdims=True), sink)
            p = jnp.exp(s - m)
            denom = jnp.sum(p, axis=-1, keepdims=True) + jnp.exp(sink - m)
            p_b = p.astype(bf16)
            zero_b = jnp.zeros((), bf16)
            o_bd = (_dot(jnp.where(mask_cur_b, p_b, zero_b), rep(v_cur, gi))
                    + _dot(jnp.where(mask_cur_b, zero_b, p_b), rep(v_prev, gi)))
            o_bd = jnp.where(bd, o_bd * (1.0 / denom), 0.0)
            og = o_bd[0:WINDOW]
            for h in range(1, SWA_GROUP):
                og = og + o_bd[h * WINDOW:(h + 1) * WINDOW]
            outs.append(og)
        o_ref[rs, :] = jnp.concatenate(outs, axis=1).astype(o_ref.dtype)
        k_prev, v_prev = k_cur, v_cur
    kprev_ref[...] = k_prev
    vprev_ref[...] = v_prev


def _swa(u_s, positions, sinks, nwin=2):
    b, s, cols = u_s.shape
    half = HEAD_DIM // 2
    nwin = nwin if s % (nwin * WINDOW) == 0 else 1
    tile = nwin * WINDOW
    inv_freq = (ROPE_THETA ** (-jnp.arange(0, HEAD_DIM, 2, dtype=f32) / HEAD_DIM)).reshape(half, 1)
    pos = positions.reshape(b, s // WINDOW, 1, WINDOW).astype(i32)
    return pl.pallas_call(
        functools.partial(_swa_body, nwin=nwin),
        out_shape=jax.ShapeDtypeStruct((b, s, SWA_WIDTH), bf16),
        grid=(b, s // tile),
        in_specs=[pl.BlockSpec(memory_space=pltpu.SMEM),
                  pl.BlockSpec((None, tile, cols), lambda bi, j: (bi, j, 0)),
                  pl.BlockSpec((None, nwin, 1, WINDOW), lambda bi, j: (bi, j, 0, 0)),
                  _const_spec((half, 1))],
        out_specs=pl.BlockSpec((None, tile, SWA_WIDTH), lambda bi, j: (bi, j, 0)),
        scratch_shapes=[pltpu.VMEM((WINDOW, SWA_KV_WIDTH), f32), pltpu.VMEM((WINDOW, SWA_KV_WIDTH), f32)],
        compiler_params=_cparams(("parallel", "arbitrary")),
        name="swa",
    )(sinks.astype(f32), u_s, pos, inv_freq)


def _mix_xattn_body(ya_ref, yb_ref, wa_ref, wb_ref, xin_ref, g1_ref, b1_ref, kv_ref, wq_ref, wo_ref, g_ref, b_ref,
                    o_ref, o3_ref, *, alpha):
    mix = _dot(ya_ref[...], wa_ref[...]) + _dot(yb_ref[...], wb_ref[...])
    x = _layer_norm(alpha * xin_ref[...] + mix, g1_ref[...], b1_ref[...])
    d = x.shape[1]
    hd = d // MEM_HEADS
    q = _dot(x.astype(bf16), wq_ref[...]) * (hd ** -0.5)
    kv = kv_ref[...]
    outs = []
    for h in range(MEM_HEADS):
        qh = q[:, h * hd:(h + 1) * hd].astype(bf16)
        kh = kv[:, h * hd:(h + 1) * hd]
        vh = kv[:, d + h * hd:d + (h + 1) * hd]
        s = _dot_nt(qh, kh)
        p = jnp.exp(s - jnp.max(s, axis=-1, keepdims=True))
        l = jnp.sum(p, axis=-1, keepdims=True)
        outs.append(_dot(p.astype(bf16), vh) / l)
    o = jnp.concatenate(outs, axis=1)
    xa = _dot(o.astype(bf16), wo_ref[...])
    y = _layer_norm(alpha * x + xa, g_ref[...], b_ref[...])
    o_ref[...] = y
    words = _pack_bf16_pairs(y)
    for j, w in enumerate(words):
        o3_ref[_lane_chunk(x.shape[0], len(words), j)] = w


def _mix_xattn(ya, yb, w_o, x, g1, b1, kv, wm_q, wm_o, g2, b2, alpha, tile, batch0=0, n_batch=None):
    bsz, s, d = x.shape
    nb = bsz if n_batch is None else n_batch
    m = kv.shape[1]
    tile = min(tile, s)
    nj = s // tile
    wa = w_o[:ya.shape[2]].astype(bf16)
    wb = w_o[ya.shape[2]:].astype(bf16)
    wq = wm_q.astype(bf16)
    wo = wm_o.astype(bf16)
    rows_in = lambda width: pl.BlockSpec((None, tile, width), lambda bi, j: (bi + batch0, j, 0))
    vec = _const_spec((1, d))
    return pl.pallas_call(
        functools.partial(_mix_xattn_body, alpha=alpha),
        out_shape=[jax.ShapeDtypeStruct((nb, s, d), f32),
                   jax.ShapeDtypeStruct((nb * s * (d // (2 * LANES)), LANES), i32)],
        grid=(nb, nj),
        in_specs=[rows_in(ya.shape[2]), rows_in(yb.shape[2]), _const_spec(wa.shape), _const_spec(wb.shape), rows_in(d),
                  vec, vec, pl.BlockSpec((None, m, 2 * d), lambda bi, j: (bi + batch0, 0, 0)),
                  _const_spec(wq.shape), _const_spec(wo.shape), vec, vec],
        out_specs=[pl.BlockSpec((None, tile, d), lambda bi, j: (bi, j, 0)),
                   pl.BlockSpec((tile * (d // (2 * LANES)), LANES), lambda bi, j: (bi * nj + j, 0))],
        compiler_params=_cparams(("parallel", "parallel")),
        name="mix_xattn",
    )(ya, yb, wa, wb, x, g1.reshape(1, d), b1.reshape(1, d), kv, wq, wo, g2.reshape(1, d), b2.reshape(1, d))


def _router_body(x_ref, wt_ref, bias_ref, upper_ref, e_ref, g_ref, r_ref, cnt_out_ref, cnt_ref, *, t):
    @pl.when(pl.program_id(0) == 0)
    def _():
        cnt_ref[...] = jnp.zeros_like(cnt_ref)

    xh, xl = _split2(x_ref[...])
    wh, wl = _split2(wt_ref[...])
    logits = _dot_nt(wh, xh) + _dot_nt(wh, xl) + _dot_nt(wl, xh)
    scores = _sigmoid(logits)
    biased = scores + bias_ref[...][:, 0:1]
    ne = N_EXPERTS
    neg = -jnp.inf

    def top1(vals):
        rows = lax.broadcasted_iota(i32, vals.shape, 0).astype(f32)
        m = jnp.max(vals, axis=0, keepdims=True)
        idx = jnp.min(jnp.where(vals == m, rows, float(vals.shape[0])), axis=0, keepdims=True)
        return m, idx, rows == idx

    gscores = []
    for gi in range(N_GROUPS):
        blk = biased[gi * GROUP_SIZE:(gi + 1) * GROUP_SIZE, :]
        m1, _, hit = top1(blk)
        m2 = jnp.max(jnp.where(hit, neg, blk), axis=0, keepdims=True)
        gscores.append(m1 + m2)
    gs = jnp.concatenate(gscores, axis=0)
    gsel = jnp.zeros(gs.shape, f32)
    for _ in range(TOPK_GROUPS):
        _, _, hit = top1(gs)
        gsel = jnp.where(hit, 1.0, gsel)
        gs = jnp.where(hit, neg, gs)
    emask = jnp.concatenate(
        [jnp.broadcast_to(gsel[gi:gi + 1, :], (GROUP_SIZE, t)) for gi in range(N_GROUPS)], axis=0) > 0.5
    cand = jnp.where(emask, biased, NEG_INF)
    idxs, sels = [], []
    chosen = jnp.zeros((ne, t), f32)
    for _ in range(TOP_K):
        _, idx, hit = top1(cand)
        idxs.append(idx)
        sels.append(jnp.sum(jnp.where(hit, scores, 0.0), axis=0, keepdims=True))
        chosen = chosen + jnp.where(hit, 1.0, 0.0)
        cand = jnp.where(hit, neg, cand)
    sel = jnp.concatenate(sels, axis=0)
    g_ref[...] = sel / jnp.sum(sel, axis=0, keepdims=True) * ROUTED_SCALE
    e_ref[...] = jnp.concatenate(idxs, axis=0).astype(i32)
    before = _dot(chosen.astype(bf16), upper_ref[...]) + cnt_ref[...][:, 0:1]
    rows = lax.broadcasted_iota(i32, (ne, t), 0).astype(f32)
    ranks = [jnp.sum(jnp.where(rows == idx, before, 0.0), axis=0, keepdims=True) for idx in idxs]
    r_ref[...] = jnp.concatenate(ranks, axis=0).astype(i32)
    cnt_ref[...] = cnt_ref[...] + jnp.sum(chosen, axis=1, keepdims=True)
    cnt_out_ref[...] = cnt_ref[...].astype(i32)


def _router(x2, w_router, router_bias, tile):
    n, d = x2.shape
    ne = N_EXPERTS
    t = min(tile, n)
    assert n % t == 0
    wt = w_router.T
    bias = jnp.broadcast_to(router_bias.reshape(ne, 1).astype(f32), (ne, LANES))
    ti = jnp.arange(t)
    upper = (ti[:, None] < ti[None, :]).astype(bf16)
    cols = pl.BlockSpec((TOP_K, t), lambda i: (0, i))
    e_t, g_t, r_t, cnt = pl.pallas_call(
        functools.partial(_router_body, t=t),
        out_shape=[jax.ShapeDtypeStruct((TOP_K, n), i32), jax.ShapeDtypeStruct((TOP_K, n), f32),
                   jax.ShapeDtypeStruct((TOP_K, n), i32), jax.ShapeDtypeStruct((ne, LANES), i32)],
        grid=(n // t,),
        in_specs=[pl.BlockSpec((t, d), lambda i: (i, 0)), _const_spec((ne, d)), _const_spec((ne, LANES)),
                  _const_spec((t, t))],
        out_specs=[cols, cols, cols, _const_spec((ne, LANES))],
        scratch_shapes=[pltpu.VMEM((ne, LANES), f32)],
        compiler_params=_cparams(("arbitrary",)),
        name="router",
    )(x2, wt, bias, upper)
    return e_t, g_t, r_t, cnt[:, 0]


def _dispatch_sc(x3d, e_t, r_t, g_t, starts, n_slots):
    n, nc, lanes = x3d.shape
    workers = SC_CORES * SC_SUBCORES
    per_w = n // workers
    tc = _sc_chunk_rows(nc, lanes, x3d.dtype)
    assert n % (workers * SC_INDEX_GROUP) == 0 and SC_INDEX_GROUP % tc == 0
    mesh = plsc.VectorSubcoreMesh(core_axis_name="c", subcore_axis_name="s")

    @functools.partial(
        pl.kernel, mesh=mesh,
        out_type=[jax.ShapeDtypeStruct((n_slots, nc, lanes), x3d.dtype),
                  jax.ShapeDtypeStruct((n_slots,), f32)],
        scratch_types=[
            pltpu.VMEM((tc, nc, lanes), x3d.dtype),
            pltpu.VMEM((TOP_K, SC_INDEX_GROUP), i32),
            pltpu.VMEM((TOP_K, SC_INDEX_GROUP), i32),
            pltpu.VMEM((TOP_K, SC_INDEX_GROUP), f32),
            pltpu.VMEM((TOP_K, tc), i32),
            pltpu.VMEM((TOP_K, tc), f32),
            pltpu.VMEM((N_EXPERTS,), i32),
            pltpu.SemaphoreType.DMA,
        ],
        compiler_params=pltpu.CompilerParams(use_tc_tiling_on_sc=True, needs_layout_passes=False),
    )
    def dispatch(x_hbm, e_hbm, r_hbm, g_hbm, st_hbm, o_hbm, gs_hbm, rows_v, e_v, r_v, g_v, slot_v, gate_v, st_v, sem):
        wid = lax.axis_index("s") * SC_CORES + lax.axis_index("c")
        pltpu.sync_copy(st_hbm, st_v)

        @pl.loop(0, per_w // SC_INDEX_GROUP)
        def _(gi):
            base = wid * per_w + gi * SC_INDEX_GROUP
            pltpu.sync_copy(e_hbm.at[:, pl.ds(base, SC_INDEX_GROUP)], e_v)
            pltpu.sync_copy(r_hbm.at[:, pl.ds(base, SC_INDEX_GROUP)], r_v)
            pltpu.sync_copy(g_hbm.at[:, pl.ds(base, SC_INDEX_GROUP)], g_v)
            for h in range(SC_INDEX_GROUP // tc):
                off = h * tc
                pltpu.sync_copy(x_hbm.at[pl.ds(base + off, tc)], rows_v)
                for kk in range(TOP_K):
                    for j in range(tc // SC_LANES):
                        src = pl.ds(off + j * SC_LANES, SC_LANES)
                        dst = pl.ds(j * SC_LANES, SC_LANES)
                        slot_v[kk, dst] = r_v[kk, src] + plsc.load_gather(st_v, [e_v[kk, src]])
                        gate_v[kk, dst] = g_v[kk, src]
                copies = [pltpu.async_copy(rows_v, o_hbm.at[slot_v.at[kk]], sem) for kk in range(TOP_K)]
                copies += [pltpu.async_copy(gate_v.at[kk], gs_hbm.at[slot_v.at[kk]], sem) for kk in range(TOP_K)]
                for cp in copies:
                    cp.wait()

    return dispatch(x3d, e_t, r_t, g_t, starts)


def _gmm_body(gid_ref, tid_ref, nrows_ref, newg_ref, nextg_ref, ord_ref, x_ref, g_ref,
              wg_hbm, wu_hbm, wd_hbm, o_ref, wg_b, wu_b, wd_b, sems, *, tm):
    v = pl.program_id(0)
    n_real = nrows_ref[v]
    nc = wg_b.shape[1] // LANES
    slot = ord_ref[v] % 2

    def weight_copies(expert, dst_slot):
        return [pltpu.make_async_copy(src.at[expert], dst.at[dst_slot], sems.at[dst_slot])
                for src, dst in ((wg_hbm, wg_b), (wu_hbm, wu_b), (wd_hbm, wd_b))]

    @pl.when(newg_ref[v] == 1)
    def _():
        @pl.when(v == 0)
        def _():
            for cp in weight_copies(gid_ref[v], slot):
                cp.start()

        for cp in weight_copies(gid_ref[v], slot):
            cp.wait()

        @pl.when(nextg_ref[v] >= 0)
        def _():
            for cp in weight_copies(nextg_ref[v], 1 - slot):
                cp.start()

    @pl.when(n_real > 0)
    def _():
        nw = nc // 2
        real = lax.broadcasted_iota(i32, (tm, 1), 0) < n_real
        x = _unpack_bf16_pairs([x_ref[_lane_chunk(tm, nw, j)] for j in range(nw)])
        x = jnp.where(real, x, 0.0).astype(bf16)
        hg = _dot(x, wg_b[slot])
        h = hg * _sigmoid(hg) * _dot(x, wu_b[slot])
        g = g_ref[...]
        g_cols = jnp.concatenate([g, jnp.zeros((SUBLANES - g.shape[0], LANES), f32)], axis=0).T
        g_col = jnp.concatenate([g_cols[:, r:r + 1] for r in range(tm // LANES)], axis=0)
        h = h * jnp.where(real, g_col, 0.0)
        y = _dot(h.astype(bf16), wd_b[slot])
        for c in range(nc):
            o_ref[_lane_chunk(tm, nc, c)] = y[:, c * LANES:(c + 1) * LANES]


def _expert_layout(counts, n_assign, tm):
    tiles = (counts + tm - 1) // tm
    starts = ((jnp.cumsum(tiles) - tiles) * tm).astype(i32)
    capacity = (n_assign // tm + counts.shape[0]) * tm
    return starts, tiles, capacity


def _gmm(xs, gates, we_gate, we_up, we_down, counts, tm):
    ne, d, de = we_gate.shape
    nc = d // LANES
    nw = nc // 2
    n_tiles = xs.shape[0] // (nw * tm)
    assert tm % LANES == 0 and tm // LANES <= SUBLANES
    gates3 = gates.reshape(n_tiles, tm // LANES, LANES)
    tiles = (counts + tm - 1) // tm
    tile_end = jnp.cumsum(tiles)
    tile_start = tile_end - tiles
    vi = jnp.arange(n_tiles, dtype=i32)
    valid = vi < tile_end[-1]
    gid = jnp.minimum(jnp.sum((tile_end[None, :] <= vi[:, None]).astype(i32), axis=1), ne - 1)
    gid = jnp.where(valid, gid, jnp.max(jnp.where(valid, gid, 0)))
    onehot = gid[:, None] == jnp.arange(ne, dtype=i32)[None, :]
    pick = lambda table: jnp.sum(jnp.where(onehot, table[None, :], 0), axis=1)
    tid = jnp.minimum(vi, tile_end[-1] - 1).astype(i32)
    n_real = jnp.where(valid, jnp.clip(pick(counts) - (vi - pick(tile_start)) * tm, 0, tm), 0).astype(i32)
    one = jnp.ones((1,), i32)
    newg = jnp.concatenate([one, (gid[1:] != gid[:-1]).astype(i32)])
    later = gid[None, :] > gid[:, None]
    nextg = jnp.min(jnp.where(later, gid[None, :], ne), axis=1)
    nextg = jnp.where(nextg < ne, nextg, -1).astype(i32)
    order = (jnp.cumsum(newg) - 1).astype(i32)
    rows = lambda chunks: pl.BlockSpec((tm * chunks, LANES), lambda v, g, t, *_: (t[v], 0))
    hbm = pl.BlockSpec(memory_space=pl.ANY)
    return pl.pallas_call(
        functools.partial(_gmm_body, tm=tm),
        out_shape=jax.ShapeDtypeStruct((n_tiles * tm * nc, LANES), f32),
        grid_spec=pltpu.PrefetchScalarGridSpec(
            num_scalar_prefetch=6, grid=(n_tiles,),
            in_specs=[rows(nw), pl.BlockSpec((None, tm // LANES, LANES), lambda v, g, t, *_: (t[v], 0, 0)),
                      hbm, hbm, hbm],
            out_specs=rows(nc),
            scratch_shapes=[pltpu.VMEM((2, d, de), bf16), pltpu.VMEM((2, d, de), bf16), pltpu.VMEM((2, de, d), bf16),
                            pltpu.SemaphoreType.DMA((2,))]),
        compiler_params=_cparams(("arbitrary",)),
        name="moe_experts",
    )(gid, tid, n_real, newg, nextg, order, xs, gates3, we_gate, we_up, we_down)


def _combine_sc(ys3d, e_t, r_t, starts, n):
    _, nc, lanes = ys3d.shape
    workers = SC_CORES * SC_SUBCORES
    per_w = n // workers
    tc = _sc_chunk_rows(nc, lanes, f32)
    assert n % (workers * SC_INDEX_GROUP) == 0 and SC_INDEX_GROUP % tc == 0
    mesh = plsc.VectorSubcoreMesh(core_axis_name="c", subcore_axis_name="s")

    @functools.partial(
        pl.kernel, mesh=mesh,
        out_type=jax.ShapeDtypeStruct((n, nc, lanes), f32),
        scratch_types=[
            pltpu.VMEM((tc, nc, lanes), f32),
            pltpu.VMEM((TOP_K, SC_INDEX_GROUP), i32),
            pltpu.VMEM((TOP_K, SC_INDEX_GROUP), i32),
            pltpu.VMEM((TOP_K, tc), i32),
            pltpu.VMEM((N_EXPERTS,), i32),
            pltpu.SemaphoreType.DMA,
        ],
        compiler_params=pltpu.CompilerParams(use_tc_tiling_on_sc=True, needs_layout_passes=False),
    )
    def combine(y_hbm, e_hbm, r_hbm, st_hbm, o_hbm, acc_v, e_v, r_v, slot_v, st_v, sem):
        wid = lax.axis_index("s") * SC_CORES + lax.axis_index("c")
        pltpu.sync_copy(st_hbm, st_v)

        @pl.loop(0, per_w // SC_INDEX_GROUP)
        def _(gi):
            base = wid * per_w + gi * SC_INDEX_GROUP
            pltpu.sync_copy(e_hbm.at[:, pl.ds(base, SC_INDEX_GROUP)], e_v)
            pltpu.sync_copy(r_hbm.at[:, pl.ds(base, SC_INDEX_GROUP)], r_v)
            for h in range(SC_INDEX_GROUP // tc):
                off = h * tc
                for kk in range(TOP_K):
                    for j in range(tc // SC_LANES):
                        src = pl.ds(off + j * SC_LANES, SC_LANES)
                        slot_v[kk, pl.ds(j * SC_LANES, SC_LANES)] = (
                            r_v[kk, src] + plsc.load_gather(st_v, [e_v[kk, src]]))
                pltpu.async_copy(y_hbm.at[slot_v.at[0]], acc_v, sem).wait()
                copies = [pltpu.async_copy(y_hbm.at[slot_v.at[kk]], acc_v, sem, add=True) for kk in range(1, TOP_K)]
                for cp in copies:
                    cp.wait()
                pltpu.sync_copy(acc_v, o_hbm.at[pl.ds(base + off, tc)])

    return combine(ys3d, e_t, r_t, starts)


def _ffn_out_body(x_ref, r_ref, wg_ref, wu_ref, wd_ref, g_ref, b_ref, *rest, alpha):
    o_ref = rest[-1]
    x = x_ref[...]
    xb = x.astype(bf16)
    hg = _dot(xb, wg_ref[...])
    h = hg * _sigmoid(hg) * _dot(xb, wu_ref[...])
    shared = _dot(h.astype(bf16), wd_ref[...])
    nc = x.shape[1] // LANES
    routed = jnp.concatenate([r_ref[_lane_chunk(x.shape[0], nc, c)] for c in range(nc)], axis=1)
    o_ref[...] = _layer_norm(alpha * x + routed + shared, g_ref[...], b_ref[...])


def _ffn_out(x2, routed_rows, ws_gate, ws_up, ws_down, g, b, alpha, tile, n_all=None, row0=0, prev=None):
    n, d = x2.shape
    n_all = n if n_all is None else n_all
    tile = min(tile, n)
    assert row0 % tile == 0 and n % tile == 0
    first = row0 // tile
    wg, wu, wd = ws_gate.astype(bf16), ws_up.astype(bf16), ws_down.astype(bf16)
    rows = pl.BlockSpec((tile, d), lambda i: (i + first, 0))
    in_specs = [pl.BlockSpec((tile, d), lambda i: (i, 0)),
                pl.BlockSpec((tile * (d // LANES), LANES), lambda i: (i, 0)), _const_spec(wg.shape),
                _const_spec(wu.shape), _const_spec(wd.shape), _const_spec((1, d)), _const_spec((1, d))]
    args = [x2, routed_rows, wg, wu, wd, g.reshape(1, d), b.reshape(1, d)]
    aliases = {}
    if prev is not None:
        in_specs.append(pl.BlockSpec(memory_space=pl.ANY))
        args.append(prev)
        aliases = {len(args) - 1: 0}
    return pl.pallas_call(
        functools.partial(_ffn_out_body, alpha=alpha),
        out_shape=jax.ShapeDtypeStruct((n_all, d), f32),
        grid=(n // tile,),
        in_specs=in_specs,
        out_specs=rows,
        input_output_aliases=aliases,
        compiler_params=_cparams(("parallel",)),
        name="ffn_out_ln3",
    )(*args)


def _layer(x, mem, positions, w_in, mu_shift, w_decay_up, w0, a_up, a0, g_up, k_k, k_a, r_k, lnx_g, lnx_b, sinks,
           w_o, ln1_g, ln1_b, wm_q, wm_kv, wm_o, ln2_g, ln2_b, w_router, router_bias, we_gate, we_up, we_down,
           ws_gate, ws_up, ws_down, ln3_g, ln3_b, *, layer, alpha):
    b, s, d = x.shape
    n = b * s
    xf = x.reshape(n, d)
    w_in_b = w_in.astype(bf16)
    u_r, u_s = _proj(xf, [w_in_b[:, :RWKV_COLS], w_in_b[:, RWKV_COLS:]], [f32, f32], tile=ROW_TILE)
    rwkv_tt = RWKV_TILE
    experts = (we_gate, we_up, we_down)
    if N_EXPERTS % (b * (s // min(rwkv_tt, s))) == 0:
        y_r, experts_b = _rwkv(u_r.reshape(b, s, RWKV_COLS), mu_shift, w_decay_up, w0, a_up, a0, g_up, k_k, k_a, r_k,
                               lnx_g, lnx_b, tt=rwkv_tt, side=experts, layer=layer)
    else:
        y_r, _ = _rwkv(u_r.reshape(b, s, RWKV_COLS), mu_shift, w_decay_up, w0, a_up, a0, g_up, k_k, k_a, r_k,
                       lnx_g, lnx_b, tt=rwkv_tt)
        experts_b = [w[layer].astype(bf16) for w in experts]
    y_s = _swa(u_s.reshape(b, s, SWA_COLS), positions, sinks)
    m = mem.shape[1]
    (kv,) = _proj(mem.reshape(b * m, d), [wm_kv.astype(bf16)], [bf16], tile=ROW_TILE)
    nc = d // LANES
    nw = nc // 2
    parts = MOE_PARTS
    if b % parts or (n // parts) % (SC_CORES * SC_SUBCORES * SC_INDEX_GROUP):
        parts = 1
    npart = n // parts
    bpp = b // parts
    x3 = None
    for part in range(parts):
        x2, x2_rows = _mix_xattn(y_r, y_s, w_o, x, ln1_g, ln1_b, kv.reshape(b, m, 2 * d), wm_q, wm_o, ln2_g, ln2_b,
                                 alpha, tile=ROW_TILE, batch0=part * bpp, n_batch=bpp)
        x2 = x2.reshape(npart, d)
        e_t, g_t, r_t, counts = _router(x2, w_router, router_bias, tile=ROW_TILE)
        starts, _, n_slots = _expert_layout(counts, npart * TOP_K, GMM_TILE)
        xs, gs = _dispatch_sc(x2_rows.reshape(npart, nw, LANES), e_t, r_t, g_t, starts, n_slots)
        ys = _gmm(xs.reshape(n_slots * nw, LANES), gs, *experts_b, counts, tm=GMM_TILE)
        routed = _combine_sc(ys.reshape(n_slots, nc, LANES), e_t, r_t, starts, npart)
        x3 = _ffn_out(x2, routed.reshape(npart * nc, LANES), ws_gate, ws_up, ws_down, ln3_g, ln3_b, alpha, tile=ROW_TILE,
                      n_all=n, row0=part * npart, prev=x3)
    return x3.reshape(b, s, d)


def kernel(x, mem, positions, w_in, mu_shift, w_decay_up, w0, a_up, a0, g_up, k_k, k_a, r_k, lnx_g, lnx_b, sinks, w_o, ln1_g, ln1_b, wm_q, wm_kv, wm_o, ln2_g, ln2_b, w_router, router_bias, we_gate, we_up, we_down, ws_gate, ws_up, ws_down, ln3_g, ln3_b):
    depth = w_in.shape[0]
    alpha = (2 * depth) ** 0.25
    for l in range(depth):
        x = _layer(x, mem, positions, w_in[l], mu_shift[l], w_decay_up[l], w0[l], a_up[l], a0[l], g_up[l], k_k[l],
                   k_a[l], r_k[l], lnx_g[l], lnx_b[l], sinks[l], w_o[l], ln1_g[l], ln1_b[l], wm_q[l], wm_kv[l],
                   wm_o[l], ln2_g[l], ln2_b[l], w_router[l], router_bias[l], we_gate, we_up, we_down,
                   ws_gate[l], ws_up[l], ws_down[l], ln3_g[l], ln3_b[l], layer=l, alpha=alpha)
    return x
```

```python
import functools

import jax
import jax.numpy as jnp
from jax import lax
from jax.experimental import pallas as pl
from jax.experimental.pallas import tpu as pltpu
from jax.experimental.pallas import tpu_sc as plsc

f32 = jnp.float32
bf16 = jnp.bfloat16
i32 = jnp.int32

RWKV_HEADS = 8
HEAD_DIM = 64
RWKV_WIDTH = RWKV_HEADS * HEAD_DIM
DECAY_RANK = 64
AAA_RANK = 64
GATE_RANK = 128
RWKV_COLS = 3 * RWKV_WIDTH + DECAY_RANK + AAA_RANK + GATE_RANK
SWA_Q_HEADS = 8
SWA_KV_HEADS = 2
SWA_GROUP = SWA_Q_HEADS // SWA_KV_HEADS
SWA_WIDTH = SWA_Q_HEADS * HEAD_DIM
SWA_KV_WIDTH = SWA_KV_HEADS * HEAD_DIM
SWA_COLS = SWA_WIDTH + 2 * SWA_KV_WIDTH
WINDOW = 128
ROPE_THETA = 10000.0
MEM_HEADS = 4
N_EXPERTS = 256
TOP_K = 8
N_GROUPS = 8
GROUP_SIZE = N_EXPERTS // N_GROUPS
TOPK_GROUPS = 4
ROUTED_SCALE = 2.5
LN_EPS = 1e-5
GN_EPS = 64e-5
NEG_INF = -1e30

LANES = 128
SUBLANES = 8
WKV_CHUNK = 64
WKV_GROUP = 2
ROW_TILE = 512
RWKV_TILE = 256
GMM_TILE = 512
VMEM_LIMIT = 56 * 1024 * 1024

SC_CORES = 2
SC_SUBCORES = 16
SC_LANES = 16
SC_INDEX_GROUP = 128
SC_CHUNK_BYTES = 256 * 1024
MOE_PARTS = 2


def _sc_chunk_rows(nc, lanes, dtype):
    return min(SC_INDEX_GROUP, SC_CHUNK_BYTES // (nc * lanes * jnp.dtype(dtype).itemsize))


def _cparams(sem):
    return pltpu.CompilerParams(dimension_semantics=sem, vmem_limit_bytes=VMEM_LIMIT)


def _const_spec(shape):
    nd = len(shape)
    return pl.BlockSpec(shape, lambda *_: (0,) * nd)


def _dot(a, b):
    return jnp.dot(a, b, preferred_element_type=f32)


def _dot_nt(a, b):
    return lax.dot_general(a, b, (((1,), (1,)), ((), ())), preferred_element_type=f32)


def _dot_tn(a, b):
    return lax.dot_general(a, b, (((0,), (0,)), ((), ())), preferred_element_type=f32)


def _split2(x):
    hi = x.astype(bf16)
    lo = (x - hi.astype(f32)).astype(bf16)
    return hi, lo


def _seg_sums(xs, seg_b):
    parts = []
    for x in xs:
        parts.extend(_split2(x))
    out = _dot(jnp.concatenate(parts, axis=0), seg_b)
    t = xs[0].shape[0]
    return [out[2 * i * t:(2 * i + 1) * t] + out[(2 * i + 1) * t:(2 * i + 2) * t] for i in range(len(xs))]


def _dot_hp(a, b):
    ah, al = _split2(a)
    bh, bl = _split2(b)
    return _dot(ah, bh) + _dot(ah, bl) + _dot(al, bh)


def _dot_exact_lhs(m_bf16, x):
    hi, lo = _split2(x)
    return _dot(m_bf16, hi) + _dot(m_bf16, lo)


def _sigmoid(x):
    return 1.0 / (1.0 + jnp.exp(-x))


def _lane_chunk(n_rows, n_chunks, c, row0=0):
    return (pl.ds(row0 * n_chunks + c, n_rows, stride=n_chunks), slice(None))


def _pack_bf16_pairs(x):
    chunks = []
    for j in range(x.shape[1] // (2 * LANES)):
        lo = x[:, 2 * j * LANES:(2 * j + 1) * LANES].astype(bf16).astype(f32)
        hi = x[:, (2 * j + 1) * LANES:(2 * j + 2) * LANES].astype(bf16).astype(f32)
        chunks.append(lax.bitcast_convert_type(hi, i32) | lax.shift_right_logical(lax.bitcast_convert_type(lo, i32), 16))
    return chunks


def _unpack_bf16_pairs(chunks):
    cols = []
    for w in chunks:
        cols.append(lax.bitcast_convert_type(lax.shift_left(w, 16), f32))
        cols.append(lax.bitcast_convert_type(w & jnp.int32(-65536), f32))
    return jnp.concatenate(cols, axis=1)


def _layer_norm(h, g, b):
    mu = jnp.mean(h, axis=-1, keepdims=True)
    d = h - mu
    var = jnp.mean(d * d, axis=-1, keepdims=True)
    return d * lax.rsqrt(var + LN_EPS) * g + b


def _proj_body(*refs, n_out):
    x_ref = refs[0]
    w_refs = refs[1:1 + n_out]
    o_refs = refs[1 + n_out:]
    xb = x_ref[...].astype(bf16)
    for w_ref, o_ref in zip(w_refs, o_refs):
        o_ref[...] = _dot(xb, w_ref[...]).astype(o_ref.dtype)


def _proj(x, ws, out_dtypes, tile):
    n, k = x.shape
    tile = min(tile, n)
    outs = pl.pallas_call(
        functools.partial(_proj_body, n_out=len(ws)),
        out_shape=[jax.ShapeDtypeStruct((n, w.shape[1]), dt) for w, dt in zip(ws, out_dtypes)],
        grid=(n // tile,),
        in_specs=[pl.BlockSpec((tile, k), lambda i: (i, 0))] + [_const_spec(w.shape) for w in ws],
        out_specs=[pl.BlockSpec((tile, w.shape[1]), lambda i: (i, 0)) for w in ws],
        compiler_params=_cparams(("parallel",)),
        name="proj",
    )(x, *ws)
    return outs


def _wkv_chunks(chains, states, masks):
    bd_b, bd, strict, incl, eye, eye_full = masks
    c, n = chains[0][1].shape
    nch = len(chains)

    def stack(x_b):
        return jnp.where(bd_b, jnp.concatenate([x_b] * WKV_GROUP, axis=0), jnp.zeros((), bf16))

    cast = [tuple(x.astype(bf16) for x in ch[1:6]) for ch in chains]
    v_s = [stack(cb[4]) for cb in cast]
    g = [_dot_nt(jnp.concatenate([cb[0], cb[3]], axis=0), jnp.concatenate([stack(cb[1]), stack(cb[2])], axis=0))
         for cb in cast]
    l_ak = [jnp.where(strict, gi[:c, n:], 0.0).astype(bf16) for gi in g]
    m_rb = [jnp.where(incl, gi[c:, :n], 0.0).astype(bf16) for gi in g]
    m_rk = [jnp.where(incl, gi[c:, n:], 0.0).astype(bf16) for gi in g]
    x = [jnp.where(strict, gi[:c, :n], 0.0) for gi in g]
    t = [eye + xi for xi in x]
    for _ in range(5):
        xb = [xi.astype(bf16) for xi in x]
        x = [_dot(xi, stack(xi)) for xi in xb]
        t = [ti + _dot(ti.astype(bf16), stack(xi.astype(bf16))) for ti, xi in zip(t, x)]
    lakv = [_dot(l_ak[i], v_s[i]).astype(bf16) for i in range(nch)]
    au = [_dot(t[i].astype(bf16), jnp.concatenate([stack(cast[i][0]), stack(lakv[i])], axis=1))
          for i in range(nch)]
    abar = [a[:, :n].astype(bf16) for a in au]
    ubar = [a[:, n:].astype(bf16) for a in au]
    ry = [_dot(m_rb[i], jnp.concatenate([stack(abar[i]), stack(ubar[i])], axis=1)) for i in range(nch)]
    r_bar = [(chains[i][4] + ry[i][:, :n]).astype(bf16) for i in range(nch)]
    y_bar = [ry[i][:, n:] + _dot(m_rk[i], v_s[i]) for i in range(nch)]
    p = [((eye_full + jnp.where(bd, _dot_tn(abar[i], cast[i][1]), 0.0)) * chains[i][6]).astype(bf16)
         for i in range(nch)]
    q = []
    for i in range(nch):
        q_bd = jnp.where(bd, _dot_tn(jnp.concatenate([ubar[i], cast[i][4]], axis=0),
                                     jnp.concatenate([cast[i][1], cast[i][2]], axis=0)), 0.0)
        qi = q_bd[0:c]
        for h in range(1, WKV_GROUP):
            qi = qi + q_bd[h * c:(h + 1) * c]
        q.append(qi * chains[i][6])
    states = list(states)
    ys = []
    for i in range(nch):
        gi = chains[i][0]
        s_b = states[gi].astype(bf16)
        ys.append(_dot_nt(r_bar[i], stack(s_b)) + y_bar[i])
        states[gi] = _dot(s_b, p[i]) + q[i]
    return ys, states


def _rwkv_body(u_ref, mu_ref, wdec_ref, w0_ref, aup_ref, a0_ref, gup_ref, kk_ref, ka_ref, rk_ref,
               lng_ref, lnb_ref, seg_ref, tri_ref, *rest, tt, n_side):
    side_in = rest[:n_side]
    y_ref = rest[n_side]
    side_out = rest[n_side + 1:2 * n_side + 1]
    state_ref, carry_ref = rest[2 * n_side + 1:]
    j = pl.program_id(1)
    for src, dst in zip(side_in, side_out):
        dst[...] = src[...].astype(dst.dtype)

    @pl.when(j == 0)
    def _():
        state_ref[...] = jnp.zeros_like(state_ref)
        carry_ref[...] = jnp.zeros_like(carry_ref)

    w = RWKV_WIDTH
    u = u_ref[...]
    row = lax.broadcasted_iota(i32, u.shape, 0)
    prev = jnp.where(row == 0, carry_ref[0:1, :], pltpu.roll(u, 1, axis=0))
    carry_ref[0:1, :] = u[tt - 1:tt, :]
    us = u + (prev - u) * mu_ref[...]
    r = us[:, 0:w]
    k = us[:, w:2 * w]
    v = us[:, 2 * w:3 * w]
    wa = us[:, 3 * w:3 * w + DECAY_RANK + AAA_RANK]
    gd = us[:, 3 * w + DECAY_RANK + AAA_RANK:]
    z = w0_ref[...] + _dot_hp(jnp.tanh(wa), wdec_ref[...])
    softplus_neg_z = jnp.maximum(-z, 0.0) + jnp.log(1.0 + jnp.exp(-jnp.abs(z)))
    lw = -jnp.exp(-softplus_neg_z - 0.5)
    a = _sigmoid(a0_ref[...] + _dot_hp(wa, aup_ref[...]))
    gate = _dot(_sigmoid(gd).astype(bf16), gup_ref[...].astype(bf16))
    seg = seg_ref[...]
    kk = k * kk_ref[...]
    kmod = k * (1.0 + (a - 1.0) * ka_ref[...])
    kk_sq, bonus_dot = _seg_sums([kk * kk, r * kmod * rk_ref[...]], seg)
    kk = kk / jnp.maximum(jnp.sqrt(kk_sq), 1e-12)
    cum = _dot_exact_lhs(tri_ref[...], lw)
    wc = jnp.exp(cum)
    iwc = jnp.exp(-cum)
    at = -kk * jnp.exp(cum - lw)
    bt = kk * a * iwc
    kt = kmod * iwc
    rt = r * wc

    n = WKV_GROUP * HEAD_DIM
    ri = lax.broadcasted_iota(i32, (n, n), 0)
    ci = lax.broadcasted_iota(i32, (n, n), 1)
    bd = (ri // WKV_CHUNK) == (ci // HEAD_DIM)
    bd_b = jnp.where(bd, 1.0, 0.0).astype(bf16) > 0
    eye_full = jnp.where(ri == ci, 1.0, 0.0).astype(f32)
    ti = lax.broadcasted_iota(i32, (WKV_CHUNK, n), 0)
    si = lax.broadcasted_iota(i32, (WKV_CHUNK, n), 1) % WKV_CHUNK
    masks = (bd_b, bd, ti > si, ti >= si, jnp.where(ti == si, 1.0, 0.0).astype(f32), eye_full)

    n_groups = w // n
    n_chunks = tt // WKV_CHUNK
    chains = []
    for c in range(n_chunks):
        rs = slice(c * WKV_CHUNK, (c + 1) * WKV_CHUNK)
        last = (c + 1) * WKV_CHUNK - 1
        for gi in range(n_groups):
            cs = slice(gi * n, (gi + 1) * n)
            chains.append((gi, at[rs, cs], bt[rs, cs], kt[rs, cs], rt[rs, cs], v[rs, cs], wc[last:last + 1, cs]))
    ys, states = _wkv_chunks(chains, [state_ref[gi] for gi in range(n_groups)], masks)
    for gi in range(n_groups):
        state_ref[gi] = states[gi]
    y = jnp.concatenate([jnp.concatenate(ys[c * n_groups:(c + 1) * n_groups], axis=1) for c in range(n_chunks)],
                        axis=0)

    inv_n = 1.0 / HEAD_DIM
    d = y - _seg_sums([y], seg)[0] * inv_n
    var = _seg_sums([d * d], seg)[0] * inv_n
    yn = d * lax.rsqrt(var + GN_EPS) * lng_ref[...] + lnb_ref[...]
    y_ref[...] = ((yn + bonus_dot * v) * gate).astype(y_ref.dtype)


def _rwkv(u_r, mu_shift, w_decay_up, w0, a_up, a0, g_up, k_k, k_a, r_k, lnx_g, lnx_b, tt, side=(), layer=0):
    b, s, cols = u_r.shape
    tt = min(tt, s)
    n_steps = b * (s // tt)
    assert all(a.shape[1] % n_steps == 0 for a in side)
    w = RWKV_WIDTH
    row = lambda p: p.reshape(1, -1).astype(f32)
    wdec = jnp.concatenate([w_decay_up, jnp.zeros((AAA_RANK, w), f32)], axis=0)
    aup = jnp.concatenate([jnp.zeros((DECAY_RANK, w), f32), a_up], axis=0)
    hid = jnp.arange(w) // HEAD_DIM
    seg = (hid[:, None] == hid[None, :]).astype(bf16)
    ti = jnp.arange(tt)
    tri = ((ti[:, None] // WKV_CHUNK == ti[None, :] // WKV_CHUNK) & (ti[:, None] >= ti[None, :])).astype(bf16)
    params = [row(mu_shift), wdec, row(w0), aup, row(a0), g_up, row(k_k), row(k_a), row(r_k), row(lnx_g),
              row(lnx_b), seg, tri]
    n = WKV_GROUP * HEAD_DIM
    nj = s // tt
    side_in = [pl.BlockSpec((None, a.shape[1] // n_steps) + a.shape[2:], lambda bi, j: (layer, bi * nj + j, 0, 0))
               for a in side]
    side_out = [pl.BlockSpec((a.shape[1] // n_steps,) + a.shape[2:], lambda bi, j: (bi * nj + j, 0, 0))
                for a in side]
    outs = pl.pallas_call(
        functools.partial(_rwkv_body, tt=tt, n_side=len(side)),
        out_shape=[jax.ShapeDtypeStruct((b, s, w), bf16)] + [jax.ShapeDtypeStruct(a.shape[1:], bf16) for a in side],
        grid=(b, nj),
        in_specs=([pl.BlockSpec((None, tt, cols), lambda bi, j: (bi, j, 0))] + [_const_spec(p.shape) for p in params]
                  + side_in),
        out_specs=[pl.BlockSpec((None, tt, w), lambda bi, j: (bi, j, 0))] + side_out,
        scratch_shapes=[pltpu.VMEM((w // n, HEAD_DIM, n), f32), pltpu.VMEM((SUBLANES, cols), f32)],
        compiler_params=_cparams(("parallel", "arbitrary")),
        name="rwkv7",
    )(u_r, *params, *side)
    return outs[0], outs[1:]


def _swa_body(sink_ref, u_ref, pos_ref, invf_ref, o_ref, kprev_ref, vprev_ref, *, nwin):
    j = pl.program_id(1)

    @pl.when(j == 0)
    def _():
        kprev_ref[...] = jnp.zeros_like(kprev_ref)
        vprev_ref[...] = jnp.zeros_like(vprev_ref)

    wq = SWA_WIDTH
    kvw = SWA_KV_WIDTH
    half = HEAD_DIM // 2
    gw = SWA_GROUP * HEAD_DIM
    rows = SWA_GROUP * WINDOW
    lane_kv = lax.broadcasted_iota(i32, (WINDOW, kvw), 1)
    ri = lax.broadcasted_iota(i32, (rows, WINDOW), 0)
    ci = lax.broadcasted_iota(i32, (rows, WINDOW), 1)
    t_idx = ri % WINDOW
    mask_cur = ci <= t_idx
    mask_cur_b = jnp.where(mask_cur, 1.0, 0.0).astype(bf16) > 0
    rb = lax.broadcasted_iota(i32, (rows, 1), 0) // WINDOW
    bd = (lax.broadcasted_iota(i32, (rows, gw), 0) // WINDOW) == (lax.broadcasted_iota(i32, (rows, gw), 1) // HEAD_DIM)

    def rope(x, c, s):
        n = x.shape[1]
        lane = lax.broadcasted_iota(i32, x.shape, 1)
        rot = jnp.where((lane % HEAD_DIM) < half, -pltpu.roll(x, n - half, axis=1), pltpu.roll(x, half, axis=1))
        return x * c + rot * s

    def rep(x, gi):
        sw = pltpu.roll(x, HEAD_DIM, axis=1)
        one = jnp.where((lane_kv // HEAD_DIM) == gi, x, sw)
        return jnp.concatenate([one] * (gw // kvw), axis=1).astype(bf16)

    k_prev = kprev_ref[...]
    v_prev = vprev_ref[...]
    for wi in range(nwin):
        rs = slice(wi * WINDOW, (wi + 1) * WINDOW)
        u = u_ref[rs, :]
        ang_t = invf_ref[...] * pos_ref[wi].astype(f32)
        cos = jnp.concatenate([jnp.cos(ang_t)] * (LANES // half), axis=0).T
        sin = jnp.concatenate([jnp.sin(ang_t)] * (LANES // half), axis=0).T
        q = rope(u[:, :wq] * (HEAD_DIM ** -0.5), jnp.concatenate([cos] * (wq // LANES), axis=1),
                 jnp.concatenate([sin] * (wq // LANES), axis=1))
        k_cur = rope(u[:, wq:wq + kvw], cos, sin)
        v_cur = u[:, wq + kvw:]
        prev_bias = jnp.where(j > 0, 0.0, NEG_INF) if wi == 0 else 0.0
        outs = []
        for gi in range(SWA_KV_HEADS):
            qg = q[:, gi * gw:(gi + 1) * gw]
            q_bd = jnp.where(bd, jnp.concatenate([qg] * SWA_GROUP, axis=0), 0.0).astype(bf16)
            s = jnp.where(mask_cur, _dot_nt(q_bd, rep(k_cur, gi)), _dot_nt(q_bd, rep(k_prev, gi)) + prev_bias)
            sink = jnp.zeros((rows, 1), f32)
            for h in range(SWA_GROUP):
                sink = jnp.where(rb == h, sink_ref[gi * SWA_GROUP + h], sink)
            m = jnp.maximum(jnp.max(s, axis=-1, keepdims=True), sink)
            p = jnp.exp(s - m)
            denom = jnp.sum(p, axis=-1, keepdims=True) + jnp.exp(sink - m)
            p_b = p.astype(bf16)
            zero_b = jnp.zeros((), bf16)
            o_bd = (_dot(jnp.where(mask_cur_b, p_b, zero_b), rep(v_cur, gi))
                    + _dot(jnp.where(mask_cur_b, zero_b, p_b), rep(v_prev, gi)))
            o_bd = jnp.where(bd, o_bd * (1.0 / denom), 0.0)
            og = o_bd[0:WINDOW]
            for h in range(1, SWA_GROUP):
                og = og + o_bd[h * WINDOW:(h + 1) * WINDOW]
            outs.append(og)
        o_ref[rs, :] = jnp.concatenate(outs, axis=1).astype(o_ref.dtype)
        k_prev, v_prev = k_cur, v_cur
    kprev_ref[...] = k_prev
    vprev_ref[...] = v_prev


def _swa(u_s, positions, sinks, nwin=2):
    b, s, cols = u_s.shape
    half = HEAD_DIM // 2
    nwin = nwin if s % (nwin * WINDOW) == 0 else 1
    tile = nwin * WINDOW
    inv_freq = (ROPE_THETA ** (-jnp.arange(0, HEAD_DIM, 2, dtype=f32) / HEAD_DIM)).reshape(half, 1)
    pos = positions.reshape(b, s // WINDOW, 1, WINDOW).astype(i32)
    return pl.pallas_call(
        functools.partial(_swa_body, nwin=nwin),
        out_shape=jax.ShapeDtypeStruct((b, s, SWA_WIDTH), bf16),
        grid=(b, s // tile),
        in_specs=[pl.BlockSpec(memory_space=pltpu.SMEM),
                  pl.BlockSpec((None, tile, cols), lambda bi, j: (bi, j, 0)),
                  pl.BlockSpec((None, nwin, 1, WINDOW), lambda bi, j: (bi, j, 0, 0)),
                  _const_spec((half, 1))],
        out_specs=pl.BlockSpec((None, tile, SWA_WIDTH), lambda bi, j: (bi, j, 0)),
        scratch_shapes=[pltpu.VMEM((WINDOW, SWA_KV_WIDTH), f32), pltpu.VMEM((WINDOW, SWA_KV_WIDTH), f32)],
        compiler_params=_cparams(("parallel", "arbitrary")),
        name="swa",
    )(sinks.astype(f32), u_s, pos, inv_freq)


def _mix_xattn_body(ya_ref, yb_ref, wa_ref, wb_ref, xin_ref, g1_ref, b1_ref, kv_ref, wq_ref, wo_ref, g_ref, b_ref,
                    o_ref, o3_ref, *, alpha):
    mix = _dot(ya_ref[...], wa_ref[...]) + _dot(yb_ref[...], wb_ref[...])
    x = _layer_norm(alpha * xin_ref[...] + mix, g1_ref[...], b1_ref[...])
    d = x.shape[1]
    hd = d // MEM_HEADS
    q = _dot(x.astype(bf16), wq_ref[...]) * (hd ** -0.5)
    kv = kv_ref[...]
    outs = []
    for h in range(MEM_HEADS):
        qh = q[:, h * hd:(h + 1) * hd].astype(bf16)
        kh = kv[:, h * hd:(h + 1) * hd]
        vh = kv[:, d + h * hd:d + (h + 1) * hd]
        s = _dot_nt(qh, kh)
        p = jnp.exp(s - jnp.max(s, axis=-1, keepdims=True))
        l = jnp.sum(p, axis=-1, keepdims=True)
        outs.append(_dot(p.astype(bf16), vh) / l)
    o = jnp.concatenate(outs, axis=1)
    xa = _dot(o.astype(bf16), wo_ref[...])
    y = _layer_norm(alpha * x + xa, g_ref[...], b_ref[...])
    o_ref[...] = y
    words = _pack_bf16_pairs(y)
    for j, w in enumerate(words):
        o3_ref[_lane_chunk(x.shape[0], len(words), j)] = w


def _mix_xattn(ya, yb, w_o, x, g1, b1, kv, wm_q, wm_o, g2, b2, alpha, tile, batch0=0, n_batch=None):
    bsz, s, d = x.shape
    nb = bsz if n_batch is None else n_batch
    m = kv.shape[1]
    tile = min(tile, s)
    nj = s // tile
    wa = w_o[:ya.shape[2]].astype(bf16)
    wb = w_o[ya.shape[2]:].astype(bf16)
    wq = wm_q.astype(bf16)
    wo = wm_o.astype(bf16)
    rows_in = lambda width: pl.BlockSpec((None, tile, width), lambda bi, j: (bi + batch0, j, 0))
    vec = _const_spec((1, d))
    return pl.pallas_call(
        functools.partial(_mix_xattn_body, alpha=alpha),
        out_shape=[jax.ShapeDtypeStruct((nb, s, d), f32),
                   jax.ShapeDtypeStruct((nb * s * (d // (2 * LANES)), LANES), i32)],
        grid=(nb, nj),
        in_specs=[rows_in(ya.shape[2]), rows_in(yb.shape[2]), _const_spec(wa.shape), _const_spec(wb.shape), rows_in(d),
                  vec, vec, pl.BlockSpec((None, m, 2 * d), lambda bi, j: (bi + batch0, 0, 0)),
                  _const_spec(wq.shape), _const_spec(wo.shape), vec, vec],
        out_specs=[pl.BlockSpec((None, tile, d), lambda bi, j: (bi, j, 0)),
                   pl.BlockSpec((tile * (d // (2 * LANES)), LANES), lambda bi, j: (bi * nj + j, 0))],
        compiler_params=_cparams(("parallel", "parallel")),
        name="mix_xattn",
    )(ya, yb, wa, wb, x, g1.reshape(1, d), b1.reshape(1, d), kv, wq, wo, g2.reshape(1, d), b2.reshape(1, d))


def _router_body(x_ref, wt_ref, bias_ref, upper_ref, e_ref, g_ref, r_ref, cnt_out_ref, cnt_ref, *, t):
    @pl.when(pl.program_id(0) == 0)
    def _():
        cnt_ref[...] = jnp.zeros_like(cnt_ref)

    xh, xl = _split2(x_ref[...])
    wh, wl = _split2(wt_ref[...])
    logits = _dot_nt(wh, xh) + _dot_nt(wh, xl) + _dot_nt(wl, xh)
    scores = _sigmoid(logits)
    biased = scores + bias_ref[...][:, 0:1]
    ne = N_EXPERTS
    neg = -jnp.inf

    def top1(vals):
        rows = lax.broadcasted_iota(i32, vals.shape, 0).astype(f32)
        m = jnp.max(vals, axis=0, keepdims=True)
        idx = jnp.min(jnp.where(vals == m, rows, float(vals.shape[0])), axis=0, keepdims=True)
        return m, idx, rows == idx

    gscores = []
    for gi in range(N_GROUPS):
        blk = biased[gi * GROUP_SIZE:(gi + 1) * GROUP_SIZE, :]
        m1, _, hit = top1(blk)
        m2 = jnp.max(jnp.where(hit, neg, blk), axis=0, keepdims=True)
        gscores.append(m1 + m2)
    gs = jnp.concatenate(gscores, axis=0)
    gsel = jnp.zeros(gs.shape, f32)
    for _ in range(TOPK_GROUPS):
        _, _, hit = top1(gs)
        gsel = jnp.where(hit, 1.0, gsel)
        gs = jnp.where(hit, neg, gs)
    emask = jnp.concatenate(
        [jnp.broadcast_to(gsel[gi:gi + 1, :], (GROUP_SIZE, t)) for gi in range(N_GROUPS)], axis=0) > 0.5
    cand = jnp.where(emask, biased, NEG_INF)
    idxs, sels = [], []
    chosen = jnp.zeros((ne, t), f32)
    for _ in range(TOP_K):
        _, idx, hit = top1(cand)
        idxs.append(idx)
        sels.append(jnp.sum(jnp.where(hit, scores, 0.0), axis=0, keepdims=True))
        chosen = chosen + jnp.where(hit, 1.0, 0.0)
        cand = jnp.where(hit, neg, cand)
    sel = jnp.concatenate(sels, axis=0)
    g_ref[...] = sel / jnp.sum(sel, axis=0, keepdims=True) * ROUTED_SCALE
    e_ref[...] = jnp.concatenate(idxs, axis=0).astype(i32)
    before = _dot(chosen.astype(bf16), upper_ref[...]) + cnt_ref[...][:, 0:1]
    rows = lax.broadcasted_iota(i32, (ne, t), 0).astype(f32)
    ranks = [jnp.sum(jnp.where(rows == idx, before, 0.0), axis=0, keepdims=True) for idx in idxs]
    r_ref[...] = jnp.concatenate(ranks, axis=0).astype(i32)
    cnt_ref[...] = cnt_ref[...] + jnp.sum(chosen, axis=1, keepdims=True)
    cnt_out_ref[...] = cnt_ref[...].astype(i32)


def _router(x2, w_router, router_bias, tile):
    n, d = x2.shape
    ne = N_EXPERTS
    t = min(tile, n)
    assert n % t == 0
    wt = w_router.T
    bias = jnp.broadcast_to(router_bias.reshape(ne, 1).astype(f32), (ne, LANES))
    ti = jnp.arange(t)
    upper = (ti[:, None] < ti[None, :]).astype(bf16)
    cols = pl.BlockSpec((TOP_K, t), lambda i: (0, i))
    e_t, g_t, r_t, cnt = pl.pallas_call(
        functools.partial(_router_body, t=t),
        out_shape=[jax.ShapeDtypeStruct((TOP_K, n), i32), jax.ShapeDtypeStruct((TOP_K, n), f32),
                   jax.ShapeDtypeStruct((TOP_K, n), i32), jax.ShapeDtypeStruct((ne, LANES), i32)],
        grid=(n // t,),
        in_specs=[pl.BlockSpec((t, d), lambda i: (i, 0)), _const_spec((ne, d)), _const_spec((ne, LANES)),
                  _const_spec((t, t))],
        out_specs=[cols, cols, cols, _const_spec((ne, LANES))],
        scratch_shapes=[pltpu.VMEM((ne, LANES), f32)],
        compiler_params=_cparams(("arbitrary",)),
        name="router",
    )(x2, wt, bias, upper)
    return e_t, g_t, r_t, cnt[:, 0]


def _dispatch_sc(x3d, e_t, r_t, g_t, starts, n_slots):
    n, nc, lanes = x3d.shape
    workers = SC_CORES * SC_SUBCORES
    per_w = n // workers
    tc = _sc_chunk_rows(nc, lanes, x3d.dtype)
    assert n % (workers * SC_INDEX_GROUP) == 0 and SC_INDEX_GROUP % tc == 0
    mesh = plsc.VectorSubcoreMesh(core_axis_name="c", subcore_axis_name="s")

    @functools.partial(
        pl.kernel, mesh=mesh,
        out_type=[jax.ShapeDtypeStruct((n_slots, nc, lanes), x3d.dtype),
                  jax.ShapeDtypeStruct((n_slots,), f32)],
        scratch_types=[
            pltpu.VMEM((tc, nc, lanes), x3d.dtype),
            pltpu.VMEM((TOP_K, SC_INDEX_GROUP), i32),
            pltpu.VMEM((TOP_K, SC_INDEX_GROUP), i32),
            pltpu.VMEM((TOP_K, SC_INDEX_GROUP), f32),
            pltpu.VMEM((TOP_K, tc), i32),
            pltpu.VMEM((TOP_K, tc), f32),
            pltpu.VMEM((N_EXPERTS,), i32),
            pltpu.SemaphoreType.DMA,
        ],
        compiler_params=pltpu.CompilerParams(use_tc_tiling_on_sc=True, needs_layout_passes=False),
    )
    def dispatch(x_hbm, e_hbm, r_hbm, g_hbm, st_hbm, o_hbm, gs_hbm, rows_v, e_v, r_v, g_v, slot_v, gate_v, st_v, sem):
        wid = lax.axis_index("s") * SC_CORES + lax.axis_index("c")
        pltpu.sync_copy(st_hbm, st_v)

        @pl.loop(0, per_w // SC_INDEX_GROUP)
        def _(gi):
            base = wid * per_w + gi * SC_INDEX_GROUP
            pltpu.sync_copy(e_hbm.at[:, pl.ds(base, SC_INDEX_GROUP)], e_v)
            pltpu.sync_copy(r_hbm.at[:, pl.ds(base, SC_INDEX_GROUP)], r_v)
            pltpu.sync_copy(g_hbm.at[:, pl.ds(base, SC_INDEX_GROUP)], g_v)
            for h in range(SC_INDEX_GROUP // tc):
                off = h * tc
                pltpu.sync_copy(x_hbm.at[pl.ds(base + off, tc)], rows_v)
                for kk in range(TOP_K):
                    for j in range(tc // SC_LANES):
                        src = pl.ds(off + j * SC_LANES, SC_LANES)
                        dst = pl.ds(j * SC_LANES, SC_LANES)
                        slot_v[kk, dst] = r_v[kk, src] + plsc.load_gather(st_v, [e_v[kk, src]])
                        gate_v[kk, dst] = g_v[kk, src]
                copies = [pltpu.async_copy(rows_v, o_hbm.at[slot_v.at[kk]], sem) for kk in range(TOP_K)]
                copies += [pltpu.async_copy(gate_v.at[kk], gs_hbm.at[slot_v.at[kk]], sem) for kk in range(TOP_K)]
                for cp in copies:
                    cp.wait()

    return dispatch(x3d, e_t, r_t, g_t, starts)


def _gmm_body(gid_ref, tid_ref, nrows_ref, newg_ref, nextg_ref, ord_ref, x_ref, g_ref,
              wg_hbm, wu_hbm, wd_hbm, o_ref, wg_b, wu_b, wd_b, sems, *, tm):
    v = pl.program_id(0)
    n_real = nrows_ref[v]
    nc = wg_b.shape[1] // LANES
    slot = ord_ref[v] % 2

    def weight_copies(expert, dst_slot):
        return [pltpu.make_async_copy(src.at[expert], dst.at[dst_slot], sems.at[dst_slot])
                for src, dst in ((wg_hbm, wg_b), (wu_hbm, wu_b), (wd_hbm, wd_b))]

    @pl.when(newg_ref[v] == 1)
    def _():
        @pl.when(v == 0)
        def _():
            for cp in weight_copies(gid_ref[v], slot):
                cp.start()

        for cp in weight_copies(gid_ref[v], slot):
            cp.wait()

        @pl.when(nextg_ref[v] >= 0)
        def _():
            for cp in weight_copies(nextg_ref[v], 1 - slot):
                cp.start()

    @pl.when(n_real > 0)
    def _():
        nw = nc // 2
        real = lax.broadcasted_iota(i32, (tm, 1), 0) < n_real
        x = _unpack_bf16_pairs([x_ref[_lane_chunk(tm, nw, j)] for j in range(nw)])
        x = jnp.where(real, x, 0.0).astype(bf16)
        hg = _dot(x, wg_b[slot])
        h = hg * _sigmoid(hg) * _dot(x, wu_b[slot])
        g = g_ref[...]
        g_cols = jnp.concatenate([g, jnp.zeros((SUBLANES - g.shape[0], LANES), f32)], axis=0).T
        g_col = jnp.concatenate([g_cols[:, r:r + 1] for r in range(tm // LANES)], axis=0)
        h = h * jnp.where(real, g_col, 0.0)
        y = _dot(h.astype(bf16), wd_b[slot])
        for c in range(nc):
            o_ref[_lane_chunk(tm, nc, c)] = y[:, c * LANES:(c + 1) * LANES]


def _expert_layout(counts, n_assign, tm):
    tiles = (counts + tm - 1) // tm
    starts = ((jnp.cumsum(tiles) - tiles) * tm).astype(i32)
    capacity = (n_assign // tm + counts.shape[0]) * tm
    return starts, tiles, capacity


def _gmm(xs, gates, we_gate, we_up, we_down, counts, tm):
    ne, d, de = we_gate.shape
    nc = d // LANES
    nw = nc // 2
    n_tiles = xs.shape[0] // (nw * tm)
    assert tm % LANES == 0 and tm // LANES <= SUBLANES
    gates3 = gates.reshape(n_tiles, tm // LANES, LANES)
    tiles = (counts + tm - 1) // tm
    tile_end = jnp.cumsum(tiles)
    tile_start = tile_end - tiles
    vi = jnp.arange(n_tiles, dtype=i32)
    valid = vi < tile_end[-1]
    gid = jnp.minimum(jnp.sum((tile_end[None, :] <= vi[:, None]).astype(i32), axis=1), ne - 1)
    gid = jnp.where(valid, gid, jnp.max(jnp.where(valid, gid, 0)))
    onehot = gid[:, None] == jnp.arange(ne, dtype=i32)[None, :]
    pick = lambda table: jnp.sum(jnp.where(onehot, table[None, :], 0), axis=1)
    tid = jnp.minimum(vi, tile_end[-1] - 1).astype(i32)
    n_real = jnp.where(valid, jnp.clip(pick(counts) - (vi - pick(tile_start)) * tm, 0, tm), 0).astype(i32)
    one = jnp.ones((1,), i32)
    newg = jnp.concatenate([one, (gid[1:] != gid[:-1]).astype(i32)])
    later = gid[None, :] > gid[:, None]
    nextg = jnp.min(jnp.where(later, gid[None, :], ne), axis=1)
    nextg = jnp.where(nextg < ne, nextg, -1).astype(i32)
    order = (jnp.cumsum(newg) - 1).astype(i32)
    rows = lambda chunks: pl.BlockSpec((tm * chunks, LANES), lambda v, g, t, *_: (t[v], 0))
    hbm = pl.BlockSpec(memory_space=pl.ANY)
    return pl.pallas_call(
        functools.partial(_gmm_body, tm=tm),
        out_shape=jax.ShapeDtypeStruct((n_tiles * tm * nc, LANES), f32),
        grid_spec=pltpu.PrefetchScalarGridSpec(
            num_scalar_prefetch=6, grid=(tile_end[-1],),
            in_specs=[rows(nw), pl.BlockSpec((None, tm // LANES, LANES), lambda v, g, t, *_: (t[v], 0, 0)),
                      hbm, hbm, hbm],
            out_specs=rows(nc),
            scratch_shapes=[pltpu.VMEM((2, d, de), bf16), pltpu.VMEM((2, d, de), bf16), pltpu.VMEM((2, de, d), bf16),
                            pltpu.SemaphoreType.DMA((2,))]),
        compiler_params=_cparams(("arbitrary",)),
        name="moe_experts",
    )(gid, tid, n_real, newg, nextg, order, xs, gates3, we_gate, we_up, we_down)


def _combine_sc(ys3d, e_t, r_t, starts, n):
    _, nc, lanes = ys3d.shape
    workers = SC_CORES * SC_SUBCORES
    per_w = n // workers
    tc = _sc_chunk_rows(nc, lanes, f32)
    assert n % (workers * SC_INDEX_GROUP) == 0 and SC_INDEX_GROUP % tc == 0
    mesh = plsc.VectorSubcoreMesh(core_axis_name="c", subcore_axis_name="s")

    @functools.partial(
        pl.kernel, mesh=mesh,
        out_type=jax.ShapeDtypeStruct((n, nc, lanes), f32),
        scratch_types=[
            pltpu.VMEM((tc, nc, lanes), f32),
            pltpu.VMEM((TOP_K, SC_INDEX_GROUP), i32),
            pltpu.VMEM((TOP_K, SC_INDEX_GROUP), i32),
            pltpu.VMEM((TOP_K, tc), i32),
            pltpu.VMEM((N_EXPERTS,), i32),
            pltpu.SemaphoreType.DMA,
        ],
        compiler_params=pltpu.CompilerParams(use_tc_tiling_on_sc=True, needs_layout_passes=False),
    )
    def combine(y_hbm, e_hbm, r_hbm, st_hbm, o_hbm, acc_v, e_v, r_v, slot_v, st_v, sem):
        wid = lax.axis_index("s") * SC_CORES + lax.axis_index("c")
        pltpu.sync_copy(st_hbm, st_v)

        @pl.loop(0, per_w // SC_INDEX_GROUP)
        def _(gi):
            base = wid * per_w + gi * SC_INDEX_GROUP
            pltpu.sync_copy(e_hbm.at[:, pl.ds(base, SC_INDEX_GROUP)], e_v)
            pltpu.sync_copy(r_hbm.at[:, pl.ds(base, SC_INDEX_GROUP)], r_v)
            for h in range(SC_INDEX_GROUP // tc):
                off = h * tc
                for kk in range(TOP_K):
                    for j in range(tc // SC_LANES):
                        src = pl.ds(off + j * SC_LANES, SC_LANES)
                        slot_v[kk, pl.ds(j * SC_LANES, SC_LANES)] = (
                            r_v[kk, src] + plsc.load_gather(st_v, [e_v[kk, src]]))
                pltpu.async_copy(y_hbm.at[slot_v.at[0]], acc_v, sem).wait()
                copies = [pltpu.async_copy(y_hbm.at[slot_v.at[kk]], acc_v, sem, add=True) for kk in range(1, TOP_K)]
                for cp in copies:
                    cp.wait()
                pltpu.sync_copy(acc_v, o_hbm.at[pl.ds(base + off, tc)])

    return combine(ys3d, e_t, r_t, starts)


def _ffn_out_body(x_ref, r_ref, wg_ref, wu_ref, wd_ref, g_ref, b_ref, *rest, alpha):
    o_ref = rest[-1]
    x = x_ref[...]
    xb = x.astype(bf16)
    hg = _dot(xb, wg_ref[...])
    h = hg * _sigmoid(hg) * _dot(xb, wu_ref[...])
    shared = _dot(h.astype(bf16), wd_ref[...])
    nc = x.shape[1] // LANES
    routed = jnp.concatenate([r_ref[_lane_chunk(x.shape[0], nc, c)] for c in range(nc)], axis=1)
    o_ref[...] = _layer_norm(alpha * x + routed + shared, g_ref[...], b_ref[...])


def _ffn_out(x2, routed_rows, ws_gate, ws_up, ws_down, g, b, alpha, tile, n_all=None, row0=0, prev=None):
    n, d = x2.shape
    n_all = n if n_all is None else n_all
    tile = min(tile, n)
    assert row0 % tile == 0 and n % tile == 0
    first = row0 // tile
    wg, wu, wd = ws_gate.astype(bf16), ws_up.astype(bf16), ws_down.astype(bf16)
    rows = pl.BlockSpec((tile, d), lambda i: (i + first, 0))
    in_specs = [pl.BlockSpec((tile, d), lambda i: (i, 0)),
                pl.BlockSpec((tile * (d // LANES), LANES), lambda i: (i, 0)), _const_spec(wg.shape),
                _const_spec(wu.shape), _const_spec(wd.shape), _const_spec((1, d)), _const_spec((1, d))]
    args = [x2, routed_rows, wg, wu, wd, g.reshape(1, d), b.reshape(1, d)]
    aliases = {}
    if prev is not None:
        in_specs.append(pl.BlockSpec(memory_space=pl.ANY))
        args.append(prev)
        aliases = {len(args) - 1: 0}
    return pl.pallas_call(
        functools.partial(_ffn_out_body, alpha=alpha),
        out_shape=jax.ShapeDtypeStruct((n_all, d), f32),
        grid=(n // tile,),
        in_specs=in_specs,
        out_specs=rows,
        input_output_aliases=aliases,
        compiler_params=_cparams(("parallel",)),
        name="ffn_out_ln3",
    )(*args)


def _layer(x, mem, positions, w_in, mu_shift, w_decay_up, w0, a_up, a0, g_up, k_k, k_a, r_k, lnx_g, lnx_b, sinks,
           w_o, ln1_g, ln1_b, wm_q, wm_kv, wm_o, ln2_g, ln2_b, w_router, router_bias, we_gate, we_up, we_down,
           ws_gate, ws_up, ws_down, ln3_g, ln3_b, *, layer, alpha):
    b, s, d = x.shape
    n = b * s
    xf = x.reshape(n, d)
    w_in_b = w_in.astype(bf16)
    u_r, u_s = _proj(xf, [w_in_b[:, :RWKV_COLS], w_in_b[:, RWKV_COLS:]], [f32, f32], tile=ROW_TILE)
    rwkv_tt = RWKV_TILE
    experts = (we_gate, we_up, we_down)
    if N_EXPERTS % (b * (s // min(rwkv_tt, s))) == 0:
        y_r, experts_b = _rwkv(u_r.reshape(b, s, RWKV_COLS), mu_shift, w_decay_up, w0, a_up, a0, g_up, k_k, k_a, r_k,
                               lnx_g, lnx_b, tt=rwkv_tt, side=experts, layer=layer)
    else:
        y_r, _ = _rwkv(u_r.reshape(b, s, RWKV_COLS), mu_shift, w_decay_up, w0, a_up, a0, g_up, k_k, k_a, r_k,
                       lnx_g, lnx_b, tt=rwkv_tt)
        experts_b = [w[layer].astype(bf16) for w in experts]
    y_s = _swa(u_s.reshape(b, s, SWA_COLS), positions, sinks)
    m = mem.shape[1]
    (kv,) = _proj(mem.reshape(b * m, d), [wm_kv.astype(bf16)], [bf16], tile=ROW_TILE)
    nc = d // LANES
    nw = nc // 2
    parts = MOE_PARTS
    if b % parts or (n // parts) % (SC_CORES * SC_SUBCORES * SC_INDEX_GROUP):
        parts = 1
    npart = n // parts
    bpp = b // parts
    x3 = None
    for part in range(parts):
        x2, x2_rows = _mix_xattn(y_r, y_s, w_o, x, ln1_g, ln1_b, kv.reshape(b, m, 2 * d), wm_q, wm_o, ln2_g, ln2_b,
                                 alpha, tile=ROW_TILE, batch0=part * bpp, n_batch=bpp)
        x2 = x2.reshape(npart, d)
        e_t, g_t, r_t, counts = _router(x2, w_router, router_bias, tile=ROW_TILE)
        starts, _, n_slots = _expert_layout(counts, npart * TOP_K, GMM_TILE)
        xs, gs = _dispatch_sc(x2_rows.reshape(npart, nw, LANES), e_t, r_t, g_t, starts, n_slots)
        ys = _gmm(xs.reshape(n_slots * nw, LANES), gs, *experts_b, counts, tm=GMM_TILE)
        routed = _combine_sc(ys.reshape(n_slots, nc, LANES), e_t, r_t, starts, npart)
        x3 = _ffn_out(x2, routed.reshape(npart * nc, LANES), ws_gate, ws_up, ws_down, ln3_g, ln3_b, alpha, tile=ROW_TILE,
                      n_all=n, row0=part * npart, prev=x3)
    return x3.reshape(b, s, d)


def kernel(x, mem, positions, w_in, mu_shift, w_decay_up, w0, a_up, a0, g_up, k_k, k_a, r_k, lnx_g, lnx_b, sinks, w_o, ln1_g, ln1_b, wm_q, wm_kv, wm_o, ln2_g, ln2_b, w_router, router_bias, we_gate, we_up, we_down, ws_gate, ws_up, ws_down, ln3_g, ln3_b):
    depth = w_in.shape[0]
    alpha = (2 * depth) ** 0.25
    for l in range(depth):
        x = _layer(x, mem, positions, w_in[l], mu_shift[l], w_decay_up[l], w0[l], a_up[l], a0[l], g_up[l], k_k[l],
                   k_a[l], r_k[l], lnx_g[l], lnx_b[l], sinks[l], w_o[l], ln1_g[l], ln1_b[l], wm_q[l], wm_kv[l],
                   wm_o[l], ln2_g[l], ln2_b[l], w_router[l], router_bias[l], we_gate, we_up, we_down,
                   ws_gate[l], ws_up[l], ws_down[l], ln3_g[l], ln3_b[l], layer=l, alpha=alpha)
    return x
```

```python
import functools

import jax
import jax.numpy as jnp
from jax import lax
from jax.experimental import pallas as pl
from jax.experimental.pallas import tpu as pltpu
from jax.experimental.pallas import tpu_sc as plsc

f32 = jnp.float32
bf16 = jnp.bfloat16
i32 = jnp.int32

RWKV_HEADS = 8
HEAD_DIM = 64
RWKV_WIDTH = RWKV_HEADS * HEAD_DIM
DECAY_RANK = 64
AAA_RANK = 64
GATE_RANK = 128
RWKV_COLS = 3 * RWKV_WIDTH + DECAY_RANK + AAA_RANK + GATE_RANK
SWA_Q_HEADS = 8
SWA_KV_HEADS = 2
SWA_GROUP = SWA_Q_HEADS // SWA_KV_HEADS
SWA_WIDTH = SWA_Q_HEADS * HEAD_DIM
SWA_KV_WIDTH = SWA_KV_HEADS * HEAD_DIM
SWA_COLS = SWA_WIDTH + 2 * SWA_KV_WIDTH
WINDOW = 128
ROPE_THETA = 10000.0
MEM_HEADS = 4
N_EXPERTS = 256
TOP_K = 8
N_GROUPS = 8
GROUP_SIZE = N_EXPERTS // N_GROUPS
TOPK_GROUPS = 4
ROUTED_SCALE = 2.5
LN_EPS = 1e-5
GN_EPS = 64e-5
NEG_INF = -1e30

LANES = 128
SUBLANES = 8
WKV_CHUNK = 64
WKV_GROUP = 2
ROW_TILE = 512
RWKV_TILE = 256
GMM_TILE = 512
VMEM_LIMIT = 56 * 1024 * 1024

SC_CORES = 2
SC_SUBCORES = 16
SC_LANES = 16
SC_INDEX_GROUP = 128
SC_CHUNK_BYTES = 256 * 1024
MOE_PARTS = 2


def _sc_chunk_rows(nc, lanes, dtype):
    return min(SC_INDEX_GROUP, SC_CHUNK_BYTES // (nc * lanes * jnp.dtype(dtype).itemsize))


def _cparams(sem):
    return pltpu.CompilerParams(dimension_semantics=sem, vmem_limit_bytes=VMEM_LIMIT)


def _const_spec(shape):
    nd = len(shape)
    return pl.BlockSpec(shape, lambda *_: (0,) * nd)


def _dot(a, b):
    return jnp.dot(a, b, preferred_element_type=f32)


def _dot_nt(a, b):
    return lax.dot_general(a, b, (((1,), (1,)), ((), ())), preferred_element_type=f32)


def _dot_tn(a, b):
    return lax.dot_general(a, b, (((0,), (0,)), ((), ())), preferred_element_type=f32)


def _split2(x):
    hi = x.astype(bf16)
    lo = (x - hi.astype(f32)).astype(bf16)
    return hi, lo


def _seg_sums(xs, seg_b):
    parts = []
    for x in xs:
        parts.extend(_split2(x))
    out = _dot(jnp.concatenate(parts, axis=0), seg_b)
    t = xs[0].shape[0]
    return [out[2 * i * t:(2 * i + 1) * t] + out[(2 * i + 1) * t:(2 * i + 2) * t] for i in range(len(xs))]


def _dot_hp(a, b):
    ah, al = _split2(a)
    bh, bl = _split2(b)
    return _dot(ah, bh) + _dot(ah, bl) + _dot(al, bh)


def _dot_exact_lhs(m_bf16, x):
    hi, lo = _split2(x)
    return _dot(m_bf16, hi) + _dot(m_bf16, lo)


def _sigmoid(x):
    return 1.0 / (1.0 + jnp.exp(-x))


def _lane_chunk(n_rows, n_chunks, c, row0=0):
    return (pl.ds(row0 * n_chunks + c, n_rows, stride=n_chunks), slice(None))


def _pack_bf16_pairs(x):
    chunks = []
    for j in range(x.shape[1] // (2 * LANES)):
        lo = x[:, 2 * j * LANES:(2 * j + 1) * LANES].astype(bf16).astype(f32)
        hi = x[:, (2 * j + 1) * LANES:(2 * j + 2) * LANES].astype(bf16).astype(f32)
        chunks.append(lax.bitcast_convert_type(hi, i32) | lax.shift_right_logical(lax.bitcast_convert_type(lo, i32), 16))
    return chunks


def _unpack_bf16_pairs(chunks):
    cols = []
    for w in chunks:
        cols.append(lax.bitcast_convert_type(lax.shift_left(w, 16), f32))
        cols.append(lax.bitcast_convert_type(w & jnp.int32(-65536), f32))
    return jnp.concatenate(cols, axis=1)


def _layer_norm(h, g, b):
    mu = jnp.mean(h, axis=-1, keepdims=True)
    d = h - mu
    var = jnp.mean(d * d, axis=-1, keepdims=True)
    return d * lax.rsqrt(var + LN_EPS) * g + b


def _proj_body(*refs, n_out):
    x_ref = refs[0]
    w_refs = refs[1:1 + n_out]
    o_refs = refs[1 + n_out:]
    xb = x_ref[...].astype(bf16)
    for w_ref, o_ref in zip(w_refs, o_refs):
        o_ref[...] = _dot(xb, w_ref[...]).astype(o_ref.dtype)


def _proj(x, ws, out_dtypes, tile):
    n, k = x.shape
    tile = min(tile, n)
    outs = pl.pallas_call(
        functools.partial(_proj_body, n_out=len(ws)),
        out_shape=[jax.ShapeDtypeStruct((n, w.shape[1]), dt) for w, dt in zip(ws, out_dtypes)],
        grid=(n // tile,),
        in_specs=[pl.BlockSpec((tile, k), lambda i: (i, 0))] + [_const_spec(w.shape) for w in ws],
        out_specs=[pl.BlockSpec((tile, w.shape[1]), lambda i: (i, 0)) for w in ws],
        compiler_params=_cparams(("parallel",)),
        name="proj",
    )(x, *ws)
    return outs


def _wkv_chunks(chains, states, masks):
    bd_b, bd, strict, incl, eye, eye_full = masks
    c, n = chains[0][1].shape
    nch = len(chains)

    def stack(x_b):
        return jnp.where(bd_b, jnp.concatenate([x_b] * WKV_GROUP, axis=0), jnp.zeros((), bf16))

    cast = [tuple(x.astype(bf16) for x in ch[1:6]) for ch in chains]
    v_s = [stack(cb[4]) for cb in cast]
    g = [_dot_nt(jnp.concatenate([cb[0], cb[3]], axis=0), jnp.concatenate([stack(cb[1]), stack(cb[2])], axis=0))
         for cb in cast]
    l_ak = [jnp.where(strict, gi[:c, n:], 0.0).astype(bf16) for gi in g]
    m_rb = [jnp.where(incl, gi[c:, :n], 0.0).astype(bf16) for gi in g]
    m_rk = [jnp.where(incl, gi[c:, n:], 0.0).astype(bf16) for gi in g]
    x = [jnp.where(strict, gi[:c, :n], 0.0) for gi in g]
    t = [eye + xi for xi in x]
    for _ in range(5):
        xb = [xi.astype(bf16) for xi in x]
        x = [_dot(xi, stack(xi)) for xi in xb]
        t = [ti + _dot(ti.astype(bf16), stack(xi.astype(bf16))) for ti, xi in zip(t, x)]
    lakv = [_dot(l_ak[i], v_s[i]).astype(bf16) for i in range(nch)]
    au = [_dot(t[i].astype(bf16), jnp.concatenate([stack(cast[i][0]), stack(lakv[i])], axis=1))
          for i in range(nch)]
    abar = [a[:, :n].astype(bf16) for a in au]
    ubar = [a[:, n:].astype(bf16) for a in au]
    ry = [_dot(m_rb[i], jnp.concatenate([stack(abar[i]), stack(ubar[i])], axis=1)) for i in range(nch)]
    r_bar = [(chains[i][4] + ry[i][:, :n]).astype(bf16) for i in range(nch)]
    y_bar = [ry[i][:, n:] + _dot(m_rk[i], v_s[i]) for i in range(nch)]
    p = [((eye_full + jnp.where(bd, _dot_tn(abar[i], cast[i][1]), 0.0)) * chains[i][6]).astype(bf16)
         for i in range(nch)]
    q = []
    for i in range(nch):
        q_bd = jnp.where(bd, _dot_tn(jnp.concatenate([ubar[i], cast[i][4]], axis=0),
                                     jnp.concatenate([cast[i][1], cast[i][2]], axis=0)), 0.0)
        qi = q_bd[0:c]
        for h in range(1, WKV_GROUP):
            qi = qi + q_bd[h * c:(h + 1) * c]
        q.append(qi * chains[i][6])
    states = list(states)
    ys = []
    for i in range(nch):
        gi = chains[i][0]
        s_b = states[gi].astype(bf16)
        ys.append(_dot_nt(r_bar[i], stack(s_b)) + y_bar[i])
        states[gi] = _dot(s_b, p[i]) + q[i]
    return ys, states


def _rwkv_body(u_ref, mu_ref, wdec_ref, w0_ref, aup_ref, a0_ref, gup_ref, kk_ref, ka_ref, rk_ref,
               lng_ref, lnb_ref, seg_ref, tri_ref, *rest, tt, n_side):
    side_in = rest[:n_side]
    y_ref = rest[n_side]
    side_out = rest[n_side + 1:2 * n_side + 1]
    state_ref, carry_ref = rest[2 * n_side + 1:]
    j = pl.program_id(1)
    for src, dst in zip(side_in, side_out):
        dst[...] = src[...].astype(dst.dtype)

    @pl.when(j == 0)
    def _():
        state_ref[...] = jnp.zeros_like(state_ref)
        carry_ref[...] = jnp.zeros_like(carry_ref)

    w = RWKV_WIDTH
    u = u_ref[...]
    row = lax.broadcasted_iota(i32, u.shape, 0)
    prev = jnp.where(row == 0, carry_ref[0:1, :], pltpu.roll(u, 1, axis=0))
    carry_ref[0:1, :] = u[tt - 1:tt, :]
    us = u + (prev - u) * mu_ref[...]
    r = us[:, 0:w]
    k = us[:, w:2 * w]
    v = us[:, 2 * w:3 * w]
    wa = us[:, 3 * w:3 * w + DECAY_RANK + AAA_RANK]
    gd = us[:, 3 * w + DECAY_RANK + AAA_RANK:]
    z = w0_ref[...] + _dot_hp(jnp.tanh(wa), wdec_ref[...])
    softplus_neg_z = jnp.maximum(-z, 0.0) + jnp.log(1.0 + jnp.exp(-jnp.abs(z)))
    lw = -jnp.exp(-softplus_neg_z - 0.5)
    a = _sigmoid(a0_ref[...] + _dot_hp(wa, aup_ref[...]))
    gate = _dot(_sigmoid(gd).astype(bf16), gup_ref[...].astype(bf16))
    seg = seg_ref[...]
    kk = k * kk_ref[...]
    kmod = k * (1.0 + (a - 1.0) * ka_ref[...])
    kk_sq, bonus_dot = _seg_sums([kk * kk, r * kmod * rk_ref[...]], seg)
    kk = kk / jnp.maximum(jnp.sqrt(kk_sq), 1e-12)
    cum = _dot_exact_lhs(tri_ref[...], lw)
    wc = jnp.exp(cum)
    iwc = jnp.exp(-cum)
    at = -kk * jnp.exp(cum - lw)
    bt = kk * a * iwc
    kt = kmod * iwc
    rt = r * wc

    n = WKV_GROUP * HEAD_DIM
    ri = lax.broadcasted_iota(i32, (n, n), 0)
    ci = lax.broadcasted_iota(i32, (n, n), 1)
    bd = (ri // WKV_CHUNK) == (ci // HEAD_DIM)
    bd_b = jnp.where(bd, 1.0, 0.0).astype(bf16) > 0
    eye_full = jnp.where(ri == ci, 1.0, 0.0).astype(f32)
    ti = lax.broadcasted_iota(i32, (WKV_CHUNK, n), 0)
    si = lax.broadcasted_iota(i32, (WKV_CHUNK, n), 1) % WKV_CHUNK
    masks = (bd_b, bd, ti > si, ti >= si, jnp.where(ti == si, 1.0, 0.0).astype(f32), eye_full)

    n_groups = w // n
    n_chunks = tt // WKV_CHUNK
    chains = []
    for c in range(n_chunks):
        rs = slice(c * WKV_CHUNK, (c + 1) * WKV_CHUNK)
        last = (c + 1) * WKV_CHUNK - 1
        for gi in range(n_groups):
            cs = slice(gi * n, (gi + 1) * n)
            chains.append((gi, at[rs, cs], bt[rs, cs], kt[rs, cs], rt[rs, cs], v[rs, cs], wc[last:last + 1, cs]))
    ys, states = _wkv_chunks(chains, [state_ref[gi] for gi in range(n_groups)], masks)
    for gi in range(n_groups):
        state_ref[gi] = states[gi]
    y = jnp.concatenate([jnp.concatenate(ys[c * n_groups:(c + 1) * n_groups], axis=1) for c in range(n_chunks)],
                        axis=0)

    inv_n = 1.0 / HEAD_DIM
    d = y - _seg_sums([y], seg)[0] * inv_n
    var = _seg_sums([d * d], seg)[0] * inv_n
    yn = d * lax.rsqrt(var + GN_EPS) * lng_ref[...] + lnb_ref[...]
    y_ref[...] = ((yn + bonus_dot * v) * gate).astype(y_ref.dtype)


def _rwkv(u_r, mu_shift, w_decay_up, w0, a_up, a0, g_up, k_k, k_a, r_k, lnx_g, lnx_b, tt, side=(), layer=0):
    b, s, cols = u_r.shape
    tt = min(tt, s)
    n_steps = b * (s // tt)
    assert all(a.shape[1] % n_steps == 0 for a in side)
    w = RWKV_WIDTH
    row = lambda p: p.reshape(1, -1).astype(f32)
    wdec = jnp.concatenate([w_decay_up, jnp.zeros((AAA_RANK, w), f32)], axis=0)
    aup = jnp.concatenate([jnp.zeros((DECAY_RANK, w), f32), a_up], axis=0)
    hid = jnp.arange(w) // HEAD_DIM
    seg = (hid[:, None] == hid[None, :]).astype(bf16)
    ti = jnp.arange(tt)
    tri = ((ti[:, None] // WKV_CHUNK == ti[None, :] // WKV_CHUNK) & (ti[:, None] >= ti[None, :])).astype(bf16)
    params = [row(mu_shift), wdec, row(w0), aup, row(a0), g_up, row(k_k), row(k_a), row(r_k), row(lnx_g),
              row(lnx_b), seg, tri]
    n = WKV_GROUP * HEAD_DIM
    nj = s // tt
    side_in = [pl.BlockSpec((None, a.shape[1] // n_steps) + a.shape[2:], lambda bi, j: (layer, bi * nj + j, 0, 0))
               for a in side]
    side_out = [pl.BlockSpec((a.shape[1] // n_steps,) + a.shape[2:], lambda bi, j: (bi * nj + j, 0, 0))
                for a in side]
    outs = pl.pallas_call(
        functools.partial(_rwkv_body, tt=tt, n_side=len(side)),
        out_shape=[jax.ShapeDtypeStruct((b, s, w), bf16)] + [jax.ShapeDtypeStruct(a.shape[1:], bf16) for a in side],
        grid=(b, nj),
        in_specs=([pl.BlockSpec((None, tt, cols), lambda bi, j: (bi, j, 0))] + [_const_spec(p.shape) for p in params]
                  + side_in),
        out_specs=[pl.BlockSpec((None, tt, w), lambda bi, j: (bi, j, 0))] + side_out,
        scratch_shapes=[pltpu.VMEM((w // n, HEAD_DIM, n), f32), pltpu.VMEM((SUBLANES, cols), f32)],
        compiler_params=_cparams(("parallel", "arbitrary")),
        name="rwkv7",
    )(u_r, *params, *side)
    return outs[0], outs[1:]


def _swa_body(sink_ref, x_ref, w_ref, pos_ref, invf_ref, o_ref, kprev_ref, vprev_ref, *, nwin):
    j = pl.program_id(1)

    @pl.when(j == 0)
    def _():
        kprev_ref[...] = jnp.zeros_like(kprev_ref)
        vprev_ref[...] = jnp.zeros_like(vprev_ref)

    wq = SWA_WIDTH
    kvw = SWA_KV_WIDTH
    half = HEAD_DIM // 2
    gw = SWA_GROUP * HEAD_DIM
    rows = SWA_GROUP * WINDOW
    lane_kv = lax.broadcasted_iota(i32, (WINDOW, kvw), 1)
    ri = lax.broadcasted_iota(i32, (rows, WINDOW), 0)
    ci = lax.broadcasted_iota(i32, (rows, WINDOW), 1)
    t_idx = ri % WINDOW
    mask_cur = ci <= t_idx
    mask_cur_b = jnp.where(mask_cur, 1.0, 0.0).astype(bf16) > 0
    rb = lax.broadcasted_iota(i32, (rows, 1), 0) // WINDOW
    bd = (lax.broadcasted_iota(i32, (rows, gw), 0) // WINDOW) == (lax.broadcasted_iota(i32, (rows, gw), 1) // HEAD_DIM)

    def rope(x, c, s):
        n = x.shape[1]
        lane = lax.broadcasted_iota(i32, x.shape, 1)
        rot = jnp.where((lane % HEAD_DIM) < half, -pltpu.roll(x, n - half, axis=1), pltpu.roll(x, half, axis=1))
        return x * c + rot * s

    def rep(x, gi):
        sw = pltpu.roll(x, HEAD_DIM, axis=1)
        one = jnp.where((lane_kv // HEAD_DIM) == gi, x, sw)
        return jnp.concatenate([one] * (gw // kvw), axis=1).astype(bf16)

    k_prev = kprev_ref[...]
    v_prev = vprev_ref[...]
    for wi in range(nwin):
        rs = slice(wi * WINDOW, (wi + 1) * WINDOW)
        u = _dot(x_ref[rs, :].astype(bf16), w_ref[...])
        ang_t = invf_ref[...] * pos_ref[wi].astype(f32)
        cos = jnp.concatenate([jnp.cos(ang_t)] * (LANES // half), axis=0).T
        sin = jnp.concatenate([jnp.sin(ang_t)] * (LANES // half), axis=0).T
        q = rope(u[:, :wq] * (HEAD_DIM ** -0.5), jnp.concatenate([cos] * (wq // LANES), axis=1),
                 jnp.concatenate([sin] * (wq // LANES), axis=1))
        k_cur = rope(u[:, wq:wq + kvw], cos, sin)
        v_cur = u[:, wq + kvw:]
        prev_bias = jnp.where(j > 0, 0.0, NEG_INF) if wi == 0 else 0.0
        outs = []
        for gi in range(SWA_KV_HEADS):
            qg = q[:, gi * gw:(gi + 1) * gw]
            q_bd = jnp.where(bd, jnp.concatenate([qg] * SWA_GROUP, axis=0), 0.0).astype(bf16)
            s = jnp.where(mask_cur, _dot_nt(q_bd, rep(k_cur, gi)), _dot_nt(q_bd, rep(k_prev, gi)) + prev_bias)
            sink = jnp.zeros((rows, 1), f32)
            for h in range(SWA_GROUP):
                sink = jnp.where(rb == h, sink_ref[gi * SWA_GROUP + h], sink)
            m = jnp.maximum(jnp.max(s, axis=-1, keepdims=True), sink)
            p = jnp.exp(s - m)
            denom = jnp.sum(p, axis=-1, keepdims=True) + jnp.exp(sink - m)
            p_b = p.astype(bf16)
            zero_b = jnp.zeros((), bf16)
            o_bd = (_dot(jnp.where(mask_cur_b, p_b, zero_b), rep(v_cur, gi))
                    + _dot(jnp.where(mask_cur_b, zero_b, p_b), rep(v_prev, gi)))
            o_bd = jnp.where(bd, o_bd * (1.0 / denom), 0.0)
            og = o_bd[0:WINDOW]
            for h in range(1, SWA_GROUP):
                og = og + o_bd[h * WINDOW:(h + 1) * WINDOW]
            outs.append(og)
        o_ref[rs, :] = jnp.concatenate(outs, axis=1).astype(o_ref.dtype)
        k_prev, v_prev = k_cur, v_cur
    kprev_ref[...] = k_prev
    vprev_ref[...] = v_prev


def _swa(x, w_s, positions, sinks, nwin=2):
    b, s, cols = x.shape
    half = HEAD_DIM // 2
    nwin = nwin if s % (nwin * WINDOW) == 0 else 1
    tile = nwin * WINDOW
    inv_freq = (ROPE_THETA ** (-jnp.arange(0, HEAD_DIM, 2, dtype=f32) / HEAD_DIM)).reshape(half, 1)
    pos = positions.reshape(b, s // WINDOW, 1, WINDOW).astype(i32)
    return pl.pallas_call(
        functools.partial(_swa_body, nwin=nwin),
        out_shape=jax.ShapeDtypeStruct((b, s, SWA_WIDTH), bf16),
        grid=(b, s // tile),
        in_specs=[pl.BlockSpec(memory_space=pltpu.SMEM),
                  pl.BlockSpec((None, tile, cols), lambda bi, j: (bi, j, 0)), _const_spec(w_s.shape),
                  pl.BlockSpec((None, nwin, 1, WINDOW), lambda bi, j: (bi, j, 0, 0)),
                  _const_spec((half, 1))],
        out_specs=pl.BlockSpec((None, tile, SWA_WIDTH), lambda bi, j: (bi, j, 0)),
        scratch_shapes=[pltpu.VMEM((WINDOW, SWA_KV_WIDTH), f32), pltpu.VMEM((WINDOW, SWA_KV_WIDTH), f32)],
        compiler_params=_cparams(("parallel", "arbitrary")),
        name="swa",
    )(sinks.astype(f32), x, w_s, pos, inv_freq)


def _mix_xattn_body(ya_ref, yb_ref, wa_ref, wb_ref, xin_ref, g1_ref, b1_ref, kv_ref, wq_ref, wo_ref, g_ref, b_ref,
                    o_ref, o3_ref, *, alpha):
    mix = _dot(ya_ref[...], wa_ref[...]) + _dot(yb_ref[...], wb_ref[...])
    x = _layer_norm(alpha * xin_ref[...] + mix, g1_ref[...], b1_ref[...])
    d = x.shape[1]
    hd = d // MEM_HEADS
    q = _dot(x.astype(bf16), wq_ref[...]) * (hd ** -0.5)
    kv = kv_ref[...]
    outs = []
    for h in range(MEM_HEADS):
        qh = q[:, h * hd:(h + 1) * hd].astype(bf16)
        kh = kv[:, h * hd:(h + 1) * hd]
        vh = kv[:, d + h * hd:d + (h + 1) * hd]
        s = _dot_nt(qh, kh)
        p = jnp.exp(s - jnp.max(s, axis=-1, keepdims=True))
        l = jnp.sum(p, axis=-1, keepdims=True)
        outs.append(_dot(p.astype(bf16), vh) / l)
    o = jnp.concatenate(outs, axis=1)
    xa = _dot(o.astype(bf16), wo_ref[...])
    y = _layer_norm(alpha * x + xa, g_ref[...], b_ref[...])
    o_ref[...] = y
    words = _pack_bf16_pairs(y)
    for j, w in enumerate(words):
        o3_ref[_lane_chunk(x.shape[0], len(words), j)] = w


def _mix_xattn(ya, yb, w_o, x, g1, b1, kv, wm_q, wm_o, g2, b2, alpha, tile, batch0=0, n_batch=None):
    bsz, s, d = x.shape
    nb = bsz if n_batch is None else n_batch
    m = kv.shape[1]
    tile = min(tile, s)
    nj = s // tile
    wa = w_o[:ya.shape[2]].astype(bf16)
    wb = w_o[ya.shape[2]:].astype(bf16)
    wq = wm_q.astype(bf16)
    wo = wm_o.astype(bf16)
    rows_in = lambda width: pl.BlockSpec((None, tile, width), lambda bi, j: (bi + batch0, j, 0))
    vec = _const_spec((1, d))
    return pl.pallas_call(
        functools.partial(_mix_xattn_body, alpha=alpha),
        out_shape=[jax.ShapeDtypeStruct((nb, s, d), f32),
                   jax.ShapeDtypeStruct((nb * s * (d // (2 * LANES)), LANES), i32)],
        grid=(nb, nj),
        in_specs=[rows_in(ya.shape[2]), rows_in(yb.shape[2]), _const_spec(wa.shape), _const_spec(wb.shape), rows_in(d),
                  vec, vec, pl.BlockSpec((None, m, 2 * d), lambda bi, j: (bi + batch0, 0, 0)),
                  _const_spec(wq.shape), _const_spec(wo.shape), vec, vec],
        out_specs=[pl.BlockSpec((None, tile, d), lambda bi, j: (bi, j, 0)),
                   pl.BlockSpec((tile * (d // (2 * LANES)), LANES), lambda bi, j: (bi * nj + j, 0))],
        compiler_params=_cparams(("parallel", "parallel")),
        name="mix_xattn",
    )(ya, yb, wa, wb, x, g1.reshape(1, d), b1.reshape(1, d), kv, wq, wo, g2.reshape(1, d), b2.reshape(1, d))


def _router_body(x_ref, wt_ref, bias_ref, upper_ref, e_ref, g_ref, r_ref, cnt_out_ref, cnt_ref, *, t):
    @pl.when(pl.program_id(0) == 0)
    def _():
        cnt_ref[...] = jnp.zeros_like(cnt_ref)

    xh, xl = _split2(x_ref[...])
    wh, wl = _split2(wt_ref[...])
    logits = _dot_nt(wh, xh) + _dot_nt(wh, xl) + _dot_nt(wl, xh)
    scores = _sigmoid(logits)
    biased = scores + bias_ref[...][:, 0:1]
    ne = N_EXPERTS
    neg = -jnp.inf

    def top1(vals):
        rows = lax.broadcasted_iota(i32, vals.shape, 0).astype(f32)
        m = jnp.max(vals, axis=0, keepdims=True)
        idx = jnp.min(jnp.where(vals == m, rows, float(vals.shape[0])), axis=0, keepdims=True)
        return m, idx, rows == idx

    gscores = []
    for gi in range(N_GROUPS):
        blk = biased[gi * GROUP_SIZE:(gi + 1) * GROUP_SIZE, :]
        m1, _, hit = top1(blk)
        m2 = jnp.max(jnp.where(hit, neg, blk), axis=0, keepdims=True)
        gscores.append(m1 + m2)
    gs = jnp.concatenate(gscores, axis=0)
    gsel = jnp.zeros(gs.shape, f32)
    for _ in range(TOPK_GROUPS):
        _, _, hit = top1(gs)
        gsel = jnp.where(hit, 1.0, gsel)
        gs = jnp.where(hit, neg, gs)
    emask = jnp.concatenate(
        [jnp.broadcast_to(gsel[gi:gi + 1, :], (GROUP_SIZE, t)) for gi in range(N_GROUPS)], axis=0) > 0.5
    cand = jnp.where(emask, biased, NEG_INF)
    idxs, sels = [], []
    chosen = jnp.zeros((ne, t), f32)
    for _ in range(TOP_K):
        _, idx, hit = top1(cand)
        idxs.append(idx)
        sels.append(jnp.sum(jnp.where(hit, scores, 0.0), axis=0, keepdims=True))
        chosen = chosen + jnp.where(hit, 1.0, 0.0)
        cand = jnp.where(hit, neg, cand)
    sel = jnp.concatenate(sels, axis=0)
    g_ref[...] = sel / jnp.sum(sel, axis=0, keepdims=True) * ROUTED_SCALE
    e_ref[...] = jnp.concatenate(idxs, axis=0).astype(i32)
    before = _dot(chosen.astype(bf16), upper_ref[...]) + cnt_ref[...][:, 0:1]
    rows = lax.broadcasted_iota(i32, (ne, t), 0).astype(f32)
    ranks = [jnp.sum(jnp.where(rows == idx, before, 0.0), axis=0, keepdims=True) for idx in idxs]
    r_ref[...] = jnp.concatenate(ranks, axis=0).astype(i32)
    cnt_ref[...] = cnt_ref[...] + jnp.sum(chosen, axis=1, keepdims=True)
    cnt_out_ref[...] = cnt_ref[...].astype(i32)


def _router(x2, w_router, router_bias, tile):
    n, d = x2.shape
    ne = N_EXPERTS
    t = min(tile, n)
    assert n % t == 0
    wt = w_router.T
    bias = jnp.broadcast_to(router_bias.reshape(ne, 1).astype(f32), (ne, LANES))
    ti = jnp.arange(t)
    upper = (ti[:, None] < ti[None, :]).astype(bf16)
    cols = pl.BlockSpec((TOP_K, t), lambda i: (0, i))
    e_t, g_t, r_t, cnt = pl.pallas_call(
        functools.partial(_router_body, t=t),
        out_shape=[jax.ShapeDtypeStruct((TOP_K, n), i32), jax.ShapeDtypeStruct((TOP_K, n), f32),
                   jax.ShapeDtypeStruct((TOP_K, n), i32), jax.ShapeDtypeStruct((ne, LANES), i32)],
        grid=(n // t,),
        in_specs=[pl.BlockSpec((t, d), lambda i: (i, 0)), _const_spec((ne, d)), _const_spec((ne, LANES)),
                  _const_spec((t, t))],
        out_specs=[cols, cols, cols, _const_spec((ne, LANES))],
        scratch_shapes=[pltpu.VMEM((ne, LANES), f32)],
        compiler_params=_cparams(("arbitrary",)),
        name="router",
    )(x2, wt, bias, upper)
    return e_t, g_t, r_t, cnt[:, 0]


def _dispatch_sc(x3d, e_t, r_t, g_t, starts, n_slots):
    n, nc, lanes = x3d.shape
    workers = SC_CORES * SC_SUBCORES
    per_w = n // workers
    tc = _sc_chunk_rows(nc, lanes, x3d.dtype)
    assert n % (workers * SC_INDEX_GROUP) == 0 and SC_INDEX_GROUP % tc == 0
    mesh = plsc.VectorSubcoreMesh(core_axis_name="c", subcore_axis_name="s")

    @functools.partial(
        pl.kernel, mesh=mesh,
        out_type=[jax.ShapeDtypeStruct((n_slots, nc, lanes), x3d.dtype),
                  jax.ShapeDtypeStruct((n_slots,), f32)],
        scratch_types=[
            pltpu.VMEM((tc, nc, lanes), x3d.dtype),
            pltpu.VMEM((TOP_K, SC_INDEX_GROUP), i32),
            pltpu.VMEM((TOP_K, SC_INDEX_GROUP), i32),
            pltpu.VMEM((TOP_K, SC_INDEX_GROUP), f32),
            pltpu.VMEM((TOP_K, tc), i32),
            pltpu.VMEM((TOP_K, tc), f32),
            pltpu.VMEM((N_EXPERTS,), i32),
            pltpu.SemaphoreType.DMA,
        ],
        compiler_params=pltpu.CompilerParams(use_tc_tiling_on_sc=True, needs_layout_passes=False),
    )
    def dispatch(x_hbm, e_hbm, r_hbm, g_hbm, st_hbm, o_hbm, gs_hbm, rows_v, e_v, r_v, g_v, slot_v, gate_v, st_v, sem):
        wid = lax.axis_index("s") * SC_CORES + lax.axis_index("c")
        pltpu.sync_copy(st_hbm, st_v)

        @pl.loop(0, per_w // SC_INDEX_GROUP)
        def _(gi):
            base = wid * per_w + gi * SC_INDEX_GROUP
            pltpu.sync_copy(e_hbm.at[:, pl.ds(base, SC_INDEX_GROUP)], e_v)
            pltpu.sync_copy(r_hbm.at[:, pl.ds(base, SC_INDEX_GROUP)], r_v)
            pltpu.sync_copy(g_hbm.at[:, pl.ds(base, SC_INDEX_GROUP)], g_v)
            for h in range(SC_INDEX_GROUP // tc):
                off = h * tc
                pltpu.sync_copy(x_hbm.at[pl.ds(base + off, tc)], rows_v)
                for kk in range(TOP_K):
                    for j in range(tc // SC_LANES):
                        src = pl.ds(off + j * SC_LANES, SC_LANES)
                        dst = pl.ds(j * SC_LANES, SC_LANES)
                        slot_v[kk, dst] = r_v[kk, src] + plsc.load_gather(st_v, [e_v[kk, src]])
                        gate_v[kk, dst] = g_v[kk, src]
                copies = [pltpu.async_copy(rows_v, o_hbm.at[slot_v.at[kk]], sem) for kk in range(TOP_K)]
                copies += [pltpu.async_copy(gate_v.at[kk], gs_hbm.at[slot_v.at[kk]], sem) for kk in range(TOP_K)]
                for cp in copies:
                    cp.wait()

    return dispatch(x3d, e_t, r_t, g_t, starts)


def _gmm_body(gid_ref, tid_ref, nrows_ref, newg_ref, nextg_ref, ord_ref, x_ref, g_ref,
              wg_hbm, wu_hbm, wd_hbm, o_ref, wg_b, wu_b, wd_b, sems, *, tm):
    v = pl.program_id(0)
    n_real = nrows_ref[v]
    nc = wg_b.shape[1] // LANES
    slot = ord_ref[v] % 2

    def weight_copies(expert, dst_slot):
        return [pltpu.make_async_copy(src.at[expert], dst.at[dst_slot], sems.at[dst_slot])
                for src, dst in ((wg_hbm, wg_b), (wu_hbm, wu_b), (wd_hbm, wd_b))]

    @pl.when(newg_ref[v] == 1)
    def _():
        @pl.when(v == 0)
        def _():
            for cp in weight_copies(gid_ref[v], slot):
                cp.start()

        for cp in weight_copies(gid_ref[v], slot):
            cp.wait()

        @pl.when(nextg_ref[v] >= 0)
        def _():
            for cp in weight_copies(nextg_ref[v], 1 - slot):
                cp.start()

    @pl.when(n_real > 0)
    def _():
        nw = nc // 2
        real = lax.broadcasted_iota(i32, (tm, 1), 0) < n_real
        x = _unpack_bf16_pairs([x_ref[_lane_chunk(tm, nw, j)] for j in range(nw)])
        x = jnp.where(real, x, 0.0).astype(bf16)
        hg = _dot(x, wg_b[slot])
        h = hg * _sigmoid(hg) * _dot(x, wu_b[slot])
        g = g_ref[...]
        g_cols = jnp.concatenate([g, jnp.zeros((SUBLANES - g.shape[0], LANES), f32)], axis=0).T
        g_col = jnp.concatenate([g_cols[:, r:r + 1] for r in range(tm // LANES)], axis=0)
        h = h * jnp.where(real, g_col, 0.0)
        y = _dot(h.astype(bf16), wd_b[slot])
        for c in range(nc):
            o_ref[_lane_chunk(tm, nc, c)] = y[:, c * LANES:(c + 1) * LANES]


def _expert_layout(counts, n_assign, tm):
    tiles = (counts + tm - 1) // tm
    starts = ((jnp.cumsum(tiles) - tiles) * tm).astype(i32)
    capacity = (n_assign // tm + counts.shape[0]) * tm
    return starts, tiles, capacity


def _gmm(xs, gates, we_gate, we_up, we_down, counts, tm):
    ne, d, de = we_gate.shape
    nc = d // LANES
    nw = nc // 2
    n_tiles = xs.shape[0] // (nw * tm)
    assert tm % LANES == 0 and tm // LANES <= SUBLANES
    gates3 = gates.reshape(n_tiles, tm // LANES, LANES)
    tiles = (counts + tm - 1) // tm
    tile_end = jnp.cumsum(tiles)
    tile_start = tile_end - tiles
    vi = jnp.arange(n_tiles, dtype=i32)
    valid = vi < tile_end[-1]
    gid = jnp.minimum(jnp.sum((tile_end[None, :] <= vi[:, None]).astype(i32), axis=1), ne - 1)
    gid = jnp.where(valid, gid, jnp.max(jnp.where(valid, gid, 0)))
    onehot = gid[:, None] == jnp.arange(ne, dtype=i32)[None, :]
    pick = lambda table: jnp.sum(jnp.where(onehot, table[None, :], 0), axis=1)
    tid = jnp.minimum(vi, tile_end[-1] - 1).astype(i32)
    n_real = jnp.where(valid, jnp.clip(pick(counts) - (vi - pick(tile_start)) * tm, 0, tm), 0).astype(i32)
    one = jnp.ones((1,), i32)
    newg = jnp.concatenate([one, (gid[1:] != gid[:-1]).astype(i32)])
    later = gid[None, :] > gid[:, None]
    nextg = jnp.min(jnp.where(later, gid[None, :], ne), axis=1)
    nextg = jnp.where(nextg < ne, nextg, -1).astype(i32)
    order = (jnp.cumsum(newg) - 1).astype(i32)
    rows = lambda chunks: pl.BlockSpec((tm * chunks, LANES), lambda v, g, t, *_: (t[v], 0))
    hbm = pl.BlockSpec(memory_space=pl.ANY)
    return pl.pallas_call(
        functools.partial(_gmm_body, tm=tm),
        out_shape=jax.ShapeDtypeStruct((n_tiles * tm * nc, LANES), f32),
        grid_spec=pltpu.PrefetchScalarGridSpec(
            num_scalar_prefetch=6, grid=(tile_end[-1],),
            in_specs=[rows(nw), pl.BlockSpec((None, tm // LANES, LANES), lambda v, g, t, *_: (t[v], 0, 0)),
                      hbm, hbm, hbm],
            out_specs=rows(nc),
            scratch_shapes=[pltpu.VMEM((2, d, de), bf16), pltpu.VMEM((2, d, de), bf16), pltpu.VMEM((2, de, d), bf16),
                            pltpu.SemaphoreType.DMA((2,))]),
        compiler_params=_cparams(("arbitrary",)),
        name="moe_experts",
    )(gid, tid, n_real, newg, nextg, order, xs, gates3, we_gate, we_up, we_down)


def _combine_sc(ys3d, e_t, r_t, starts, n):
    _, nc, lanes = ys3d.shape
    workers = SC_CORES * SC_SUBCORES
    per_w = n // workers
    tc = _sc_chunk_rows(nc, lanes, f32)
    assert n % (workers * SC_INDEX_GROUP) == 0 and SC_INDEX_GROUP % tc == 0
    mesh = plsc.VectorSubcoreMesh(core_axis_name="c", subcore_axis_name="s")

    @functools.partial(
        pl.kernel, mesh=mesh,
        out_type=jax.ShapeDtypeStruct((n, nc, lanes), f32),
        scratch_types=[
            pltpu.VMEM((tc, nc, lanes), f32),
            pltpu.VMEM((TOP_K, SC_INDEX_GROUP), i32),
            pltpu.VMEM((TOP_K, SC_INDEX_GROUP), i32),
            pltpu.VMEM((TOP_K, tc), i32),
            pltpu.VMEM((N_EXPERTS,), i32),
            pltpu.SemaphoreType.DMA,
        ],
        compiler_params=pltpu.CompilerParams(use_tc_tiling_on_sc=True, needs_layout_passes=False),
    )
    def combine(y_hbm, e_hbm, r_hbm, st_hbm, o_hbm, acc_v, e_v, r_v, slot_v, st_v, sem):
        wid = lax.axis_index("s") * SC_CORES + lax.axis_index("c")
        pltpu.sync_copy(st_hbm, st_v)

        @pl.loop(0, per_w // SC_INDEX_GROUP)
        def _(gi):
            base = wid * per_w + gi * SC_INDEX_GROUP
            pltpu.sync_copy(e_hbm.at[:, pl.ds(base, SC_INDEX_GROUP)], e_v)
            pltpu.sync_copy(r_hbm.at[:, pl.ds(base, SC_INDEX_GROUP)], r_v)
            for h in range(SC_INDEX_GROUP // tc):
                off = h * tc
                for kk in range(TOP_K):
                    for j in range(tc // SC_LANES):
                        src = pl.ds(off + j * SC_LANES, SC_LANES)
                        slot_v[kk, pl.ds(j * SC_LANES, SC_LANES)] = (
                            r_v[kk, src] + plsc.load_gather(st_v, [e_v[kk, src]]))
                pltpu.async_copy(y_hbm.at[slot_v.at[0]], acc_v, sem).wait()
                copies = [pltpu.async_copy(y_hbm.at[slot_v.at[kk]], acc_v, sem, add=True) for kk in range(1, TOP_K)]
                for cp in copies:
                    cp.wait()
                pltpu.sync_copy(acc_v, o_hbm.at[pl.ds(base + off, tc)])

    return combine(ys3d, e_t, r_t, starts)


def _ffn_out_body(x_ref, r_ref, wg_ref, wu_ref, wd_ref, g_ref, b_ref, *rest, alpha):
    o_ref = rest[-1]
    x = x_ref[...]
    xb = x.astype(bf16)
    hg = _dot(xb, wg_ref[...])
    h = hg * _sigmoid(hg) * _dot(xb, wu_ref[...])
    shared = _dot(h.astype(bf16), wd_ref[...])
    nc = x.shape[1] // LANES
    routed = jnp.concatenate([r_ref[_lane_chunk(x.shape[0], nc, c)] for c in range(nc)], axis=1)
    o_ref[...] = _layer_norm(alpha * x + routed + shared, g_ref[...], b_ref[...])


def _ffn_out(x2, routed_rows, ws_gate, ws_up, ws_down, g, b, alpha, tile, n_all=None, row0=0, prev=None):
    n, d = x2.shape
    n_all = n if n_all is None else n_all
    tile = min(tile, n)
    assert row0 % tile == 0 and n % tile == 0
    first = row0 // tile
    wg, wu, wd = ws_gate.astype(bf16), ws_up.astype(bf16), ws_down.astype(bf16)
    rows = pl.BlockSpec((tile, d), lambda i: (i + first, 0))
    in_specs = [pl.BlockSpec((tile, d), lambda i: (i, 0)),
                pl.BlockSpec((tile * (d // LANES), LANES), lambda i: (i, 0)), _const_spec(wg.shape),
                _const_spec(wu.shape), _const_spec(wd.shape), _const_spec((1, d)), _const_spec((1, d))]
    args = [x2, routed_rows, wg, wu, wd, g.reshape(1, d), b.reshape(1, d)]
    aliases = {}
    if prev is not None:
        in_specs.append(pl.BlockSpec(memory_space=pl.ANY))
        args.append(prev)
        aliases = {len(args) - 1: 0}
    return pl.pallas_call(
        functools.partial(_ffn_out_body, alpha=alpha),
        out_shape=jax.ShapeDtypeStruct((n_all, d), f32),
        grid=(n // tile,),
        in_specs=in_specs,
        out_specs=rows,
        input_output_aliases=aliases,
        compiler_params=_cparams(("parallel",)),
        name="ffn_out_ln3",
    )(*args)


def _layer(x, mem, positions, w_in, mu_shift, w_decay_up, w0, a_up, a0, g_up, k_k, k_a, r_k, lnx_g, lnx_b, sinks,
           w_o, ln1_g, ln1_b, wm_q, wm_kv, wm_o, ln2_g, ln2_b, w_router, router_bias, we_gate, we_up, we_down,
           ws_gate, ws_up, ws_down, ln3_g, ln3_b, *, layer, alpha):
    b, s, d = x.shape
    n = b * s
    xf = x.reshape(n, d)
    w_in_b = w_in.astype(bf16)
    (u_r,) = _proj(xf, [w_in_b[:, :RWKV_COLS]], [f32], tile=ROW_TILE)
    rwkv_tt = RWKV_TILE
    experts = (we_gate, we_up, we_down)
    if N_EXPERTS % (b * (s // min(rwkv_tt, s))) == 0:
        y_r, experts_b = _rwkv(u_r.reshape(b, s, RWKV_COLS), mu_shift, w_decay_up, w0, a_up, a0, g_up, k_k, k_a, r_k,
                               lnx_g, lnx_b, tt=rwkv_tt, side=experts, layer=layer)
    else:
        y_r, _ = _rwkv(u_r.reshape(b, s, RWKV_COLS), mu_shift, w_decay_up, w0, a_up, a0, g_up, k_k, k_a, r_k,
                       lnx_g, lnx_b, tt=rwkv_tt)
        experts_b = [w[layer].astype(bf16) for w in experts]
    y_s = _swa(x, w_in_b[:, RWKV_COLS:], positions, sinks)
    m = mem.shape[1]
    (kv,) = _proj(mem.reshape(b * m, d), [wm_kv.astype(bf16)], [bf16], tile=ROW_TILE)
    nc = d // LANES
    nw = nc // 2
    parts = MOE_PARTS
    if b % parts or (n // parts) % (SC_CORES * SC_SUBCORES * SC_INDEX_GROUP):
        parts = 1
    npart = n // parts
    bpp = b // parts
    x3 = None
    for part in range(parts):
        x2, x2_rows = _mix_xattn(y_r, y_s, w_o, x, ln1_g, ln1_b, kv.reshape(b, m, 2 * d), wm_q, wm_o, ln2_g, ln2_b,
                                 alpha, tile=ROW_TILE, batch0=part * bpp, n_batch=bpp)
        x2 = x2.reshape(npart, d)
        e_t, g_t, r_t, counts = _router(x2, w_router, router_bias, tile=ROW_TILE)
        starts, _, n_slots = _expert_layout(counts, npart * TOP_K, GMM_TILE)
        xs, gs = _dispatch_sc(x2_rows.reshape(npart, nw, LANES), e_t, r_t, g_t, starts, n_slots)
        ys = _gmm(xs.reshape(n_slots * nw, LANES), gs, *experts_b, counts, tm=GMM_TILE)
        routed = _combine_sc(ys.reshape(n_slots, nc, LANES), e_t, r_t, starts, npart)
        x3 = _ffn_out(x2, routed.reshape(npart * nc, LANES), ws_gate, ws_up, ws_down, ln3_g, ln3_b, alpha, tile=ROW_TILE,
                      n_all=n, row0=part * npart, prev=x3)
    return x3.reshape(b, s, d)


def kernel(x, mem, positions, w_in, mu_shift, w_decay_up, w0, a_up, a0, g_up, k_k, k_a, r_k, lnx_g, lnx_b, sinks, w_o, ln1_g, ln1_b, wm_q, wm_kv, wm_o, ln2_g, ln2_b, w_router, router_bias, we_gate, we_up, we_down, ws_gate, ws_up, ws_down, ln3_g, ln3_b):
    depth = w_in.shape[0]
    alpha = (2 * depth) ** 0.25
    for l in range(depth):
        x = _layer(x, mem, positions, w_in[l], mu_shift[l], w_decay_up[l], w0[l], a_up[l], a0[l], g_up[l], k_k[l],
                   k_a[l], r_k[l], lnx_g[l], lnx_b[l], sinks[l], w_o[l], ln1_g[l], ln1_b[l], wm_q[l], wm_kv[l],
                   wm_o[l], ln2_g[l], ln2_b[l], w_router[l], router_bias[l], we_gate, we_up, we_down,
                   ws_gate[l], ws_up[l], ws_down[l], ln3_g[l], ln3_b[l], layer=l, alpha=alpha)
    return x
```
